```python
import math
import jax
import jax.numpy as jnp
from jax import lax
import numpy as np

D_MODEL = 1024
BATCH = 16
SEQ = 2048
DEPTH = 1
DEC_BATCH = 128
DEC_SEQ = 1
PAST_LEN = 8192
PAGE_SIZE = 128

MLA_HEADS = 8
MLA_NOPE = 64
MLA_ROPE = 32
MLA_V = 64
MLA_Q_RANK = 384
MLA_KV_RANK = 256
MLA_ROW = MLA_KV_RANK + MLA_ROPE
DSA_HEADS = 8
DSA_KV_HEADS = 2
DSA_HEAD_DIM = 64
DSA_GROUP = DSA_HEADS // DSA_KV_HEADS
IDX_HEADS = 8
IDX_DIM = 64
TOPK_MAX = 256
D_MIX = MLA_HEADS * MLA_V + DSA_HEADS * DSA_HEAD_DIM
N_BUCKETS = 32
MAX_DISTANCE = 128
ROPE_THETA = 10000.0
D_FF = ((8 * D_MODEL + 3 * 256 - 1) // (3 * 256)) * 256
Q_BLOCK = 128
EPS = 1e-6
IN_SPLITS = (MLA_Q_RANK, MLA_KV_RANK, MLA_ROPE,
             DSA_HEADS * DSA_HEAD_DIM, DSA_KV_HEADS * DSA_HEAD_DIM, DSA_KV_HEADS * DSA_HEAD_DIM,
             IDX_HEADS * IDX_DIM, IDX_DIM, IDX_HEADS)
D_IN = sum(IN_SPLITS)
MLA_SCALE = (MLA_NOPE + MLA_ROPE) ** -0.5
DSA_SCALE = DSA_HEAD_DIM ** -0.5
IDX_SCALE = (IDX_DIM * IDX_HEADS) ** -0.5

kernel_name = 'hymba_mla_dsa_adaln_decode_step'


def rms_norm(x, g):
    xf = x.astype(jnp.float32)
    y = xf * lax.rsqrt(jnp.mean(xf * xf, axis=-1, keepdims=True) + EPS)
    return (y * g.astype(jnp.float32)).astype(x.dtype)


def rope(x, pos):
    half = x.shape[-1] // 2
    freq = ROPE_THETA ** (-jnp.arange(half, dtype=jnp.float32) / half)
    ang = pos.astype(jnp.float32)[:, None] * freq[None, :]
    ang = ang.reshape((ang.shape[0],) + (1,) * (x.ndim - 3) + (half,))
    cos, sin = jnp.cos(ang), jnp.sin(ang)
    xf = x.astype(jnp.float32)
    x1, x2 = xf[..., :half], xf[..., half:]
    return jnp.concatenate([x1 * cos - x2 * sin, x1 * sin + x2 * cos], axis=-1).astype(x.dtype)


def t5_bucket(dist):
    max_exact = N_BUCKETS // 2
    d = jnp.maximum(dist, 0)
    log_ratio = jnp.log(jnp.maximum(d, max_exact).astype(jnp.float32) / max_exact) / math.log(MAX_DISTANCE / max_exact)
    large = jnp.minimum(max_exact + (log_ratio * (N_BUCKETS - max_exact)).astype(jnp.int32), N_BUCKETS - 1)
    return jnp.where(d < max_exact, d, large)


def to_blocks(a, n_blocks):
    return a.reshape((a.shape[0], n_blocks, Q_BLOCK) + a.shape[2:]).swapaxes(0, 1)


def gather_rows(a, idx):
    return jax.vmap(lambda ab, ib: ab[ib])(a, idx)


def mla_expand(lat, w_ukv, g_k_nope):
    kv = jnp.einsum('btr,rhd->bthd', lat, w_ukv)
    return rms_norm(kv[..., :MLA_NOPE], g_k_nope), kv[..., MLA_NOPE:]


def mla_partial(q_nope, q_rope, k_nope, k_rope, v, mask):
    s = (jnp.einsum('bqhd,bkhd->bhqk', q_nope, k_nope)
         + jnp.einsum('bqhd,bkd->bhqk', q_rope, k_rope)).astype(jnp.float32) * MLA_SCALE
    if mask is not None:
        s = jnp.where(mask, s, -jnp.inf)
    m = jnp.max(s, axis=-1)
    p = jnp.exp(s - m[..., None])
    o = jnp.einsum('bhqk,bkhd->bqhd', p, v.astype(jnp.float32))
    return m, jnp.sum(p, axis=-1), o


def mla_prompt(q_nope, q_rope, lat, k_rope, w_ukv, g_k_nope):
    b, s = q_nope.shape[:2]
    nb = s // Q_BLOCK
    k_nope, v = mla_expand(lat, w_ukv, g_k_nope)
    k_pos = jnp.arange(s)

    def block(args):
        qn, qr, i = args
        q_pos = i * Q_BLOCK + jnp.arange(Q_BLOCK)
        m, l, o = mla_partial(qn, qr, k_nope, k_rope, v, k_pos[None, :] <= q_pos[:, None])
        return o / jnp.swapaxes(l, 1, 2)[..., None]

    out = lax.map(block, (to_blocks(q_nope, nb), to_blocks(q_rope, nb), jnp.arange(nb)))
    return out.swapaxes(0, 1).reshape(b, s, MLA_HEADS * MLA_V).astype(q_nope.dtype)


def mla_sample(q_nope, q_rope, lat_new, krope_new, cache_mla, page_table, w_ukv, g_k_nope):
    b, t = q_nope.shape[:2]

    def page(j):
        rows = cache_mla[page_table[:, j]]
        k_nope, v = mla_expand(rows[..., :MLA_KV_RANK], w_ukv, g_k_nope)
        return mla_partial(q_nope, q_rope, k_nope, rows[..., MLA_KV_RANK:], v, None)

    m_p, l_p, o_p = lax.map(page, jnp.arange(page_table.shape[1]))
    k_nope_n, v_n = mla_expand(lat_new, w_ukv, g_k_nope)
    causal = jnp.tril(jnp.ones((t, t), dtype=bool))
    m_n, l_n, o_n = mla_partial(q_nope, q_rope, k_nope_n, krope_new, v_n, causal)
    m_all = jnp.maximum(jnp.max(m_p, axis=0), m_n)
    a_p = jnp.exp(m_p - m_all)
    a_n = jnp.exp(m_n - m_all)
    l = jnp.sum(a_p * l_p, axis=0) + a_n * l_n
    o = jnp.einsum('nbhq,nbqhd->bqhd', a_p, o_p) + jnp.swapaxes(a_n, 1, 2)[..., None] * o_n
    o = o / jnp.swapaxes(l, 1, 2)[..., None]
    return o.reshape(b, t, MLA_HEADS * MLA_V).astype(q_nope.dtype)


def index_scores(q_idx, w_idx, k_idx):
    dots = jnp.einsum('bqhd,bkd->bqhk', q_idx, k_idx).astype(jnp.float32)
    return jnp.einsum('bqhk,bqh->bqk', jax.nn.relu(dots), w_idx.astype(jnp.float32)) * IDX_SCALE


def dsa_attend(q, q_pos, k_sel, v_sel, sel, valid, rel_bias):
    b, t = q.shape[:2]
    n = sel.shape[-1]
    qg = q.reshape(b, t, DSA_KV_HEADS, DSA_GROUP, DSA_HEAD_DIM)
    s = jnp.einsum('bqcgd,bqncd->bqcgn', qg, k_sel).astype(jnp.float32) * DSA_SCALE
    bias = rel_bias[t5_bucket(q_pos[None, :, None] - sel)]
    bias = bias.reshape(b, t, n, DSA_KV_HEADS, DSA_GROUP).transpose(0, 1, 3, 4, 2)
    s = jnp.where(valid[:, :, None, None, :], s + bias.astype(jnp.float32), -jnp.inf)
    p = jax.nn.softmax(s, axis=-1)
    o = jnp.einsum('bqcgn,bqncd->bqcgd', p, v_sel.astype(jnp.float32))
    return o.reshape(b, t, DSA_HEADS * DSA_HEAD_DIM).astype(q.dtype)


def dsa_prompt(q, k, v, q_idx, w_idx, k_idx, rel_bias):
    b, s = q.shape[:2]
    nb = s // Q_BLOCK
    n_sel = min(TOPK_MAX, s // 4)
    k_pos = jnp.arange(s)

    def block(args):
        q_b, qi_b, w_b, i = args
        q_pos = i * Q_BLOCK + jnp.arange(Q_BLOCK)
        sc = jnp.where(k_pos[None, None, :] <= q_pos[None, :, None], index_scores(qi_b, w_b, k_idx), -jnp.inf)
        _, sel = lax.top_k(sc, n_sel)
        valid = sel <= q_pos[None, :, None]
        return dsa_attend(q_b, q_pos, gather_rows(k, sel), gather_rows(v, sel), sel, valid, rel_bias)

    out = lax.map(block, (to_blocks(q, nb), to_blocks(q_idx, nb), to_blocks(w_idx, nb), jnp.arange(nb)))
    return out.swapaxes(0, 1).reshape(b, s, DSA_HEADS * DSA_HEAD_DIM)


def dsa_sample(q, kv_new, q_idx, w_idx, k_idx_new, cache_kv, cache_idx, page_table, rel_bias):
    b, t = q.shape[:2]
    past = page_table.shape[1] * PAGE_SIZE
    n_keys = past + t
    n_sel = min(TOPK_MAX, n_keys // 4)
    kidx_past = cache_idx[page_table].reshape(b, past, IDX_DIM)
    kidx_all = jnp.concatenate([kidx_past, k_idx_new], axis=1)
    q_pos = past + jnp.arange(t)
    sc = jnp.where(jnp.arange(n_keys)[None, None, :] <= q_pos[None, :, None],
                   index_scores(q_idx, w_idx, kidx_all), -jnp.inf)
    _, sel = lax.top_k(sc, n_sel)
    valid = sel <= q_pos[None, :, None]
    in_past = sel < past
    pidx = jnp.minimum(sel, past - 1)
    phys = jnp.take_along_axis(page_table, (pidx // PAGE_SIZE).reshape(b, -1), axis=1).reshape(sel.shape)
    rows_past = cache_kv[phys, pidx % PAGE_SIZE]
    rows_new = gather_rows(kv_new, jnp.clip(sel - past, 0, t - 1))
    kv_sel = jnp.where(in_past[..., None, None, None], rows_past, rows_new)
    return dsa_attend(q, q_pos, kv_sel[:, :, :, 0], kv_sel[:, :, :, 1], sel, valid, rel_bias)


def mixer_projections(h, pos, lw):
    b, t = h.shape[:2]
    split_at = np.cumsum(IN_SPLITS)[:-1].tolist()
    q_lat, kv_lat, k_rope, q, k, v, q_idx, k_idx, w_idx = jnp.split(h @ lw['w_in'], split_at, axis=-1)
    qm = jnp.einsum('btr,rhd->bthd', rms_norm(q_lat, lw['g_q_lat']), lw['w_uq'])
    q_nope = rms_norm(qm[..., :MLA_NOPE], lw['g_mla_q_nope'])
    q_rope = rope(rms_norm(qm[..., MLA_NOPE:], lw['g_mla_q_rope']), pos)
    lat = rms_norm(kv_lat, lw['g_kv_lat'])
    k_rope = rope(rms_norm(k_rope, lw['g_mla_k_rope']), pos)
    q = rms_norm(q.reshape(b, t, DSA_HEADS, DSA_HEAD_DIM), lw['g_dsa_q'])
    k = rms_norm(k.reshape(b, t, DSA_KV_HEADS, DSA_HEAD_DIM), lw['g_dsa_k'])
    v = v.reshape(b, t, DSA_KV_HEADS, DSA_HEAD_DIM)
    q_idx = q_idx.reshape(b, t, IDX_HEADS, IDX_DIM)
    kv = jnp.stack([k, v], axis=2)
    return (q_nope, q_rope, lat, k_rope), (q, kv, q_idx, w_idx, k_idx)


def merge_groups(o_mla, o_dsa, g_out, w_out):
    g_a, g_b = jnp.split(g_out, [MLA_HEADS * MLA_V])
    return jnp.concatenate([rms_norm(o_mla, g_a), rms_norm(o_dsa, g_b)], axis=-1) @ w_out


def swiglu(h, w_ffn_in, w_ffn_out):
    g, u = jnp.split(h @ w_ffn_in, 2, axis=-1)
    return (jax.nn.silu(g) * u) @ w_ffn_out


def decoder_layer(x, c, pos, attend, lw):
    mod = jax.nn.silu(c) @ lw['w_ada'] + lw['b_ada']
    shift_m, scale_m, gate_m, shift_f, scale_f, gate_f = jnp.split(mod[:, None, :], 6, axis=-1)
    h = rms_norm(x, lw['g_norm_mix']) * (1 + scale_m) + shift_m
    mla_t, dsa_t = mixer_projections(h, pos, lw)
    o_mla, o_dsa = attend(mla_t, dsa_t)
    x = x + gate_m * merge_groups(o_mla, o_dsa, lw['g_out'], lw['w_out'])
    h = rms_norm(x, lw['g_norm_ffn']) * (1 + scale_f) + shift_f
    x = x + gate_f * swiglu(h, lw['w_ffn_in'], lw['w_ffn_out'])
    rows = (jnp.concatenate([mla_t[2], mla_t[3]], axis=-1), dsa_t[1], dsa_t[4])
    return x, rows


def setup_inputs(seed: int = 0) -> dict:
    key = jax.random.key(seed)
    ks = iter(jax.random.split(key, 40))

    def nrm(shape, scale):
        return jax.random.normal(next(ks), shape, jnp.float32) * scale

    def gain(shape):
        return 1.0 + 0.05 * jax.random.normal(next(ks), shape, jnp.float32)

    n_pages = PAST_LEN // PAGE_SIZE
    n_used = DEC_BATCH * n_pages
    n_pool = n_used + max(1, n_used // 4)
    return {
        'x_prompt': nrm((BATCH, SEQ, D_MODEL), 1.0),
        'x_sample': nrm((DEC_BATCH, DEC_SEQ, D_MODEL), 1.0),
        'c_prompt': nrm((BATCH, D_MODEL), 1.0),
        'c_sample': nrm((DEC_BATCH, D_MODEL), 1.0),
        'cache_mla': nrm((DEPTH, n_pool, PAGE_SIZE, MLA_ROW), 1.0),
        'cache_kv': nrm((DEPTH, n_pool, PAGE_SIZE, 2, DSA_KV_HEADS, DSA_HEAD_DIM), 1.0),
        'cache_idx': nrm((DEPTH, n_pool, PAGE_SIZE, IDX_DIM), 1.0),
        'page_table': jax.random.permutation(next(ks), n_pool)[:n_used].reshape(DEC_BATCH, n_pages).astype(jnp.int32),
        'rel_bias': nrm((N_BUCKETS, DSA_HEADS), 0.5),
        'w_ada': nrm((DEPTH, D_MODEL, 6 * D_MODEL), 0.5 * D_MODEL ** -0.5),
        'b_ada': nrm((DEPTH, 6 * D_MODEL), 0.01),
        'g_norm_mix': gain((DEPTH, D_MODEL)),
        'g_norm_ffn': gain((DEPTH, D_MODEL)),
        'w_in': nrm((DEPTH, D_MODEL, D_IN), D_MODEL ** -0.5),
        'g_q_lat': gain((DEPTH, MLA_Q_RANK)),
        'w_uq': nrm((DEPTH, MLA_Q_RANK, MLA_HEADS, MLA_NOPE + MLA_ROPE), MLA_Q_RANK ** -0.5),
        'g_kv_lat': gain((DEPTH, MLA_KV_RANK)),
        'w_ukv': nrm((DEPTH, MLA_KV_RANK, MLA_HEADS, MLA_NOPE + MLA_V), MLA_KV_RANK ** -0.5),
        'g_mla_q_nope': gain((DEPTH, MLA_NOPE)),
        'g_mla_q_rope': gain((DEPTH, MLA_ROPE)),
        'g_mla_k_nope': gain((DEPTH, MLA_NOPE)),
        'g_mla_k_rope': gain((DEPTH, MLA_ROPE)),
        'g_dsa_q': gain((DEPTH, DSA_HEAD_DIM)),
        'g_dsa_k': gain((DEPTH, DSA_HEAD_DIM)),
        'g_out': gain((DEPTH, D_MIX)),
        'w_out': nrm((DEPTH, D_MIX, D_MODEL), D_MIX ** -0.5),
        'w_ffn_in': nrm((DEPTH, D_MODEL, 2 * D_FF), D_MODEL ** -0.5),
        'w_ffn_out': nrm((DEPTH, D_FF, D_MODEL), D_FF ** -0.5),
    }


def reference(x_prompt, x_sample, c_prompt, c_sample, cache_mla, cache_kv, cache_idx, page_table,
              rel_bias, w_ada, b_ada, g_norm_mix, g_norm_ffn, w_in, g_q_lat, w_uq, g_kv_lat, w_ukv,
              g_mla_q_nope, g_mla_q_rope, g_mla_k_nope, g_mla_k_rope, g_dsa_q, g_dsa_k, g_out, w_out,
              w_ffn_in, w_ffn_out):
    past_len = page_table.shape[1] * PAGE_SIZE
    pos_p = jnp.arange(x_prompt.shape[1])
    pos_s = past_len + jnp.arange(x_sample.shape[1])
    xp, xs = x_prompt, x_sample
    rows_p, rows_s = [], []
    for layer in range(DEPTH):
        lw = {
            'w_ada': w_ada[layer], 'b_ada': b_ada[layer],
            'g_norm_mix': g_norm_mix[layer], 'g_norm_ffn': g_norm_ffn[layer],
            'w_in': w_in[layer], 'g_q_lat': g_q_lat[layer], 'w_uq': w_uq[layer],
            'g_kv_lat': g_kv_lat[layer], 'w_ukv': w_ukv[layer],
            'g_mla_q_nope': g_mla_q_nope[layer], 'g_mla_q_rope': g_mla_q_rope[layer],
            'g_mla_k_nope': g_mla_k_nope[layer], 'g_mla_k_rope': g_mla_k_rope[layer],
            'g_dsa_q': g_dsa_q[layer], 'g_dsa_k': g_dsa_k[layer],
            'g_out': g_out[layer], 'w_out': w_out[layer],
            'w_ffn_in': w_ffn_in[layer], 'w_ffn_out': w_ffn_out[layer],
        }

        def attend_prompt(mla_t, dsa_t, lw=lw):
            q_nope, q_rope, lat, k_rope = mla_t
            q, kv, q_idx, w_idx, k_idx = dsa_t
            return (mla_prompt(q_nope, q_rope, lat, k_rope, lw['w_ukv'], lw['g_mla_k_nope']),
                    dsa_prompt(q, kv[:, :, 0], kv[:, :, 1], q_idx, w_idx, k_idx, rel_bias))

        def attend_sample(mla_t, dsa_t, lw=lw, layer=layer):
            q_nope, q_rope, lat, k_rope = mla_t
            q, kv, q_idx, w_idx, k_idx = dsa_t
            return (mla_sample(q_nope, q_rope, lat, k_rope, cache_mla[layer], page_table,
                               lw['w_ukv'], lw['g_mla_k_nope']),
                    dsa_sample(q, kv, q_idx, w_idx, k_idx, cache_kv[layer], cache_idx[layer],
                               page_table, rel_bias))

        xp, rp = decoder_layer(xp, c_prompt, pos_p, attend_prompt, lw)
        xs, rs = decoder_layer(xs, c_sample, pos_s, attend_sample, lw)
        rows_p.append(rp)
        rows_s.append(rs)
    new_mla_prompt = jnp.stack([r[0] for r in rows_p])
    new_kv_prompt = jnp.stack([r[1] for r in rows_p])
    new_idx_prompt = jnp.stack([r[2] for r in rows_p])
    new_mla_sample = jnp.stack([r[0] for r in rows_s])
    new_kv_sample = jnp.stack([r[1] for r in rows_s])
    new_idx_sample = jnp.stack([r[2] for r in rows_s])
    return (xp, xs, new_mla_prompt, new_kv_prompt, new_idx_prompt, new_mla_sample, new_kv_sample, new_idx_sample)
```

```python
import functools
import math

import numpy as np
import jax
import jax.numpy as jnp
from jax import lax
from jax.experimental import pallas as pl
from jax.experimental.pallas import tpu as pltpu

D_MODEL = 1024
PAGE_SIZE = 128
MLA_HEADS = 8
MLA_NOPE = 64
MLA_ROPE = 32
MLA_V = 64
MLA_Q_RANK = 384
MLA_KV_RANK = 256
MLA_ROW = MLA_KV_RANK + MLA_ROPE
DSA_HEADS = 8
DSA_KV_HEADS = 2
DSA_HEAD_DIM = 64
DSA_GROUP = DSA_HEADS // DSA_KV_HEADS
IDX_HEADS = 8
IDX_DIM = 64
TOPK_MAX = 256
N_BUCKETS = 32
MAX_DISTANCE = 128
ROPE_THETA = 10000.0
D_FF = ((8 * D_MODEL + 3 * 256 - 1) // (3 * 256)) * 256
EPS = 1e-6
MLA_SCALE = (MLA_NOPE + MLA_ROPE) ** -0.5
DSA_SCALE = DSA_HEAD_DIM ** -0.5
IDX_SCALE = (IDX_DIM * IDX_HEADS) ** -0.5

LANES = 128
VMEM_LIMIT = 56 * 1024 * 1024

F32 = jnp.float32
BF16 = jnp.bfloat16
I32 = jnp.int32
NEG_BIG = -1e30
INT_MIN = -(2 ** 31)

C_QLAT = 0
C_KVLAT = C_QLAT + MLA_Q_RANK
C_Q = C_KVLAT + MLA_KV_RANK
C_K = C_Q + DSA_HEADS * DSA_HEAD_DIM
C_V = C_K + DSA_KV_HEADS * DSA_HEAD_DIM
C_QIDX = C_V + DSA_KV_HEADS * DSA_HEAD_DIM
C_TAIL = C_QIDX + IDX_HEADS * IDX_DIM
C_END = C_TAIL + LANES
T_ROPE = IDX_DIM
T_WIDX = IDX_DIM + MLA_ROPE
HALF_ROPE = MLA_ROPE // 2


def _dot(a, b):
    return jnp.dot(a, b, preferred_element_type=F32)


def _dot_nt(a, b):
    return lax.dot_general(a, b, (((1,), (1,)), ((), ())), preferred_element_type=F32)


def _split(a):
    hi = a.astype(BF16)
    lo = (a - hi.astype(F32)).astype(BF16)
    return hi, lo


def _dot3(a, b):
    ah, al = _split(a)
    bh, bl = _split(b)
    return _dot(ah, bh) + (_dot(al, bh) + _dot(ah, bl))


def _cparams(sem):
    return pltpu.CompilerParams(dimension_semantics=sem, vmem_limit_bytes=VMEM_LIMIT)


def _full(shape):
    n = len(shape)
    return pl.BlockSpec(shape, lambda *a, _n=n: (0,) * _n, pipeline_mode=pl.Buffered(1))


def _ada_kernel(c_ref, w_ref, b_ref, o_ref):
    c = c_ref[...]
    s = c / (1.0 + jnp.exp(-c))
    o_ref[...] = _dot3(s, w_ref[...]) + b_ref[...]


def _ada(c, w, b):
    n = c.shape[0]
    return pl.pallas_call(
        _ada_kernel,
        grid=(6,),
        in_specs=[pl.BlockSpec((n, D_MODEL), lambda j: (0, 0)),
                  pl.BlockSpec((D_MODEL, D_MODEL), lambda j: (0, j)),
                  pl.BlockSpec((1, D_MODEL), lambda j: (0, j))],
        out_specs=pl.BlockSpec((n, D_MODEL), lambda j: (0, j)),
        out_shape=jax.ShapeDtypeStruct((n, 6 * D_MODEL), F32),
        compiler_params=_cparams(("arbitrary",)),
    )(c, w, b.reshape(1, -1))


def _group_matrix(width, groups):
    g = np.zeros((width, width), np.float32)
    for lo, hi in groups:
        g[lo:hi, lo:hi] = 1.0 / (hi - lo)
    return jnp.asarray(g, BF16)


def _in_kernel(x_ref, sc_ref, sh_ref, cos_ref, sin_ref, gmix_ref, win_ref, gqlat_ref, wuq_ref, gqcat_ref,
               gkvlat_ref, wukv_ref, gkcat_ref, gtail_ref, gq_ref, gk_ref, mcat_ref, m64_ref,
               qcat_o, kcat_o, vmla_o, row_o, kv_o, kidx_o, tail_o, kidxdup_o, qdsa_o, kdup_o, vdup_o, qidx_o):
    tm = x_ref.shape[0]
    x = x_ref[...]
    h = x * lax.rsqrt(jnp.mean(x * x, axis=-1, keepdims=True) + EPS) * gmix_ref[...]
    hb = (h * (1.0 + sc_ref[...]) + sh_ref[...]).astype(BF16)

    def proj(lo, hi):
        return _dot(hb, win_ref[:, lo:hi])

    cos = cos_ref[...]
    sin = sin_ref[...]
    lane = lax.broadcasted_iota(I32, (tm, LANES), 1)
    first_half = lane < T_ROPE + HALF_ROPE
    rope_lanes = (lane >= T_ROPE) & (lane < T_ROPE + MLA_ROPE)
    low64 = lane < 64

    def rope(xh):
        rot = jnp.where(first_half, pltpu.roll(xh, LANES - HALF_ROPE, 1), pltpu.roll(xh, HALF_ROPE, 1))
        return xh * cos + rot * sin

    def gnorm(blk, m_ref):
        ms = _dot((blk * blk).astype(BF16), m_ref[...])
        return blk * lax.rsqrt(ms + EPS)

    ql = proj(C_QLAT, C_KVLAT)
    qln = ql * lax.rsqrt(jnp.mean(ql * ql, axis=-1, keepdims=True) + EPS) * gqlat_ref[...]
    qm = _dot(qln.astype(BF16), wuq_ref[...])
    for j in range(4):
        y = gnorm(qm[:, 256 * j:256 * j + 256], mcat_ref) * gqcat_ref[:, 256 * j:256 * j + 256]
        for t in range(2):
            qcat_o[:, 256 * j + 128 * t:256 * j + 128 * t + 128] = rope(y[:, 128 * t:128 * t + 128]).astype(BF16)

    tail = proj(C_TAIL, C_END)
    ssq = jnp.sum(jnp.where(rope_lanes, tail * tail, 0.0), axis=-1, keepdims=True) * (1.0 / MLA_ROPE)
    tn = jnp.where(rope_lanes, tail * lax.rsqrt(ssq + EPS) * gtail_ref[...], tail)
    tr = rope(tn)
    kidx_o[...] = tail[:, 0:IDX_DIM]
    tail_o[...] = tail
    kidxdup_o[...] = jnp.where(low64, tail, pltpu.roll(tail, 64, 1)).astype(BF16)
    krope = jnp.where(rope_lanes, tr, 0.0)

    kvl = proj(C_KVLAT, C_Q)
    lat = kvl * lax.rsqrt(jnp.mean(kvl * kvl, axis=-1, keepdims=True) + EPS) * gkvlat_ref[...]
    row_o[:, 0:MLA_KV_RANK] = lat
    row_o[:, MLA_KV_RANK:MLA_ROW] = tr[:, T_ROPE:T_ROPE + MLA_ROPE]
    latb = lat.astype(BF16)
    kx = _dot(latb, wukv_ref[:, 0:1024])
    for j in range(4):
        y = gnorm(kx[:, 256 * j:256 * j + 256], mcat_ref) * gkcat_ref[:, 256 * j:256 * j + 256]
        for t in range(2):
            kcat_o[:, 256 * j + 128 * t:256 * j + 128 * t + 128] = (y[:, 128 * t:128 * t + 128] + krope).astype(BF16)
    vmla_o[...] = _dot(latb, wukv_ref[:, 1024:1536]).astype(BF16)

    q = proj(C_Q, C_K)
    for j in range(2):
        qdsa_o[:, 256 * j:256 * j + 256] = (gnorm(q[:, 256 * j:256 * j + 256], m64_ref)
                                            * gq_ref[:, 256 * j:256 * j + 256]).astype(BF16)
    k = proj(C_K, C_V)
    ms = _dot((k * k).astype(BF16), m64_ref[0:128, 0:128])
    kn = k * lax.rsqrt(ms + EPS) * gk_ref[...]
    v = proj(C_V, C_QIDX)
    kv_o[:, 0:128] = kn
    kv_o[:, 128:256] = v
    kr = pltpu.roll(kn, 64, 1)
    kdup_o[:, 0:128] = jnp.where(low64, kn, kr).astype(BF16)
    kdup_o[:, 128:256] = jnp.where(low64, kr, kn).astype(BF16)
    vr = pltpu.roll(v, 64, 1)
    vdup_o[:, 0:128] = jnp.where(low64, v, vr).astype(BF16)
    vdup_o[:, 128:256] = jnp.where(low64, vr, v).astype(BF16)
    qidx_o[...] = proj(C_QIDX, C_TAIL).astype(BF16)


_IN_OUT_WIDTHS = (("qcat", 1024, BF16), ("kcat", 1024, BF16), ("vmla", 512, BF16), ("row", MLA_ROW, F32),
                  ("kv", 256, F32), ("kidx", IDX_DIM, F32), ("tail", LANES, F32), ("kidxdup", LANES, BF16),
                  ("qdsa", 512, BF16), ("kdup", 256, BF16), ("vdup", 256, BF16), ("qidx", 512, BF16))


def _in_stage(x2d, scale, shift, cos, sin, wp, tm, per_token):
    t_total = x2d.shape[0]
    n_tiles = t_total // tm
    if per_token:
        mod_spec = pl.BlockSpec((tm, D_MODEL), lambda i: (i, 0))
        tab_spec = pl.BlockSpec((1, LANES), lambda i: (0, 0))
    else:
        tiles_per_seq = cos.shape[0] // tm
        mod_spec = pl.BlockSpec((None, 1, D_MODEL), lambda i: (i // tiles_per_seq, 0, 0))
        tab_spec = pl.BlockSpec((tm, LANES), lambda i: (i % tiles_per_seq, 0))
    consts = [wp["g_mix"], wp["w_in"], wp["g_qlat"], wp["w_uq"], wp["g_qcat"], wp["g_kvlat"], wp["w_ukv"],
              wp["g_kcat"], wp["g_tail"], wp["g_q"], wp["g_k"], wp["m_cat"], wp["m_64"]]
    in_specs = ([pl.BlockSpec((tm, D_MODEL), lambda i: (i, 0)), mod_spec, mod_spec, tab_spec, tab_spec]
                + [_full(c.shape) for c in consts])
    out_specs = [pl.BlockSpec((tm, w), lambda i: (i, 0)) for _, w, _ in _IN_OUT_WIDTHS]
    out_shape = [jax.ShapeDtypeStruct((t_total, w), dt) for _, w, dt in _IN_OUT_WIDTHS]
    outs = pl.pallas_call(
        _in_kernel, grid=(n_tiles,), in_specs=in_specs, out_specs=out_specs, out_shape=out_shape,
        compiler_params=_cparams(("parallel",)),
    )(x2d, scale, shift, cos, sin, *consts)
    return {name: o for (name, _, _), o in zip(_IN_OUT_WIDTHS, outs)}


def _mla_prompt_kernel(q_ref, k_ref, v_ref, o_ref, m_scr, l_scr, acc_scr):
    tq = q_ref.shape[0]
    qi = pl.program_id(1)
    m_scr[...] = jnp.full(m_scr.shape, -jnp.inf, F32)
    l_scr[...] = jnp.zeros(l_scr.shape, F32)
    acc_scr[...] = jnp.zeros(acc_scr.shape, F32)
    lane = lax.broadcasted_iota(I32, (tq, LANES), 1)
    low64 = lane < 64
    causal = lax.broadcasted_iota(I32, (tq, tq), 1) <= lax.broadcasted_iota(I32, (tq, tq), 0)

    def step(kb, masked):
        off = pl.multiple_of(kb * tq, tq)
        for h in range(MLA_HEADS):
            s = _dot_nt(q_ref[:, h * 128:(h + 1) * 128], k_ref[pl.ds(off, tq), h * 128:(h + 1) * 128]) * MLA_SCALE
            if masked:
                s = jnp.where(causal, s, -jnp.inf)
            m_old = m_scr[h]
            m_new = jnp.maximum(m_old, jnp.max(s, axis=-1, keepdims=True))
            alpha = jnp.exp(m_old - m_new)
            p = jnp.exp(s - m_new)
            l_scr[h] = alpha * l_scr[h] + jnp.sum(p, axis=-1, keepdims=True)
            m_scr[h] = m_new
            pv = _dot(p.astype(BF16), v_ref[pl.ds(off, tq), (h // 2) * 128:(h // 2 + 1) * 128])
            a = acc_scr[h // 2]
            mine = low64 if h % 2 == 0 else jnp.logical_not(low64)
            acc_scr[h // 2] = jnp.where(mine, alpha * a + pv, a)

    def body(kb, carry):
        step(kb, False)
        return carry

    lax.fori_loop(0, qi, body, 0)
    step(qi, True)
    for p in range(MLA_HEADS // 2):
        inv = jnp.where(low64, 1.0 / l_scr[2 * p], 1.0 / l_scr[2 * p + 1])
        o_ref[:, 128 * p:128 * p + 128] = acc_scr[p] * inv


def _mla_prompt(qcat, kcat, vmla, batch, seq, tq):
    nq = seq // tq
    return pl.pallas_call(
        _mla_prompt_kernel,
        grid=(batch, nq),
        in_specs=[pl.BlockSpec((tq, 1024), lambda b, i: (b * nq + i, 0)),
                  pl.BlockSpec((seq, 1024), lambda b, i: (b, 0)),
                  pl.BlockSpec((seq, 512), lambda b, i: (b, 0))],
        out_specs=pl.BlockSpec((tq, 512), lambda b, i: (b * nq + i, 0)),
        out_shape=jax.ShapeDtypeStruct((batch * seq, 512), F32),
        scratch_shapes=[pltpu.VMEM((MLA_HEADS, tq, 1), F32), pltpu.VMEM((MLA_HEADS, tq, 1), F32),
                        pltpu.VMEM((MLA_HEADS // 2, tq, LANES), F32)],
        compiler_params=_cparams(("parallel", "arbitrary")),
    )(qcat, kcat, vmla)


def _sort_key(score):
    bits = lax.bitcast_convert_type(score + 0.0, I32)
    return bits ^ ((bits >> 31) & 0x7FFFFFFF)


def _kth_key(count_ge, n_sel, rows):
    def body(i, t):
        cand = t + (jnp.int32(1) << (31 - i))
        return jnp.where(count_ge(cand) >= n_sel, cand, t)

    return lax.fori_loop(0, 32, body, jnp.full((rows, 1), INT_MIN, I32))


def _bucket(dist):
    max_exact = N_BUCKETS // 2
    d = jnp.maximum(dist, 0)
    log_ratio = jnp.log(jnp.maximum(d, max_exact).astype(F32) / max_exact) / math.log(MAX_DISTANCE / max_exact)
    large = jnp.minimum(max_exact + (log_ratio * (N_BUCKETS - max_exact)).astype(I32), N_BUCKETS - 1)
    return jnp.where(d < max_exact, d, large)


def _bias_kernel(rb_ref, o_ref, *, a, b, c):
    rows, cols = o_ref.shape[1], o_ref.shape[2]
    dist = (a * lax.broadcasted_iota(I32, (rows, cols), 0) + b * lax.broadcasted_iota(I32, (rows, cols), 1) + c)
    bucket = _bucket(dist)
    for h in range(DSA_HEADS):
        acc = jnp.zeros((rows, cols), F32)
        for n in range(N_BUCKETS):
            acc = jnp.where(bucket == n, rb_ref[n, h], acc)
        o_ref[h] = acc


def _bias_table(rel_bias, rows, cols, a, b, c):
    return pl.pallas_call(
        functools.partial(_bias_kernel, a=a, b=b, c=c),
        in_specs=[pl.BlockSpec(memory_space=pltpu.SMEM)],
        out_specs=pl.BlockSpec(memory_space=pltpu.VMEM),
        out_shape=jax.ShapeDtypeStruct((DSA_HEADS, rows, cols), F32),
    )(rel_bias)


def _dsa_prompt_kernel(rb_ref, qidx_ref, tail_ref, kidx_ref, q_ref, k_ref, v_ref, bias_ref, o_ref,
                       qs_scr, qd_scr, wb_scr, key_scr, j_scr, m_scr, l_scr, acc_scr, *, n_sel, idx_bits):
    tq = q_ref.shape[0]
    qi = pl.program_id(1)
    lane = lax.broadcasted_iota(I32, (tq, LANES), 1)
    low64 = lane < 64
    causal = lane <= lax.broadcasted_iota(I32, (tq, LANES), 0)

    tail = tail_ref[...]
    for h in range(IDX_HEADS):
        mine = low64 if h % 2 == 0 else jnp.logical_not(low64)
        zero = jnp.zeros((tq, LANES), BF16)
        qs_scr[h * tq:(h + 1) * tq, :] = jnp.where(mine, qidx_ref[:, (h // 2) * 128:(h // 2 + 1) * 128], zero)
        qd_scr[h * tq:(h + 1) * tq, :] = jnp.where(mine, q_ref[:, (h // 2) * 128:(h // 2 + 1) * 128], zero)
        wb_scr[h] = jnp.broadcast_to(tail[:, T_WIDX + h:T_WIDX + h + 1], (tq, LANES))

    def score_block(kb, masked):
        off = pl.multiple_of(kb * tq, tq)
        d = _dot_nt(qs_scr[...], kidx_ref[pl.ds(off, tq), :])
        sc = jnp.zeros((tq, LANES), F32)
        for h in range(IDX_HEADS):
            sc = sc + jnp.maximum(d[h * tq:(h + 1) * tq, :], 0.0) * wb_scr[h]
        sc = sc * IDX_SCALE
        if masked:
            sc = jnp.where(causal, sc, -jnp.inf)
        key_scr[kb] = _sort_key(sc)

    def score_body(kb, carry):
        score_block(kb, False)
        return carry

    lax.fori_loop(0, qi, score_body, 0)
    score_block(qi, True)

    def count(pred):
        def body(kb, c):
            return c + jnp.where(pred(key_scr[kb], kb), 1.0, 0.0)
        c = lax.fori_loop(0, qi + 1, body, jnp.zeros((tq, LANES), F32))
        return jnp.sum(c, axis=-1, keepdims=True)

    thr = _kth_key(lambda cand: count(lambda k, kb: k >= cand), float(n_sel), tq)
    cnt_gt = count(lambda k, kb: k > thr)
    cnt_ge = count(lambda k, kb: k >= thr)
    j_scr[...] = jnp.full((tq, 1), 2 ** 30, I32)
    excess = jnp.where((cnt_ge > float(n_sel)) & (thr > INT_MIN), 1.0, 0.0)

    @pl.when(jnp.max(excess) > 0.0)
    def _():
        need = float(n_sel) - cnt_gt

        def body(i, p):
            cand = p + (jnp.int32(1) << (idx_bits - 1 - i))
            c = count(lambda k, kb: (k == thr) & (kb * tq + lane < cand))
            return jnp.where(c < need, cand, p)

        j_scr[...] = lax.fori_loop(0, idx_bits, body, jnp.zeros((tq, 1), I32))

    last_tie = j_scr[...]

    m_scr[...] = jnp.full(m_scr.shape, NEG_BIG, F32)
    l_scr[...] = jnp.zeros(l_scr.shape, F32)
    acc_scr[...] = jnp.zeros(acc_scr.shape, F32)

    def attend(kb, kind):
        off = pl.multiple_of(kb * tq, tq)
        key = key_scr[kb]
        sel = (key > thr) | ((key == thr) & (kb * tq + lane <= last_tie))
        if kind == "diag":
            sel = sel & causal
        for g in range(DSA_KV_HEADS):
            s = _dot_nt(qd_scr[g * DSA_GROUP * tq:(g + 1) * DSA_GROUP * tq, :],
                        k_ref[pl.ds(off, tq), g * 128:(g + 1) * 128]) * DSA_SCALE
            vd = v_ref[pl.ds(off, tq), g * 128:(g + 1) * 128]
            for hh in range(DSA_GROUP):
                h = g * DSA_GROUP + hh
                if kind == "diag":
                    bias = bias_ref[h, :, 128:256]
                elif kind == "prev":
                    bias = bias_ref[h, :, 0:128]
                else:
                    bias = rb_ref[N_BUCKETS - 1, h]
                sh = jnp.where(sel, s[hh * tq:(hh + 1) * tq, :] + bias, NEG_BIG)
                m_old = m_scr[h]
                m_new = jnp.maximum(m_old, jnp.max(sh, axis=-1, keepdims=True))
                alpha = jnp.exp(m_old - m_new)
                p = jnp.where(sel, jnp.exp(sh - m_new), 0.0)
                l_scr[h] = alpha * l_scr[h] + jnp.sum(p, axis=-1, keepdims=True)
                m_scr[h] = m_new
                pv = _dot(p.astype(BF16), vd)
                a = acc_scr[h // 2]
                mine = low64 if h % 2 == 0 else jnp.logical_not(low64)
                acc_scr[h // 2] = jnp.where(mine, alpha * a + pv, a)

    def far_body(kb, carry):
        attend(kb, "far")
        return carry

    lax.fori_loop(0, qi - 1, far_body, 0)

    @pl.when(qi >= 1)
    def _():
        attend(qi - 1, "prev")

    attend(qi, "diag")
    for p in range(DSA_HEADS // 2):
        inv = jnp.where(low64, 1.0 / l_scr[2 * p], 1.0 / l_scr[2 * p + 1])
        o_ref[:, 128 * p:128 * p + 128] = acc_scr[p] * inv


def _dsa_prompt(rel_bias, bias_tiles, qidx, tail, kidxdup, qdsa, kdup, vdup, batch, seq):
    tq = LANES
    assert MAX_DISTANCE <= tq, "keys two or more blocks back must all fall in the last bucket"
    nq = seq // tq
    n_sel = min(TOPK_MAX, seq // 4)
    idx_bits = max(1, (seq - 1).bit_length())
    kern = functools.partial(_dsa_prompt_kernel, n_sel=n_sel, idx_bits=idx_bits)
    qblk = lambda w: pl.BlockSpec((tq, w), lambda b, i: (b * nq + i, 0))
    seqblk = lambda w: pl.BlockSpec((seq, w), lambda b, i: (b, 0))
    return pl.pallas_call(
        kern,
        grid=(batch, nq),
        in_specs=[pl.BlockSpec(memory_space=pltpu.SMEM),
                  qblk(512), qblk(LANES), seqblk(LANES), qblk(512), seqblk(256), seqblk(256),
                  pl.BlockSpec((DSA_HEADS, tq, 2 * tq), lambda b, i: (0, 0, 0))],
        out_specs=qblk(512),
        out_shape=jax.ShapeDtypeStruct((batch * seq, 512), F32),
        scratch_shapes=[pltpu.VMEM((IDX_HEADS * tq, LANES), BF16), pltpu.VMEM((DSA_HEADS * tq, LANES), BF16),
                        pltpu.VMEM((IDX_HEADS, tq, LANES), F32), pltpu.VMEM((nq, tq, LANES), I32),
                        pltpu.VMEM((tq, 1), I32),
                        pltpu.VMEM((DSA_HEADS, tq, 1), F32), pltpu.VMEM((DSA_HEADS, tq, 1), F32),
                        pltpu.VMEM((DSA_HEADS // 2, tq, LANES), F32)],
        compiler_params=_cparams(("parallel", "arbitrary")),
    )(rel_bias, qidx, tail, kidxdup, qdsa, kdup, vdup, bias_tiles)


def _page_specs(n, width, pages_per_step, index_of):
    return [pl.BlockSpec((None, None, PAGE_SIZE, width), functools.partial(index_of, i)) for i in range(n)]


def _sample_score_kernel(pt_ref, q_ref, w_ref, *refs):
    pages, o_ref, kb_scr = refs[:-2], refs[-2], refs[-1]
    bi = pl.program_id(2)
    for i, pg in enumerate(pages):
        kb_scr[i * PAGE_SIZE:(i + 1) * PAGE_SIZE, :] = pg[...].astype(BF16)
    d = _dot_nt(q_ref[...], kb_scr[...])
    sc = jnp.sum(jnp.maximum(d, 0.0) * w_ref[...], axis=0, keepdims=True) * IDX_SCALE
    o_ref[pl.ds(bi, 1), :] = sc


def _sample_scores(page_table, qidx3, widx3, cache_idx, pps):
    dec, n_pages = page_table.shape
    n_chunks = n_pages // pps
    grp = 8
    page_of = lambda i, bo, c, bi, pt: (0, pt[bo * grp + bi, c * pps + i], 0, 0)
    grid_spec = pltpu.PrefetchScalarGridSpec(
        num_scalar_prefetch=1,
        grid=(dec // grp, n_chunks, grp),
        in_specs=[pl.BlockSpec((None, IDX_HEADS, IDX_DIM), lambda bo, c, bi, pt: (bo * grp + bi, 0, 0)),
                  pl.BlockSpec((None, IDX_HEADS, 1), lambda bo, c, bi, pt: (bo * grp + bi, 0, 0))]
                 + _page_specs(pps, IDX_DIM, pps, page_of),
        out_specs=pl.BlockSpec((grp, pps * PAGE_SIZE), lambda bo, c, bi, pt: (bo, c)),
        scratch_shapes=[pltpu.VMEM((pps * PAGE_SIZE, IDX_DIM), BF16)],
    )
    return pl.pallas_call(
        _sample_score_kernel, grid_spec=grid_spec,
        out_shape=jax.ShapeDtypeStruct((dec, n_pages * PAGE_SIZE), F32),
        compiler_params=_cparams(("arbitrary", "arbitrary", "arbitrary")),
    )(page_table, qidx3, widx3, *([cache_idx] * pps))


def _sample_select_kernel(sc_ref, qidx_ref, tail_ref, selp_ref, seln_ref, key_scr, *, n_sel, idx_bits):
    rows, past = sc_ref.shape
    nblk = past // LANES
    lane = lax.broadcasted_iota(I32, (rows, LANES), 1)
    tail = tail_ref[...]
    kidx2 = jnp.where(lane < 64, tail, pltpu.roll(tail, 64, 1))
    sc_new = jnp.zeros((rows, 1), F32)
    for h in range(IDX_HEADS):
        prod = qidx_ref[:, (h // 2) * 128:(h // 2 + 1) * 128].astype(F32) * kidx2.astype(BF16).astype(F32)
        mine = (lane < 64) if h % 2 == 0 else (lane >= 64)
        dot = jnp.sum(jnp.where(mine, prod, 0.0), axis=-1, keepdims=True)
        sc_new = sc_new + jnp.maximum(dot, 0.0) * tail[:, T_WIDX + h:T_WIDX + h + 1]
    key_new = _sort_key(sc_new * IDX_SCALE)

    for j in range(nblk):
        key_scr[j] = _sort_key(sc_ref[:, j * LANES:(j + 1) * LANES])

    def count(pred, pred_new):
        def body(j, c):
            return c + jnp.where(pred(key_scr[j], j), 1.0, 0.0)
        c = lax.fori_loop(0, nblk, body, jnp.zeros((rows, LANES), F32))
        return jnp.sum(c, axis=-1, keepdims=True) + jnp.where(pred_new, 1.0, 0.0)

    thr = _kth_key(lambda cand: count(lambda k, j: k >= cand, key_new >= cand), float(n_sel), rows)
    cnt_gt = count(lambda k, j: k > thr, key_new > thr)
    need = float(n_sel) - cnt_gt

    def body(i, p):
        cand = p + (jnp.int32(1) << (idx_bits - 1 - i))
        c = count(lambda k, j: (k == thr) & (j * LANES + lane < cand), (key_new == thr) & (past < cand))
        return jnp.where(c < need, cand, p)

    last_tie = lax.fori_loop(0, idx_bits, body, jnp.zeros((rows, 1), I32))

    for j in range(nblk):
        k = key_scr[j]
        sel = (k > thr) | ((k == thr) & (j * LANES + lane <= last_tie))
        selp_ref[:, j * LANES:(j + 1) * LANES] = jnp.where(sel, 1.0, 0.0)
    sel_new = (key_new > thr) | ((key_new == thr) & (past <= last_tie))
    seln_ref[...] = jnp.broadcast_to(jnp.where(sel_new, 1.0, 0.0), (rows, LANES))


def _sample_select(scores, qidx, tail):
    dec, past = scores.shape
    n_sel = min(TOPK_MAX, (past + 1) // 4)
    idx_bits = max(1, past.bit_length())
    rows = 32 if dec % 32 == 0 else dec
    kern = functools.partial(_sample_select_kernel, n_sel=n_sel, idx_bits=idx_bits)
    return pl.pallas_call(
        kern,
        grid=(dec // rows,),
        in_specs=[pl.BlockSpec((rows, past), lambda i: (i, 0)), pl.BlockSpec((rows, 512), lambda i: (i, 0)),
                  pl.BlockSpec((rows, LANES), lambda i: (i, 0))],
        out_specs=[pl.BlockSpec((rows, past), lambda i: (i, 0)), pl.BlockSpec((rows, LANES), lambda i: (i, 0))],
        out_shape=[jax.ShapeDtypeStruct((dec, past), F32), jax.ShapeDtypeStruct((dec, LANES), F32)],
        scratch_shapes=[pltpu.VMEM((past // LANES, rows, LANES), I32)],
        compiler_params=_cparams(("parallel",)),
    )(scores, qidx, tail)


def _mla_sample_kernel(pt_ref, qn_ref, qr_ref, rown_ref, wuk_ref, wuv_ref, gk_ref, *refs):
    pages, o_ref, lat_scr, kr_scr, m_scr, l_scr, acc_scr = refs[:-6], refs[-6], refs[-5], refs[-4], refs[-3], refs[-2], refs[-1]
    c = pl.program_id(1)
    n_chunks = pl.num_programs(1)
    heads = MLA_HEADS

    @pl.when(c == 0)
    def _():
        m_scr[...] = jnp.full(m_scr.shape, -jnp.inf, F32)
        l_scr[...] = jnp.zeros(l_scr.shape, F32)
        acc_scr[...] = jnp.zeros(acc_scr.shape, F32)

    qg = qn_ref[...] * gk_ref[...]
    qr = qr_ref[...]

    def update(latb, krb, n_valid):
        n = latb.shape[0]
        knt = _dot_nt(wuk_ref[...], latb)
        ms = jnp.mean((knt * knt).reshape(heads, MLA_NOPE, n), axis=1)
        num = jnp.sum((knt * qg).reshape(heads, MLA_NOPE, n), axis=1)
        s = (num * lax.rsqrt(ms + EPS) + _dot_nt(qr, krb)) * MLA_SCALE
        if n_valid < n:
            s = jnp.where(lax.broadcasted_iota(I32, s.shape, 1) < n_valid, s, -jnp.inf)
        m_old = m_scr[...]
        m_new = jnp.maximum(m_old, jnp.max(s, axis=-1, keepdims=True))
        alpha = jnp.exp(m_old - m_new)
        p = jnp.exp(s - m_new)
        l_scr[...] = alpha * l_scr[...] + jnp.sum(p, axis=-1, keepdims=True)
        m_scr[...] = m_new
        acc_scr[...] = alpha * acc_scr[...] + _dot(p.astype(BF16), latb)

    for i, pg in enumerate(pages):
        lat_scr[i * PAGE_SIZE:(i + 1) * PAGE_SIZE, :] = pg[:, 0:MLA_KV_RANK].astype(BF16)
        kr_scr[i * PAGE_SIZE:(i + 1) * PAGE_SIZE, :] = pg[:, MLA_KV_RANK:MLA_ROW].astype(BF16)
    update(lat_scr[...], kr_scr[...], lat_scr.shape[0])

    @pl.when(c == n_chunks - 1)
    def _():
        rn = jnp.broadcast_to(rown_ref[...], (PAGE_SIZE, MLA_ROW))
        update(rn[:, 0:MLA_KV_RANK].astype(BF16), rn[:, MLA_KV_RANK:MLA_ROW].astype(BF16), 1)
        o_lat = (acc_scr[...] / l_scr[...]).astype(BF16)
        full = _dot(o_lat, wuv_ref[...])
        hd = lax.broadcasted_iota(I32, full.shape, 1) // MLA_V
        own = hd == lax.broadcasted_iota(I32, full.shape, 0)
        o_ref[...] = jnp.sum(jnp.where(own, full, 0.0), axis=0, keepdims=True)


def _mla_sample(page_table, qn_col, qr, row_new, wuk_t, wuv, gk_col, cache_mla, pps):
    dec, n_pages = page_table.shape
    n_chunks = n_pages // pps
    page_of = lambda i, b, c, pt: (0, pt[b, c * pps + i], 0, 0)
    grid_spec = pltpu.PrefetchScalarGridSpec(
        num_scalar_prefetch=1,
        grid=(dec, n_chunks),
        in_specs=[pl.BlockSpec((None, MLA_HEADS * MLA_NOPE, 1), lambda b, c, pt: (b, 0, 0)),
                  pl.BlockSpec((None, MLA_HEADS, MLA_ROPE), lambda b, c, pt: (b, 0, 0)),
                  pl.BlockSpec((None, 1, MLA_ROW), lambda b, c, pt: (b, 0, 0)),
                  pl.BlockSpec(wuk_t.shape, lambda b, c, pt: (0, 0)),
                  pl.BlockSpec(wuv.shape, lambda b, c, pt: (0, 0)),
                  pl.BlockSpec(gk_col.shape, lambda b, c, pt: (0, 0))]
                 + _page_specs(pps, MLA_ROW, pps, page_of),
        out_specs=pl.BlockSpec((None, 1, MLA_HEADS * MLA_V), lambda b, c, pt: (b, 0, 0)),
        scratch_shapes=[pltpu.VMEM((pps * PAGE_SIZE, MLA_KV_RANK), BF16), pltpu.VMEM((pps * PAGE_SIZE, MLA_ROPE), BF16),
                        pltpu.VMEM((MLA_HEADS, 1), F32), pltpu.VMEM((MLA_HEADS, 1), F32),
                        pltpu.VMEM((MLA_HEADS, MLA_KV_RANK), F32)],
    )
    return pl.pallas_call(
        _mla_sample_kernel, grid_spec=grid_spec,
        out_shape=jax.ShapeDtypeStruct((dec, 1, MLA_HEADS * MLA_V), F32),
        compiler_params=_cparams(("parallel", "arbitrary")),
    )(page_table, qn_col, qr, row_new, wuk_t, wuv, gk_col, *([cache_mla] * pps))


def _dsa_sample_kernel(pt_ref, q_ref, kvn_ref, selp_ref, seln_ref, biasp_ref, biasn_ref, *refs):
    pages, o_ref, k_scr, v_scr, m_scr, l_scr, acc_scr = refs[:-6], refs[-6], refs[-5], refs[-4], refs[-3], refs[-2], refs[-1]
    c = pl.program_id(1)
    n_chunks = pl.num_programs(1)

    @pl.when(c == 0)
    def _():
        m_scr[...] = jnp.full(m_scr.shape, NEG_BIG, F32)
        l_scr[...] = jnp.zeros(l_scr.shape, F32)
        acc_scr[...] = jnp.zeros(acc_scr.shape, F32)

    q = q_ref[...]

    def update(s, sel, pv_of):
        sh = jnp.where(sel, s, NEG_BIG)
        m_old = m_scr[...]
        m_new = jnp.maximum(m_old, jnp.max(sh, axis=-1, keepdims=True))
        alpha = jnp.exp(m_old - m_new)
        p = jnp.where(sel, jnp.exp(sh - m_new), 0.0)
        l_scr[...] = alpha * l_scr[...] + jnp.sum(p, axis=-1, keepdims=True)
        m_scr[...] = m_new
        acc_scr[...] = alpha * acc_scr[...] + pv_of(p)

    for i, pg in enumerate(pages):
        k_scr[i * PAGE_SIZE:(i + 1) * PAGE_SIZE, :] = pg[:, 0:128].astype(BF16)
        v_scr[i * PAGE_SIZE:(i + 1) * PAGE_SIZE, :] = pg[:, 128:256].astype(BF16)
    s = _dot_nt(q.astype(BF16), k_scr[...]) * DSA_SCALE + biasp_ref[...]
    update(s, selp_ref[...] > 0.5, lambda p: _dot(p.astype(BF16), v_scr[...]))

    @pl.when(c == n_chunks - 1)
    def _():
        kvn = kvn_ref[...]
        kn = kvn[:, 0:128].astype(BF16).astype(F32)
        vn = kvn[:, 128:256].astype(BF16).astype(F32)
        s_new = jnp.sum(q.astype(BF16).astype(F32) * kn, axis=-1, keepdims=True) * DSA_SCALE + biasn_ref[...]
        update(s_new, seln_ref[:, 0:1] > 0.5, lambda p: p.astype(BF16).astype(F32) * vn)
        o = acc_scr[...] / l_scr[...]
        lane = lax.broadcasted_iota(I32, o.shape, 1)
        row = lax.broadcasted_iota(I32, o.shape, 0)
        own = (lane // DSA_HEAD_DIM) == (row // DSA_GROUP)
        o_ref[...] = jnp.where(own, o, 0.0)


def _dsa_sample(page_table, qs, kv_new, selp, seln, bias_past, bias_new, cache_kv4, pps):
    dec, n_pages = page_table.shape
    n_chunks = n_pages // pps
    chunk = pps * PAGE_SIZE
    page_of = lambda i, b, c, pt: (0, pt[b, c * pps + i], 0, 0)
    grid_spec = pltpu.PrefetchScalarGridSpec(
        num_scalar_prefetch=1,
        grid=(dec, n_chunks),
        in_specs=[pl.BlockSpec((None, DSA_HEADS, LANES), lambda b, c, pt: (b, 0, 0)),
                  pl.BlockSpec((None, 1, 256), lambda b, c, pt: (b, 0, 0)),
                  pl.BlockSpec((None, 1, chunk), lambda b, c, pt: (b, 0, c)),
                  pl.BlockSpec((None, 1, LANES), lambda b, c, pt: (b, 0, 0)),
                  pl.BlockSpec((DSA_HEADS, chunk), lambda b, c, pt: (0, c)),
                  pl.BlockSpec((DSA_HEADS, 1), lambda b, c, pt: (0, 0))]
                 + _page_specs(pps, 256, pps, page_of),
        out_specs=pl.BlockSpec((None, DSA_HEADS, LANES), lambda b, c, pt: (b, 0, 0)),
        scratch_shapes=[pltpu.VMEM((chunk, 128), BF16), pltpu.VMEM((chunk, 128), BF16),
                        pltpu.VMEM((DSA_HEADS, 1), F32), pltpu.VMEM((DSA_HEADS, 1), F32),
                        pltpu.VMEM((DSA_HEADS, LANES), F32)],
    )
    return pl.pallas_call(
        _dsa_sample_kernel, grid_spec=grid_spec,
        out_shape=jax.ShapeDtypeStruct((dec, DSA_HEADS, LANES), F32),
        compiler_params=_cparams(("parallel", "arbitrary")),
    )(page_table, qs, kv_new, selp, seln, bias_past, bias_new, *([cache_kv4] * pps))


def _out_kernel(x_ref, oa_ref, ob_ref, gm_ref, sf_ref, scf_ref, gf_ref, gout_ref, gffn_ref,
                wout_ref, wg_ref, wu_ref, wdown_ref, y_ref, *, ff_chunk):
    def rms(v, g):
        return v * lax.rsqrt(jnp.mean(v * v, axis=-1, keepdims=True) + EPS) * g

    half = oa_ref.shape[1]
    na = rms(oa_ref[...], gout_ref[:, 0:half]).astype(BF16)
    nb = rms(ob_ref[...], gout_ref[:, half:2 * half]).astype(BF16)
    mix = _dot(na, wout_ref[0:half, :]) + _dot(nb, wout_ref[half:2 * half, :])
    x1 = x_ref[...] + gm_ref[...] * mix
    hb = (rms(x1, gffn_ref[...]) * (1.0 + scf_ref[...]) + sf_ref[...]).astype(BF16)
    acc = jnp.zeros(x1.shape, F32)
    for j in range(D_FF // ff_chunk):
        g = _dot(hb, wg_ref[:, j * ff_chunk:(j + 1) * ff_chunk])
        u = _dot(hb, wu_ref[:, j * ff_chunk:(j + 1) * ff_chunk])
        a = (g / (1.0 + jnp.exp(-g))) * u
        acc = acc + _dot(a.astype(BF16), wdown_ref[j * ff_chunk:(j + 1) * ff_chunk, :])
    y_ref[...] = x1 + gf_ref[...] * acc


def _out_stage(x2d, oa, ob, gate_m, shift_f, scale_f, gate_f, wp, tm, per_token, tiles_per_seq):
    t_total = x2d.shape[0]
    if per_token:
        mod_spec = pl.BlockSpec((tm, D_MODEL), lambda i: (i, 0))
    else:
        mod_spec = pl.BlockSpec((None, 1, D_MODEL), lambda i: (i // tiles_per_seq, 0, 0))
    consts = [wp["g_out"], wp["g_ffn"], wp["w_out"], wp["w_gate"], wp["w_up"], wp["w_down"]]
    tok = lambda w: pl.BlockSpec((tm, w), lambda i: (i, 0))
    return pl.pallas_call(
        functools.partial(_out_kernel, ff_chunk=256),
        grid=(t_total // tm,),
        in_specs=[tok(D_MODEL), tok(512), tok(512), mod_spec, mod_spec, mod_spec, mod_spec]
                 + [_full(c.shape) for c in consts],
        out_specs=tok(D_MODEL),
        out_shape=jax.ShapeDtypeStruct((t_total, D_MODEL), F32),
        compiler_params=_cparams(("parallel",)),
    )(x2d, oa, ob, gate_m, shift_f, scale_f, gate_f, *consts)


def _prep_weights(w_in, g_norm_mix, g_norm_ffn, g_q_lat, w_uq, g_kv_lat, w_ukv, g_mla_q_nope, g_mla_q_rope,
                  g_mla_k_nope, g_mla_k_rope, g_dsa_q, g_dsa_k, g_out, w_out, w_ffn_in, w_ffn_out):
    splits = np.cumsum([MLA_Q_RANK, MLA_KV_RANK, MLA_ROPE, 512, 128, 128, 512, IDX_DIM, IDX_HEADS])
    s = [0] + splits.tolist()
    col = lambda i: w_in[:, s[i]:s[i + 1]]
    pad = jnp.zeros((D_MODEL, C_END - C_TAIL - IDX_DIM - MLA_ROPE - IDX_HEADS), w_in.dtype)
    w_in_r = jnp.concatenate([col(0), col(1), col(3), col(4), col(5), col(6), col(7), col(2), col(8), pad], axis=1)
    zq = jnp.zeros((MLA_Q_RANK, MLA_HEADS, LANES - MLA_NOPE - MLA_ROPE), w_uq.dtype)
    w_uq_cat = jnp.concatenate([w_uq, zq], axis=2).reshape(MLA_Q_RANK, MLA_HEADS * LANES)
    zk = jnp.zeros((MLA_KV_RANK, MLA_HEADS, LANES - MLA_NOPE), w_ukv.dtype)
    w_uk_cat = jnp.concatenate([w_ukv[:, :, :MLA_NOPE], zk], axis=2).reshape(MLA_KV_RANK, MLA_HEADS * LANES)
    w_uv = w_ukv[:, :, MLA_NOPE:].reshape(MLA_KV_RANK, MLA_HEADS * MLA_V)
    w_uk_t = w_ukv[:, :, :MLA_NOPE].reshape(MLA_KV_RANK, MLA_HEADS * MLA_NOPE).T
    z32 = jnp.zeros((LANES - MLA_NOPE - MLA_ROPE,), F32)
    z64 = jnp.zeros((LANES - MLA_NOPE,), F32)
    g_qcat = jnp.tile(jnp.concatenate([g_mla_q_nope, g_mla_q_rope, z32]), MLA_HEADS)
    g_kcat = jnp.tile(jnp.concatenate([g_mla_k_nope, z64]), MLA_HEADS)
    g_tail = jnp.concatenate([jnp.zeros((T_ROPE,), F32), g_mla_k_rope, z32])
    row = lambda v: v.reshape(1, -1).astype(F32)
    return {
        "g_mix": row(g_norm_mix), "g_ffn": row(g_norm_ffn), "w_in": w_in_r.astype(BF16),
        "g_qlat": row(g_q_lat), "w_uq": w_uq_cat.astype(BF16), "g_qcat": row(g_qcat),
        "g_kvlat": row(g_kv_lat), "w_ukv": jnp.concatenate([w_uk_cat, w_uv], axis=1).astype(BF16),
        "g_kcat": row(g_kcat), "g_tail": row(g_tail),
        "g_q": row(jnp.tile(g_dsa_q, DSA_HEADS)), "g_k": row(jnp.tile(g_dsa_k, DSA_KV_HEADS)),
        "m_cat": _group_matrix(256, [(0, 64), (64, 96), (128, 192), (192, 224)]),
        "m_64": _group_matrix(256, [(0, 64), (64, 128), (128, 192), (192, 256)]),
        "w_uk_t": w_uk_t.astype(BF16), "w_uv": w_uv.astype(BF16), "g_k_nope": g_mla_k_nope,
        "g_out": row(g_out), "w_out": w_out.astype(BF16),
        "w_gate": w_ffn_in[:, :D_FF].astype(BF16), "w_up": w_ffn_in[:, D_FF:].astype(BF16),
        "w_down": w_ffn_out.astype(BF16),
    }


def _rope_tables(pos):
    freq = ROPE_THETA ** (-jnp.arange(HALF_ROPE, dtype=F32) / HALF_ROPE)
    ang = pos.astype(F32)[:, None] * freq[None, :]
    cos, sin = jnp.cos(ang), jnp.sin(ang)
    n = pos.shape[0]
    ones = jnp.ones((n, T_ROPE), F32)
    zeros = jnp.zeros((n, T_ROPE), F32)
    tail1 = jnp.ones((n, LANES - T_ROPE - MLA_ROPE), F32)
    tail0 = jnp.zeros((n, LANES - T_ROPE - MLA_ROPE), F32)
    return (jnp.concatenate([ones, cos, cos, tail1], axis=1),
            jnp.concatenate([zeros, -sin, sin, tail0], axis=1))


def _pick_tile(n, prefs):
    for t in prefs:
        if n % t == 0:
            return t
    return n


def kernel(x_prompt, x_sample, c_prompt, c_sample, cache_mla, cache_kv, cache_idx, page_table, rel_bias, w_ada, b_ada, g_norm_mix, g_norm_ffn, w_in, g_q_lat, w_uq, g_kv_lat, w_ukv, g_mla_q_nope, g_mla_q_rope, g_mla_k_nope, g_mla_k_rope, g_dsa_q, g_dsa_k, g_out, w_out, w_ffn_in, w_ffn_out):
    assert w_ada.shape[0] == 1 and x_sample.shape[1] == 1, "one layer, one new token per sample"
    batch, seq, _ = x_prompt.shape
    dec = x_sample.shape[0]
    n_pages = page_table.shape[1]
    past = n_pages * PAGE_SIZE
    n_pool = cache_mla.shape[1]
    wp = _prep_weights(w_in[0], g_norm_mix[0], g_norm_ffn[0], g_q_lat[0], w_uq[0], g_kv_lat[0], w_ukv[0],
                       g_mla_q_nope[0], g_mla_q_rope[0], g_mla_k_nope[0], g_mla_k_rope[0], g_dsa_q[0], g_dsa_k[0],
                       g_out[0], w_out[0], w_ffn_in[0], w_ffn_out[0])

    mod = _ada(jnp.concatenate([c_prompt, c_sample], axis=0), w_ada[0], b_ada[0])
    mods_p = [m.reshape(batch, 1, D_MODEL) for m in jnp.split(mod[:batch], 6, axis=-1)]
    mods_s = jnp.split(mod[batch:], 6, axis=-1)

    tm = _pick_tile(seq, (512, 256, 128))
    xp2 = x_prompt.reshape(batch * seq, D_MODEL)
    cos_p, sin_p = _rope_tables(jnp.arange(seq))
    pin = _in_stage(xp2, mods_p[1], mods_p[0], cos_p, sin_p, wp, tm, per_token=False)
    tq = _pick_tile(seq, (256, 128))
    o_mla_p = _mla_prompt(pin["qcat"], pin["kcat"], pin["vmla"], batch, seq, tq)
    bias_tiles = _bias_table(rel_bias, LANES, 2 * LANES, 1, -1, LANES)
    o_dsa_p = _dsa_prompt(rel_bias, bias_tiles, pin["qidx"], pin["tail"], pin["kidxdup"], pin["qdsa"],
                          pin["kdup"], pin["vdup"], batch, seq)
    y_p = _out_stage(xp2, o_mla_p, o_dsa_p, mods_p[2], mods_p[3], mods_p[4], mods_p[5], wp, tm,
                     per_token=False, tiles_per_seq=seq // tm)

    xs2 = x_sample.reshape(dec, D_MODEL)
    cos_s, sin_s = _rope_tables(jnp.full((1,), past))
    sin_ = _in_stage(xs2, mods_s[1], mods_s[0], cos_s, sin_s, wp, dec, per_token=True)
    pps = _pick_tile(n_pages, (16, 8, 4, 2))
    qidx3 = sin_["qidx"].reshape(dec, IDX_HEADS, IDX_DIM)
    widx3 = sin_["tail"][:, T_WIDX:T_WIDX + IDX_HEADS].reshape(dec, IDX_HEADS, 1)
    scores = _sample_scores(page_table, qidx3, widx3, cache_idx, pps)
    selp, seln = _sample_select(scores, sin_["qidx"], sin_["tail"])
    qc = sin_["qcat"].reshape(dec, MLA_HEADS, LANES)
    qn_col = qc[:, :, 0:MLA_NOPE].astype(F32).reshape(dec, MLA_HEADS * MLA_NOPE, 1)
    qr = qc[:, :, T_ROPE:T_ROPE + MLA_ROPE]
    gk_col = jnp.tile(wp["g_k_nope"], MLA_HEADS).reshape(MLA_HEADS * MLA_NOPE, 1)
    o_mla_s = _mla_sample(page_table, qn_col, qr, sin_["row"].reshape(dec, 1, MLA_ROW), wp["w_uk_t"], wp["w_uv"],
                          gk_col, cache_mla, pps).reshape(dec, MLA_HEADS * MLA_V)
    qd = sin_["qdsa"].astype(F32).reshape(dec, DSA_HEADS, DSA_HEAD_DIM)
    on_c = (jnp.arange(DSA_KV_HEADS)[None, :] == (jnp.arange(DSA_HEADS) // DSA_GROUP)[:, None]).astype(F32)
    qd2 = (qd[:, :, None, :] * on_c[None, :, :, None]).reshape(dec, DSA_HEADS, LANES)
    bias_past = _bias_table(rel_bias, 1, past, 0, -1, past).reshape(DSA_HEADS, past)
    bias_new = _bias_table(rel_bias, 1, LANES, 0, 0, 0)[:, 0, 0:1]
    o8 = _dsa_sample(page_table, qd2, sin_["kv"].reshape(dec, 1, 256), selp.reshape(dec, 1, past),
                     seln.reshape(dec, 1, LANES), bias_past, bias_new,
                     cache_kv.reshape(1, n_pool, PAGE_SIZE, 256), pps)
    o_dsa_s = o8.reshape(dec, DSA_HEADS, DSA_KV_HEADS, DSA_HEAD_DIM).sum(axis=2).reshape(dec, DSA_HEADS * DSA_HEAD_DIM)
    y_s = _out_stage(xs2, o_mla_s, o_dsa_s, mods_s[2], mods_s[3], mods_s[4], mods_s[5], wp, dec,
                     per_token=True, tiles_per_seq=1)

    return (y_p.reshape(batch, seq, D_MODEL), y_s.reshape(dec, 1, D_MODEL),
            pin["row"].reshape(1, batch, seq, MLA_ROW),
            pin["kv"].reshape(1, batch, seq, 2, DSA_KV_HEADS, DSA_HEAD_DIM),
            pin["kidx"].reshape(1, batch, seq, IDX_DIM),
            sin_["row"].reshape(1, dec, 1, MLA_ROW),
            sin_["kv"].reshape(1, dec, 1, 2, DSA_KV_HEADS, DSA_HEAD_DIM),
            sin_["kidx"].reshape(1, dec, 1, IDX_DIM))
```

```python
import functools
import math

import numpy as np
import jax
import jax.numpy as jnp
from jax import lax
from jax.experimental import pallas as pl
from jax.experimental.pallas import tpu as pltpu

D_MODEL = 1024
PAGE_SIZE = 128
MLA_HEADS = 8
MLA_NOPE = 64
MLA_ROPE = 32
MLA_V = 64
MLA_Q_RANK = 384
MLA_KV_RANK = 256
MLA_ROW = MLA_KV_RANK + MLA_ROPE
DSA_HEADS = 8
DSA_KV_HEADS = 2
DSA_HEAD_DIM = 64
DSA_GROUP = DSA_HEADS // DSA_KV_HEADS
IDX_HEADS = 8
IDX_DIM = 64
TOPK_MAX = 256
N_BUCKETS = 32
MAX_DISTANCE = 128
ROPE_THETA = 10000.0
D_FF = ((8 * D_MODEL + 3 * 256 - 1) // (3 * 256)) * 256
EPS = 1e-6
MLA_SCALE = (MLA_NOPE + MLA_ROPE) ** -0.5
DSA_SCALE = DSA_HEAD_DIM ** -0.5
IDX_SCALE = (IDX_DIM * IDX_HEADS) ** -0.5

LANES = 128
VMEM_LIMIT = 56 * 1024 * 1024

F32 = jnp.float32
BF16 = jnp.bfloat16
I32 = jnp.int32
NEG_BIG = -1e30
INT_MIN = -(2 ** 31)

C_QLAT = 0
C_KVLAT = C_QLAT + MLA_Q_RANK
C_Q = C_KVLAT + MLA_KV_RANK
C_K = C_Q + DSA_HEADS * DSA_HEAD_DIM
C_V = C_K + DSA_KV_HEADS * DSA_HEAD_DIM
C_QIDX = C_V + DSA_KV_HEADS * DSA_HEAD_DIM
C_TAIL = C_QIDX + IDX_HEADS * IDX_DIM
C_END = C_TAIL + LANES
T_ROPE = IDX_DIM
T_WIDX = IDX_DIM + MLA_ROPE
HALF_ROPE = MLA_ROPE // 2
KB = 128
VT_PAD = 16
VT_ROWS = DSA_HEAD_DIM + VT_PAD


def _dot(a, b):
    return jnp.dot(a, b, preferred_element_type=F32)


def _dot_nt(a, b):
    return lax.dot_general(a, b, (((1,), (1,)), ((), ())), preferred_element_type=F32)


def _split(a):
    hi = a.astype(BF16)
    lo = (a - hi.astype(F32)).astype(BF16)
    return hi, lo


def _dot3(a, b):
    ah, al = _split(a)
    bh, bl = _split(b)
    return _dot(ah, bh) + (_dot(al, bh) + _dot(ah, bl))


def _cparams(sem):
    return pltpu.CompilerParams(dimension_semantics=sem, vmem_limit_bytes=VMEM_LIMIT)


def _full(shape):
    n = len(shape)
    return pl.BlockSpec(shape, lambda *a, _n=n: (0,) * _n, pipeline_mode=pl.Buffered(1))


def _ada_kernel(c_ref, w_ref, b_ref, o_ref):
    c = c_ref[...]
    s = c / (1.0 + jnp.exp(-c))
    o_ref[...] = _dot3(s, w_ref[...]) + b_ref[...]


def _ada(c, w, b):
    n = c.shape[0]
    return pl.pallas_call(
        _ada_kernel,
        grid=(6,),
        in_specs=[pl.BlockSpec((n, D_MODEL), lambda j: (0, 0)),
                  pl.BlockSpec((D_MODEL, D_MODEL), lambda j: (0, j)),
                  pl.BlockSpec((1, D_MODEL), lambda j: (0, j))],
        out_specs=pl.BlockSpec((n, D_MODEL), lambda j: (0, j)),
        out_shape=jax.ShapeDtypeStruct((n, 6 * D_MODEL), F32),
        compiler_params=_cparams(("arbitrary",)),
    )(c, w, b.reshape(1, -1))


def _group_matrix(width, groups):
    g = np.zeros((width, width), np.float32)
    for lo, hi in groups:
        g[lo:hi, lo:hi] = 1.0 / (hi - lo)
    return jnp.asarray(g, BF16)


def _in_kernel(x_ref, sc_ref, sh_ref, cos_ref, sin_ref, gmix_ref, win_ref, gqlat_ref, wuq_ref, gqcat_ref,
               gkvlat_ref, wukv_ref, gkcat_ref, gtail_ref, gq_ref, gk_ref, mcat_ref, m64_ref,
               qcat_o, kcat_o, vmla_o, row_o, kv_o, kidx_o, tail_o, kidxdup_o, qdsa_o, kdup_o, qidx_o, vt_o):
    tm = x_ref.shape[0]
    x = x_ref[...]
    h = x * lax.rsqrt(jnp.mean(x * x, axis=-1, keepdims=True) + EPS) * gmix_ref[...]
    hb = (h * (1.0 + sc_ref[...]) + sh_ref[...]).astype(BF16)

    def proj(lo, hi):
        return _dot(hb, win_ref[:, lo:hi])

    cos = cos_ref[...]
    sin = sin_ref[...]
    lane = lax.broadcasted_iota(I32, (tm, LANES), 1)
    first_half = lane < T_ROPE + HALF_ROPE
    rope_lanes = (lane >= T_ROPE) & (lane < T_ROPE + MLA_ROPE)
    low64 = lane < 64

    def rope(xh):
        rot = jnp.where(first_half, pltpu.roll(xh, LANES - HALF_ROPE, 1), pltpu.roll(xh, HALF_ROPE, 1))
        return xh * cos + rot * sin

    def gnorm(blk, m_ref):
        ms = _dot((blk * blk).astype(BF16), m_ref[...])
        return blk * lax.rsqrt(ms + EPS)

    ql = proj(C_QLAT, C_KVLAT)
    qln = ql * lax.rsqrt(jnp.mean(ql * ql, axis=-1, keepdims=True) + EPS) * gqlat_ref[...]
    qm = _dot(qln.astype(BF16), wuq_ref[...])
    for j in range(4):
        y = gnorm(qm[:, 256 * j:256 * j + 256], mcat_ref) * gqcat_ref[:, 256 * j:256 * j + 256]
        for t in range(2):
            qcat_o[:, 256 * j + 128 * t:256 * j + 128 * t + 128] = rope(y[:, 128 * t:128 * t + 128]).astype(BF16)

    tail = proj(C_TAIL, C_END)
    ssq = jnp.sum(jnp.where(rope_lanes, tail * tail, 0.0), axis=-1, keepdims=True) * (1.0 / MLA_ROPE)
    tn = jnp.where(rope_lanes, tail * lax.rsqrt(ssq + EPS) * gtail_ref[...], tail)
    tr = rope(tn)
    kidx_o[...] = tail[:, 0:IDX_DIM]
    tail_o[...] = tail
    kidxdup_o[...] = jnp.where(low64, tail, pltpu.roll(tail, 64, 1)).astype(BF16)
    krope = jnp.where(rope_lanes, tr, 0.0)

    kvl = proj(C_KVLAT, C_Q)
    lat = kvl * lax.rsqrt(jnp.mean(kvl * kvl, axis=-1, keepdims=True) + EPS) * gkvlat_ref[...]
    row_o[:, 0:MLA_KV_RANK] = lat
    row_o[:, MLA_KV_RANK:MLA_ROW] = tr[:, T_ROPE:T_ROPE + MLA_ROPE]
    latb = lat.astype(BF16)
    kx = _dot(latb, wukv_ref[:, 0:1024])
    for j in range(4):
        y = gnorm(kx[:, 256 * j:256 * j + 256], mcat_ref) * gkcat_ref[:, 256 * j:256 * j + 256]
        for t in range(2):
            kcat_o[:, 256 * j + 128 * t:256 * j + 128 * t + 128] = (y[:, 128 * t:128 * t + 128] + krope).astype(BF16)
    vmla_o[...] = _dot(latb, wukv_ref[:, 1024:1536]).astype(BF16)

    q = proj(C_Q, C_K)
    for j in range(2):
        qdsa_o[:, 256 * j:256 * j + 256] = (gnorm(q[:, 256 * j:256 * j + 256], m64_ref)
                                            * gq_ref[:, 256 * j:256 * j + 256] * DSA_SCALE).astype(BF16)
    k = proj(C_K, C_V)
    ms = _dot((k * k).astype(BF16), m64_ref[0:128, 0:128])
    kn = k * lax.rsqrt(ms + EPS) * gk_ref[...]
    v = proj(C_V, C_QIDX)
    kv_o[:, 0:128] = kn
    kv_o[:, 128:256] = v
    kr = pltpu.roll(kn, 64, 1)
    kdup_o[:, 0:128] = jnp.where(low64, kn, kr).astype(BF16)
    kdup_o[:, 128:256] = jnp.where(low64, kr, kn).astype(BF16)
    ones = jnp.ones((VT_PAD, LANES), BF16)
    for i in range(tm // LANES):
        vt = v[i * LANES:(i + 1) * LANES, :].T.astype(BF16)
        for c in range(DSA_KV_HEADS):
            vt_o[i, c * VT_ROWS:c * VT_ROWS + DSA_HEAD_DIM, :] = vt[c * DSA_HEAD_DIM:(c + 1) * DSA_HEAD_DIM, :]
            vt_o[i, c * VT_ROWS + DSA_HEAD_DIM:(c + 1) * VT_ROWS, :] = ones
    qidx_o[...] = proj(C_QIDX, C_TAIL).astype(BF16)


_IN_OUT_WIDTHS = (("qcat", 1024, BF16), ("kcat", 1024, BF16), ("vmla", 512, BF16), ("row", MLA_ROW, F32),
                  ("kv", 256, F32), ("kidx", IDX_DIM, F32), ("tail", LANES, F32), ("kidxdup", LANES, BF16),
                  ("qdsa", 512, BF16), ("kdup", 256, BF16), ("qidx", 512, BF16))


def _in_stage(x2d, scale, shift, cos, sin, wp, tm, per_token):
    t_total = x2d.shape[0]
    n_tiles = t_total // tm
    if per_token:
        mod_spec = pl.BlockSpec((tm, D_MODEL), lambda i: (i, 0))
        tab_spec = pl.BlockSpec((1, LANES), lambda i: (0, 0))
    else:
        tiles_per_seq = cos.shape[0] // tm
        mod_spec = pl.BlockSpec((None, 1, D_MODEL), lambda i: (i // tiles_per_seq, 0, 0))
        tab_spec = pl.BlockSpec((tm, LANES), lambda i: (i % tiles_per_seq, 0))
    consts = [wp["g_mix"], wp["w_in"], wp["g_qlat"], wp["w_uq"], wp["g_qcat"], wp["g_kvlat"], wp["w_ukv"],
              wp["g_kcat"], wp["g_tail"], wp["g_q"], wp["g_k"], wp["m_cat"], wp["m_64"]]
    in_specs = ([pl.BlockSpec((tm, D_MODEL), lambda i: (i, 0)), mod_spec, mod_spec, tab_spec, tab_spec]
                + [_full(c.shape) for c in consts])
    out_specs = [pl.BlockSpec((tm, w), lambda i: (i, 0)) for _, w, _ in _IN_OUT_WIDTHS]
    out_shape = [jax.ShapeDtypeStruct((t_total, w), dt) for _, w, dt in _IN_OUT_WIDTHS]
    out_specs.append(pl.BlockSpec((tm // LANES, DSA_KV_HEADS * VT_ROWS, LANES), lambda i: (i, 0, 0)))
    out_shape.append(jax.ShapeDtypeStruct((t_total // LANES, DSA_KV_HEADS * VT_ROWS, LANES), BF16))
    outs = pl.pallas_call(
        _in_kernel, grid=(n_tiles,), in_specs=in_specs, out_specs=out_specs, out_shape=out_shape,
        compiler_params=_cparams(("parallel",)),
    )(x2d, scale, shift, cos, sin, *consts)
    res = {name: o for (name, _, _), o in zip(_IN_OUT_WIDTHS, outs)}
    res["vt"] = outs[-1]
    return res


def _mla_prompt_kernel(q_ref, k_ref, v_ref, o_ref, m_scr, l_scr, acc_scr):
    tq = q_ref.shape[0]
    qi = pl.program_id(1)
    m_scr[...] = jnp.full(m_scr.shape, -jnp.inf, F32)
    l_scr[...] = jnp.zeros(l_scr.shape, F32)
    acc_scr[...] = jnp.zeros(acc_scr.shape, F32)
    lane = lax.broadcasted_iota(I32, (tq, LANES), 1)
    low64 = lane < 64
    causal = lax.broadcasted_iota(I32, (tq, tq), 1) <= lax.broadcasted_iota(I32, (tq, tq), 0)

    def step(kb, masked):
        off = pl.multiple_of(kb * tq, tq)
        for h in range(MLA_HEADS):
            s = _dot_nt(q_ref[:, h * 128:(h + 1) * 128], k_ref[pl.ds(off, tq), h * 128:(h + 1) * 128]) * MLA_SCALE
            if masked:
                s = jnp.where(causal, s, -jnp.inf)
            m_old = m_scr[h]
            m_new = jnp.maximum(m_old, jnp.max(s, axis=-1, keepdims=True))
            alpha = jnp.exp(m_old - m_new)
            p = jnp.exp(s - m_new)
            l_scr[h] = alpha * l_scr[h] + jnp.sum(p, axis=-1, keepdims=True)
            m_scr[h] = m_new
            pv = _dot(p.astype(BF16), v_ref[pl.ds(off, tq), (h // 2) * 128:(h // 2 + 1) * 128])
            a = acc_scr[h // 2]
            mine = low64 if h % 2 == 0 else jnp.logical_not(low64)
            acc_scr[h // 2] = jnp.where(mine, alpha * a + pv, a)

    def body(kb, carry):
        step(kb, False)
        return carry

    lax.fori_loop(0, qi, body, 0)
    step(qi, True)
    for p in range(MLA_HEADS // 2):
        inv = jnp.where(low64, 1.0 / l_scr[2 * p], 1.0 / l_scr[2 * p + 1])
        o_ref[:, 128 * p:128 * p + 128] = acc_scr[p] * inv


def _mla_prompt(qcat, kcat, vmla, batch, seq, tq):
    nq = seq // tq
    return pl.pallas_call(
        _mla_prompt_kernel,
        grid=(batch, nq),
        in_specs=[pl.BlockSpec((tq, 1024), lambda b, i: (b * nq + i, 0)),
                  pl.BlockSpec((seq, 1024), lambda b, i: (b, 0)),
                  pl.BlockSpec((seq, 512), lambda b, i: (b, 0))],
        out_specs=pl.BlockSpec((tq, 512), lambda b, i: (b * nq + i, 0)),
        out_shape=jax.ShapeDtypeStruct((batch * seq, 512), F32),
        scratch_shapes=[pltpu.VMEM((MLA_HEADS, tq, 1), F32), pltpu.VMEM((MLA_HEADS, tq, 1), F32),
                        pltpu.VMEM((MLA_HEADS // 2, tq, LANES), F32)],
        compiler_params=_cparams(("parallel", "arbitrary")),
    )(qcat, kcat, vmla)


def _sort_key(score):
    bits = lax.bitcast_convert_type(score + 0.0, I32)
    return bits ^ ((bits >> 31) & 0x7FFFFFFF)


def _kth_key(count_ge, n_sel, shape):
    def body(i, t):
        cand = t + (jnp.int32(1) << (31 - i))
        return jnp.where(count_ge(cand) >= n_sel, cand, t)

    return lax.fori_loop(0, 32, body, jnp.full(shape, INT_MIN, I32))


def _bucket(dist):
    max_exact = N_BUCKETS // 2
    d = jnp.maximum(dist, 0)
    log_ratio = jnp.log(jnp.maximum(d, max_exact).astype(F32) / max_exact) / math.log(MAX_DISTANCE / max_exact)
    large = jnp.minimum(max_exact + (log_ratio * (N_BUCKETS - max_exact)).astype(I32), N_BUCKETS - 1)
    return jnp.where(d < max_exact, d, large)


def _bias_kernel(rb_ref, o_ref, *, a, b, c):
    rows, cols = o_ref.shape[1], o_ref.shape[2]
    dist = (a * lax.broadcasted_iota(I32, (rows, cols), 0) + b * lax.broadcasted_iota(I32, (rows, cols), 1) + c)
    bucket = _bucket(dist)
    for h in range(DSA_HEADS):
        acc = jnp.zeros((rows, cols), F32)
        for n in range(N_BUCKETS):
            acc = jnp.where(bucket == n, rb_ref[n, h], acc)
        o_ref[h] = acc


def _bias_table(rel_bias, rows, cols, a, b, c):
    return pl.pallas_call(
        functools.partial(_bias_kernel, a=a, b=b, c=c),
        in_specs=[pl.BlockSpec(memory_space=pltpu.SMEM)],
        out_specs=pl.BlockSpec(memory_space=pltpu.VMEM),
        out_shape=jax.ShapeDtypeStruct((DSA_HEADS, rows, cols), F32),
    )(rel_bias)


def _colreduce(x, op):
    return op(x.reshape(x.shape[0] // 8, 8, x.shape[1]), axis=0)


def _dsa_prompt_kernel(rb_ref, qidx_ref, tail_ref, kidx_ref, q_ref, k_ref, vt_ref, bias_ref, o_ref,
                       qs_scr, qd_scr, key_scr, sb_scr, tie_scr, m_scr, p_scr, acc_scr, *, n_sel, idx_bits):
    tq = q_ref.shape[0]
    kq = tq // KB
    qi = pl.program_id(1)
    first = kq * qi
    krow = lax.broadcasted_iota(I32, (KB, tq), 0)
    qcol = lax.broadcasted_iota(I32, (KB, tq), 1)
    low64 = lax.broadcasted_iota(I32, (tq, LANES), 1) < 64

    for h in range(IDX_HEADS):
        mine = low64 if h % 2 == 0 else jnp.logical_not(low64)
        zero = jnp.zeros((tq, LANES), BF16)
        qs_scr[h * tq:(h + 1) * tq, :] = jnp.where(mine, qidx_ref[:, (h // 2) * 128:(h // 2 + 1) * 128], zero)
        qd_scr[h * tq:(h + 1) * tq, :] = jnp.where(mine, q_ref[:, (h // 2) * 128:(h // 2 + 1) * 128], zero)
    tail_t = tail_ref[...].T
    w_rows = [tail_t[T_WIDX + h:T_WIDX + h + 1, :] for h in range(IDX_HEADS)]

    def in_tile(step):
        for t in range(kq):
            step(first + t, t)

    def score_block(kb, t):
        d = _dot_nt(kidx_ref[kb], qs_scr[...])
        sc = jnp.zeros((KB, tq), F32)
        for h in range(IDX_HEADS):
            sc = sc + jnp.maximum(d[:, h * tq:(h + 1) * tq], 0.0) * w_rows[h]
        sc = sc * IDX_SCALE
        if t is not None:
            sc = jnp.where(t * KB + krow <= qcol, sc, -jnp.inf)
        key_scr[kb] = _sort_key(sc)

    def score_body(kb, carry):
        score_block(kb, None)
        return carry

    lax.fori_loop(0, first, score_body, 0)
    in_tile(score_block)
    n_blocks = first + kq

    def count(pred):
        def body(kb, c):
            return c + jnp.where(pred(key_scr[kb], kb), 1.0, 0.0)
        c = lax.fori_loop(0, n_blocks, body, jnp.zeros((KB, tq), F32))
        return jnp.sum(_colreduce(c, jnp.sum), axis=0, keepdims=True)

    thr = _kth_key(lambda cand: count(lambda k, kb: k >= cand), float(n_sel), (1, tq))
    cnt_gt = count(lambda k, kb: k > thr)
    cnt_ge = count(lambda k, kb: k >= thr)
    tie_scr[...] = jnp.full((1, tq), 2 ** 30, I32)
    excess = jnp.where((cnt_ge > float(n_sel)) & (thr > INT_MIN), 1.0, 0.0)

    @pl.when(jnp.max(excess) > 0.0)
    def _():
        need = float(n_sel) - cnt_gt

        def body(i, p):
            cand = p + (jnp.int32(1) << (idx_bits - 1 - i))
            c = count(lambda k, kb: (k == thr) & (kb * KB + krow < cand))
            return jnp.where(c < need, cand, p)

        tie_scr[...] = lax.fori_loop(0, idx_bits, body, jnp.zeros((1, tq), I32))

    last_tie = tie_scr[...]

    far_bias = [rb_ref[N_BUCKETS - 1, h] for h in range(DSA_HEADS)]

    def scores(kb, sb, tile):
        out = []
        kblk = k_ref[kb]
        for g in range(DSA_KV_HEADS):
            sg = _dot_nt(kblk[:, g * 128:(g + 1) * 128], qd_scr[g * DSA_GROUP * tq:(g + 1) * DSA_GROUP * tq, :])
            for hh in range(DSA_GROUP):
                h = g * DSA_GROUP + hh
                s = sg[:, hh * tq:(hh + 1) * tq] + sb
                if tile is not None:
                    s = s + (bias_ref[tile, h] - far_bias[h])
                out.append(s)
        return out

    m_scr[...] = jnp.full(m_scr.shape, NEG_BIG, F32)
    acc_scr[...] = jnp.zeros(acc_scr.shape, F32)

    def pass_max(kb, tile):
        key = key_scr[kb]
        sel = (key > thr) | ((key == thr) & (kb * KB + krow <= last_tie))
        if tile is not None and tile >= 1:
            sel = sel & ((tile - 1) * KB + krow <= qcol)
        sb = jnp.where(sel, 0.0, NEG_BIG)
        sb_scr[kb] = sb
        for h, s in enumerate(scores(kb, sb, tile)):
            m_scr[h] = jnp.maximum(m_scr[h], _colreduce(s, jnp.max))

    def pass_sum(kb, tile, m_rows):
        for h, s in enumerate(scores(kb, sb_scr[kb], tile)):
            p_scr[:, h * tq:(h + 1) * tq] = jnp.exp(s - m_rows[h]).astype(BF16)
        vt = vt_ref[kb]
        for g in range(DSA_KV_HEADS):
            acc_scr[g] += _dot(vt[g * VT_ROWS:(g + 1) * VT_ROWS, :],
                               p_scr[:, g * DSA_GROUP * tq:(g + 1) * DSA_GROUP * tq])

    def run(step):
        def body(kb, carry):
            step(kb, None)
            return carry

        lax.fori_loop(0, first - 1, body, 0)

        @pl.when(qi >= 1)
        def _():
            step(first - 1, 0)

        in_tile(lambda kb, t: step(kb, 1 + t))

    run(pass_max)
    m_rows = [jnp.max(m_scr[h], axis=0, keepdims=True) for h in range(DSA_HEADS)]
    run(lambda kb, tile: pass_sum(kb, tile, m_rows))

    for p in range(DSA_HEADS // 2):
        pair = []
        for h in (2 * p, 2 * p + 1):
            a = acc_scr[h // DSA_GROUP][:, (h % DSA_GROUP) * tq:(h % DSA_GROUP + 1) * tq]
            pair.append(a[0:DSA_HEAD_DIM, :] / a[DSA_HEAD_DIM:DSA_HEAD_DIM + 1, :])
        o_ref[:, 128 * p:128 * p + 128] = jnp.concatenate(pair, axis=0).T


def _dsa_prompt(rel_bias, qidx, tail, kidxdup, qdsa, kdup, vt, batch, seq, tq):
    assert MAX_DISTANCE <= KB, "keys two or more blocks back must all fall in the last bucket"
    nq = seq // tq
    nkb = seq // KB
    kq = tq // KB
    n_sel = min(TOPK_MAX, seq // 4)
    idx_bits = max(1, (seq - 1).bit_length())
    bias = jnp.stack([_bias_table(rel_bias, KB, tq, -1, 1, KB - t * KB) for t in range(kq + 1)])
    kern = functools.partial(_dsa_prompt_kernel, n_sel=n_sel, idx_bits=idx_bits)
    qblk = lambda w: pl.BlockSpec((tq, w), lambda b, i: (b * nq + i, 0))
    seqblk = lambda r, w: pl.BlockSpec((nkb, r, w), lambda b, i: (b, 0, 0))
    return pl.pallas_call(
        kern,
        grid=(batch, nq),
        in_specs=[pl.BlockSpec(memory_space=pltpu.SMEM),
                  qblk(512), qblk(LANES), seqblk(KB, LANES), qblk(512), seqblk(KB, 256),
                  seqblk(DSA_KV_HEADS * VT_ROWS, KB), _full(bias.shape)],
        out_specs=qblk(512),
        out_shape=jax.ShapeDtypeStruct((batch * seq, 512), F32),
        scratch_shapes=[pltpu.VMEM((IDX_HEADS * tq, LANES), BF16), pltpu.VMEM((DSA_HEADS * tq, LANES), BF16),
                        pltpu.VMEM((nkb, KB, tq), I32), pltpu.VMEM((nkb, KB, tq), F32),
                        pltpu.VMEM((1, tq), I32), pltpu.VMEM((DSA_HEADS, 8, tq), F32),
                        pltpu.VMEM((KB, DSA_HEADS * tq), BF16),
                        pltpu.VMEM((DSA_KV_HEADS, VT_ROWS, DSA_GROUP * tq), F32)],
        compiler_params=_cparams(("parallel", "arbitrary")),
    )(rel_bias, qidx, tail, kidxdup.reshape(batch * nkb, KB, LANES), qdsa, kdup.reshape(batch * nkb, KB, 256),
      vt, bias)


def _page_specs(n, width, index_of):
    return [pl.BlockSpec((None, None, width, PAGE_SIZE), functools.partial(index_of, i)) for i in range(n)]


def _pages_feature_major(cache):
    c = jnp.moveaxis(cache, 2, -1)
    return c.reshape(cache.shape[0], cache.shape[1], -1, PAGE_SIZE)


def _sample_score_kernel(pt_ref, q_ref, w_ref, *refs):
    pages, o_ref, kb_scr = refs[:-2], refs[-2], refs[-1]
    bi = pl.program_id(2)
    for i, pg in enumerate(pages):
        kb_scr[:, i * PAGE_SIZE:(i + 1) * PAGE_SIZE] = pg[...].astype(BF16)
    d = _dot(q_ref[...], kb_scr[...])
    sc = jnp.sum(jnp.maximum(d, 0.0) * w_ref[...], axis=0, keepdims=True) * IDX_SCALE
    o_ref[pl.ds(bi, 1), :] = sc


def _sample_scores(page_table, qidx3, widx3, cache_idx_t, pps):
    dec, n_pages = page_table.shape
    n_chunks = n_pages // pps
    grp = 8
    page_of = lambda i, bo, c, bi, pt: (0, pt[bo * grp + bi, c * pps + i], 0, 0)
    grid_spec = pltpu.PrefetchScalarGridSpec(
        num_scalar_prefetch=1,
        grid=(dec // grp, n_chunks, grp),
        in_specs=[pl.BlockSpec((None, IDX_HEADS, IDX_DIM), lambda bo, c, bi, pt: (bo * grp + bi, 0, 0)),
                  pl.BlockSpec((None, IDX_HEADS, 1), lambda bo, c, bi, pt: (bo * grp + bi, 0, 0))]
                 + _page_specs(pps, IDX_DIM, page_of),
        out_specs=pl.BlockSpec((grp, pps * PAGE_SIZE), lambda bo, c, bi, pt: (bo, c)),
        scratch_shapes=[pltpu.VMEM((IDX_DIM, pps * PAGE_SIZE), BF16)],
    )
    return pl.pallas_call(
        _sample_score_kernel, grid_spec=grid_spec,
        out_shape=jax.ShapeDtypeStruct((dec, n_pages * PAGE_SIZE), F32),
        compiler_params=_cparams(("arbitrary", "arbitrary", "arbitrary")),
    )(page_table, qidx3, widx3, *([cache_idx_t] * pps))


def _sample_select_kernel(sc_ref, qidx_ref, tail_ref, selp_ref, seln_ref, key_scr, *, n_sel, idx_bits):
    rows, past = sc_ref.shape
    nblk = past // LANES
    lane = lax.broadcasted_iota(I32, (rows, LANES), 1)
    tail = tail_ref[...]
    kidx2 = jnp.where(lane < 64, tail, pltpu.roll(tail, 64, 1))
    sc_new = jnp.zeros((rows, 1), F32)
    for h in range(IDX_HEADS):
        prod = qidx_ref[:, (h // 2) * 128:(h // 2 + 1) * 128].astype(F32) * kidx2.astype(BF16).astype(F32)
        mine = (lane < 64) if h % 2 == 0 else (lane >= 64)
        dot = jnp.sum(jnp.where(mine, prod, 0.0), axis=-1, keepdims=True)
        sc_new = sc_new + jnp.maximum(dot, 0.0) * tail[:, T_WIDX + h:T_WIDX + h + 1]
    key_new = _sort_key(sc_new * IDX_SCALE)

    for j in range(nblk):
        key_scr[j] = _sort_key(sc_ref[:, j * LANES:(j + 1) * LANES])

    def count(pred, pred_new):
        def body(j, c):
            return c + jnp.where(pred(key_scr[j], j), 1.0, 0.0)
        c = lax.fori_loop(0, nblk, body, jnp.zeros((rows, LANES), F32), unroll=4)
        return jnp.sum(c, axis=-1, keepdims=True) + jnp.where(pred_new, 1.0, 0.0)

    thr = _kth_key(lambda cand: count(lambda k, j: k >= cand, key_new >= cand), float(n_sel), (rows, 1))
    cnt_gt = count(lambda k, j: k > thr, key_new > thr)
    need = float(n_sel) - cnt_gt

    def body(i, p):
        cand = p + (jnp.int32(1) << (idx_bits - 1 - i))
        c = count(lambda k, j: (k == thr) & (j * LANES + lane < cand), (key_new == thr) & (past < cand))
        return jnp.where(c < need, cand, p)

    last_tie = lax.fori_loop(0, idx_bits, body, jnp.zeros((rows, 1), I32))

    for j in range(nblk):
        k = key_scr[j]
        sel = (k > thr) | ((k == thr) & (j * LANES + lane <= last_tie))
        selp_ref[:, j * LANES:(j + 1) * LANES] = jnp.where(sel, 1.0, 0.0)
    sel_new = (key_new > thr) | ((key_new == thr) & (past <= last_tie))
    seln_ref[...] = jnp.broadcast_to(jnp.where(sel_new, 1.0, 0.0), (rows, LANES))


def _sample_select(scores, qidx, tail):
    dec, past = scores.shape
    n_sel = min(TOPK_MAX, (past + 1) // 4)
    idx_bits = max(1, past.bit_length())
    rows = 64 if dec % 64 == 0 else dec
    kern = functools.partial(_sample_select_kernel, n_sel=n_sel, idx_bits=idx_bits)
    return pl.pallas_call(
        kern,
        grid=(dec // rows,),
        in_specs=[pl.BlockSpec((rows, past), lambda i: (i, 0)), pl.BlockSpec((rows, 512), lambda i: (i, 0)),
                  pl.BlockSpec((rows, LANES), lambda i: (i, 0))],
        out_specs=[pl.BlockSpec((rows, past), lambda i: (i, 0)), pl.BlockSpec((rows, LANES), lambda i: (i, 0))],
        out_shape=[jax.ShapeDtypeStruct((dec, past), F32), jax.ShapeDtypeStruct((dec, LANES), F32)],
        scratch_shapes=[pltpu.VMEM((past // LANES, rows, LANES), I32)],
        compiler_params=_cparams(("parallel",)),
    )(scores, qidx, tail)


def _mla_sample_kernel(pt_ref, qn_ref, qr_ref, rown_ref, wuk_ref, wuv_ref, gk_ref, *refs):
    pages, o_ref, lat_scr, kr_scr, m_scr, l_scr, acc_scr = refs[:-6], refs[-6], refs[-5], refs[-4], refs[-3], refs[-2], refs[-1]
    c = pl.program_id(1)
    n_chunks = pl.num_programs(1)
    heads = MLA_HEADS

    @pl.when(c == 0)
    def _():
        m_scr[...] = jnp.full(m_scr.shape, -jnp.inf, F32)
        l_scr[...] = jnp.zeros(l_scr.shape, F32)
        acc_scr[...] = jnp.zeros(acc_scr.shape, F32)

    qg = qn_ref[...] * gk_ref[...]
    qr = qr_ref[...]

    def update(latb, krb, n_valid):
        n = latb.shape[1]
        knt = _dot(wuk_ref[...], latb)
        ms = jnp.mean((knt * knt).reshape(heads, MLA_NOPE, n), axis=1)
        num = jnp.sum((knt * qg).reshape(heads, MLA_NOPE, n), axis=1)
        s = (num * lax.rsqrt(ms + EPS) + _dot(qr, krb)) * MLA_SCALE
        if n_valid < n:
            s = jnp.where(lax.broadcasted_iota(I32, s.shape, 1) < n_valid, s, -jnp.inf)
        m_old = m_scr[...]
        m_new = jnp.maximum(m_old, jnp.max(s, axis=-1, keepdims=True))
        alpha = jnp.exp(m_old - m_new)
        p = jnp.exp(s - m_new)
        l_scr[...] = alpha * l_scr[...] + jnp.sum(p, axis=-1, keepdims=True)
        m_scr[...] = m_new
        acc_scr[...] = alpha * acc_scr[...] + _dot_nt(p.astype(BF16), latb)

    for i, pg in enumerate(pages):
        lat_scr[:, i * PAGE_SIZE:(i + 1) * PAGE_SIZE] = pg[0:MLA_KV_RANK, :].astype(BF16)
        kr_scr[:, i * PAGE_SIZE:(i + 1) * PAGE_SIZE] = pg[MLA_KV_RANK:MLA_ROW, :].astype(BF16)
    update(lat_scr[...], kr_scr[...], lat_scr.shape[1])

    @pl.when(c == n_chunks - 1)
    def _():
        rn = jnp.broadcast_to(rown_ref[...], (MLA_ROW, PAGE_SIZE))
        update(rn[0:MLA_KV_RANK, :].astype(BF16), rn[MLA_KV_RANK:MLA_ROW, :].astype(BF16), 1)
        o_lat = (acc_scr[...] / l_scr[...]).astype(BF16)
        full = _dot(o_lat, wuv_ref[...])
        hd = lax.broadcasted_iota(I32, full.shape, 1) // MLA_V
        own = hd == lax.broadcasted_iota(I32, full.shape, 0)
        o_ref[...] = jnp.sum(jnp.where(own, full, 0.0), axis=0, keepdims=True)


def _mla_sample(page_table, qn_col, qr, row_new, wuk_t, wuv, gk_col, cache_mla, pps):
    dec, n_pages = page_table.shape
    n_chunks = n_pages // pps
    page_of = lambda i, b, c, pt: (0, pt[b, c * pps + i], 0, 0)
    grid_spec = pltpu.PrefetchScalarGridSpec(
        num_scalar_prefetch=1,
        grid=(dec, n_chunks),
        in_specs=[pl.BlockSpec((None, MLA_HEADS * MLA_NOPE, 1), lambda b, c, pt: (b, 0, 0)),
                  pl.BlockSpec((None, MLA_HEADS, MLA_ROPE), lambda b, c, pt: (b, 0, 0)),
                  pl.BlockSpec((None, MLA_ROW, 1), lambda b, c, pt: (b, 0, 0)),
                  pl.BlockSpec(wuk_t.shape, lambda b, c, pt: (0, 0)),
                  pl.BlockSpec(wuv.shape, lambda b, c, pt: (0, 0)),
                  pl.BlockSpec(gk_col.shape, lambda b, c, pt: (0, 0))]
                 + _page_specs(pps, MLA_ROW, page_of),
        out_specs=pl.BlockSpec((None, 1, MLA_HEADS * MLA_V), lambda b, c, pt: (b, 0, 0)),
        scratch_shapes=[pltpu.VMEM((MLA_KV_RANK, pps * PAGE_SIZE), BF16), pltpu.VMEM((MLA_ROPE, pps * PAGE_SIZE), BF16),
                        pltpu.VMEM((MLA_HEADS, 1), F32), pltpu.VMEM((MLA_HEADS, 1), F32),
                        pltpu.VMEM((MLA_HEADS, MLA_KV_RANK), F32)],
    )
    return pl.pallas_call(
        _mla_sample_kernel, grid_spec=grid_spec,
        out_shape=jax.ShapeDtypeStruct((dec, 1, MLA_HEADS * MLA_V), F32),
        compiler_params=_cparams(("parallel", "arbitrary")),
    )(page_table, qn_col, qr, row_new, wuk_t, wuv, gk_col, *([cache_mla] * pps))


def _dsa_sample_kernel(pt_ref, q_ref, kvn_ref, selp_ref, seln_ref, biasp_ref, biasn_ref, *refs):
    pages, o_ref, k_scr, v_scr, m_scr, l_scr, acc_scr = refs[:-6], refs[-6], refs[-5], refs[-4], refs[-3], refs[-2], refs[-1]
    c = pl.program_id(1)
    n_chunks = pl.num_programs(1)

    @pl.when(c == 0)
    def _():
        m_scr[...] = jnp.full(m_scr.shape, NEG_BIG, F32)
        l_scr[...] = jnp.zeros(l_scr.shape, F32)
        acc_scr[...] = jnp.zeros(acc_scr.shape, F32)

    q = q_ref[...]

    def update(s, sel, pv_of):
        sh = jnp.where(sel, s, NEG_BIG)
        m_old = m_scr[...]
        m_new = jnp.maximum(m_old, jnp.max(sh, axis=-1, keepdims=True))
        alpha = jnp.exp(m_old - m_new)
        p = jnp.where(sel, jnp.exp(sh - m_new), 0.0)
        l_scr[...] = alpha * l_scr[...] + jnp.sum(p, axis=-1, keepdims=True)
        m_scr[...] = m_new
        acc_scr[...] = alpha * acc_scr[...] + pv_of(p)

    for i, pg in enumerate(pages):
        k_scr[:, i * PAGE_SIZE:(i + 1) * PAGE_SIZE] = pg[0:128, :].astype(BF16)
        v_scr[:, i * PAGE_SIZE:(i + 1) * PAGE_SIZE] = pg[128:256, :].astype(BF16)
    s = _dot(q.astype(BF16), k_scr[...]) + biasp_ref[...]
    update(s, selp_ref[...] > 0.5, lambda p: _dot_nt(p.astype(BF16), v_scr[...]))

    @pl.when(c == n_chunks - 1)
    def _():
        kvn = kvn_ref[...]
        kn = kvn[:, 0:128].astype(BF16).astype(F32)
        vn = kvn[:, 128:256].astype(BF16).astype(F32)
        s_new = jnp.sum(q.astype(F32) * kn, axis=-1, keepdims=True) + biasn_ref[...]
        update(s_new, seln_ref[:, 0:1] > 0.5, lambda p: p.astype(BF16).astype(F32) * vn)
        o = acc_scr[...] / l_scr[...]
        lane = lax.broadcasted_iota(I32, o.shape, 1)
        row = lax.broadcasted_iota(I32, o.shape, 0)
        own = (lane // DSA_HEAD_DIM) == (row // DSA_GROUP)
        o_ref[...] = jnp.where(own, o, 0.0)


def _dsa_sample(page_table, qs, kv_new, selp, seln, bias_past, bias_new, cache_kv4, pps):
    dec, n_pages = page_table.shape
    n_chunks = n_pages // pps
    chunk = pps * PAGE_SIZE
    page_of = lambda i, b, c, pt: (0, pt[b, c * pps + i], 0, 0)
    grid_spec = pltpu.PrefetchScalarGridSpec(
        num_scalar_prefetch=1,
        grid=(dec, n_chunks),
        in_specs=[pl.BlockSpec((None, DSA_HEADS, LANES), lambda b, c, pt: (b, 0, 0)),
                  pl.BlockSpec((None, 1, 256), lambda b, c, pt: (b, 0, 0)),
                  pl.BlockSpec((None, 1, chunk), lambda b, c, pt: (b, 0, c)),
                  pl.BlockSpec((None, 1, LANES), lambda b, c, pt: (b, 0, 0)),
                  pl.BlockSpec((DSA_HEADS, chunk), lambda b, c, pt: (0, c)),
                  pl.BlockSpec((DSA_HEADS, 1), lambda b, c, pt: (0, 0))]
                 + _page_specs(pps, 256, page_of),
        out_specs=pl.BlockSpec((None, DSA_HEADS, LANES), lambda b, c, pt: (b, 0, 0)),
        scratch_shapes=[pltpu.VMEM((128, chunk), BF16), pltpu.VMEM((128, chunk), BF16),
                        pltpu.VMEM((DSA_HEADS, 1), F32), pltpu.VMEM((DSA_HEADS, 1), F32),
                        pltpu.VMEM((DSA_HEADS, LANES), F32)],
    )
    return pl.pallas_call(
        _dsa_sample_kernel, grid_spec=grid_spec,
        out_shape=jax.ShapeDtypeStruct((dec, DSA_HEADS, LANES), F32),
        compiler_params=_cparams(("parallel", "arbitrary")),
    )(page_table, qs, kv_new, selp, seln, bias_past, bias_new, *([cache_kv4] * pps))


def _out_kernel(x_ref, oa_ref, ob_ref, gm_ref, sf_ref, scf_ref, gf_ref, gout_ref, gffn_ref,
                wout_ref, wg_ref, wu_ref, wdown_ref, y_ref, *, ff_chunk):
    def rms(v, g):
        return v * lax.rsqrt(jnp.mean(v * v, axis=-1, keepdims=True) + EPS) * g

    half = oa_ref.shape[1]
    na = rms(oa_ref[...], gout_ref[:, 0:half]).astype(BF16)
    nb = rms(ob_ref[...], gout_ref[:, half:2 * half]).astype(BF16)
    mix = _dot(na, wout_ref[0:half, :]) + _dot(nb, wout_ref[half:2 * half, :])
    x1 = x_ref[...] + gm_ref[...] * mix
    hb = (rms(x1, gffn_ref[...]) * (1.0 + scf_ref[...]) + sf_ref[...]).astype(BF16)
    acc = jnp.zeros(x1.shape, F32)
    for j in range(D_FF // ff_chunk):
        g = _dot(hb, wg_ref[:, j * ff_chunk:(j + 1) * ff_chunk])
        u = _dot(hb, wu_ref[:, j * ff_chunk:(j + 1) * ff_chunk])
        a = (g / (1.0 + jnp.exp(-g))) * u
        acc = acc + _dot(a.astype(BF16), wdown_ref[j * ff_chunk:(j + 1) * ff_chunk, :])
    y_ref[...] = x1 + gf_ref[...] * acc


def _out_stage(x2d, oa, ob, gate_m, shift_f, scale_f, gate_f, wp, tm, per_token, tiles_per_seq):
    t_total = x2d.shape[0]
    if per_token:
        mod_spec = pl.BlockSpec((tm, D_MODEL), lambda i: (i, 0))
    else:
        mod_spec = pl.BlockSpec((None, 1, D_MODEL), lambda i: (i // tiles_per_seq, 0, 0))
    consts = [wp["g_out"], wp["g_ffn"], wp["w_out"], wp["w_gate"], wp["w_up"], wp["w_down"]]
    tok = lambda w: pl.BlockSpec((tm, w), lambda i: (i, 0))
    return pl.pallas_call(
        functools.partial(_out_kernel, ff_chunk=256),
        grid=(t_total // tm,),
        in_specs=[tok(D_MODEL), tok(512), tok(512), mod_spec, mod_spec, mod_spec, mod_spec]
                 + [_full(c.shape) for c in consts],
        out_specs=tok(D_MODEL),
        out_shape=jax.ShapeDtypeStruct((t_total, D_MODEL), F32),
        compiler_params=_cparams(("parallel",)),
    )(x2d, oa, ob, gate_m, shift_f, scale_f, gate_f, *consts)


def _prep_weights(w_in, g_norm_mix, g_norm_ffn, g_q_lat, w_uq, g_kv_lat, w_ukv, g_mla_q_nope, g_mla_q_rope,
                  g_mla_k_nope, g_mla_k_rope, g_dsa_q, g_dsa_k, g_out, w_out, w_ffn_in, w_ffn_out):
    splits = np.cumsum([MLA_Q_RANK, MLA_KV_RANK, MLA_ROPE, 512, 128, 128, 512, IDX_DIM, IDX_HEADS])
    s = [0] + splits.tolist()
    col = lambda i: w_in[:, s[i]:s[i + 1]]
    pad = jnp.zeros((D_MODEL, C_END - C_TAIL - IDX_DIM - MLA_ROPE - IDX_HEADS), w_in.dtype)
    w_in_r = jnp.concatenate([col(0), col(1), col(3), col(4), col(5), col(6), col(7), col(2), col(8), pad], axis=1)
    zq = jnp.zeros((MLA_Q_RANK, MLA_HEADS, LANES - MLA_NOPE - MLA_ROPE), w_uq.dtype)
    w_uq_cat = jnp.concatenate([w_uq, zq], axis=2).reshape(MLA_Q_RANK, MLA_HEADS * LANES)
    zk = jnp.zeros((MLA_KV_RANK, MLA_HEADS, LANES - MLA_NOPE), w_ukv.dtype)
    w_uk_cat = jnp.concatenate([w_ukv[:, :, :MLA_NOPE], zk], axis=2).reshape(MLA_KV_RANK, MLA_HEADS * LANES)
    w_uv = w_ukv[:, :, MLA_NOPE:].reshape(MLA_KV_RANK, MLA_HEADS * MLA_V)
    w_uk_t = w_ukv[:, :, :MLA_NOPE].reshape(MLA_KV_RANK, MLA_HEADS * MLA_NOPE).T
    z32 = jnp.zeros((LANES - MLA_NOPE - MLA_ROPE,), F32)
    z64 = jnp.zeros((LANES - MLA_NOPE,), F32)
    g_qcat = jnp.tile(jnp.concatenate([g_mla_q_nope, g_mla_q_rope, z32]), MLA_HEADS)
    g_kcat = jnp.tile(jnp.concatenate([g_mla_k_nope, z64]), MLA_HEADS)
    g_tail = jnp.concatenate([jnp.zeros((T_ROPE,), F32), g_mla_k_rope, z32])
    row = lambda v: v.reshape(1, -1).astype(F32)
    return {
        "g_mix": row(g_norm_mix), "g_ffn": row(g_norm_ffn), "w_in": w_in_r.astype(BF16),
        "g_qlat": row(g_q_lat), "w_uq": w_uq_cat.astype(BF16), "g_qcat": row(g_qcat),
        "g_kvlat": row(g_kv_lat), "w_ukv": jnp.concatenate([w_uk_cat, w_uv], axis=1).astype(BF16),
        "g_kcat": row(g_kcat), "g_tail": row(g_tail),
        "g_q": row(jnp.tile(g_dsa_q, DSA_HEADS)), "g_k": row(jnp.tile(g_dsa_k, DSA_KV_HEADS)),
        "m_cat": _group_matrix(256, [(0, 64), (64, 96), (128, 192), (192, 224)]),
        "m_64": _group_matrix(256, [(0, 64), (64, 128), (128, 192), (192, 256)]),
        "w_uk_t": w_uk_t.astype(BF16), "w_uv": w_uv.astype(BF16), "g_k_nope": g_mla_k_nope,
        "g_out": row(g_out), "w_out": w_out.astype(BF16),
        "w_gate": w_ffn_in[:, :D_FF].astype(BF16), "w_up": w_ffn_in[:, D_FF:].astype(BF16),
        "w_down": w_ffn_out.astype(BF16),
    }


def _rope_tables(pos):
    freq = ROPE_THETA ** (-jnp.arange(HALF_ROPE, dtype=F32) / HALF_ROPE)
    ang = pos.astype(F32)[:, None] * freq[None, :]
    cos, sin = jnp.cos(ang), jnp.sin(ang)
    n = pos.shape[0]
    ones = jnp.ones((n, T_ROPE), F32)
    zeros = jnp.zeros((n, T_ROPE), F32)
    tail1 = jnp.ones((n, LANES - T_ROPE - MLA_ROPE), F32)
    tail0 = jnp.zeros((n, LANES - T_ROPE - MLA_ROPE), F32)
    return (jnp.concatenate([ones, cos, cos, tail1], axis=1),
            jnp.concatenate([zeros, -sin, sin, tail0], axis=1))


def _pick_tile(n, prefs):
    for t in prefs:
        if n % t == 0:
            return t
    return n


def kernel(x_prompt, x_sample, c_prompt, c_sample, cache_mla, cache_kv, cache_idx, page_table, rel_bias, w_ada, b_ada, g_norm_mix, g_norm_ffn, w_in, g_q_lat, w_uq, g_kv_lat, w_ukv, g_mla_q_nope, g_mla_q_rope, g_mla_k_nope, g_mla_k_rope, g_dsa_q, g_dsa_k, g_out, w_out, w_ffn_in, w_ffn_out):
    assert w_ada.shape[0] == 1 and x_sample.shape[1] == 1, "one layer, one new token per sample"
    batch, seq, _ = x_prompt.shape
    dec = x_sample.shape[0]
    n_pages = page_table.shape[1]
    past = n_pages * PAGE_SIZE
    wp = _prep_weights(w_in[0], g_norm_mix[0], g_norm_ffn[0], g_q_lat[0], w_uq[0], g_kv_lat[0], w_ukv[0],
                       g_mla_q_nope[0], g_mla_q_rope[0], g_mla_k_nope[0], g_mla_k_rope[0], g_dsa_q[0], g_dsa_k[0],
                       g_out[0], w_out[0], w_ffn_in[0], w_ffn_out[0])

    mod = _ada(jnp.concatenate([c_prompt, c_sample], axis=0), w_ada[0], b_ada[0])
    mods_p = [m.reshape(batch, 1, D_MODEL) for m in jnp.split(mod[:batch], 6, axis=-1)]
    mods_s = jnp.split(mod[batch:], 6, axis=-1)

    tm = _pick_tile(seq, (512, 256, 128))
    xp2 = x_prompt.reshape(batch * seq, D_MODEL)
    cos_p, sin_p = _rope_tables(jnp.arange(seq))
    pin = _in_stage(xp2, mods_p[1], mods_p[0], cos_p, sin_p, wp, tm, per_token=False)
    tq = _pick_tile(seq, (256, 128))
    o_mla_p = _mla_prompt(pin["qcat"], pin["kcat"], pin["vmla"], batch, seq, tq)
    o_dsa_p = _dsa_prompt(rel_bias, pin["qidx"], pin["tail"], pin["kidxdup"], pin["qdsa"], pin["kdup"], pin["vt"],
                          batch, seq, _pick_tile(seq, (256, 128)))
    y_p = _out_stage(xp2, o_mla_p, o_dsa_p, mods_p[2], mods_p[3], mods_p[4], mods_p[5], wp, tm,
                     per_token=False, tiles_per_seq=seq // tm)

    xs2 = x_sample.reshape(dec, D_MODEL)
    cos_s, sin_s = _rope_tables(jnp.full((1,), past))
    sin_ = _in_stage(xs2, mods_s[1], mods_s[0], cos_s, sin_s, wp, dec, per_token=True)
    pps = _pick_tile(n_pages, (16, 8, 4, 2))
    qidx3 = sin_["qidx"].reshape(dec, IDX_HEADS, IDX_DIM)
    widx3 = sin_["tail"][:, T_WIDX:T_WIDX + IDX_HEADS].reshape(dec, IDX_HEADS, 1)
    scores = _sample_scores(page_table, qidx3, widx3, _pages_feature_major(cache_idx), pps)
    selp, seln = _sample_select(scores, sin_["qidx"], sin_["tail"])
    qc = sin_["qcat"].reshape(dec, MLA_HEADS, LANES)
    qn_col = qc[:, :, 0:MLA_NOPE].astype(F32).reshape(dec, MLA_HEADS * MLA_NOPE, 1)
    qr = qc[:, :, T_ROPE:T_ROPE + MLA_ROPE]
    gk_col = jnp.tile(wp["g_k_nope"], MLA_HEADS).reshape(MLA_HEADS * MLA_NOPE, 1)
    o_mla_s = _mla_sample(page_table, qn_col, qr, sin_["row"].reshape(dec, MLA_ROW, 1), wp["w_uk_t"], wp["w_uv"],
                          gk_col, _pages_feature_major(cache_mla), pps).reshape(dec, MLA_HEADS * MLA_V)
    qd = sin_["qdsa"].astype(F32).reshape(dec, DSA_HEADS, DSA_HEAD_DIM)
    on_c = (jnp.arange(DSA_KV_HEADS)[None, :] == (jnp.arange(DSA_HEADS) // DSA_GROUP)[:, None]).astype(F32)
    qd2 = (qd[:, :, None, :] * on_c[None, :, :, None]).reshape(dec, DSA_HEADS, LANES)
    bias_past = _bias_table(rel_bias, 1, past, 0, -1, past).reshape(DSA_HEADS, past)
    bias_new = _bias_table(rel_bias, 1, LANES, 0, 0, 0)[:, 0, 0:1]
    o8 = _dsa_sample(page_table, qd2, sin_["kv"].reshape(dec, 1, 256), selp.reshape(dec, 1, past),
                     seln.reshape(dec, 1, LANES), bias_past, bias_new, _pages_feature_major(cache_kv), pps)
    o_dsa_s = o8.reshape(dec, DSA_HEADS, DSA_KV_HEADS, DSA_HEAD_DIM).sum(axis=2).reshape(dec, DSA_HEADS * DSA_HEAD_DIM)
    y_s = _out_stage(xs2, o_mla_s, o_dsa_s, mods_s[2], mods_s[3], mods_s[4], mods_s[5], wp, dec,
                     per_token=True, tiles_per_seq=1)

    return (y_p.reshape(batch, seq, D_MODEL), y_s.reshape(dec, 1, D_MODEL),
            pin["row"].reshape(1, batch, seq, MLA_ROW),
            pin["kv"].reshape(1, batch, seq, 2, DSA_KV_HEADS, DSA_HEAD_DIM),
            pin["kidx"].reshape(1, batch, seq, IDX_DIM),
            sin_["row"].reshape(1, dec, 1, MLA_ROW),
            sin_["kv"].reshape(1, dec, 1, 2, DSA_KV_HEADS, DSA_HEAD_DIM),
            sin_["kidx"].reshape(1, dec, 1, IDX_DIM))
```

```python
import functools
import math

import numpy as np
import jax
import jax.numpy as jnp
from jax import lax
from jax.experimental import pallas as pl
from jax.experimental.pallas import tpu as pltpu

D_MODEL = 1024
PAGE_SIZE = 128
MLA_HEADS = 8
MLA_NOPE = 64
MLA_ROPE = 32
MLA_V = 64
MLA_Q_RANK = 384
MLA_KV_RANK = 256
MLA_ROW = MLA_KV_RANK + MLA_ROPE
DSA_HEADS = 8
DSA_KV_HEADS = 2
DSA_HEAD_DIM = 64
DSA_GROUP = DSA_HEADS // DSA_KV_HEADS
IDX_HEADS = 8
IDX_DIM = 64
TOPK_MAX = 256
N_BUCKETS = 32
MAX_DISTANCE = 128
ROPE_THETA = 10000.0
D_FF = ((8 * D_MODEL + 3 * 256 - 1) // (3 * 256)) * 256
EPS = 1e-6
MLA_SCALE = (MLA_NOPE + MLA_ROPE) ** -0.5
DSA_SCALE = DSA_HEAD_DIM ** -0.5
IDX_SCALE = (IDX_DIM * IDX_HEADS) ** -0.5

LANES = 128
VMEM_LIMIT = 56 * 1024 * 1024

F32 = jnp.float32
BF16 = jnp.bfloat16
I32 = jnp.int32
NEG_BIG = -1e30
INT_MIN = -(2 ** 31)

C_QLAT = 0
C_KVLAT = C_QLAT + MLA_Q_RANK
C_Q = C_KVLAT + MLA_KV_RANK
C_K = C_Q + DSA_HEADS * DSA_HEAD_DIM
C_V = C_K + DSA_KV_HEADS * DSA_HEAD_DIM
C_QIDX = C_V + DSA_KV_HEADS * DSA_HEAD_DIM
C_TAIL = C_QIDX + IDX_HEADS * IDX_DIM
C_END = C_TAIL + LANES
T_ROPE = IDX_DIM
T_WIDX = IDX_DIM + MLA_ROPE
HALF_ROPE = MLA_ROPE // 2
KB = 128
VT_PAD = 16
VT_ROWS = DSA_HEAD_DIM + VT_PAD


def _dot(a, b):
    return jnp.dot(a, b, preferred_element_type=F32)


def _dot_nt(a, b):
    return lax.dot_general(a, b, (((1,), (1,)), ((), ())), preferred_element_type=F32)


def _split(a):
    hi = a.astype(BF16)
    lo = (a - hi.astype(F32)).astype(BF16)
    return hi, lo


def _dot3(a, b):
    ah, al = _split(a)
    bh, bl = _split(b)
    return _dot(ah, bh) + (_dot(al, bh) + _dot(ah, bl))


def _cparams(sem):
    return pltpu.CompilerParams(dimension_semantics=sem, vmem_limit_bytes=VMEM_LIMIT)


def _full(shape):
    n = len(shape)
    return pl.BlockSpec(shape, lambda *a, _n=n: (0,) * _n, pipeline_mode=pl.Buffered(1))


def _ada_kernel(c_ref, w_ref, b_ref, o_ref):
    c = c_ref[...]
    s = c / (1.0 + jnp.exp(-c))
    o_ref[...] = _dot3(s, w_ref[...]) + b_ref[...]


def _ada(c, w, b):
    n = c.shape[0]
    return pl.pallas_call(
        _ada_kernel,
        grid=(6,),
        in_specs=[pl.BlockSpec((n, D_MODEL), lambda j: (0, 0)),
                  pl.BlockSpec((D_MODEL, D_MODEL), lambda j: (0, j)),
                  pl.BlockSpec((1, D_MODEL), lambda j: (0, j))],
        out_specs=pl.BlockSpec((n, D_MODEL), lambda j: (0, j)),
        out_shape=jax.ShapeDtypeStruct((n, 6 * D_MODEL), F32),
        compiler_params=_cparams(("arbitrary",)),
    )(c, w, b.reshape(1, -1))


def _group_matrix(width, groups):
    g = np.zeros((width, width), np.float32)
    for lo, hi in groups:
        g[lo:hi, lo:hi] = 1.0 / (hi - lo)
    return jnp.asarray(g, BF16)


def _in_kernel(x_ref, sc_ref, sh_ref, cos_ref, sin_ref, gmix_ref, win_ref, gqlat_ref, wuq_ref, gqcat_ref,
               gkvlat_ref, wukv_ref, gkcat_ref, gtail_ref, gq_ref, gk_ref, mcat_ref, m64_ref,
               qcat_o, kcat_o, vmla_o, row_o, kv_o, kidx_o, tail_o, kidxdup_o, qdsa_o, kdup_o, qidx_o, vt_o):
    tm = x_ref.shape[0]
    x = x_ref[...]
    h = x * lax.rsqrt(jnp.mean(x * x, axis=-1, keepdims=True) + EPS) * gmix_ref[...]
    hb = (h * (1.0 + sc_ref[...]) + sh_ref[...]).astype(BF16)

    def proj(lo, hi):
        return _dot(hb, win_ref[:, lo:hi])

    cos = cos_ref[...]
    sin = sin_ref[...]
    lane = lax.broadcasted_iota(I32, (tm, LANES), 1)
    first_half = lane < T_ROPE + HALF_ROPE
    rope_lanes = (lane >= T_ROPE) & (lane < T_ROPE + MLA_ROPE)
    low64 = lane < 64

    def rope(xh):
        rot = jnp.where(first_half, pltpu.roll(xh, LANES - HALF_ROPE, 1), pltpu.roll(xh, HALF_ROPE, 1))
        return xh * cos + rot * sin

    def gnorm(blk, m_ref):
        ms = _dot((blk * blk).astype(BF16), m_ref[...])
        return blk * lax.rsqrt(ms + EPS)

    ql = proj(C_QLAT, C_KVLAT)
    qln = ql * lax.rsqrt(jnp.mean(ql * ql, axis=-1, keepdims=True) + EPS) * gqlat_ref[...]
    qm = _dot(qln.astype(BF16), wuq_ref[...])
    for j in range(4):
        y = gnorm(qm[:, 256 * j:256 * j + 256], mcat_ref) * gqcat_ref[:, 256 * j:256 * j + 256]
        for t in range(2):
            qcat_o[:, 256 * j + 128 * t:256 * j + 128 * t + 128] = rope(y[:, 128 * t:128 * t + 128]).astype(BF16)

    tail = proj(C_TAIL, C_END)
    ssq = jnp.sum(jnp.where(rope_lanes, tail * tail, 0.0), axis=-1, keepdims=True) * (1.0 / MLA_ROPE)
    tn = jnp.where(rope_lanes, tail * lax.rsqrt(ssq + EPS) * gtail_ref[...], tail)
    tr = rope(tn)
    kidx_o[...] = tail[:, 0:IDX_DIM]
    tail_o[...] = tail
    kidxdup_o[...] = jnp.where(low64, tail, pltpu.roll(tail, 64, 1)).astype(BF16)
    krope = jnp.where(rope_lanes, tr, 0.0)

    kvl = proj(C_KVLAT, C_Q)
    lat = kvl * lax.rsqrt(jnp.mean(kvl * kvl, axis=-1, keepdims=True) + EPS) * gkvlat_ref[...]
    row_o[:, 0:MLA_KV_RANK] = lat
    row_o[:, MLA_KV_RANK:MLA_ROW] = tr[:, T_ROPE:T_ROPE + MLA_ROPE]
    latb = lat.astype(BF16)
    kx = _dot(latb, wukv_ref[:, 0:1024])
    for j in range(4):
        y = gnorm(kx[:, 256 * j:256 * j + 256], mcat_ref) * gkcat_ref[:, 256 * j:256 * j + 256]
        for t in range(2):
            kcat_o[:, 256 * j + 128 * t:256 * j + 128 * t + 128] = (y[:, 128 * t:128 * t + 128] + krope).astype(BF16)
    ones_half = (lax.broadcasted_iota(I32, (tm, 1024), 1) % LANES) >= MLA_V
    vmla_o[...] = jnp.where(ones_half, 1.0, _dot(latb, wukv_ref[:, 1024:2048])).astype(BF16)

    q = proj(C_Q, C_K)
    for j in range(2):
        qdsa_o[:, 256 * j:256 * j + 256] = (gnorm(q[:, 256 * j:256 * j + 256], m64_ref)
                                            * gq_ref[:, 256 * j:256 * j + 256] * DSA_SCALE).astype(BF16)
    k = proj(C_K, C_V)
    ms = _dot((k * k).astype(BF16), m64_ref[0:128, 0:128])
    kn = k * lax.rsqrt(ms + EPS) * gk_ref[...]
    v = proj(C_V, C_QIDX)
    kv_o[:, 0:128] = kn
    kv_o[:, 128:256] = v
    kr = pltpu.roll(kn, 64, 1)
    kdup_o[:, 0:128] = jnp.where(low64, kn, kr).astype(BF16)
    kdup_o[:, 128:256] = jnp.where(low64, kr, kn).astype(BF16)
    ones = jnp.ones((VT_PAD, LANES), BF16)
    for i in range(tm // LANES):
        vt = v[i * LANES:(i + 1) * LANES, :].T.astype(BF16)
        for c in range(DSA_KV_HEADS):
            vt_o[i, c * VT_ROWS:c * VT_ROWS + DSA_HEAD_DIM, :] = vt[c * DSA_HEAD_DIM:(c + 1) * DSA_HEAD_DIM, :]
            vt_o[i, c * VT_ROWS + DSA_HEAD_DIM:(c + 1) * VT_ROWS, :] = ones
    qidx_o[...] = proj(C_QIDX, C_TAIL).astype(BF16)


_IN_OUT_WIDTHS = (("qcat", 1024, BF16), ("kcat", 1024, BF16), ("vmla", 1024, BF16), ("row", MLA_ROW, F32),
                  ("kv", 256, F32), ("kidx", IDX_DIM, F32), ("tail", LANES, F32), ("kidxdup", LANES, BF16),
                  ("qdsa", 512, BF16), ("kdup", 256, BF16), ("qidx", 512, BF16))


def _in_stage(x2d, scale, shift, cos, sin, wp, tm, per_token):
    t_total = x2d.shape[0]
    n_tiles = t_total // tm
    if per_token:
        mod_spec = pl.BlockSpec((tm, D_MODEL), lambda i: (i, 0))
        tab_spec = pl.BlockSpec((1, LANES), lambda i: (0, 0))
    else:
        tiles_per_seq = cos.shape[0] // tm
        mod_spec = pl.BlockSpec((None, 1, D_MODEL), lambda i: (i // tiles_per_seq, 0, 0))
        tab_spec = pl.BlockSpec((tm, LANES), lambda i: (i % tiles_per_seq, 0))
    consts = [wp["g_mix"], wp["w_in"], wp["g_qlat"], wp["w_uq"], wp["g_qcat"], wp["g_kvlat"], wp["w_ukv"],
              wp["g_kcat"], wp["g_tail"], wp["g_q"], wp["g_k"], wp["m_cat"], wp["m_64"]]
    in_specs = ([pl.BlockSpec((tm, D_MODEL), lambda i: (i, 0)), mod_spec, mod_spec, tab_spec, tab_spec]
                + [_full(c.shape) for c in consts])
    out_specs = [pl.BlockSpec((tm, w), lambda i: (i, 0)) for _, w, _ in _IN_OUT_WIDTHS]
    out_shape = [jax.ShapeDtypeStruct((t_total, w), dt) for _, w, dt in _IN_OUT_WIDTHS]
    out_specs.append(pl.BlockSpec((tm // LANES, DSA_KV_HEADS * VT_ROWS, LANES), lambda i: (i, 0, 0)))
    out_shape.append(jax.ShapeDtypeStruct((t_total // LANES, DSA_KV_HEADS * VT_ROWS, LANES), BF16))
    outs = pl.pallas_call(
        _in_kernel, grid=(n_tiles,), in_specs=in_specs, out_specs=out_specs, out_shape=out_shape,
        compiler_params=_cparams(("parallel",)),
    )(x2d, scale, shift, cos, sin, *consts)
    res = {name: o for (name, _, _), o in zip(_IN_OUT_WIDTHS, outs)}
    res["vt"] = outs[-1]
    return res


def _mla_prompt_kernel(q_ref, k_ref, v_ref, o_ref, m_scr, acc_scr, *, tk):
    tq = q_ref.shape[0]
    qi = pl.program_id(1)
    n_chunks = (qi * tq + tq + tk - 1) // tk
    m_scr[...] = jnp.full(m_scr.shape, -jnp.inf, F32)
    acc_scr[...] = jnp.zeros(acc_scr.shape, F32)
    low64 = lax.broadcasted_iota(I32, (tq, LANES), 1) < 64
    qpos = qi * tq + lax.broadcasted_iota(I32, (tq, tk), 0)
    kcol = lax.broadcasted_iota(I32, (tq, tk), 1)
    exp_scale = MLA_SCALE * math.log2(math.e)

    def raw_scores(c, h, masked):
        off = pl.multiple_of(c * tk, tk)
        s = _dot_nt(q_ref[:, h * 128:(h + 1) * 128], k_ref[pl.ds(off, tk), h * 128:(h + 1) * 128])
        if masked:
            s = jnp.where(c * tk + kcol <= qpos, s, -jnp.inf)
        return s

    def pass_max(c, masked):
        for h in range(MLA_HEADS):
            s = raw_scores(c, h, masked)
            m = m_scr[h]
            for j in range(tk // LANES):
                m = jnp.maximum(m, s[:, j * LANES:(j + 1) * LANES])
            m_scr[h] = m

    def pass_sum(c, masked):
        off = pl.multiple_of(c * tk, tk)
        for h in range(MLA_HEADS):
            s = raw_scores(c, h, masked)
            m = m_scr[h]
            p = jnp.concatenate([jnp.exp2((s[:, j * LANES:(j + 1) * LANES] - m) * exp_scale)
                                 for j in range(tk // LANES)], axis=1).astype(BF16)
            acc_scr[h] += _dot(p, v_ref[pl.ds(off, tk), h * 128:(h + 1) * 128])

    def run(step):
        def body(c, carry):
            step(c, False)
            return carry

        lax.fori_loop(0, n_chunks - 1, body, 0)
        step(n_chunks - 1, True)

    run(pass_max)
    for h in range(MLA_HEADS):
        m_scr[h] = jnp.broadcast_to(jnp.max(m_scr[h], axis=-1, keepdims=True), (tq, LANES))
    run(pass_sum)
    for p in range(MLA_HEADS // 2):
        even, odd = acc_scr[2 * p], acc_scr[2 * p + 1]
        o_ref[:, 128 * p:128 * p + 128] = jnp.where(low64, even / pltpu.roll(even, 64, 1),
                                                    pltpu.roll(odd, 64, 1) / odd)


def _mla_prompt(qcat, kcat, vmla, batch, seq, tq, tk):
    nq = seq // tq
    return pl.pallas_call(
        functools.partial(_mla_prompt_kernel, tk=tk),
        grid=(batch, nq),
        in_specs=[pl.BlockSpec((tq, 1024), lambda b, i: (b * nq + i, 0)),
                  pl.BlockSpec((seq, 1024), lambda b, i: (b, 0)),
                  pl.BlockSpec((seq, 1024), lambda b, i: (b, 0))],
        out_specs=pl.BlockSpec((tq, 512), lambda b, i: (b * nq + i, 0)),
        out_shape=jax.ShapeDtypeStruct((batch * seq, 512), F32),
        scratch_shapes=[pltpu.VMEM((MLA_HEADS, tq, LANES), F32), pltpu.VMEM((MLA_HEADS, tq, LANES), F32)],
        compiler_params=_cparams(("parallel", "arbitrary")),
    )(qcat, kcat, vmla)


def _sort_key(score):
    bits = lax.bitcast_convert_type(score + 0.0, I32)
    return bits ^ ((bits >> 31) & 0x7FFFFFFF)


def _kth_key(count_ge, n_sel, done0):
    def cond(state):
        i, _, done = state
        return (i < 32) & (jnp.min(done) < 0.5)

    def body(state):
        i, t, done = state
        cand = t + (jnp.int32(1) << (31 - i))
        cnt = count_ge(cand)
        live = done < 0.5
        t = jnp.where((cnt >= n_sel) & live, cand, t)
        done = jnp.where(cnt == n_sel, 1.0, done)
        return i + 1, t, done

    _, t, _ = lax.while_loop(cond, body, (jnp.int32(0), jnp.full(done0.shape, INT_MIN, I32), done0))
    return t


def _bucket(dist):
    max_exact = N_BUCKETS // 2
    d = jnp.maximum(dist, 0)
    log_ratio = jnp.log(jnp.maximum(d, max_exact).astype(F32) / max_exact) / math.log(MAX_DISTANCE / max_exact)
    large = jnp.minimum(max_exact + (log_ratio * (N_BUCKETS - max_exact)).astype(I32), N_BUCKETS - 1)
    return jnp.where(d < max_exact, d, large)


def _bias_kernel(rb_ref, o_ref, *, a, b, c):
    rows, cols = o_ref.shape[1], o_ref.shape[2]
    dist = (a * lax.broadcasted_iota(I32, (rows, cols), 0) + b * lax.broadcasted_iota(I32, (rows, cols), 1) + c)
    bucket = _bucket(dist)
    for h in range(DSA_HEADS):
        acc = jnp.zeros((rows, cols), F32)
        for n in range(N_BUCKETS):
            acc = jnp.where(bucket == n, rb_ref[n, h], acc)
        o_ref[h] = acc


def _bias_table(rel_bias, rows, cols, a, b, c):
    return pl.pallas_call(
        functools.partial(_bias_kernel, a=a, b=b, c=c),
        in_specs=[pl.BlockSpec(memory_space=pltpu.SMEM)],
        out_specs=pl.BlockSpec(memory_space=pltpu.VMEM),
        out_shape=jax.ShapeDtypeStruct((DSA_HEADS, rows, cols), F32),
    )(rel_bias)


def _colreduce(x, op):
    return op(x.reshape(x.shape[0] // 8, 8, x.shape[1]), axis=0)


def _dsa_prompt_kernel(rb_ref, qidx_ref, tail_ref, kidx_ref, q_ref, k_ref, vt_ref, bias_ref, o_ref,
                       qs_scr, qd_scr, key_scr, sb_scr, tie_scr, m_scr, p_scr, acc_scr, *, n_sel, idx_bits):
    tq = q_ref.shape[0]
    kq = tq // KB
    qi = pl.program_id(1)
    first = kq * qi
    krow = lax.broadcasted_iota(I32, (KB, tq), 0)
    qcol = lax.broadcasted_iota(I32, (KB, tq), 1)
    low64 = lax.broadcasted_iota(I32, (tq, LANES), 1) < 64

    for h in range(IDX_HEADS):
        mine = low64 if h % 2 == 0 else jnp.logical_not(low64)
        zero = jnp.zeros((tq, LANES), BF16)
        qs_scr[h * tq:(h + 1) * tq, :] = jnp.where(mine, qidx_ref[:, (h // 2) * 128:(h // 2 + 1) * 128], zero)
        qd_scr[h * tq:(h + 1) * tq, :] = jnp.where(mine, q_ref[:, (h // 2) * 128:(h // 2 + 1) * 128], zero)
    tail_t = tail_ref[...].T
    w_rows = [tail_t[T_WIDX + h:T_WIDX + h + 1, :] for h in range(IDX_HEADS)]

    def in_tile(step):
        for t in range(kq):
            step(first + t, t)

    def score_block(kb, t):
        d = _dot_nt(kidx_ref[kb], qs_scr[...])
        sc = jnp.zeros((KB, tq), F32)
        for h in range(IDX_HEADS):
            sc = sc + jnp.maximum(d[:, h * tq:(h + 1) * tq], 0.0) * w_rows[h]
        sc = sc * IDX_SCALE
        if t is not None:
            sc = jnp.where(t * KB + krow <= qcol, sc, -jnp.inf)
        key_scr[kb] = _sort_key(sc)

    def score_body(kb, carry):
        score_block(kb, None)
        return carry

    lax.fori_loop(0, first, score_body, 0)
    in_tile(score_block)
    n_blocks = first + kq

    def count(pred):
        def body(kb, c):
            return c + jnp.where(pred(key_scr[kb], kb), 1.0, 0.0)
        c = lax.fori_loop(0, n_blocks, body, jnp.zeros((KB, tq), F32))
        return jnp.sum(_colreduce(c, jnp.sum), axis=0, keepdims=True)

    n_causal = qi * tq + lax.broadcasted_iota(I32, (1, tq), 1) + 1
    thr = _kth_key(lambda cand: count(lambda k, kb: k >= cand), float(n_sel),
                   jnp.where(n_causal <= n_sel, 1.0, 0.0))
    cnt_gt = count(lambda k, kb: k > thr)
    cnt_ge = count(lambda k, kb: k >= thr)
    tie_scr[...] = jnp.full((1, tq), 2 ** 30, I32)
    excess = jnp.where((cnt_ge > float(n_sel)) & (thr > INT_MIN), 1.0, 0.0)

    @pl.when(jnp.max(excess) > 0.0)
    def _():
        need = float(n_sel) - cnt_gt

        def body(i, p):
            cand = p + (jnp.int32(1) << (idx_bits - 1 - i))
            c = count(lambda k, kb: (k == thr) & (kb * KB + krow < cand))
            return jnp.where(c < need, cand, p)

        tie_scr[...] = lax.fori_loop(0, idx_bits, body, jnp.zeros((1, tq), I32))

    last_tie = tie_scr[...]

    far_bias = [rb_ref[N_BUCKETS - 1, h] for h in range(DSA_HEADS)]

    def scores(kb, sb, tile):
        out = []
        kblk = k_ref[kb]
        for g in range(DSA_KV_HEADS):
            sg = _dot_nt(kblk[:, g * 128:(g + 1) * 128], qd_scr[g * DSA_GROUP * tq:(g + 1) * DSA_GROUP * tq, :])
            for hh in range(DSA_GROUP):
                h = g * DSA_GROUP + hh
                s = sg[:, hh * tq:(hh + 1) * tq] + sb
                if tile is not None:
                    s = s + (bias_ref[tile, h] - far_bias[h])
                out.append(s)
        return out

    m_scr[...] = jnp.full(m_scr.shape, NEG_BIG, F32)
    acc_scr[...] = jnp.zeros(acc_scr.shape, F32)

    def pass_max(kb, tile):
        key = key_scr[kb]
        sel = (key > thr) | ((key == thr) & (kb * KB + krow <= last_tie))
        if tile is not None and tile >= 1:
            sel = sel & ((tile - 1) * KB + krow <= qcol)
        sb = jnp.where(sel, 0.0, NEG_BIG)
        sb_scr[kb] = sb
        for h, s in enumerate(scores(kb, sb, tile)):
            m_scr[h] = jnp.maximum(m_scr[h], _colreduce(s, jnp.max))

    def pass_sum(kb, tile, m_rows):
        for h, s in enumerate(scores(kb, sb_scr[kb], tile)):
            p_scr[:, h * tq:(h + 1) * tq] = jnp.exp(s - m_rows[h]).astype(BF16)
        vt = vt_ref[kb]
        for g in range(DSA_KV_HEADS):
            acc_scr[g] += _dot(vt[g * VT_ROWS:(g + 1) * VT_ROWS, :],
                               p_scr[:, g * DSA_GROUP * tq:(g + 1) * DSA_GROUP * tq])

    def run(step):
        def body(kb, carry):
            step(kb, None)
            return carry

        lax.fori_loop(0, first - 1, body, 0)

        @pl.when(qi >= 1)
        def _():
            step(first - 1, 0)

        in_tile(lambda kb, t: step(kb, 1 + t))

    run(pass_max)
    m_rows = [jnp.max(m_scr[h], axis=0, keepdims=True) for h in range(DSA_HEADS)]
    run(lambda kb, tile: pass_sum(kb, tile, m_rows))

    for p in range(DSA_HEADS // 2):
        pair = []
        for h in (2 * p, 2 * p + 1):
            a = acc_scr[h // DSA_GROUP][:, (h % DSA_GROUP) * tq:(h % DSA_GROUP + 1) * tq]
            pair.append(a[0:DSA_HEAD_DIM, :] / a[DSA_HEAD_DIM:DSA_HEAD_DIM + 1, :])
        o_ref[:, 128 * p:128 * p + 128] = jnp.concatenate(pair, axis=0).T


def _dsa_prompt(rel_bias, qidx, tail, kidxdup, qdsa, kdup, vt, batch, seq, tq):
    assert MAX_DISTANCE <= KB, "keys two or more blocks back must all fall in the last bucket"
    nq = seq // tq
    nkb = seq // KB
    kq = tq // KB
    n_sel = min(TOPK_MAX, seq // 4)
    idx_bits = max(1, (seq - 1).bit_length())
    bias = jnp.stack([_bias_table(rel_bias, KB, tq, -1, 1, KB - t * KB) for t in range(kq + 1)])
    kern = functools.partial(_dsa_prompt_kernel, n_sel=n_sel, idx_bits=idx_bits)
    qblk = lambda w: pl.BlockSpec((tq, w), lambda b, i: (b * nq + i, 0))
    seqblk = lambda r, w: pl.BlockSpec((nkb, r, w), lambda b, i: (b, 0, 0))
    return pl.pallas_call(
        kern,
        grid=(batch, nq),
        in_specs=[pl.BlockSpec(memory_space=pltpu.SMEM),
                  qblk(512), qblk(LANES), seqblk(KB, LANES), qblk(512), seqblk(KB, 256),
                  seqblk(DSA_KV_HEADS * VT_ROWS, KB), _full(bias.shape)],
        out_specs=qblk(512),
        out_shape=jax.ShapeDtypeStruct((batch * seq, 512), F32),
        scratch_shapes=[pltpu.VMEM((IDX_HEADS * tq, LANES), BF16), pltpu.VMEM((DSA_HEADS * tq, LANES), BF16),
                        pltpu.VMEM((nkb, KB, tq), I32), pltpu.VMEM((nkb, KB, tq), F32),
                        pltpu.VMEM((1, tq), I32), pltpu.VMEM((DSA_HEADS, 8, tq), F32),
                        pltpu.VMEM((KB, DSA_HEADS * tq), BF16),
                        pltpu.VMEM((DSA_KV_HEADS, VT_ROWS, DSA_GROUP * tq), F32)],
        compiler_params=_cparams(("parallel", "arbitrary")),
    )(rel_bias, qidx, tail, kidxdup.reshape(batch * nkb, KB, LANES), qdsa, kdup.reshape(batch * nkb, KB, 256),
      vt, bias)


def _page_specs(n, width, index_of):
    return [pl.BlockSpec((None, None, width, PAGE_SIZE), functools.partial(index_of, i)) for i in range(n)]


def _pages_feature_major(cache):
    c = jnp.moveaxis(cache, 2, -1)
    return c.reshape(cache.shape[0], cache.shape[1], -1, PAGE_SIZE)


def _sample_score_kernel(pt_ref, q_ref, w_ref, *refs):
    pages, o_ref, kb_scr = refs[:-2], refs[-2], refs[-1]
    bi = pl.program_id(2)
    for i, pg in enumerate(pages):
        kb_scr[:, i * PAGE_SIZE:(i + 1) * PAGE_SIZE] = pg[...].astype(BF16)
    d = _dot(q_ref[...], kb_scr[...])
    sc = jnp.sum(jnp.maximum(d, 0.0) * w_ref[...], axis=0, keepdims=True) * IDX_SCALE
    o_ref[pl.ds(bi, 1), :] = sc


def _sample_scores(page_table, qidx3, widx3, cache_idx_t, pps):
    dec, n_pages = page_table.shape
    n_chunks = n_pages // pps
    grp = 8
    page_of = lambda i, bo, c, bi, pt: (0, pt[bo * grp + bi, c * pps + i], 0, 0)
    grid_spec = pltpu.PrefetchScalarGridSpec(
        num_scalar_prefetch=1,
        grid=(dec // grp, n_chunks, grp),
        in_specs=[pl.BlockSpec((None, IDX_HEADS, IDX_DIM), lambda bo, c, bi, pt: (bo * grp + bi, 0, 0)),
                  pl.BlockSpec((None, IDX_HEADS, 1), lambda bo, c, bi, pt: (bo * grp + bi, 0, 0))]
                 + _page_specs(pps, IDX_DIM, page_of),
        out_specs=pl.BlockSpec((grp, pps * PAGE_SIZE), lambda bo, c, bi, pt: (bo, c)),
        scratch_shapes=[pltpu.VMEM((IDX_DIM, pps * PAGE_SIZE), BF16)],
    )
    return pl.pallas_call(
        _sample_score_kernel, grid_spec=grid_spec,
        out_shape=jax.ShapeDtypeStruct((dec, n_pages * PAGE_SIZE), F32),
        compiler_params=_cparams(("arbitrary", "arbitrary", "arbitrary")),
    )(page_table, qidx3, widx3, *([cache_idx_t] * pps))


def _sample_select_kernel(sc_ref, qidx_ref, tail_ref, selp_ref, seln_ref, key_scr, *, n_sel, idx_bits):
    rows, past = sc_ref.shape
    nblk = past // LANES
    lane = lax.broadcasted_iota(I32, (rows, LANES), 1)
    tail = tail_ref[...]
    kidx2 = jnp.where(lane < 64, tail, pltpu.roll(tail, 64, 1))
    sc_new = jnp.zeros((rows, 1), F32)
    for h in range(IDX_HEADS):
        prod = qidx_ref[:, (h // 2) * 128:(h // 2 + 1) * 128].astype(F32) * kidx2.astype(BF16).astype(F32)
        mine = (lane < 64) if h % 2 == 0 else (lane >= 64)
        dot = jnp.sum(jnp.where(mine, prod, 0.0), axis=-1, keepdims=True)
        sc_new = sc_new + jnp.maximum(dot, 0.0) * tail[:, T_WIDX + h:T_WIDX + h + 1]
    key_new = _sort_key(sc_new * IDX_SCALE)

    for j in range(nblk):
        key_scr[j] = _sort_key(sc_ref[:, j * LANES:(j + 1) * LANES])

    def count(pred, pred_new):
        def body(j, c):
            return c + jnp.where(pred(key_scr[j], j), 1.0, 0.0)
        c = lax.fori_loop(0, nblk, body, jnp.zeros((rows, LANES), F32), unroll=4)
        return jnp.sum(c, axis=-1, keepdims=True) + jnp.where(pred_new, 1.0, 0.0)

    thr = _kth_key(lambda cand: count(lambda k, j: k >= cand, key_new >= cand), float(n_sel),
                   jnp.zeros((rows, 1), F32))
    cnt_gt = count(lambda k, j: k > thr, key_new > thr)
    need = float(n_sel) - cnt_gt

    def body(i, p):
        cand = p + (jnp.int32(1) << (idx_bits - 1 - i))
        c = count(lambda k, j: (k == thr) & (j * LANES + lane < cand), (key_new == thr) & (past < cand))
        return jnp.where(c < need, cand, p)

    last_tie = lax.fori_loop(0, idx_bits, body, jnp.zeros((rows, 1), I32))

    for j in range(nblk):
        k = key_scr[j]
        sel = (k > thr) | ((k == thr) & (j * LANES + lane <= last_tie))
        selp_ref[:, j * LANES:(j + 1) * LANES] = jnp.where(sel, 1.0, 0.0)
    sel_new = (key_new > thr) | ((key_new == thr) & (past <= last_tie))
    seln_ref[...] = jnp.broadcast_to(jnp.where(sel_new, 1.0, 0.0), (rows, LANES))


def _sample_select(scores, qidx, tail):
    dec, past = scores.shape
    n_sel = min(TOPK_MAX, (past + 1) // 4)
    idx_bits = max(1, past.bit_length())
    rows = 64 if dec % 64 == 0 else dec
    kern = functools.partial(_sample_select_kernel, n_sel=n_sel, idx_bits=idx_bits)
    return pl.pallas_call(
        kern,
        grid=(dec // rows,),
        in_specs=[pl.BlockSpec((rows, past), lambda i: (i, 0)), pl.BlockSpec((rows, 512), lambda i: (i, 0)),
                  pl.BlockSpec((rows, LANES), lambda i: (i, 0))],
        out_specs=[pl.BlockSpec((rows, past), lambda i: (i, 0)), pl.BlockSpec((rows, LANES), lambda i: (i, 0))],
        out_shape=[jax.ShapeDtypeStruct((dec, past), F32), jax.ShapeDtypeStruct((dec, LANES), F32)],
        scratch_shapes=[pltpu.VMEM((past // LANES, rows, LANES), I32)],
        compiler_params=_cparams(("parallel",)),
    )(scores, qidx, tail)


def _mla_sample_kernel(pt_ref, qn_ref, qr_ref, rown_ref, wuk_ref, wuv_ref, gk_ref, *refs):
    pages, o_ref, lat_scr, kr_scr, a_scr, m_scr, l_scr, acc_scr = (refs[:-7],) + tuple(refs[-7:])
    c = pl.program_id(1)
    n_chunks = pl.num_programs(1)
    heads = MLA_HEADS

    @pl.when(c == 0)
    def _():
        m_scr[...] = jnp.full(m_scr.shape, -jnp.inf, F32)
        l_scr[...] = jnp.zeros(l_scr.shape, F32)
        acc_scr[...] = jnp.zeros(acc_scr.shape, F32)
        qg = qn_ref[...] * gk_ref[...]
        own = (lax.broadcasted_iota(I32, (heads, heads * MLA_NOPE), 1) // MLA_NOPE
               == lax.broadcasted_iota(I32, (heads, heads * MLA_NOPE), 0))
        a_scr[...] = _dot(jnp.where(own, qg, 0.0).astype(BF16), wuk_ref[...]).astype(BF16)

    qr = qr_ref[...]

    def scores(latb, krb):
        knt = _dot(wuk_ref[...], latb)
        ms = jnp.mean((knt * knt).reshape(heads, MLA_NOPE, latb.shape[1]), axis=1)
        return (_dot(a_scr[...], latb) * lax.rsqrt(ms + EPS) + _dot(qr, krb)) * MLA_SCALE

    def update(latb, krb, n_valid):
        n = latb.shape[1]
        sub = min(n, 512)
        s = jnp.concatenate([scores(latb[:, j:j + sub], krb[:, j:j + sub]) for j in range(0, n, sub)], axis=1)
        if n_valid < n:
            s = jnp.where(lax.broadcasted_iota(I32, s.shape, 1) < n_valid, s, -jnp.inf)
        m_old = m_scr[...]
        m_new = jnp.maximum(m_old, jnp.max(s, axis=-1, keepdims=True))
        alpha = jnp.exp(m_old - m_new)
        p = jnp.exp(s - m_new)
        l_scr[...] = alpha * l_scr[...] + jnp.sum(p, axis=-1, keepdims=True)
        m_scr[...] = m_new
        acc_scr[...] = alpha * acc_scr[...] + _dot_nt(p.astype(BF16), latb)

    for i, pg in enumerate(pages):
        lat_scr[:, i * PAGE_SIZE:(i + 1) * PAGE_SIZE] = pg[0:MLA_KV_RANK, :].astype(BF16)
        kr_scr[:, i * PAGE_SIZE:(i + 1) * PAGE_SIZE] = pg[MLA_KV_RANK:MLA_ROW, :].astype(BF16)
    update(lat_scr[...], kr_scr[...], lat_scr.shape[1])

    @pl.when(c == n_chunks - 1)
    def _():
        rn = jnp.broadcast_to(rown_ref[...], (MLA_ROW, PAGE_SIZE))
        update(rn[0:MLA_KV_RANK, :].astype(BF16), rn[MLA_KV_RANK:MLA_ROW, :].astype(BF16), 1)
        o_lat = (acc_scr[...] / l_scr[...]).astype(BF16)
        full = _dot(o_lat, wuv_ref[...])
        hd = lax.broadcasted_iota(I32, full.shape, 1) // MLA_V
        own = hd == lax.broadcasted_iota(I32, full.shape, 0)
        o_ref[...] = jnp.sum(jnp.where(own, full, 0.0), axis=0, keepdims=True)


def _mla_sample(page_table, qn_row, qr, row_new, wuk_t, wuv, gk_row, cache_mla, pps):
    dec, n_pages = page_table.shape
    n_chunks = n_pages // pps
    page_of = lambda i, b, c, pt: (0, pt[b, c * pps + i], 0, 0)
    grid_spec = pltpu.PrefetchScalarGridSpec(
        num_scalar_prefetch=1,
        grid=(dec, n_chunks),
        in_specs=[pl.BlockSpec((None, 1, MLA_HEADS * MLA_NOPE), lambda b, c, pt: (b, 0, 0)),
                  pl.BlockSpec((None, MLA_HEADS, MLA_ROPE), lambda b, c, pt: (b, 0, 0)),
                  pl.BlockSpec((None, MLA_ROW, 1), lambda b, c, pt: (b, 0, 0)),
                  pl.BlockSpec(wuk_t.shape, lambda b, c, pt: (0, 0)),
                  pl.BlockSpec(wuv.shape, lambda b, c, pt: (0, 0)),
                  pl.BlockSpec(gk_row.shape, lambda b, c, pt: (0, 0))]
                 + _page_specs(pps, MLA_ROW, page_of),
        out_specs=pl.BlockSpec((None, 1, MLA_HEADS * MLA_V), lambda b, c, pt: (b, 0, 0)),
        scratch_shapes=[pltpu.VMEM((MLA_KV_RANK, pps * PAGE_SIZE), BF16), pltpu.VMEM((MLA_ROPE, pps * PAGE_SIZE), BF16),
                        pltpu.VMEM((MLA_HEADS, MLA_KV_RANK), BF16),
                        pltpu.VMEM((MLA_HEADS, 1), F32), pltpu.VMEM((MLA_HEADS, 1), F32),
                        pltpu.VMEM((MLA_HEADS, MLA_KV_RANK), F32)],
    )
    return pl.pallas_call(
        _mla_sample_kernel, grid_spec=grid_spec,
        out_shape=jax.ShapeDtypeStruct((dec, 1, MLA_HEADS * MLA_V), F32),
        compiler_params=_cparams(("parallel", "arbitrary")),
    )(page_table, qn_row, qr, row_new, wuk_t, wuv, gk_row, *([cache_mla] * pps))


def _dsa_sample_kernel(pt_ref, q_ref, kvn_ref, selp_ref, seln_ref, biasp_ref, biasn_ref, *refs):
    pages, o_ref, k_scr, v_scr, m_scr, l_scr, acc_scr = refs[:-6], refs[-6], refs[-5], refs[-4], refs[-3], refs[-2], refs[-1]
    c = pl.program_id(1)
    n_chunks = pl.num_programs(1)

    @pl.when(c == 0)
    def _():
        m_scr[...] = jnp.full(m_scr.shape, NEG_BIG, F32)
        l_scr[...] = jnp.zeros(l_scr.shape, F32)
        acc_scr[...] = jnp.zeros(acc_scr.shape, F32)

    q = q_ref[...]

    def update(s, sel, pv_of):
        sh = jnp.where(sel, s, NEG_BIG)
        m_old = m_scr[...]
        m_new = jnp.maximum(m_old, jnp.max(sh, axis=-1, keepdims=True))
        alpha = jnp.exp(m_old - m_new)
        p = jnp.where(sel, jnp.exp(sh - m_new), 0.0)
        l_scr[...] = alpha * l_scr[...] + jnp.sum(p, axis=-1, keepdims=True)
        m_scr[...] = m_new
        acc_scr[...] = alpha * acc_scr[...] + pv_of(p)

    for i, pg in enumerate(pages):
        k_scr[:, i * PAGE_SIZE:(i + 1) * PAGE_SIZE] = pg[0:128, :].astype(BF16)
        v_scr[:, i * PAGE_SIZE:(i + 1) * PAGE_SIZE] = pg[128:256, :].astype(BF16)
    s = _dot(q.astype(BF16), k_scr[...]) + biasp_ref[...]
    update(s, selp_ref[...] > 0.5, lambda p: _dot_nt(p.astype(BF16), v_scr[...]))

    @pl.when(c == n_chunks - 1)
    def _():
        kvn = kvn_ref[...]
        kn = kvn[:, 0:128].astype(BF16).astype(F32)
        vn = kvn[:, 128:256].astype(BF16).astype(F32)
        s_new = jnp.sum(q.astype(F32) * kn, axis=-1, keepdims=True) + biasn_ref[...]
        update(s_new, seln_ref[:, 0:1] > 0.5, lambda p: p.astype(BF16).astype(F32) * vn)
        o = acc_scr[...] / l_scr[...]
        lane = lax.broadcasted_iota(I32, o.shape, 1)
        row = lax.broadcasted_iota(I32, o.shape, 0)
        own = (lane // DSA_HEAD_DIM) == (row // DSA_GROUP)
        o_ref[...] = jnp.where(own, o, 0.0)


def _dsa_sample(page_table, qs, kv_new, selp, seln, bias_past, bias_new, cache_kv4, pps):
    dec, n_pages = page_table.shape
    n_chunks = n_pages // pps
    chunk = pps * PAGE_SIZE
    page_of = lambda i, b, c, pt: (0, pt[b, c * pps + i], 0, 0)
    grid_spec = pltpu.PrefetchScalarGridSpec(
        num_scalar_prefetch=1,
        grid=(dec, n_chunks),
        in_specs=[pl.BlockSpec((None, DSA_HEADS, LANES), lambda b, c, pt: (b, 0, 0)),
                  pl.BlockSpec((None, 1, 256), lambda b, c, pt: (b, 0, 0)),
                  pl.BlockSpec((None, 1, chunk), lambda b, c, pt: (b, 0, c)),
                  pl.BlockSpec((None, 1, LANES), lambda b, c, pt: (b, 0, 0)),
                  pl.BlockSpec((DSA_HEADS, chunk), lambda b, c, pt: (0, c)),
                  pl.BlockSpec((DSA_HEADS, 1), lambda b, c, pt: (0, 0))]
                 + _page_specs(pps, 256, page_of),
        out_specs=pl.BlockSpec((None, DSA_HEADS, LANES), lambda b, c, pt: (b, 0, 0)),
        scratch_shapes=[pltpu.VMEM((128, chunk), BF16), pltpu.VMEM((128, chunk), BF16),
                        pltpu.VMEM((DSA_HEADS, 1), F32), pltpu.VMEM((DSA_HEADS, 1), F32),
                        pltpu.VMEM((DSA_HEADS, LANES), F32)],
    )
    return pl.pallas_call(
        _dsa_sample_kernel, grid_spec=grid_spec,
        out_shape=jax.ShapeDtypeStruct((dec, DSA_HEADS, LANES), F32),
        compiler_params=_cparams(("parallel", "arbitrary")),
    )(page_table, qs, kv_new, selp, seln, bias_past, bias_new, *([cache_kv4] * pps))


def _out_kernel(x_ref, oa_ref, ob_ref, gm_ref, sf_ref, scf_ref, gf_ref, gout_ref, gffn_ref,
                wout_ref, wg_ref, wu_ref, wdown_ref, y_ref, *, ff_chunk):
    def rms(v, g):
        return v * lax.rsqrt(jnp.mean(v * v, axis=-1, keepdims=True) + EPS) * g

    half = oa_ref.shape[1]
    na = rms(oa_ref[...], gout_ref[:, 0:half]).astype(BF16)
    nb = rms(ob_ref[...], gout_ref[:, half:2 * half]).astype(BF16)
    mix = _dot(na, wout_ref[0:half, :]) + _dot(nb, wout_ref[half:2 * half, :])
    x1 = x_ref[...] + gm_ref[...] * mix
    hb = (rms(x1, gffn_ref[...]) * (1.0 + scf_ref[...]) + sf_ref[...]).astype(BF16)
    acc = jnp.zeros(x1.shape, F32)
    for j in range(D_FF // ff_chunk):
        g = _dot(hb, wg_ref[:, j * ff_chunk:(j + 1) * ff_chunk])
        u = _dot(hb, wu_ref[:, j * ff_chunk:(j + 1) * ff_chunk])
        a = (g / (1.0 + jnp.exp(-g))) * u
        acc = acc + _dot(a.astype(BF16), wdown_ref[j * ff_chunk:(j + 1) * ff_chunk, :])
    y_ref[...] = x1 + gf_ref[...] * acc


def _out_stage(x2d, oa, ob, gate_m, shift_f, scale_f, gate_f, wp, tm, per_token, tiles_per_seq):
    t_total = x2d.shape[0]
    if per_token:
        mod_spec = pl.BlockSpec((tm, D_MODEL), lambda i: (i, 0))
    else:
        mod_spec = pl.BlockSpec((None, 1, D_MODEL), lambda i: (i // tiles_per_seq, 0, 0))
    consts = [wp["g_out"], wp["g_ffn"], wp["w_out"], wp["w_gate"], wp["w_up"], wp["w_down"]]
    tok = lambda w: pl.BlockSpec((tm, w), lambda i: (i, 0))
    return pl.pallas_call(
        functools.partial(_out_kernel, ff_chunk=256),
        grid=(t_total // tm,),
        in_specs=[tok(D_MODEL), tok(512), tok(512), mod_spec, mod_spec, mod_spec, mod_spec]
                 + [_full(c.shape) for c in consts],
        out_specs=tok(D_MODEL),
        out_shape=jax.ShapeDtypeStruct((t_total, D_MODEL), F32),
        compiler_params=_cparams(("parallel",)),
    )(x2d, oa, ob, gate_m, shift_f, scale_f, gate_f, *consts)


def _prep_weights(w_in, g_norm_mix, g_norm_ffn, g_q_lat, w_uq, g_kv_lat, w_ukv, g_mla_q_nope, g_mla_q_rope,
                  g_mla_k_nope, g_mla_k_rope, g_dsa_q, g_dsa_k, g_out, w_out, w_ffn_in, w_ffn_out):
    splits = np.cumsum([MLA_Q_RANK, MLA_KV_RANK, MLA_ROPE, 512, 128, 128, 512, IDX_DIM, IDX_HEADS])
    s = [0] + splits.tolist()
    col = lambda i: w_in[:, s[i]:s[i + 1]]
    pad = jnp.zeros((D_MODEL, C_END - C_TAIL - IDX_DIM - MLA_ROPE - IDX_HEADS), w_in.dtype)
    w_in_r = jnp.concatenate([col(0), col(1), col(3), col(4), col(5), col(6), col(7), col(2), col(8), pad], axis=1)
    zq = jnp.zeros((MLA_Q_RANK, MLA_HEADS, LANES - MLA_NOPE - MLA_ROPE), w_uq.dtype)
    w_uq_cat = jnp.concatenate([w_uq, zq], axis=2).reshape(MLA_Q_RANK, MLA_HEADS * LANES)
    zk = jnp.zeros((MLA_KV_RANK, MLA_HEADS, LANES - MLA_NOPE), w_ukv.dtype)
    w_uk_cat = jnp.concatenate([w_ukv[:, :, :MLA_NOPE], zk], axis=2).reshape(MLA_KV_RANK, MLA_HEADS * LANES)
    w_uv = w_ukv[:, :, MLA_NOPE:].reshape(MLA_KV_RANK, MLA_HEADS * MLA_V)
    zv = jnp.zeros((MLA_KV_RANK, MLA_HEADS, LANES - MLA_V), w_ukv.dtype)
    w_uv_pad = jnp.concatenate([w_ukv[:, :, MLA_NOPE:], zv], axis=2).reshape(MLA_KV_RANK, MLA_HEADS * LANES)
    w_uk_t = w_ukv[:, :, :MLA_NOPE].reshape(MLA_KV_RANK, MLA_HEADS * MLA_NOPE).T
    z32 = jnp.zeros((LANES - MLA_NOPE - MLA_ROPE,), F32)
    z64 = jnp.zeros((LANES - MLA_NOPE,), F32)
    g_qcat = jnp.tile(jnp.concatenate([g_mla_q_nope, g_mla_q_rope, z32]), MLA_HEADS)
    g_kcat = jnp.tile(jnp.concatenate([g_mla_k_nope, z64]), MLA_HEADS)
    g_tail = jnp.concatenate([jnp.zeros((T_ROPE,), F32), g_mla_k_rope, z32])
    row = lambda v: v.reshape(1, -1).astype(F32)
    return {
        "g_mix": row(g_norm_mix), "g_ffn": row(g_norm_ffn), "w_in": w_in_r.astype(BF16),
        "g_qlat": row(g_q_lat), "w_uq": w_uq_cat.astype(BF16), "g_qcat": row(g_qcat),
        "g_kvlat": row(g_kv_lat), "w_ukv": jnp.concatenate([w_uk_cat, w_uv_pad], axis=1).astype(BF16),
        "g_kcat": row(g_kcat), "g_tail": row(g_tail),
        "g_q": row(jnp.tile(g_dsa_q, DSA_HEADS)), "g_k": row(jnp.tile(g_dsa_k, DSA_KV_HEADS)),
        "m_cat": _group_matrix(256, [(0, 64), (64, 96), (128, 192), (192, 224)]),
        "m_64": _group_matrix(256, [(0, 64), (64, 128), (128, 192), (192, 256)]),
        "w_uk_t": w_uk_t.astype(BF16), "w_uv": w_uv.astype(BF16), "g_k_nope": g_mla_k_nope,
        "g_out": row(g_out), "w_out": w_out.astype(BF16),
        "w_gate": w_ffn_in[:, :D_FF].astype(BF16), "w_up": w_ffn_in[:, D_FF:].astype(BF16),
        "w_down": w_ffn_out.astype(BF16),
    }


def _rope_tables(pos):
    freq = ROPE_THETA ** (-jnp.arange(HALF_ROPE, dtype=F32) / HALF_ROPE)
    ang = pos.astype(F32)[:, None] * freq[None, :]
    cos, sin = jnp.cos(ang), jnp.sin(ang)
    n = pos.shape[0]
    ones = jnp.ones((n, T_ROPE), F32)
    zeros = jnp.zeros((n, T_ROPE), F32)
    tail1 = jnp.ones((n, LANES - T_ROPE - MLA_ROPE), F32)
    tail0 = jnp.zeros((n, LANES - T_ROPE - MLA_ROPE), F32)
    return (jnp.concatenate([ones, cos, cos, tail1], axis=1),
            jnp.concatenate([zeros, -sin, sin, tail0], axis=1))


def _pick_tile(n, prefs):
    for t in prefs:
        if n % t == 0:
            return t
    return n


def kernel(x_prompt, x_sample, c_prompt, c_sample, cache_mla, cache_kv, cache_idx, page_table, rel_bias, w_ada, b_ada, g_norm_mix, g_norm_ffn, w_in, g_q_lat, w_uq, g_kv_lat, w_ukv, g_mla_q_nope, g_mla_q_rope, g_mla_k_nope, g_mla_k_rope, g_dsa_q, g_dsa_k, g_out, w_out, w_ffn_in, w_ffn_out):
    assert w_ada.shape[0] == 1 and x_sample.shape[1] == 1, "one layer, one new token per sample"
    batch, seq, _ = x_prompt.shape
    dec = x_sample.shape[0]
    n_pages = page_table.shape[1]
    past = n_pages * PAGE_SIZE
    wp = _prep_weights(w_in[0], g_norm_mix[0], g_norm_ffn[0], g_q_lat[0], w_uq[0], g_kv_lat[0], w_ukv[0],
                       g_mla_q_nope[0], g_mla_q_rope[0], g_mla_k_nope[0], g_mla_k_rope[0], g_dsa_q[0], g_dsa_k[0],
                       g_out[0], w_out[0], w_ffn_in[0], w_ffn_out[0])

    mod = _ada(jnp.concatenate([c_prompt, c_sample], axis=0), w_ada[0], b_ada[0])
    mods_p = [m.reshape(batch, 1, D_MODEL) for m in jnp.split(mod[:batch], 6, axis=-1)]
    mods_s = jnp.split(mod[batch:], 6, axis=-1)

    tm = _pick_tile(seq, (512, 256, 128))
    xp2 = x_prompt.reshape(batch * seq, D_MODEL)
    cos_p, sin_p = _rope_tables(jnp.arange(seq))
    pin = _in_stage(xp2, mods_p[1], mods_p[0], cos_p, sin_p, wp, tm, per_token=False)
    o_mla_p = _mla_prompt(pin["qcat"], pin["kcat"], pin["vmla"], batch, seq,
                          _pick_tile(seq, (256, 128)), _pick_tile(seq, (512, 256, 128)))
    o_dsa_p = _dsa_prompt(rel_bias, pin["qidx"], pin["tail"], pin["kidxdup"], pin["qdsa"], pin["kdup"], pin["vt"],
                          batch, seq, _pick_tile(seq, (256, 128)))
    y_p = _out_stage(xp2, o_mla_p, o_dsa_p, mods_p[2], mods_p[3], mods_p[4], mods_p[5], wp, tm,
                     per_token=False, tiles_per_seq=seq // tm)

    xs2 = x_sample.reshape(dec, D_MODEL)
    cos_s, sin_s = _rope_tables(jnp.full((1,), past))
    sin_ = _in_stage(xs2, mods_s[1], mods_s[0], cos_s, sin_s, wp, dec, per_token=True)
    pps = _pick_tile(n_pages, (16, 8, 4, 2))
    pps_wide = _pick_tile(n_pages, (32, 16, 8, 4, 2))
    qidx3 = sin_["qidx"].reshape(dec, IDX_HEADS, IDX_DIM)
    widx3 = sin_["tail"][:, T_WIDX:T_WIDX + IDX_HEADS].reshape(dec, IDX_HEADS, 1)
    scores = _sample_scores(page_table, qidx3, widx3, _pages_feature_major(cache_idx), pps_wide)
    selp, seln = _sample_select(scores, sin_["qidx"], sin_["tail"])
    qc = sin_["qcat"].reshape(dec, MLA_HEADS, LANES)
    qn_row = qc[:, :, 0:MLA_NOPE].astype(F32).reshape(dec, 1, MLA_HEADS * MLA_NOPE)
    qr = qc[:, :, T_ROPE:T_ROPE + MLA_ROPE]
    gk_row = jnp.tile(wp["g_k_nope"], MLA_HEADS).reshape(1, MLA_HEADS * MLA_NOPE)
    o_mla_s = _mla_sample(page_table, qn_row, qr, sin_["row"].reshape(dec, MLA_ROW, 1), wp["w_uk_t"], wp["w_uv"],
                          gk_row, _pages_feature_major(cache_mla), pps).reshape(dec, MLA_HEADS * MLA_V)
    qd = sin_["qdsa"].astype(F32).reshape(dec, DSA_HEADS, DSA_HEAD_DIM)
    on_c = (jnp.arange(DSA_KV_HEADS)[None, :] == (jnp.arange(DSA_HEADS) // DSA_GROUP)[:, None]).astype(F32)
    qd2 = (qd[:, :, None, :] * on_c[None, :, :, None]).reshape(dec, DSA_HEADS, LANES)
    bias_past = _bias_table(rel_bias, 1, past, 0, -1, past).reshape(DSA_HEADS, past)
    bias_new = _bias_table(rel_bias, 1, LANES, 0, 0, 0)[:, 0, 0:1]
    o8 = _dsa_sample(page_table, qd2, sin_["kv"].reshape(dec, 1, 256), selp.reshape(dec, 1, past),
                     seln.reshape(dec, 1, LANES), bias_past, bias_new, _pages_feature_major(cache_kv), pps_wide)
    o_dsa_s = o8.reshape(dec, DSA_HEADS, DSA_KV_HEADS, DSA_HEAD_DIM).sum(axis=2).reshape(dec, DSA_HEADS * DSA_HEAD_DIM)
    y_s = _out_stage(xs2, o_mla_s, o_dsa_s, mods_s[2], mods_s[3], mods_s[4], mods_s[5], wp, dec,
                     per_token=True, tiles_per_seq=1)

    return (y_p.reshape(batch, seq, D_MODEL), y_s.reshape(dec, 1, D_MODEL),
            pin["row"].reshape(1, batch, seq, MLA_ROW),
            pin["kv"].reshape(1, batch, seq, 2, DSA_KV_HEADS, DSA_HEAD_DIM),
            pin["kidx"].reshape(1, batch, seq, IDX_DIM),
            sin_["row"].reshape(1, dec, 1, MLA_ROW),
            sin_["kv"].reshape(1, dec, 1, 2, DSA_KV_HEADS, DSA_HEAD_DIM),
            sin_["kidx"].reshape(1, dec, 1, IDX_DIM))
```

```python
import functools
import math

import numpy as np
import jax
import jax.numpy as jnp
from jax import lax
from jax.experimental import pallas as pl
from jax.experimental.pallas import tpu as pltpu

D_MODEL = 1024
PAGE_SIZE = 128
MLA_HEADS = 8
MLA_NOPE = 64
MLA_ROPE = 32
MLA_V = 64
MLA_Q_RANK = 384
MLA_KV_RANK = 256
MLA_ROW = MLA_KV_RANK + MLA_ROPE
DSA_HEADS = 8
DSA_KV_HEADS = 2
DSA_HEAD_DIM = 64
DSA_GROUP = DSA_HEADS // DSA_KV_HEADS
IDX_HEADS = 8
IDX_DIM = 64
TOPK_MAX = 256
N_BUCKETS = 32
MAX_DISTANCE = 128
ROPE_THETA = 10000.0
D_FF = ((8 * D_MODEL + 3 * 256 - 1) // (3 * 256)) * 256
EPS = 1e-6
MLA_SCALE = (MLA_NOPE + MLA_ROPE) ** -0.5
DSA_SCALE = DSA_HEAD_DIM ** -0.5
IDX_SCALE = (IDX_DIM * IDX_HEADS) ** -0.5

LANES = 128
VMEM_LIMIT = 56 * 1024 * 1024

F32 = jnp.float32
BF16 = jnp.bfloat16
I32 = jnp.int32
NEG_BIG = -1e30
INT_MIN = -(2 ** 31)

C_QLAT = 0
C_KVLAT = C_QLAT + MLA_Q_RANK
C_Q = C_KVLAT + MLA_KV_RANK
C_K = C_Q + DSA_HEADS * DSA_HEAD_DIM
C_V = C_K + DSA_KV_HEADS * DSA_HEAD_DIM
C_QIDX = C_V + DSA_KV_HEADS * DSA_HEAD_DIM
C_TAIL = C_QIDX + IDX_HEADS * IDX_DIM
C_END = C_TAIL + LANES
T_ROPE = IDX_DIM
T_WIDX = IDX_DIM + MLA_ROPE
HALF_ROPE = MLA_ROPE // 2
KB = 128


def _dot(a, b):
    return jnp.dot(a, b, preferred_element_type=F32)


def _dot_nt(a, b):
    return lax.dot_general(a, b, (((1,), (1,)), ((), ())), preferred_element_type=F32)


def _split(a):
    hi = a.astype(BF16)
    lo = (a - hi.astype(F32)).astype(BF16)
    return hi, lo


def _dot3(a, b):
    ah, al = _split(a)
    bh, bl = _split(b)
    return _dot(ah, bh) + (_dot(al, bh) + _dot(ah, bl))


def _cparams(sem):
    return pltpu.CompilerParams(dimension_semantics=sem, vmem_limit_bytes=VMEM_LIMIT)


def _full(shape):
    n = len(shape)
    return pl.BlockSpec(shape, lambda *a, _n=n: (0,) * _n, pipeline_mode=pl.Buffered(1))


def _ada_kernel(c_ref, w_ref, b_ref, o_ref):
    c = c_ref[...]
    s = c / (1.0 + jnp.exp(-c))
    o_ref[...] = _dot3(s, w_ref[...]) + b_ref[...]


def _ada(c, w, b):
    n = c.shape[0]
    return pl.pallas_call(
        _ada_kernel,
        grid=(6,),
        in_specs=[pl.BlockSpec((n, D_MODEL), lambda j: (0, 0)),
                  pl.BlockSpec((D_MODEL, D_MODEL), lambda j: (0, j)),
                  pl.BlockSpec((1, D_MODEL), lambda j: (0, j))],
        out_specs=pl.BlockSpec((n, D_MODEL), lambda j: (0, j)),
        out_shape=jax.ShapeDtypeStruct((n, 6 * D_MODEL), F32),
        compiler_params=_cparams(("arbitrary",)),
    )(c, w, b.reshape(1, -1))


def _group_matrix(width, groups):
    g = np.zeros((width, width), np.float32)
    for lo, hi in groups:
        g[lo:hi, lo:hi] = 1.0 / (hi - lo)
    return jnp.asarray(g, BF16)


def _in_kernel(x_ref, sc_ref, sh_ref, cos_ref, sin_ref, gmix_ref, win_ref, gqlat_ref, wuq_ref, gqcat_ref,
               gkvlat_ref, wukv_ref, gkcat_ref, gtail_ref, gq_ref, gk_ref, mcat_ref, m64_ref,
               qcat_o, kcat_o, vmla_o, row_o, kv_o, kidx_o, tail_o, kidxdup_o, qdsa_o, kdup_o, vone_o, qidx_o):
    tm = x_ref.shape[0]
    x = x_ref[...]
    h = x * lax.rsqrt(jnp.mean(x * x, axis=-1, keepdims=True) + EPS) * gmix_ref[...]
    hb = (h * (1.0 + sc_ref[...]) + sh_ref[...]).astype(BF16)

    def proj(lo, hi):
        return _dot(hb, win_ref[:, lo:hi])

    cos = cos_ref[...]
    sin = sin_ref[...]
    lane = lax.broadcasted_iota(I32, (tm, LANES), 1)
    first_half = lane < T_ROPE + HALF_ROPE
    rope_lanes = (lane >= T_ROPE) & (lane < T_ROPE + MLA_ROPE)
    low64 = lane < 64

    def rope(xh):
        rot = jnp.where(first_half, pltpu.roll(xh, LANES - HALF_ROPE, 1), pltpu.roll(xh, HALF_ROPE, 1))
        return xh * cos + rot * sin

    def gnorm(blk, m_ref):
        ms = _dot((blk * blk).astype(BF16), m_ref[...])
        return blk * lax.rsqrt(ms + EPS)

    ql = proj(C_QLAT, C_KVLAT)
    qln = ql * lax.rsqrt(jnp.mean(ql * ql, axis=-1, keepdims=True) + EPS) * gqlat_ref[...]
    qm = _dot(qln.astype(BF16), wuq_ref[...])
    for j in range(4):
        y = gnorm(qm[:, 256 * j:256 * j + 256], mcat_ref) * gqcat_ref[:, 256 * j:256 * j + 256]
        for t in range(2):
            qcat_o[:, 256 * j + 128 * t:256 * j + 128 * t + 128] = rope(y[:, 128 * t:128 * t + 128]).astype(BF16)

    tail = proj(C_TAIL, C_END)
    ssq = jnp.sum(jnp.where(rope_lanes, tail * tail, 0.0), axis=-1, keepdims=True) * (1.0 / MLA_ROPE)
    tn = jnp.where(rope_lanes, tail * lax.rsqrt(ssq + EPS) * gtail_ref[...], tail)
    tr = rope(tn)
    kidx_o[...] = tail[:, 0:IDX_DIM]
    tail_o[...] = tail
    kidxdup_o[...] = jnp.where(low64, tail, pltpu.roll(tail, 64, 1)).astype(BF16)
    krope = jnp.where(rope_lanes, tr, 0.0)

    kvl = proj(C_KVLAT, C_Q)
    lat = kvl * lax.rsqrt(jnp.mean(kvl * kvl, axis=-1, keepdims=True) + EPS) * gkvlat_ref[...]
    row_o[:, 0:MLA_KV_RANK] = lat
    row_o[:, MLA_KV_RANK:MLA_ROW] = tr[:, T_ROPE:T_ROPE + MLA_ROPE]
    latb = lat.astype(BF16)
    kx = _dot(latb, wukv_ref[:, 0:1024])
    for j in range(4):
        y = gnorm(kx[:, 256 * j:256 * j + 256], mcat_ref) * gkcat_ref[:, 256 * j:256 * j + 256]
        for t in range(2):
            kcat_o[:, 256 * j + 128 * t:256 * j + 128 * t + 128] = (y[:, 128 * t:128 * t + 128] + krope).astype(BF16)
    ones_half = (lax.broadcasted_iota(I32, (tm, 1024), 1) % LANES) >= MLA_V
    vmla_o[...] = jnp.where(ones_half, 1.0, _dot(latb, wukv_ref[:, 1024:2048])).astype(BF16)

    q = proj(C_Q, C_K)
    for j in range(2):
        qdsa_o[:, 256 * j:256 * j + 256] = (gnorm(q[:, 256 * j:256 * j + 256], m64_ref)
                                            * gq_ref[:, 256 * j:256 * j + 256] * DSA_SCALE).astype(BF16)
    k = proj(C_K, C_V)
    ms = _dot((k * k).astype(BF16), m64_ref[0:128, 0:128])
    kn = k * lax.rsqrt(ms + EPS) * gk_ref[...]
    v = proj(C_V, C_QIDX)
    kv_o[:, 0:128] = kn
    kv_o[:, 128:256] = v
    kr = pltpu.roll(kn, 64, 1)
    kdup_o[:, 0:128] = jnp.where(low64, kn, kr).astype(BF16)
    kdup_o[:, 128:256] = jnp.where(low64, kr, kn).astype(BF16)
    vone_o[:, 0:128] = jnp.where(low64, v, 1.0).astype(BF16)
    vone_o[:, 128:256] = jnp.where(low64, pltpu.roll(v, 64, 1), 1.0).astype(BF16)
    qidx_o[...] = proj(C_QIDX, C_TAIL).astype(BF16)


_IN_OUT_WIDTHS = (("qcat", 1024, BF16), ("kcat", 1024, BF16), ("vmla", 1024, BF16), ("row", MLA_ROW, F32),
                  ("kv", 256, F32), ("kidx", IDX_DIM, F32), ("tail", LANES, F32), ("kidxdup", LANES, BF16),
                  ("qdsa", 512, BF16), ("kdup", 256, BF16), ("vone", 256, BF16), ("qidx", 512, BF16))


def _in_stage(x2d, scale, shift, cos, sin, wp, tm, per_token):
    t_total = x2d.shape[0]
    n_tiles = t_total // tm
    if per_token:
        mod_spec = pl.BlockSpec((tm, D_MODEL), lambda i: (i, 0))
        tab_spec = pl.BlockSpec((1, LANES), lambda i: (0, 0))
    else:
        tiles_per_seq = cos.shape[0] // tm
        mod_spec = pl.BlockSpec((None, 1, D_MODEL), lambda i: (i // tiles_per_seq, 0, 0))
        tab_spec = pl.BlockSpec((tm, LANES), lambda i: (i % tiles_per_seq, 0))
    consts = [wp["g_mix"], wp["w_in"], wp["g_qlat"], wp["w_uq"], wp["g_qcat"], wp["g_kvlat"], wp["w_ukv"],
              wp["g_kcat"], wp["g_tail"], wp["g_q"], wp["g_k"], wp["m_cat"], wp["m_64"]]
    in_specs = ([pl.BlockSpec((tm, D_MODEL), lambda i: (i, 0)), mod_spec, mod_spec, tab_spec, tab_spec]
                + [_full(c.shape) for c in consts])
    out_specs = [pl.BlockSpec((tm, w), lambda i: (i, 0)) for _, w, _ in _IN_OUT_WIDTHS]
    out_shape = [jax.ShapeDtypeStruct((t_total, w), dt) for _, w, dt in _IN_OUT_WIDTHS]
    outs = pl.pallas_call(
        _in_kernel, grid=(n_tiles,), in_specs=in_specs, out_specs=out_specs, out_shape=out_shape,
        compiler_params=_cparams(("parallel",)),
    )(x2d, scale, shift, cos, sin, *consts)
    return {name: o for (name, _, _), o in zip(_IN_OUT_WIDTHS, outs)}


def _mla_prompt_kernel(q_ref, k_ref, v_ref, o_ref, m_scr, acc_scr, s_scr, *, tk):
    tq = q_ref.shape[0]
    qi = pl.program_id(1)
    n_chunks = (qi * tq + tq + tk - 1) // tk
    m_scr[...] = jnp.full(m_scr.shape, -jnp.inf, F32)
    acc_scr[...] = jnp.zeros(acc_scr.shape, F32)
    low64 = lax.broadcasted_iota(I32, (tq, LANES), 1) < 64
    qpos = qi * tq + lax.broadcasted_iota(I32, (tq, tk), 0)
    kcol = lax.broadcasted_iota(I32, (tq, tk), 1)
    exp_scale = MLA_SCALE * math.log2(math.e)

    def raw_scores(c, h, masked):
        off = pl.multiple_of(c * tk, tk)
        s = _dot_nt(q_ref[:, h * 128:(h + 1) * 128], k_ref[pl.ds(off, tk), h * 128:(h + 1) * 128])
        if masked:
            s = jnp.where(c * tk + kcol <= qpos, s, -jnp.inf)
        return s

    def pass_max(c, masked):
        for h in range(MLA_HEADS):
            s = raw_scores(c, h, masked)
            s_scr[c, h] = s
            m = m_scr[h]
            for j in range(tk // LANES):
                m = jnp.maximum(m, s[:, j * LANES:(j + 1) * LANES])
            m_scr[h] = m

    def pass_sum(c, masked):
        off = pl.multiple_of(c * tk, tk)
        for h in range(MLA_HEADS):
            s = s_scr[c, h]
            m = m_scr[h]
            p = jnp.concatenate([jnp.exp2((s[:, j * LANES:(j + 1) * LANES] - m) * exp_scale)
                                 for j in range(tk // LANES)], axis=1).astype(BF16)
            acc_scr[h] += _dot(p, v_ref[pl.ds(off, tk), h * 128:(h + 1) * 128])

    def run(step):
        def body(c, carry):
            step(c, False)
            return carry

        lax.fori_loop(0, n_chunks - 1, body, 0)
        step(n_chunks - 1, True)

    run(pass_max)
    for h in range(MLA_HEADS):
        m_scr[h] = jnp.broadcast_to(jnp.max(m_scr[h], axis=-1, keepdims=True), (tq, LANES))
    run(pass_sum)
    for p in range(MLA_HEADS // 2):
        even, odd = acc_scr[2 * p], acc_scr[2 * p + 1]
        o_ref[:, 128 * p:128 * p + 128] = jnp.where(low64, even / pltpu.roll(even, 64, 1),
                                                    pltpu.roll(odd, 64, 1) / odd)


def _mla_prompt(qcat, kcat, vmla, batch, seq, tq, tk):
    nq = seq // tq
    return pl.pallas_call(
        functools.partial(_mla_prompt_kernel, tk=tk),
        grid=(batch, nq),
        in_specs=[pl.BlockSpec((tq, 1024), lambda b, i: (b * nq + i, 0)),
                  pl.BlockSpec((seq, 1024), lambda b, i: (b, 0)),
                  pl.BlockSpec((seq, 1024), lambda b, i: (b, 0))],
        out_specs=pl.BlockSpec((tq, 512), lambda b, i: (b * nq + i, 0)),
        out_shape=jax.ShapeDtypeStruct((batch * seq, 512), F32),
        scratch_shapes=[pltpu.VMEM((MLA_HEADS, tq, LANES), F32), pltpu.VMEM((MLA_HEADS, tq, LANES), F32),
                        pltpu.VMEM((seq // tk, MLA_HEADS, tq, tk), F32)],
        compiler_params=_cparams(("parallel", "arbitrary")),
    )(qcat, kcat, vmla)


def _sort_key(score):
    bits = lax.bitcast_convert_type(score + 0.0, I32)
    return bits ^ ((bits >> 31) & 0x7FFFFFFF)


def _kth_key(count_ge, n_sel, shape):
    def body(i, t):
        cand = t + (jnp.int32(1) << (31 - i))
        return jnp.where(count_ge(cand) >= n_sel, cand, t)

    return lax.fori_loop(0, 32, body, jnp.full(shape, INT_MIN, I32))


def _bucket(dist):
    max_exact = N_BUCKETS // 2
    d = jnp.maximum(dist, 0)
    log_ratio = jnp.log(jnp.maximum(d, max_exact).astype(F32) / max_exact) / math.log(MAX_DISTANCE / max_exact)
    large = jnp.minimum(max_exact + (log_ratio * (N_BUCKETS - max_exact)).astype(I32), N_BUCKETS - 1)
    return jnp.where(d < max_exact, d, large)


def _bias_kernel(rb_ref, o_ref, *, a, b, c):
    rows, cols = o_ref.shape[1], o_ref.shape[2]
    dist = (a * lax.broadcasted_iota(I32, (rows, cols), 0) + b * lax.broadcasted_iota(I32, (rows, cols), 1) + c)
    bucket = _bucket(dist)
    for h in range(DSA_HEADS):
        acc = jnp.zeros((rows, cols), F32)
        for n in range(N_BUCKETS):
            acc = jnp.where(bucket == n, rb_ref[n, h], acc)
        o_ref[h] = acc


def _bias_table(rel_bias, rows, cols, a, b, c):
    return pl.pallas_call(
        functools.partial(_bias_kernel, a=a, b=b, c=c),
        in_specs=[pl.BlockSpec(memory_space=pltpu.SMEM)],
        out_specs=pl.BlockSpec(memory_space=pltpu.VMEM),
        out_shape=jax.ShapeDtypeStruct((DSA_HEADS, rows, cols), F32),
    )(rel_bias)


def _colreduce(x, op):
    return op(x.reshape(x.shape[0] // 8, 8, x.shape[1]), axis=0)


def _dsa_prompt_kernel(rb_ref, qidx_ref, tail_ref, kidx_ref, q_ref, k_ref, v_ref, bias_ref, o_ref,
                       qs_scr, qd_scr, wb_scr, key_scr, sbq_scr, tie_scr, m_scr, p_scr, acc_scr, *, n_sel, idx_bits):
    tq = q_ref.shape[0]
    kq = tq // KB
    qi = pl.program_id(1)
    krow = lax.broadcasted_iota(I32, (KB, tq), 0)
    qcol = lax.broadcasted_iota(I32, (KB, tq), 1)
    low64 = lax.broadcasted_iota(I32, (tq, LANES), 1) < 64

    tail = tail_ref[...]
    for h in range(IDX_HEADS):
        mine = low64 if h % 2 == 0 else jnp.logical_not(low64)
        zero = jnp.zeros((tq, LANES), BF16)
        qs_scr[h * tq:(h + 1) * tq, :] = jnp.where(mine, qidx_ref[:, (h // 2) * 128:(h // 2 + 1) * 128], zero)
        qd_scr[h * tq:(h + 1) * tq, :] = jnp.where(mine, q_ref[:, (h // 2) * 128:(h // 2 + 1) * 128], zero)
        wb_scr[h] = jnp.broadcast_to(tail[:, T_WIDX + h:T_WIDX + h + 1], (tq, LANES))

    def rows_of(c):
        return pl.ds(pl.multiple_of(c * tq, tq), tq)

    def score_chunk(c, in_tile):
        d = _dot_nt(qs_scr[...], kidx_ref[rows_of(c), :])
        for t in range(kq):
            sc = jnp.zeros((tq, KB), F32)
            for h in range(IDX_HEADS):
                sc = sc + jnp.maximum(d[h * tq:(h + 1) * tq, t * KB:(t + 1) * KB], 0.0) * wb_scr[h]
            sc_t = (sc * IDX_SCALE).T
            if in_tile:
                sc_t = jnp.where(t * KB + krow <= qcol, sc_t, -jnp.inf)
            key_scr[c * kq + t] = _sort_key(sc_t)

    def score_body(c, carry):
        score_chunk(c, False)
        return carry

    lax.fori_loop(0, qi, score_body, 0)
    score_chunk(qi, True)
    n_chunks = qi + 1

    def count(pred):
        def body(c, acc):
            for t in range(kq):
                kb = c * kq + t
                acc = acc + jnp.where(pred(key_scr[kb], kb), 1.0, 0.0)
            return acc
        acc = lax.fori_loop(0, n_chunks, body, jnp.zeros((KB, tq), F32))
        return jnp.sum(_colreduce(acc, jnp.sum), axis=0, keepdims=True)

    thr = _kth_key(lambda cand: count(lambda k, kb: k >= cand), float(n_sel), (1, tq))
    cnt_gt = count(lambda k, kb: k > thr)
    cnt_ge = count(lambda k, kb: k >= thr)
    tie_scr[...] = jnp.full((1, tq), 2 ** 30, I32)
    excess = jnp.where((cnt_ge > float(n_sel)) & (thr > INT_MIN), 1.0, 0.0)

    @pl.when(jnp.max(excess) > 0.0)
    def _():
        need = float(n_sel) - cnt_gt

        def body(i, p):
            cand = p + (jnp.int32(1) << (idx_bits - 1 - i))
            c = count(lambda k, kb: (k == thr) & (kb * KB + krow < cand))
            return jnp.where(c < need, cand, p)

        tie_scr[...] = lax.fori_loop(0, idx_bits, body, jnp.zeros((1, tq), I32))

    last_tie = tie_scr[...]

    def mask_chunk(c, in_tile):
        for t in range(kq):
            kb = c * kq + t
            key = key_scr[kb]
            sel = (key > thr) | ((key == thr) & (kb * KB + krow <= last_tie))
            if in_tile:
                sel = sel & (t * KB + krow <= qcol)
            sbq_scr[c, :, t * KB:(t + 1) * KB] = jnp.where(sel, 0.0, NEG_BIG).T

    def mask_body(c, carry):
        mask_chunk(c, False)
        return carry

    lax.fori_loop(0, qi, mask_body, 0)
    mask_chunk(qi, True)

    far_bias = [rb_ref[N_BUCKETS - 1, h] for h in range(DSA_HEADS)]

    def scores(c, g, kind):
        sg = _dot_nt(qd_scr[g * DSA_GROUP * tq:(g + 1) * DSA_GROUP * tq, :], k_ref[rows_of(c), g * 128:(g + 1) * 128])
        sb = sbq_scr[c]
        out = []
        for hh in range(DSA_GROUP):
            h = g * DSA_GROUP + hh
            cols = []
            for t in range(kq):
                s = sg[hh * tq:(hh + 1) * tq, t * KB:(t + 1) * KB] + sb[:, t * KB:(t + 1) * KB]
                if kind == "tile":
                    s = s + (bias_ref[1 + t, h] - far_bias[h])
                elif kind == "before" and t == kq - 1:
                    s = s + (bias_ref[0, h] - far_bias[h])
                cols.append(s)
            out.append(cols)
        return out

    m_scr[...] = jnp.full(m_scr.shape, NEG_BIG, F32)
    acc_scr[...] = jnp.zeros(acc_scr.shape, F32)

    def pass_max(c, kind):
        for g in range(DSA_KV_HEADS):
            for hh, cols in enumerate(scores(c, g, kind)):
                h = g * DSA_GROUP + hh
                m = m_scr[h]
                for s in cols:
                    m = jnp.maximum(m, s)
                m_scr[h] = m

    def pass_sum(c, kind):
        for g in range(DSA_KV_HEADS):
            for hh, cols in enumerate(scores(c, g, kind)):
                m = m_scr[g * DSA_GROUP + hh]
                for t, s in enumerate(cols):
                    p_scr[hh * tq:(hh + 1) * tq, t * KB:(t + 1) * KB] = jnp.exp(s - m).astype(BF16)
            acc_scr[g] += _dot(p_scr[...], v_ref[rows_of(c), g * 128:(g + 1) * 128])

    def run(step):
        def body(c, carry):
            step(c, "far")
            return carry

        lax.fori_loop(0, qi - 1, body, 0)

        @pl.when(qi >= 1)
        def _():
            step(qi - 1, "before")

        step(qi, "tile")

    run(pass_max)
    for h in range(DSA_HEADS):
        m_scr[h] = jnp.broadcast_to(jnp.max(m_scr[h], axis=-1, keepdims=True), (tq, LANES))
    run(pass_sum)

    for p in range(DSA_HEADS // 2):
        g, hh = (2 * p) // DSA_GROUP, (2 * p) % DSA_GROUP
        even = acc_scr[g, hh * tq:(hh + 1) * tq, :]
        odd = acc_scr[g, (hh + 1) * tq:(hh + 2) * tq, :]
        o_ref[:, 128 * p:128 * p + 128] = jnp.where(low64, even / pltpu.roll(even, 64, 1),
                                                    pltpu.roll(odd, 64, 1) / odd)


def _dsa_prompt(rel_bias, qidx, tail, kidxdup, qdsa, kdup, vone, batch, seq, tq):
    assert MAX_DISTANCE <= KB, "keys two or more blocks back must all fall in the last bucket"
    nq = seq // tq
    nkb = seq // KB
    kq = tq // KB
    n_sel = min(TOPK_MAX, seq // 4)
    idx_bits = max(1, (seq - 1).bit_length())
    bias = jnp.stack([_bias_table(rel_bias, tq, KB, 1, -1, KB - t * KB) for t in range(kq + 1)])
    kern = functools.partial(_dsa_prompt_kernel, n_sel=n_sel, idx_bits=idx_bits)
    qblk = lambda w: pl.BlockSpec((tq, w), lambda b, i: (b * nq + i, 0))
    seqblk = lambda w: pl.BlockSpec((seq, w), lambda b, i: (b, 0))
    return pl.pallas_call(
        kern,
        grid=(batch, nq),
        in_specs=[pl.BlockSpec(memory_space=pltpu.SMEM),
                  qblk(512), qblk(LANES), seqblk(LANES), qblk(512), seqblk(256), seqblk(256), _full(bias.shape)],
        out_specs=qblk(512),
        out_shape=jax.ShapeDtypeStruct((batch * seq, 512), F32),
        scratch_shapes=[pltpu.VMEM((IDX_HEADS * tq, LANES), BF16), pltpu.VMEM((DSA_HEADS * tq, LANES), BF16),
                        pltpu.VMEM((IDX_HEADS, tq, LANES), F32),
                        pltpu.VMEM((nkb, KB, tq), I32), pltpu.VMEM((nq, tq, tq), F32),
                        pltpu.VMEM((1, tq), I32), pltpu.VMEM((DSA_HEADS, tq, LANES), F32),
                        pltpu.VMEM((DSA_GROUP * tq, tq), BF16),
                        pltpu.VMEM((DSA_KV_HEADS, DSA_GROUP * tq, LANES), F32)],
        compiler_params=_cparams(("parallel", "arbitrary")),
    )(rel_bias, qidx, tail, kidxdup, qdsa, kdup, vone, bias)


def _page_specs(n, width, index_of):
    return [pl.BlockSpec((None, None, width, PAGE_SIZE), functools.partial(index_of, i)) for i in range(n)]


def _pages_feature_major(cache):
    c = jnp.moveaxis(cache, 2, -1)
    return c.reshape(cache.shape[0], cache.shape[1], -1, PAGE_SIZE)


def _sample_score_kernel(pt_ref, q_ref, w_ref, *refs):
    pages, o_ref, kb_scr = refs[:-2], refs[-2], refs[-1]
    bi = pl.program_id(2)
    for i, pg in enumerate(pages):
        kb_scr[:, i * PAGE_SIZE:(i + 1) * PAGE_SIZE] = pg[...].astype(BF16)
    d = _dot(q_ref[...], kb_scr[...])
    sc = jnp.sum(jnp.maximum(d, 0.0) * w_ref[...], axis=0, keepdims=True) * IDX_SCALE
    o_ref[pl.ds(bi, 1), :] = sc


def _sample_scores(page_table, qidx3, widx3, cache_idx_t, pps):
    dec, n_pages = page_table.shape
    n_chunks = n_pages // pps
    grp = 8
    page_of = lambda i, bo, c, bi, pt: (0, pt[bo * grp + bi, c * pps + i], 0, 0)
    grid_spec = pltpu.PrefetchScalarGridSpec(
        num_scalar_prefetch=1,
        grid=(dec // grp, n_chunks, grp),
        in_specs=[pl.BlockSpec((None, IDX_HEADS, IDX_DIM), lambda bo, c, bi, pt: (bo * grp + bi, 0, 0)),
                  pl.BlockSpec((None, IDX_HEADS, 1), lambda bo, c, bi, pt: (bo * grp + bi, 0, 0))]
                 + _page_specs(pps, IDX_DIM, page_of),
        out_specs=pl.BlockSpec((grp, pps * PAGE_SIZE), lambda bo, c, bi, pt: (bo, c)),
        scratch_shapes=[pltpu.VMEM((IDX_DIM, pps * PAGE_SIZE), BF16)],
    )
    return pl.pallas_call(
        _sample_score_kernel, grid_spec=grid_spec,
        out_shape=jax.ShapeDtypeStruct((dec, n_pages * PAGE_SIZE), F32),
        compiler_params=_cparams(("arbitrary", "arbitrary", "arbitrary")),
    )(page_table, qidx3, widx3, *([cache_idx_t] * pps))


def _sample_select_kernel(sc_ref, qidx_ref, tail_ref, selp_ref, seln_ref, key_scr, *, n_sel, idx_bits):
    rows, past = sc_ref.shape
    nblk = past // LANES
    lane = lax.broadcasted_iota(I32, (rows, LANES), 1)
    tail = tail_ref[...]
    kidx2 = jnp.where(lane < 64, tail, pltpu.roll(tail, 64, 1))
    sc_new = jnp.zeros((rows, 1), F32)
    for h in range(IDX_HEADS):
        prod = qidx_ref[:, (h // 2) * 128:(h // 2 + 1) * 128].astype(F32) * kidx2.astype(BF16).astype(F32)
        mine = (lane < 64) if h % 2 == 0 else (lane >= 64)
        dot = jnp.sum(jnp.where(mine, prod, 0.0), axis=-1, keepdims=True)
        sc_new = sc_new + jnp.maximum(dot, 0.0) * tail[:, T_WIDX + h:T_WIDX + h + 1]
    key_new = _sort_key(sc_new * IDX_SCALE)

    for j in range(nblk):
        key_scr[j] = _sort_key(sc_ref[:, j * LANES:(j + 1) * LANES])

    def count(pred, pred_new):
        def body(j, c):
            return c + jnp.where(pred(key_scr[j], j), 1.0, 0.0)
        c = lax.fori_loop(0, nblk, body, jnp.zeros((rows, LANES), F32), unroll=4)
        return jnp.sum(c, axis=-1, keepdims=True) + jnp.where(pred_new, 1.0, 0.0)

    thr = _kth_key(lambda cand: count(lambda k, j: k >= cand, key_new >= cand), float(n_sel), (rows, 1))
    cnt_gt = count(lambda k, j: k > thr, key_new > thr)
    need = float(n_sel) - cnt_gt

    def body(i, p):
        cand = p + (jnp.int32(1) << (idx_bits - 1 - i))
        c = count(lambda k, j: (k == thr) & (j * LANES + lane < cand), (key_new == thr) & (past < cand))
        return jnp.where(c < need, cand, p)

    last_tie = lax.fori_loop(0, idx_bits, body, jnp.zeros((rows, 1), I32))

    for j in range(nblk):
        k = key_scr[j]
        sel = (k > thr) | ((k == thr) & (j * LANES + lane <= last_tie))
        selp_ref[:, j * LANES:(j + 1) * LANES] = jnp.where(sel, 1.0, 0.0)
    sel_new = (key_new > thr) | ((key_new == thr) & (past <= last_tie))
    seln_ref[...] = jnp.broadcast_to(jnp.where(sel_new, 1.0, 0.0), (rows, LANES))


def _sample_select(scores, qidx, tail):
    dec, past = scores.shape
    n_sel = min(TOPK_MAX, (past + 1) // 4)
    idx_bits = max(1, past.bit_length())
    rows = 64 if dec % 64 == 0 else dec
    kern = functools.partial(_sample_select_kernel, n_sel=n_sel, idx_bits=idx_bits)
    return pl.pallas_call(
        kern,
        grid=(dec // rows,),
        in_specs=[pl.BlockSpec((rows, past), lambda i: (i, 0)), pl.BlockSpec((rows, 512), lambda i: (i, 0)),
                  pl.BlockSpec((rows, LANES), lambda i: (i, 0))],
        out_specs=[pl.BlockSpec((rows, past), lambda i: (i, 0)), pl.BlockSpec((rows, LANES), lambda i: (i, 0))],
        out_shape=[jax.ShapeDtypeStruct((dec, past), F32), jax.ShapeDtypeStruct((dec, LANES), F32)],
        scratch_shapes=[pltpu.VMEM((past // LANES, rows, LANES), I32)],
        compiler_params=_cparams(("parallel",)),
    )(scores, qidx, tail)


def _mla_sample_kernel(pt_ref, qn_ref, qr_ref, rown_ref, wuk_ref, wuv_ref, gk_ref, *refs):
    pages, o_ref, lat_scr, kr_scr, a_scr, m_scr, l_scr, acc_scr = (refs[:-7],) + tuple(refs[-7:])
    c = pl.program_id(1)
    n_chunks = pl.num_programs(1)
    heads = MLA_HEADS

    @pl.when(c == 0)
    def _():
        m_scr[...] = jnp.full(m_scr.shape, -jnp.inf, F32)
        l_scr[...] = jnp.zeros(l_scr.shape, F32)
        acc_scr[...] = jnp.zeros(acc_scr.shape, F32)
        qg = qn_ref[...] * gk_ref[...]
        own = (lax.broadcasted_iota(I32, (heads, heads * MLA_NOPE), 1) // MLA_NOPE
               == lax.broadcasted_iota(I32, (heads, heads * MLA_NOPE), 0))
        a_scr[...] = _dot(jnp.where(own, qg, 0.0).astype(BF16), wuk_ref[...]).astype(BF16)

    qr = qr_ref[...]

    def scores(latb, krb):
        knt = _dot(wuk_ref[...], latb)
        ms = jnp.mean((knt * knt).reshape(heads, MLA_NOPE, latb.shape[1]), axis=1)
        return (_dot(a_scr[...], latb) * lax.rsqrt(ms + EPS) + _dot(qr, krb)) * MLA_SCALE

    def update(latb, krb, n_valid):
        n = latb.shape[1]
        sub = min(n, 512)
        s = jnp.concatenate([scores(latb[:, j:j + sub], krb[:, j:j + sub]) for j in range(0, n, sub)], axis=1)
        if n_valid < n:
            s = jnp.where(lax.broadcasted_iota(I32, s.shape, 1) < n_valid, s, -jnp.inf)
        m_old = m_scr[...]
        m_new = jnp.maximum(m_old, jnp.max(s, axis=-1, keepdims=True))
        alpha = jnp.exp(m_old - m_new)
        p = jnp.exp(s - m_new)
        l_scr[...] = alpha * l_scr[...] + jnp.sum(p, axis=-1, keepdims=True)
        m_scr[...] = m_new
        acc_scr[...] = alpha * acc_scr[...] + _dot_nt(p.astype(BF16), latb)

    for i, pg in enumerate(pages):
        lat_scr[:, i * PAGE_SIZE:(i + 1) * PAGE_SIZE] = pg[0:MLA_KV_RANK, :].astype(BF16)
        kr_scr[:, i * PAGE_SIZE:(i + 1) * PAGE_SIZE] = pg[MLA_KV_RANK:MLA_ROW, :].astype(BF16)
    update(lat_scr[...], kr_scr[...], lat_scr.shape[1])

    @pl.when(c == n_chunks - 1)
    def _():
        rn = jnp.broadcast_to(rown_ref[...], (MLA_ROW, PAGE_SIZE))
        update(rn[0:MLA_KV_RANK, :].astype(BF16), rn[MLA_KV_RANK:MLA_ROW, :].astype(BF16), 1)
        o_lat = (acc_scr[...] / l_scr[...]).astype(BF16)
        full = _dot(o_lat, wuv_ref[...])
        hd = lax.broadcasted_iota(I32, full.shape, 1) // MLA_V
        own = hd == lax.broadcasted_iota(I32, full.shape, 0)
        o_ref[...] = jnp.sum(jnp.where(own, full, 0.0), axis=0, keepdims=True)


def _mla_sample(page_table, qn_row, qr, row_new, wuk_t, wuv, gk_row, cache_mla, pps):
    dec, n_pages = page_table.shape
    n_chunks = n_pages // pps
    page_of = lambda i, b, c, pt: (0, pt[b, c * pps + i], 0, 0)
    grid_spec = pltpu.PrefetchScalarGridSpec(
        num_scalar_prefetch=1,
        grid=(dec, n_chunks),
        in_specs=[pl.BlockSpec((None, 1, MLA_HEADS * MLA_NOPE), lambda b, c, pt: (b, 0, 0)),
                  pl.BlockSpec((None, MLA_HEADS, MLA_ROPE), lambda b, c, pt: (b, 0, 0)),
                  pl.BlockSpec((None, MLA_ROW, 1), lambda b, c, pt: (b, 0, 0)),
                  pl.BlockSpec(wuk_t.shape, lambda b, c, pt: (0, 0)),
                  pl.BlockSpec(wuv.shape, lambda b, c, pt: (0, 0)),
                  pl.BlockSpec(gk_row.shape, lambda b, c, pt: (0, 0))]
                 + _page_specs(pps, MLA_ROW, page_of),
        out_specs=pl.BlockSpec((None, 1, MLA_HEADS * MLA_V), lambda b, c, pt: (b, 0, 0)),
        scratch_shapes=[pltpu.VMEM((MLA_KV_RANK, pps * PAGE_SIZE), BF16), pltpu.VMEM((MLA_ROPE, pps * PAGE_SIZE), BF16),
                        pltpu.VMEM((MLA_HEADS, MLA_KV_RANK), BF16),
                        pltpu.VMEM((MLA_HEADS, 1), F32), pltpu.VMEM((MLA_HEADS, 1), F32),
                        pltpu.VMEM((MLA_HEADS, MLA_KV_RANK), F32)],
    )
    return pl.pallas_call(
        _mla_sample_kernel, grid_spec=grid_spec,
        out_shape=jax.ShapeDtypeStruct((dec, 1, MLA_HEADS * MLA_V), F32),
        compiler_params=_cparams(("parallel", "arbitrary")),
    )(page_table, qn_row, qr, row_new, wuk_t, wuv, gk_row, *([cache_mla] * pps))


def _dsa_sample_kernel(pt_ref, q_ref, kvn_ref, selp_ref, seln_ref, biasp_ref, biasn_ref, *refs):
    pages, o_ref, k_scr, v_scr, m_scr, l_scr, acc_scr = refs[:-6], refs[-6], refs[-5], refs[-4], refs[-3], refs[-2], refs[-1]
    c = pl.program_id(1)
    n_chunks = pl.num_programs(1)

    @pl.when(c == 0)
    def _():
        m_scr[...] = jnp.full(m_scr.shape, NEG_BIG, F32)
        l_scr[...] = jnp.zeros(l_scr.shape, F32)
        acc_scr[...] = jnp.zeros(acc_scr.shape, F32)

    q = q_ref[...]

    def update(s, sel, pv_of):
        sh = jnp.where(sel, s, NEG_BIG)
        m_old = m_scr[...]
        m_new = jnp.maximum(m_old, jnp.max(sh, axis=-1, keepdims=True))
        alpha = jnp.exp(m_old - m_new)
        p = jnp.where(sel, jnp.exp(sh - m_new), 0.0)
        l_scr[...] = alpha * l_scr[...] + jnp.sum(p, axis=-1, keepdims=True)
        m_scr[...] = m_new
        acc_scr[...] = alpha * acc_scr[...] + pv_of(p)

    for i, pg in enumerate(pages):
        k_scr[:, i * PAGE_SIZE:(i + 1) * PAGE_SIZE] = pg[0:128, :].astype(BF16)
        v_scr[:, i * PAGE_SIZE:(i + 1) * PAGE_SIZE] = pg[128:256, :].astype(BF16)
    s = _dot(q.astype(BF16), k_scr[...]) + biasp_ref[...]
    update(s, selp_ref[...] > 0.5, lambda p: _dot_nt(p.astype(BF16), v_scr[...]))

    @pl.when(c == n_chunks - 1)
    def _():
        kvn = kvn_ref[...]
        kn = kvn[:, 0:128].astype(BF16).astype(F32)
        vn = kvn[:, 128:256].astype(BF16).astype(F32)
        s_new = jnp.sum(q.astype(F32) * kn, axis=-1, keepdims=True) + biasn_ref[...]
        update(s_new, seln_ref[:, 0:1] > 0.5, lambda p: p.astype(BF16).astype(F32) * vn)
        o = acc_scr[...] / l_scr[...]
        lane = lax.broadcasted_iota(I32, o.shape, 1)
        row = lax.broadcasted_iota(I32, o.shape, 0)
        own = (lane // DSA_HEAD_DIM) == (row // DSA_GROUP)
        o_ref[...] = jnp.where(own, o, 0.0)


def _dsa_sample(page_table, qs, kv_new, selp, seln, bias_past, bias_new, cache_kv4, pps):
    dec, n_pages = page_table.shape
    n_chunks = n_pages // pps
    chunk = pps * PAGE_SIZE
    page_of = lambda i, b, c, pt: (0, pt[b, c * pps + i], 0, 0)
    grid_spec = pltpu.PrefetchScalarGridSpec(
        num_scalar_prefetch=1,
        grid=(dec, n_chunks),
        in_specs=[pl.BlockSpec((None, DSA_HEADS, LANES), lambda b, c, pt: (b, 0, 0)),
                  pl.BlockSpec((None, 1, 256), lambda b, c, pt: (b, 0, 0)),
                  pl.BlockSpec((None, 1, chunk), lambda b, c, pt: (b, 0, c)),
                  pl.BlockSpec((None, 1, LANES), lambda b, c, pt: (b, 0, 0)),
                  pl.BlockSpec((DSA_HEADS, chunk), lambda b, c, pt: (0, c)),
                  pl.BlockSpec((DSA_HEADS, 1), lambda b, c, pt: (0, 0))]
                 + _page_specs(pps, 256, page_of),
        out_specs=pl.BlockSpec((None, DSA_HEADS, LANES), lambda b, c, pt: (b, 0, 0)),
        scratch_shapes=[pltpu.VMEM((128, chunk), BF16), pltpu.VMEM((128, chunk), BF16),
                        pltpu.VMEM((DSA_HEADS, 1), F32), pltpu.VMEM((DSA_HEADS, 1), F32),
                        pltpu.VMEM((DSA_HEADS, LANES), F32)],
    )
    return pl.pallas_call(
        _dsa_sample_kernel, grid_spec=grid_spec,
        out_shape=jax.ShapeDtypeStruct((dec, DSA_HEADS, LANES), F32),
        compiler_params=_cparams(("parallel", "arbitrary")),
    )(page_table, qs, kv_new, selp, seln, bias_past, bias_new, *([cache_kv4] * pps))


def _out_kernel(x_ref, oa_ref, ob_ref, gm_ref, sf_ref, scf_ref, gf_ref, gout_ref, gffn_ref,
                wout_ref, wg_ref, wu_ref, wdown_ref, y_ref, *, ff_chunk):
    def rms(v, g):
        return v * lax.rsqrt(jnp.mean(v * v, axis=-1, keepdims=True) + EPS) * g

    half = oa_ref.shape[1]
    na = rms(oa_ref[...], gout_ref[:, 0:half]).astype(BF16)
    nb = rms(ob_ref[...], gout_ref[:, half:2 * half]).astype(BF16)
    mix = _dot(na, wout_ref[0:half, :]) + _dot(nb, wout_ref[half:2 * half, :])
    x1 = x_ref[...] + gm_ref[...] * mix
    hb = (rms(x1, gffn_ref[...]) * (1.0 + scf_ref[...]) + sf_ref[...]).astype(BF16)
    acc = jnp.zeros(x1.shape, F32)
    for j in range(D_FF // ff_chunk):
        g = _dot(hb, wg_ref[:, j * ff_chunk:(j + 1) * ff_chunk])
        u = _dot(hb, wu_ref[:, j * ff_chunk:(j + 1) * ff_chunk])
        a = (g / (1.0 + jnp.exp(-g))) * u
        acc = acc + _dot(a.astype(BF16), wdown_ref[j * ff_chunk:(j + 1) * ff_chunk, :])
    y_ref[...] = x1 + gf_ref[...] * acc


def _out_stage(x2d, oa, ob, gate_m, shift_f, scale_f, gate_f, wp, tm, per_token, tiles_per_seq):
    t_total = x2d.shape[0]
    if per_token:
        mod_spec = pl.BlockSpec((tm, D_MODEL), lambda i: (i, 0))
    else:
        mod_spec = pl.BlockSpec((None, 1, D_MODEL), lambda i: (i // tiles_per_seq, 0, 0))
    consts = [wp["g_out"], wp["g_ffn"], wp["w_out"], wp["w_gate"], wp["w_up"], wp["w_down"]]
    tok = lambda w: pl.BlockSpec((tm, w), lambda i: (i, 0))
    return pl.pallas_call(
        functools.partial(_out_kernel, ff_chunk=256),
        grid=(t_total // tm,),
        in_specs=[tok(D_MODEL), tok(512), tok(512), mod_spec, mod_spec, mod_spec, mod_spec]
                 + [_full(c.shape) for c in consts],
        out_specs=tok(D_MODEL),
        out_shape=jax.ShapeDtypeStruct((t_total, D_MODEL), F32),
        compiler_params=_cparams(("parallel",)),
    )(x2d, oa, ob, gate_m, shift_f, scale_f, gate_f, *consts)


def _prep_weights(w_in, g_norm_mix, g_norm_ffn, g_q_lat, w_uq, g_kv_lat, w_ukv, g_mla_q_nope, g_mla_q_rope,
                  g_mla_k_nope, g_mla_k_rope, g_dsa_q, g_dsa_k, g_out, w_out, w_ffn_in, w_ffn_out):
    splits = np.cumsum([MLA_Q_RANK, MLA_KV_RANK, MLA_ROPE, 512, 128, 128, 512, IDX_DIM, IDX_HEADS])
    s = [0] + splits.tolist()
    col = lambda i: w_in[:, s[i]:s[i + 1]]
    pad = jnp.zeros((D_MODEL, C_END - C_TAIL - IDX_DIM - MLA_ROPE - IDX_HEADS), w_in.dtype)
    w_in_r = jnp.concatenate([col(0), col(1), col(3), col(4), col(5), col(6), col(7), col(2), col(8), pad], axis=1)
    zq = jnp.zeros((MLA_Q_RANK, MLA_HEADS, LANES - MLA_NOPE - MLA_ROPE), w_uq.dtype)
    w_uq_cat = jnp.concatenate([w_uq, zq], axis=2).reshape(MLA_Q_RANK, MLA_HEADS * LANES)
    zk = jnp.zeros((MLA_KV_RANK, MLA_HEADS, LANES - MLA_NOPE), w_ukv.dtype)
    w_uk_cat = jnp.concatenate([w_ukv[:, :, :MLA_NOPE], zk], axis=2).reshape(MLA_KV_RANK, MLA_HEADS * LANES)
    w_uv = w_ukv[:, :, MLA_NOPE:].reshape(MLA_KV_RANK, MLA_HEADS * MLA_V)
    zv = jnp.zeros((MLA_KV_RANK, MLA_HEADS, LANES - MLA_V), w_ukv.dtype)
    w_uv_pad = jnp.concatenate([w_ukv[:, :, MLA_NOPE:], zv], axis=2).reshape(MLA_KV_RANK, MLA_HEADS * LANES)
    w_uk_t = w_ukv[:, :, :MLA_NOPE].reshape(MLA_KV_RANK, MLA_HEADS * MLA_NOPE).T
    z32 = jnp.zeros((LANES - MLA_NOPE - MLA_ROPE,), F32)
    z64 = jnp.zeros((LANES - MLA_NOPE,), F32)
    g_qcat = jnp.tile(jnp.concatenate([g_mla_q_nope, g_mla_q_rope, z32]), MLA_HEADS)
    g_kcat = jnp.tile(jnp.concatenate([g_mla_k_nope, z64]), MLA_HEADS)
    g_tail = jnp.concatenate([jnp.zeros((T_ROPE,), F32), g_mla_k_rope, z32])
    row = lambda v: v.reshape(1, -1).astype(F32)
    return {
        "g_mix": row(g_norm_mix), "g_ffn": row(g_norm_ffn), "w_in": w_in_r.astype(BF16),
        "g_qlat": row(g_q_lat), "w_uq": w_uq_cat.astype(BF16), "g_qcat": row(g_qcat),
        "g_kvlat": row(g_kv_lat), "w_ukv": jnp.concatenate([w_uk_cat, w_uv_pad], axis=1).astype(BF16),
        "g_kcat": row(g_kcat), "g_tail": row(g_tail),
        "g_q": row(jnp.tile(g_dsa_q, DSA_HEADS)), "g_k": row(jnp.tile(g_dsa_k, DSA_KV_HEADS)),
        "m_cat": _group_matrix(256, [(0, 64), (64, 96), (128, 192), (192, 224)]),
        "m_64": _group_matrix(256, [(0, 64), (64, 128), (128, 192), (192, 256)]),
        "w_uk_t": w_uk_t.astype(BF16), "w_uv": w_uv.astype(BF16), "g_k_nope": g_mla_k_nope,
        "g_out": row(g_out), "w_out": w_out.astype(BF16),
        "w_gate": w_ffn_in[:, :D_FF].astype(BF16), "w_up": w_ffn_in[:, D_FF:].astype(BF16),
        "w_down": w_ffn_out.astype(BF16),
    }


def _rope_tables(pos):
    freq = ROPE_THETA ** (-jnp.arange(HALF_ROPE, dtype=F32) / HALF_ROPE)
    ang = pos.astype(F32)[:, None] * freq[None, :]
    cos, sin = jnp.cos(ang), jnp.sin(ang)
    n = pos.shape[0]
    ones = jnp.ones((n, T_ROPE), F32)
    zeros = jnp.zeros((n, T_ROPE), F32)
    tail1 = jnp.ones((n, LANES - T_ROPE - MLA_ROPE), F32)
    tail0 = jnp.zeros((n, LANES - T_ROPE - MLA_ROPE), F32)
    return (jnp.concatenate([ones, cos, cos, tail1], axis=1),
            jnp.concatenate([zeros, -sin, sin, tail0], axis=1))


def _pick_tile(n, prefs):
    for t in prefs:
        if n % t == 0:
            return t
    return n


def kernel(x_prompt, x_sample, c_prompt, c_sample, cache_mla, cache_kv, cache_idx, page_table, rel_bias, w_ada, b_ada, g_norm_mix, g_norm_ffn, w_in, g_q_lat, w_uq, g_kv_lat, w_ukv, g_mla_q_nope, g_mla_q_rope, g_mla_k_nope, g_mla_k_rope, g_dsa_q, g_dsa_k, g_out, w_out, w_ffn_in, w_ffn_out):
    assert w_ada.shape[0] == 1 and x_sample.shape[1] == 1, "one layer, one new token per sample"
    batch, seq, _ = x_prompt.shape
    dec = x_sample.shape[0]
    n_pages = page_table.shape[1]
    past = n_pages * PAGE_SIZE
    wp = _prep_weights(w_in[0], g_norm_mix[0], g_norm_ffn[0], g_q_lat[0], w_uq[0], g_kv_lat[0], w_ukv[0],
                       g_mla_q_nope[0], g_mla_q_rope[0], g_mla_k_nope[0], g_mla_k_rope[0], g_dsa_q[0], g_dsa_k[0],
                       g_out[0], w_out[0], w_ffn_in[0], w_ffn_out[0])

    mod = _ada(jnp.concatenate([c_prompt, c_sample], axis=0), w_ada[0], b_ada[0])
    mods_p = [m.reshape(batch, 1, D_MODEL) for m in jnp.split(mod[:batch], 6, axis=-1)]
    mods_s = jnp.split(mod[batch:], 6, axis=-1)

    tm = _pick_tile(seq, (512, 256, 128))
    xp2 = x_prompt.reshape(batch * seq, D_MODEL)
    cos_p, sin_p = _rope_tables(jnp.arange(seq))
    pin = _in_stage(xp2, mods_p[1], mods_p[0], cos_p, sin_p, wp, tm, per_token=False)
    o_mla_p = _mla_prompt(pin["qcat"], pin["kcat"], pin["vmla"], batch, seq,
                          _pick_tile(seq, (256, 128)), _pick_tile(seq, (512, 256, 128)))
    o_dsa_p = _dsa_prompt(rel_bias, pin["qidx"], pin["tail"], pin["kidxdup"], pin["qdsa"], pin["kdup"], pin["vone"],
                          batch, seq, _pick_tile(seq, (256, 128)))
    y_p = _out_stage(xp2, o_mla_p, o_dsa_p, mods_p[2], mods_p[3], mods_p[4], mods_p[5], wp, tm,
                     per_token=False, tiles_per_seq=seq // tm)

    xs2 = x_sample.reshape(dec, D_MODEL)
    cos_s, sin_s = _rope_tables(jnp.full((1,), past))
    sin_ = _in_stage(xs2, mods_s[1], mods_s[0], cos_s, sin_s, wp, dec, per_token=True)
    pps = _pick_tile(n_pages, (16, 8, 4, 2))
    pps_wide = _pick_tile(n_pages, (32, 16, 8, 4, 2))
    qidx3 = sin_["qidx"].reshape(dec, IDX_HEADS, IDX_DIM)
    widx3 = sin_["tail"][:, T_WIDX:T_WIDX + IDX_HEADS].reshape(dec, IDX_HEADS, 1)
    scores = _sample_scores(page_table, qidx3, widx3, _pages_feature_major(cache_idx), pps_wide)
    selp, seln = _sample_select(scores, sin_["qidx"], sin_["tail"])
    qc = sin_["qcat"].reshape(dec, MLA_HEADS, LANES)
    qn_row = qc[:, :, 0:MLA_NOPE].astype(F32).reshape(dec, 1, MLA_HEADS * MLA_NOPE)
    qr = qc[:, :, T_ROPE:T_ROPE + MLA_ROPE]
    gk_row = jnp.tile(wp["g_k_nope"], MLA_HEADS).reshape(1, MLA_HEADS * MLA_NOPE)
    o_mla_s = _mla_sample(page_table, qn_row, qr, sin_["row"].reshape(dec, MLA_ROW, 1), wp["w_uk_t"], wp["w_uv"],
                          gk_row, _pages_feature_major(cache_mla), pps).reshape(dec, MLA_HEADS * MLA_V)
    qd = sin_["qdsa"].astype(F32).reshape(dec, DSA_HEADS, DSA_HEAD_DIM)
    on_c = (jnp.arange(DSA_KV_HEADS)[None, :] == (jnp.arange(DSA_HEADS) // DSA_GROUP)[:, None]).astype(F32)
    qd2 = (qd[:, :, None, :] * on_c[None, :, :, None]).reshape(dec, DSA_HEADS, LANES)
    bias_past = _bias_table(rel_bias, 1, past, 0, -1, past).reshape(DSA_HEADS, past)
    bias_new = _bias_table(rel_bias, 1, LANES, 0, 0, 0)[:, 0, 0:1]
    o8 = _dsa_sample(page_table, qd2, sin_["kv"].reshape(dec, 1, 256), selp.reshape(dec, 1, past),
                     seln.reshape(dec, 1, LANES), bias_past, bias_new, _pages_feature_major(cache_kv), pps_wide)
    o_dsa_s = o8.reshape(dec, DSA_HEADS, DSA_KV_HEADS, DSA_HEAD_DIM).sum(axis=2).reshape(dec, DSA_HEADS * DSA_HEAD_DIM)
    y_s = _out_stage(xs2, o_mla_s, o_dsa_s, mods_s[2], mods_s[3], mods_s[4], mods_s[5], wp, dec,
                     per_token=True, tiles_per_seq=1)

    return (y_p.reshape(batch, seq, D_MODEL), y_s.reshape(dec, 1, D_MODEL),
            pin["row"].reshape(1, batch, seq, MLA_ROW),
            pin["kv"].reshape(1, batch, seq, 2, DSA_KV_HEADS, DSA_HEAD_DIM),
            pin["kidx"].reshape(1, batch, seq, IDX_DIM),
            sin_["row"].reshape(1, dec, 1, MLA_ROW),
            sin_["kv"].reshape(1, dec, 1, 2, DSA_KV_HEADS, DSA_HEAD_DIM),
            sin_["kidx"].reshape(1, dec, 1, IDX_DIM))
```

```python
import functools
import math

import numpy as np
import jax
import jax.numpy as jnp
from jax import lax
from jax.experimental import pallas as pl
from jax.experimental.pallas import tpu as pltpu

D_MODEL = 1024
PAGE_SIZE = 128
MLA_HEADS = 8
MLA_NOPE = 64
MLA_ROPE = 32
MLA_V = 64
MLA_Q_RANK = 384
MLA_KV_RANK = 256
MLA_ROW = MLA_KV_RANK + MLA_ROPE
DSA_HEADS = 8
DSA_KV_HEADS = 2
DSA_HEAD_DIM = 64
DSA_GROUP = DSA_HEADS // DSA_KV_HEADS
IDX_HEADS = 8
IDX_DIM = 64
TOPK_MAX = 256
N_BUCKETS = 32
MAX_DISTANCE = 128
ROPE_THETA = 10000.0
D_FF = ((8 * D_MODEL + 3 * 256 - 1) // (3 * 256)) * 256
EPS = 1e-6
MLA_SCALE = (MLA_NOPE + MLA_ROPE) ** -0.5
DSA_SCALE = DSA_HEAD_DIM ** -0.5
IDX_SCALE = (IDX_DIM * IDX_HEADS) ** -0.5

LANES = 128
VMEM_LIMIT = 56 * 1024 * 1024

F32 = jnp.float32
BF16 = jnp.bfloat16
I32 = jnp.int32
NEG_BIG = -1e30
INT_MIN = -(2 ** 31)

C_QLAT = 0
C_KVLAT = C_QLAT + MLA_Q_RANK
C_Q = C_KVLAT + MLA_KV_RANK
C_K = C_Q + DSA_HEADS * DSA_HEAD_DIM
C_V = C_K + DSA_KV_HEADS * DSA_HEAD_DIM
C_QIDX = C_V + DSA_KV_HEADS * DSA_HEAD_DIM
C_TAIL = C_QIDX + IDX_HEADS * IDX_DIM
C_END = C_TAIL + LANES
T_ROPE = IDX_DIM
T_WIDX = IDX_DIM + MLA_ROPE
HALF_ROPE = MLA_ROPE // 2
KB = 128


def _dot(a, b):
    return jnp.dot(a, b, preferred_element_type=F32)


def _dot_nt(a, b):
    return lax.dot_general(a, b, (((1,), (1,)), ((), ())), preferred_element_type=F32)


def _split(a):
    hi = a.astype(BF16)
    lo = (a - hi.astype(F32)).astype(BF16)
    return hi, lo


def _dot3(a, b):
    ah, al = _split(a)
    bh, bl = _split(b)
    return _dot(ah, bh) + (_dot(al, bh) + _dot(ah, bl))


def _cparams(sem):
    return pltpu.CompilerParams(dimension_semantics=sem, vmem_limit_bytes=VMEM_LIMIT)


def _full(shape):
    n = len(shape)
    return pl.BlockSpec(shape, lambda *a, _n=n: (0,) * _n, pipeline_mode=pl.Buffered(1))


def _ada_kernel(c_ref, w_ref, b_ref, o_ref):
    c = c_ref[...]
    s = c / (1.0 + jnp.exp(-c))
    o_ref[...] = _dot3(s, w_ref[...]) + b_ref[...]


def _ada(c, w, b):
    n = c.shape[0]
    return pl.pallas_call(
        _ada_kernel,
        grid=(6,),
        in_specs=[pl.BlockSpec((n, D_MODEL), lambda j: (0, 0)),
                  pl.BlockSpec((D_MODEL, D_MODEL), lambda j: (0, j)),
                  pl.BlockSpec((1, D_MODEL), lambda j: (0, j))],
        out_specs=pl.BlockSpec((n, D_MODEL), lambda j: (0, j)),
        out_shape=jax.ShapeDtypeStruct((n, 6 * D_MODEL), F32),
        compiler_params=_cparams(("arbitrary",)),
    )(c, w, b.reshape(1, -1))


def _group_matrix(width, groups):
    g = np.zeros((width, width), np.float32)
    for lo, hi in groups:
        g[lo:hi, lo:hi] = 1.0 / (hi - lo)
    return jnp.asarray(g, BF16)


def _in_kernel(x_ref, sc_ref, sh_ref, cos_ref, sin_ref, gmix_ref, win_ref, gqlat_ref, wuq_ref, gqcat_ref,
               gkvlat_ref, wukv_ref, gkcat_ref, gtail_ref, gq_ref, gk_ref, mcat_ref, m64_ref,
               qcat_o, kcat_o, vmla_o, row_o, kv_o, kidx_o, tail_o, kidxdup_o, qdsa_o, kdup_o, vone_o, qidx_o):
    tm = x_ref.shape[0]
    x = x_ref[...]
    h = x * lax.rsqrt(jnp.mean(x * x, axis=-1, keepdims=True) + EPS) * gmix_ref[...]
    hb = (h * (1.0 + sc_ref[...]) + sh_ref[...]).astype(BF16)

    def proj(lo, hi):
        return _dot(hb, win_ref[:, lo:hi])

    cos = cos_ref[...]
    sin = sin_ref[...]
    lane = lax.broadcasted_iota(I32, (tm, LANES), 1)
    first_half = lane < T_ROPE + HALF_ROPE
    rope_lanes = (lane >= T_ROPE) & (lane < T_ROPE + MLA_ROPE)
    low64 = lane < 64

    def rope(xh):
        rot = jnp.where(first_half, pltpu.roll(xh, LANES - HALF_ROPE, 1), pltpu.roll(xh, HALF_ROPE, 1))
        return xh * cos + rot * sin

    def gnorm(blk, m_ref):
        ms = _dot((blk * blk).astype(BF16), m_ref[...])
        return blk * lax.rsqrt(ms + EPS)

    ql = proj(C_QLAT, C_KVLAT)
    qln = ql * lax.rsqrt(jnp.mean(ql * ql, axis=-1, keepdims=True) + EPS) * gqlat_ref[...]
    qm = _dot(qln.astype(BF16), wuq_ref[...])
    for j in range(4):
        y = gnorm(qm[:, 256 * j:256 * j + 256], mcat_ref) * gqcat_ref[:, 256 * j:256 * j + 256]
        for t in range(2):
            qcat_o[:, 256 * j + 128 * t:256 * j + 128 * t + 128] = rope(y[:, 128 * t:128 * t + 128]).astype(BF16)

    tail = proj(C_TAIL, C_END)
    ssq = jnp.sum(jnp.where(rope_lanes, tail * tail, 0.0), axis=-1, keepdims=True) * (1.0 / MLA_ROPE)
    tn = jnp.where(rope_lanes, tail * lax.rsqrt(ssq + EPS) * gtail_ref[...], tail)
    tr = rope(tn)
    kidx_o[...] = tail[:, 0:IDX_DIM]
    tail_o[...] = tail
    kidxdup_o[...] = jnp.where(low64, tail, pltpu.roll(tail, 64, 1)).astype(BF16)
    krope = jnp.where(rope_lanes, tr, 0.0)

    kvl = proj(C_KVLAT, C_Q)
    lat = kvl * lax.rsqrt(jnp.mean(kvl * kvl, axis=-1, keepdims=True) + EPS) * gkvlat_ref[...]
    row_o[:, 0:MLA_KV_RANK] = lat
    row_o[:, MLA_KV_RANK:MLA_ROW] = tr[:, T_ROPE:T_ROPE + MLA_ROPE]
    latb = lat.astype(BF16)
    kx = _dot(latb, wukv_ref[:, 0:1024])
    for j in range(4):
        y = gnorm(kx[:, 256 * j:256 * j + 256], mcat_ref) * gkcat_ref[:, 256 * j:256 * j + 256]
        for t in range(2):
            kcat_o[:, 256 * j + 128 * t:256 * j + 128 * t + 128] = (y[:, 128 * t:128 * t + 128] + krope).astype(BF16)
    ones_half = (lax.broadcasted_iota(I32, (tm, 1024), 1) % LANES) >= MLA_V
    vmla_o[...] = jnp.where(ones_half, 1.0, _dot(latb, wukv_ref[:, 1024:2048])).astype(BF16)

    q = proj(C_Q, C_K)
    for j in range(2):
        qdsa_o[:, 256 * j:256 * j + 256] = (gnorm(q[:, 256 * j:256 * j + 256], m64_ref)
                                            * gq_ref[:, 256 * j:256 * j + 256] * DSA_SCALE).astype(BF16)
    k = proj(C_K, C_V)
    ms = _dot((k * k).astype(BF16), m64_ref[0:128, 0:128])
    kn = k * lax.rsqrt(ms + EPS) * gk_ref[...]
    v = proj(C_V, C_QIDX)
    kv_o[:, 0:128] = kn
    kv_o[:, 128:256] = v
    kr = pltpu.roll(kn, 64, 1)
    kdup_o[:, 0:128] = jnp.where(low64, kn, kr).astype(BF16)
    kdup_o[:, 128:256] = jnp.where(low64, kr, kn).astype(BF16)
    vone_o[:, 0:128] = jnp.where(low64, v, 1.0).astype(BF16)
    vone_o[:, 128:256] = jnp.where(low64, pltpu.roll(v, 64, 1), 1.0).astype(BF16)
    qidx_o[...] = proj(C_QIDX, C_TAIL).astype(BF16)


_IN_OUT_WIDTHS = (("qcat", 1024, BF16), ("kcat", 1024, BF16), ("vmla", 1024, BF16), ("row", MLA_ROW, F32),
                  ("kv", 256, F32), ("kidx", IDX_DIM, F32), ("tail", LANES, F32), ("kidxdup", LANES, BF16),
                  ("qdsa", 512, BF16), ("kdup", 256, BF16), ("vone", 256, BF16), ("qidx", 512, BF16))


def _in_stage(x2d, scale, shift, cos, sin, wp, tm, per_token):
    t_total = x2d.shape[0]
    n_tiles = t_total // tm
    if per_token:
        mod_spec = pl.BlockSpec((tm, D_MODEL), lambda i: (i, 0))
        tab_spec = pl.BlockSpec((1, LANES), lambda i: (0, 0))
    else:
        tiles_per_seq = cos.shape[0] // tm
        mod_spec = pl.BlockSpec((None, 1, D_MODEL), lambda i: (i // tiles_per_seq, 0, 0))
        tab_spec = pl.BlockSpec((tm, LANES), lambda i: (i % tiles_per_seq, 0))
    consts = [wp["g_mix"], wp["w_in"], wp["g_qlat"], wp["w_uq"], wp["g_qcat"], wp["g_kvlat"], wp["w_ukv"],
              wp["g_kcat"], wp["g_tail"], wp["g_q"], wp["g_k"], wp["m_cat"], wp["m_64"]]
    in_specs = ([pl.BlockSpec((tm, D_MODEL), lambda i: (i, 0)), mod_spec, mod_spec, tab_spec, tab_spec]
                + [_full(c.shape) for c in consts])
    out_specs = [pl.BlockSpec((tm, w), lambda i: (i, 0)) for _, w, _ in _IN_OUT_WIDTHS]
    out_shape = [jax.ShapeDtypeStruct((t_total, w), dt) for _, w, dt in _IN_OUT_WIDTHS]
    outs = pl.pallas_call(
        _in_kernel, grid=(n_tiles,), in_specs=in_specs, out_specs=out_specs, out_shape=out_shape,
        compiler_params=_cparams(("parallel",)),
    )(x2d, scale, shift, cos, sin, *consts)
    return {name: o for (name, _, _), o in zip(_IN_OUT_WIDTHS, outs)}


def _mla_prompt_kernel(q_ref, k_ref, v_ref, o_ref, m_scr, acc_scr, s_scr, *, tk):
    tq = q_ref.shape[0]
    qi = pl.program_id(1)
    n_chunks = (qi * tq + tq + tk - 1) // tk
    m_scr[...] = jnp.full(m_scr.shape, -jnp.inf, F32)
    acc_scr[...] = jnp.zeros(acc_scr.shape, F32)
    low64 = lax.broadcasted_iota(I32, (tq, LANES), 1) < 64
    qpos = qi * tq + lax.broadcasted_iota(I32, (tq, tk), 0)
    kcol = lax.broadcasted_iota(I32, (tq, tk), 1)
    exp_scale = MLA_SCALE * math.log2(math.e)

    def raw_scores(c, h, masked):
        off = pl.multiple_of(c * tk, tk)
        s = _dot_nt(q_ref[:, h * 128:(h + 1) * 128], k_ref[pl.ds(off, tk), h * 128:(h + 1) * 128])
        if masked:
            s = jnp.where(c * tk + kcol <= qpos, s, -jnp.inf)
        return s

    def pass_max(c, masked):
        for h in range(MLA_HEADS):
            s = raw_scores(c, h, masked)
            s_scr[c, h] = s
            m = m_scr[h]
            for j in range(tk // LANES):
                m = jnp.maximum(m, s[:, j * LANES:(j + 1) * LANES])
            m_scr[h] = m

    def pass_sum(c, masked):
        off = pl.multiple_of(c * tk, tk)
        for h in range(MLA_HEADS):
            s = s_scr[c, h]
            m = m_scr[h]
            p = jnp.concatenate([jnp.exp2((s[:, j * LANES:(j + 1) * LANES] - m) * exp_scale)
                                 for j in range(tk // LANES)], axis=1).astype(BF16)
            acc_scr[h] += _dot(p, v_ref[pl.ds(off, tk), h * 128:(h + 1) * 128])

    def run(step):
        def body(c, carry):
            step(c, False)
            return carry

        lax.fori_loop(0, n_chunks - 1, body, 0)
        step(n_chunks - 1, True)

    run(pass_max)
    for h in range(MLA_HEADS):
        m_scr[h] = jnp.broadcast_to(jnp.max(m_scr[h], axis=-1, keepdims=True), (tq, LANES))
    run(pass_sum)
    for p in range(MLA_HEADS // 2):
        even, odd = acc_scr[2 * p], acc_scr[2 * p + 1]
        o_ref[:, 128 * p:128 * p + 128] = jnp.where(low64, even / pltpu.roll(even, 64, 1),
                                                    pltpu.roll(odd, 64, 1) / odd)


def _mla_prompt(qcat, kcat, vmla, batch, seq, tq, tk):
    nq = seq // tq
    return pl.pallas_call(
        functools.partial(_mla_prompt_kernel, tk=tk),
        grid=(batch, nq),
        in_specs=[pl.BlockSpec((tq, 1024), lambda b, i: (b * nq + i, 0)),
                  pl.BlockSpec((seq, 1024), lambda b, i: (b, 0)),
                  pl.BlockSpec((seq, 1024), lambda b, i: (b, 0))],
        out_specs=pl.BlockSpec((tq, 512), lambda b, i: (b * nq + i, 0)),
        out_shape=jax.ShapeDtypeStruct((batch * seq, 512), F32),
        scratch_shapes=[pltpu.VMEM((MLA_HEADS, tq, LANES), F32), pltpu.VMEM((MLA_HEADS, tq, LANES), F32),
                        pltpu.VMEM((seq // tk, MLA_HEADS, tq, tk), F32)],
        compiler_params=_cparams(("parallel", "arbitrary")),
    )(qcat, kcat, vmla)


def _sort_key(score):
    bits = lax.bitcast_convert_type(score + 0.0, I32)
    return bits ^ ((bits >> 31) & 0x7FFFFFFF)


def _kth_key(count_ge, n_sel, shape):
    def body(i, t):
        cand = t + (jnp.int32(1) << (31 - i))
        return jnp.where(count_ge(cand) >= n_sel, cand, t)

    return lax.fori_loop(0, 32, body, jnp.full(shape, INT_MIN, I32))


def _bucket(dist):
    max_exact = N_BUCKETS // 2
    d = jnp.maximum(dist, 0)
    log_ratio = jnp.log(jnp.maximum(d, max_exact).astype(F32) / max_exact) / math.log(MAX_DISTANCE / max_exact)
    large = jnp.minimum(max_exact + (log_ratio * (N_BUCKETS - max_exact)).astype(I32), N_BUCKETS - 1)
    return jnp.where(d < max_exact, d, large)


def _bias_kernel(rb_ref, o_ref, *, a, b, c):
    rows, cols = o_ref.shape[1], o_ref.shape[2]
    dist = (a * lax.broadcasted_iota(I32, (rows, cols), 0) + b * lax.broadcasted_iota(I32, (rows, cols), 1) + c)
    bucket = _bucket(dist)
    for h in range(DSA_HEADS):
        acc = jnp.zeros((rows, cols), F32)
        for n in range(N_BUCKETS):
            acc = jnp.where(bucket == n, rb_ref[n, h], acc)
        o_ref[h] = acc


def _bias_table(rel_bias, rows, cols, a, b, c):
    return pl.pallas_call(
        functools.partial(_bias_kernel, a=a, b=b, c=c),
        in_specs=[pl.BlockSpec(memory_space=pltpu.SMEM)],
        out_specs=pl.BlockSpec(memory_space=pltpu.VMEM),
        out_shape=jax.ShapeDtypeStruct((DSA_HEADS, rows, cols), F32),
    )(rel_bias)


def _colreduce(x, op):
    return op(x.reshape(x.shape[0] // 8, 8, x.shape[1]), axis=0)


def _dsa_prompt_kernel(rb_ref, qidx_ref, tail_ref, kidx_ref, q_ref, k_ref, v_ref, bias_ref, o_ref,
                       qs_scr, qd_scr, wb_scr, key_scr, sbq_scr, tie_scr, m_scr, p_scr, acc_scr, *, n_sel, idx_bits):
    tq = q_ref.shape[0]
    kq = tq // KB
    qi = pl.program_id(1)
    krow = lax.broadcasted_iota(I32, (KB, tq), 0)
    qcol = lax.broadcasted_iota(I32, (KB, tq), 1)
    low64 = lax.broadcasted_iota(I32, (tq, LANES), 1) < 64

    tail = tail_ref[...]
    for h in range(IDX_HEADS):
        mine = low64 if h % 2 == 0 else jnp.logical_not(low64)
        zero = jnp.zeros((tq, LANES), BF16)
        qs_scr[h * tq:(h + 1) * tq, :] = jnp.where(mine, qidx_ref[:, (h // 2) * 128:(h // 2 + 1) * 128], zero)
        qd_scr[h * tq:(h + 1) * tq, :] = jnp.where(mine, q_ref[:, (h // 2) * 128:(h // 2 + 1) * 128], zero)
        wb_scr[h] = jnp.broadcast_to(tail[:, T_WIDX + h:T_WIDX + h + 1], (tq, LANES))

    def rows_of(c):
        return pl.ds(pl.multiple_of(c * tq, tq), tq)

    def score_chunk(c, in_tile):
        d = _dot_nt(qs_scr[...], kidx_ref[rows_of(c), :])
        for t in range(kq):
            sc = jnp.zeros((tq, KB), F32)
            for h in range(IDX_HEADS):
                sc = sc + jnp.maximum(d[h * tq:(h + 1) * tq, t * KB:(t + 1) * KB], 0.0) * wb_scr[h]
            sc_t = (sc * IDX_SCALE).T
            if in_tile:
                sc_t = jnp.where(t * KB + krow <= qcol, sc_t, -jnp.inf)
            key_scr[c * kq + t] = _sort_key(sc_t)

    def score_body(c, carry):
        score_chunk(c, False)
        return carry

    lax.fori_loop(0, qi, score_body, 0)
    score_chunk(qi, True)
    n_chunks = qi + 1

    def count(pred):
        def body(c, acc):
            for t in range(kq):
                kb = c * kq + t
                acc = acc + jnp.where(pred(key_scr[kb], kb), 1.0, 0.0)
            return acc
        acc = lax.fori_loop(0, n_chunks, body, jnp.zeros((KB, tq), F32))
        return jnp.sum(_colreduce(acc, jnp.sum), axis=0, keepdims=True)

    thr = _kth_key(lambda cand: count(lambda k, kb: k >= cand), float(n_sel), (1, tq))
    cnt_gt = count(lambda k, kb: k > thr)
    cnt_ge = count(lambda k, kb: k >= thr)
    tie_scr[...] = jnp.full((1, tq), 2 ** 30, I32)
    excess = jnp.where((cnt_ge > float(n_sel)) & (thr > INT_MIN), 1.0, 0.0)

    @pl.when(jnp.max(excess) > 0.0)
    def _():
        need = float(n_sel) - cnt_gt

        def body(i, p):
            cand = p + (jnp.int32(1) << (idx_bits - 1 - i))
            c = count(lambda k, kb: (k == thr) & (kb * KB + krow < cand))
            return jnp.where(c < need, cand, p)

        tie_scr[...] = lax.fori_loop(0, idx_bits, body, jnp.zeros((1, tq), I32))

    last_tie = tie_scr[...]

    def mask_chunk(c, in_tile):
        for t in range(kq):
            kb = c * kq + t
            key = key_scr[kb]
            sel = (key > thr) | ((key == thr) & (kb * KB + krow <= last_tie))
            if in_tile:
                sel = sel & (t * KB + krow <= qcol)
            sbq_scr[c, :, t * KB:(t + 1) * KB] = jnp.where(sel, 0.0, NEG_BIG).T

    def mask_body(c, carry):
        mask_chunk(c, False)
        return carry

    lax.fori_loop(0, qi, mask_body, 0)
    mask_chunk(qi, True)

    far_bias = [rb_ref[N_BUCKETS - 1, h] for h in range(DSA_HEADS)]

    def scores(c, g, kind):
        sg = _dot_nt(qd_scr[g * DSA_GROUP * tq:(g + 1) * DSA_GROUP * tq, :], k_ref[rows_of(c), g * 128:(g + 1) * 128])
        sb = sbq_scr[c]
        out = []
        for hh in range(DSA_GROUP):
            h = g * DSA_GROUP + hh
            cols = []
            for t in range(kq):
                s = sg[hh * tq:(hh + 1) * tq, t * KB:(t + 1) * KB] + sb[:, t * KB:(t + 1) * KB]
                if kind == "tile":
                    s = s + (bias_ref[1 + t, h] - far_bias[h])
                elif kind == "before" and t == kq - 1:
                    s = s + (bias_ref[0, h] - far_bias[h])
                cols.append(s)
            out.append(cols)
        return out

    m_scr[...] = jnp.full(m_scr.shape, NEG_BIG, F32)
    acc_scr[...] = jnp.zeros(acc_scr.shape, F32)

    def pass_max(c, kind):
        for g in range(DSA_KV_HEADS):
            for hh, cols in enumerate(scores(c, g, kind)):
                h = g * DSA_GROUP + hh
                m = m_scr[h]
                for s in cols:
                    m = jnp.maximum(m, s)
                m_scr[h] = m

    def pass_sum(c, kind):
        for g in range(DSA_KV_HEADS):
            for hh, cols in enumerate(scores(c, g, kind)):
                m = m_scr[g * DSA_GROUP + hh]
                for t, s in enumerate(cols):
                    p_scr[hh * tq:(hh + 1) * tq, t * KB:(t + 1) * KB] = jnp.exp(s - m).astype(BF16)
            acc_scr[g] += _dot(p_scr[...], v_ref[rows_of(c), g * 128:(g + 1) * 128])

    def run(step):
        def body(c, carry):
            step(c, "far")
            return carry

        lax.fori_loop(0, qi - 1, body, 0)

        @pl.when(qi >= 1)
        def _():
            step(qi - 1, "before")

        step(qi, "tile")

    run(pass_max)
    for h in range(DSA_HEADS):
        m_scr[h] = jnp.broadcast_to(jnp.max(m_scr[h], axis=-1, keepdims=True), (tq, LANES))
    run(pass_sum)

    for p in range(DSA_HEADS // 2):
        g, hh = (2 * p) // DSA_GROUP, (2 * p) % DSA_GROUP
        even = acc_scr[g, hh * tq:(hh + 1) * tq, :]
        odd = acc_scr[g, (hh + 1) * tq:(hh + 2) * tq, :]
        o_ref[:, 128 * p:128 * p + 128] = jnp.where(low64, even / pltpu.roll(even, 64, 1),
                                                    pltpu.roll(odd, 64, 1) / odd)


def _dsa_prompt(rel_bias, qidx, tail, kidxdup, qdsa, kdup, vone, batch, seq, tq):
    assert MAX_DISTANCE <= KB, "keys two or more blocks back must all fall in the last bucket"
    nq = seq // tq
    nkb = seq // KB
    kq = tq // KB
    n_sel = min(TOPK_MAX, seq // 4)
    idx_bits = max(1, (seq - 1).bit_length())
    bias = jnp.stack([_bias_table(rel_bias, tq, KB, 1, -1, KB - t * KB) for t in range(kq + 1)])
    kern = functools.partial(_dsa_prompt_kernel, n_sel=n_sel, idx_bits=idx_bits)
    qblk = lambda w: pl.BlockSpec((tq, w), lambda b, i: (b * nq + i, 0))
    seqblk = lambda w: pl.BlockSpec((seq, w), lambda b, i: (b, 0))
    return pl.pallas_call(
        kern,
        grid=(batch, nq),
        in_specs=[pl.BlockSpec(memory_space=pltpu.SMEM),
                  qblk(512), qblk(LANES), seqblk(LANES), qblk(512), seqblk(256), seqblk(256), _full(bias.shape)],
        out_specs=qblk(512),
        out_shape=jax.ShapeDtypeStruct((batch * seq, 512), F32),
        scratch_shapes=[pltpu.VMEM((IDX_HEADS * tq, LANES), BF16), pltpu.VMEM((DSA_HEADS * tq, LANES), BF16),
                        pltpu.VMEM((IDX_HEADS, tq, LANES), F32),
                        pltpu.VMEM((nkb, KB, tq), I32), pltpu.VMEM((nq, tq, tq), F32),
                        pltpu.VMEM((1, tq), I32), pltpu.VMEM((DSA_HEADS, tq, LANES), F32),
                        pltpu.VMEM((DSA_GROUP * tq, tq), BF16),
                        pltpu.VMEM((DSA_KV_HEADS, DSA_GROUP * tq, LANES), F32)],
        compiler_params=_cparams(("parallel", "arbitrary")),
    )(rel_bias, qidx, tail, kidxdup, qdsa, kdup, vone, bias)


def _fetch_pages(cache_ref, pt_ref, buf, sem, step, n_steps, locate, pps):
    def copy(slot, i, page):
        return pltpu.make_async_copy(cache_ref.at[0, page], buf.at[slot, i], sem.at[slot])

    def start(st, slot):
        b, first = locate(st)
        for i in range(pps):
            copy(slot, i, pt_ref[b, first + i]).start()

    @pl.when(step == 0)
    def _():
        start(step, 0)

    @pl.when(step + 1 < n_steps)
    def _():
        start(step + 1, (step + 1) % 2)

    slot = step % 2
    for i in range(pps):
        copy(slot, i, 0).wait()
    return slot


def _page_scratch(pps, width):
    return [pltpu.VMEM((2, pps, width, PAGE_SIZE), F32), pltpu.SemaphoreType.DMA((2,))]


def _pages_feature_major(cache):
    c = jnp.moveaxis(cache, 2, -1)
    return c.reshape(cache.shape[0], cache.shape[1], -1, PAGE_SIZE)


def _sample_score_kernel(pt_ref, q_ref, w_ref, cache_ref, o_ref, buf, sem, kb_scr, *, pps, grp):
    bo, c, bi = pl.program_id(0), pl.program_id(1), pl.program_id(2)
    n_chunks = pl.num_programs(1)
    step = (bo * n_chunks + c) * grp + bi

    def locate(st):
        return (st // (n_chunks * grp)) * grp + st % grp, ((st // grp) % n_chunks) * pps

    slot = _fetch_pages(cache_ref, pt_ref, buf, sem, step, pl.num_programs(0) * n_chunks * grp, locate, pps)
    for i in range(pps):
        kb_scr[:, i * PAGE_SIZE:(i + 1) * PAGE_SIZE] = buf[slot, i].astype(BF16)
    d = _dot(q_ref[...], kb_scr[...])
    sc = jnp.sum(jnp.maximum(d, 0.0) * w_ref[...], axis=0, keepdims=True) * IDX_SCALE
    o_ref[pl.ds(bi, 1), :] = sc


def _sample_scores(page_table, qidx3, widx3, cache_idx_t, pps):
    dec, n_pages = page_table.shape
    n_chunks = n_pages // pps
    grp = 8
    grid_spec = pltpu.PrefetchScalarGridSpec(
        num_scalar_prefetch=1,
        grid=(dec // grp, n_chunks, grp),
        in_specs=[pl.BlockSpec((None, IDX_HEADS, IDX_DIM), lambda bo, c, bi, pt: (bo * grp + bi, 0, 0)),
                  pl.BlockSpec((None, IDX_HEADS, 1), lambda bo, c, bi, pt: (bo * grp + bi, 0, 0)),
                  pl.BlockSpec(memory_space=pl.ANY)],
        out_specs=pl.BlockSpec((grp, pps * PAGE_SIZE), lambda bo, c, bi, pt: (bo, c)),
        scratch_shapes=_page_scratch(pps, IDX_DIM) + [pltpu.VMEM((IDX_DIM, pps * PAGE_SIZE), BF16)],
    )
    return pl.pallas_call(
        functools.partial(_sample_score_kernel, pps=pps, grp=grp), grid_spec=grid_spec,
        out_shape=jax.ShapeDtypeStruct((dec, n_pages * PAGE_SIZE), F32),
        compiler_params=_cparams(("arbitrary", "arbitrary", "arbitrary")),
    )(page_table, qidx3, widx3, cache_idx_t)


def _sample_select_kernel(sc_ref, qidx_ref, tail_ref, selp_ref, seln_ref, key_scr, *, n_sel, idx_bits):
    rows, past = sc_ref.shape
    nblk = past // LANES
    lane = lax.broadcasted_iota(I32, (rows, LANES), 1)
    tail = tail_ref[...]
    kidx2 = jnp.where(lane < 64, tail, pltpu.roll(tail, 64, 1))
    sc_new = jnp.zeros((rows, 1), F32)
    for h in range(IDX_HEADS):
        prod = qidx_ref[:, (h // 2) * 128:(h // 2 + 1) * 128].astype(F32) * kidx2.astype(BF16).astype(F32)
        mine = (lane < 64) if h % 2 == 0 else (lane >= 64)
        dot = jnp.sum(jnp.where(mine, prod, 0.0), axis=-1, keepdims=True)
        sc_new = sc_new + jnp.maximum(dot, 0.0) * tail[:, T_WIDX + h:T_WIDX + h + 1]
    key_new = _sort_key(sc_new * IDX_SCALE)

    for j in range(nblk):
        key_scr[j] = _sort_key(sc_ref[:, j * LANES:(j + 1) * LANES])

    def count(pred, pred_new):
        def body(j, c):
            return c + jnp.where(pred(key_scr[j], j), 1.0, 0.0)
        c = lax.fori_loop(0, nblk, body, jnp.zeros((rows, LANES), F32), unroll=4)
        return jnp.sum(c, axis=-1, keepdims=True) + jnp.where(pred_new, 1.0, 0.0)

    thr = _kth_key(lambda cand: count(lambda k, j: k >= cand, key_new >= cand), float(n_sel), (rows, 1))
    cnt_gt = count(lambda k, j: k > thr, key_new > thr)
    need = float(n_sel) - cnt_gt

    def body(i, p):
        cand = p + (jnp.int32(1) << (idx_bits - 1 - i))
        c = count(lambda k, j: (k == thr) & (j * LANES + lane < cand), (key_new == thr) & (past < cand))
        return jnp.where(c < need, cand, p)

    last_tie = lax.fori_loop(0, idx_bits, body, jnp.zeros((rows, 1), I32))

    for j in range(nblk):
        k = key_scr[j]
        sel = (k > thr) | ((k == thr) & (j * LANES + lane <= last_tie))
        selp_ref[:, j * LANES:(j + 1) * LANES] = jnp.where(sel, 1.0, 0.0)
    sel_new = (key_new > thr) | ((key_new == thr) & (past <= last_tie))
    seln_ref[...] = jnp.broadcast_to(jnp.where(sel_new, 1.0, 0.0), (rows, LANES))


def _sample_select(scores, qidx, tail):
    dec, past = scores.shape
    n_sel = min(TOPK_MAX, (past + 1) // 4)
    idx_bits = max(1, past.bit_length())
    rows = 64 if dec % 64 == 0 else dec
    kern = functools.partial(_sample_select_kernel, n_sel=n_sel, idx_bits=idx_bits)
    return pl.pallas_call(
        kern,
        grid=(dec // rows,),
        in_specs=[pl.BlockSpec((rows, past), lambda i: (i, 0)), pl.BlockSpec((rows, 512), lambda i: (i, 0)),
                  pl.BlockSpec((rows, LANES), lambda i: (i, 0))],
        out_specs=[pl.BlockSpec((rows, past), lambda i: (i, 0)), pl.BlockSpec((rows, LANES), lambda i: (i, 0))],
        out_shape=[jax.ShapeDtypeStruct((dec, past), F32), jax.ShapeDtypeStruct((dec, LANES), F32)],
        scratch_shapes=[pltpu.VMEM((past // LANES, rows, LANES), I32)],
        compiler_params=_cparams(("parallel",)),
    )(scores, qidx, tail)


def _mla_sample_kernel(pt_ref, qn_ref, qr_ref, rown_ref, wuk_ref, wuv_ref, gk_ref, cache_ref, o_ref,
                       buf, sem, lat_scr, kr_scr, a_scr, m_scr, l_scr, acc_scr, *, pps):
    c = pl.program_id(1)
    n_chunks = pl.num_programs(1)
    heads = MLA_HEADS
    slot = _fetch_pages(cache_ref, pt_ref, buf, sem, pl.program_id(0) * n_chunks + c,
                        pl.num_programs(0) * n_chunks, lambda st: (st // n_chunks, (st % n_chunks) * pps), pps)

    @pl.when(c == 0)
    def _():
        m_scr[...] = jnp.full(m_scr.shape, -jnp.inf, F32)
        l_scr[...] = jnp.zeros(l_scr.shape, F32)
        acc_scr[...] = jnp.zeros(acc_scr.shape, F32)
        qg = qn_ref[...] * gk_ref[...]
        own = (lax.broadcasted_iota(I32, (heads, heads * MLA_NOPE), 1) // MLA_NOPE
               == lax.broadcasted_iota(I32, (heads, heads * MLA_NOPE), 0))
        a_scr[...] = _dot(jnp.where(own, qg, 0.0).astype(BF16), wuk_ref[...]).astype(BF16)

    qr = qr_ref[...]

    def scores(latb, krb):
        knt = _dot(wuk_ref[...], latb)
        ms = jnp.mean((knt * knt).reshape(heads, MLA_NOPE, latb.shape[1]), axis=1)
        return (_dot(a_scr[...], latb) * lax.rsqrt(ms + EPS) + _dot(qr, krb)) * MLA_SCALE

    def update(latb, krb, n_valid):
        n = latb.shape[1]
        sub = min(n, 512)
        s = jnp.concatenate([scores(latb[:, j:j + sub], krb[:, j:j + sub]) for j in range(0, n, sub)], axis=1)
        if n_valid < n:
            s = jnp.where(lax.broadcasted_iota(I32, s.shape, 1) < n_valid, s, -jnp.inf)
        m_old = m_scr[...]
        m_new = jnp.maximum(m_old, jnp.max(s, axis=-1, keepdims=True))
        alpha = jnp.exp(m_old - m_new)
        p = jnp.exp(s - m_new)
        l_scr[...] = alpha * l_scr[...] + jnp.sum(p, axis=-1, keepdims=True)
        m_scr[...] = m_new
        acc_scr[...] = alpha * acc_scr[...] + _dot_nt(p.astype(BF16), latb)

    for i in range(pps):
        lat_scr[:, i * PAGE_SIZE:(i + 1) * PAGE_SIZE] = buf[slot, i, 0:MLA_KV_RANK, :].astype(BF16)
        kr_scr[:, i * PAGE_SIZE:(i + 1) * PAGE_SIZE] = buf[slot, i, MLA_KV_RANK:MLA_ROW, :].astype(BF16)
    update(lat_scr[...], kr_scr[...], lat_scr.shape[1])

    @pl.when(c == n_chunks - 1)
    def _():
        rn = jnp.broadcast_to(rown_ref[...], (MLA_ROW, PAGE_SIZE))
        update(rn[0:MLA_KV_RANK, :].astype(BF16), rn[MLA_KV_RANK:MLA_ROW, :].astype(BF16), 1)
        o_lat = (acc_scr[...] / l_scr[...]).astype(BF16)
        full = _dot(o_lat, wuv_ref[...])
        hd = lax.broadcasted_iota(I32, full.shape, 1) // MLA_V
        own = hd == lax.broadcasted_iota(I32, full.shape, 0)
        o_ref[...] = jnp.sum(jnp.where(own, full, 0.0), axis=0, keepdims=True)


def _mla_sample(page_table, qn_row, qr, row_new, wuk_t, wuv, gk_row, cache_mla, pps):
    dec, n_pages = page_table.shape
    n_chunks = n_pages // pps
    grid_spec = pltpu.PrefetchScalarGridSpec(
        num_scalar_prefetch=1,
        grid=(dec, n_chunks),
        in_specs=[pl.BlockSpec((None, 1, MLA_HEADS * MLA_NOPE), lambda b, c, pt: (b, 0, 0)),
                  pl.BlockSpec((None, MLA_HEADS, MLA_ROPE), lambda b, c, pt: (b, 0, 0)),
                  pl.BlockSpec((None, MLA_ROW, 1), lambda b, c, pt: (b, 0, 0)),
                  pl.BlockSpec(wuk_t.shape, lambda b, c, pt: (0, 0)),
                  pl.BlockSpec(wuv.shape, lambda b, c, pt: (0, 0)),
                  pl.BlockSpec(gk_row.shape, lambda b, c, pt: (0, 0)),
                  pl.BlockSpec(memory_space=pl.ANY)],
        out_specs=pl.BlockSpec((None, 1, MLA_HEADS * MLA_V), lambda b, c, pt: (b, 0, 0)),
        scratch_shapes=_page_scratch(pps, MLA_ROW) + [
            pltpu.VMEM((MLA_KV_RANK, pps * PAGE_SIZE), BF16), pltpu.VMEM((MLA_ROPE, pps * PAGE_SIZE), BF16),
            pltpu.VMEM((MLA_HEADS, MLA_KV_RANK), BF16),
            pltpu.VMEM((MLA_HEADS, 1), F32), pltpu.VMEM((MLA_HEADS, 1), F32),
            pltpu.VMEM((MLA_HEADS, MLA_KV_RANK), F32)],
    )
    return pl.pallas_call(
        functools.partial(_mla_sample_kernel, pps=pps), grid_spec=grid_spec,
        out_shape=jax.ShapeDtypeStruct((dec, 1, MLA_HEADS * MLA_V), F32),
        compiler_params=_cparams(("arbitrary", "arbitrary")),
    )(page_table, qn_row, qr, row_new, wuk_t, wuv, gk_row, cache_mla)


def _dsa_sample_kernel(pt_ref, q_ref, kvn_ref, selp_ref, seln_ref, biasp_ref, biasn_ref, cache_ref, o_ref,
                       buf, sem, k_scr, v_scr, m_scr, l_scr, acc_scr, *, pps):
    c = pl.program_id(1)
    n_chunks = pl.num_programs(1)
    slot = _fetch_pages(cache_ref, pt_ref, buf, sem, pl.program_id(0) * n_chunks + c,
                        pl.num_programs(0) * n_chunks, lambda st: (st // n_chunks, (st % n_chunks) * pps), pps)

    @pl.when(c == 0)
    def _():
        m_scr[...] = jnp.full(m_scr.shape, NEG_BIG, F32)
        l_scr[...] = jnp.zeros(l_scr.shape, F32)
        acc_scr[...] = jnp.zeros(acc_scr.shape, F32)

    q = q_ref[...]

    def update(s, sel, pv_of):
        sh = jnp.where(sel, s, NEG_BIG)
        m_old = m_scr[...]
        m_new = jnp.maximum(m_old, jnp.max(sh, axis=-1, keepdims=True))
        alpha = jnp.exp(m_old - m_new)
        p = jnp.where(sel, jnp.exp(sh - m_new), 0.0)
        l_scr[...] = alpha * l_scr[...] + jnp.sum(p, axis=-1, keepdims=True)
        m_scr[...] = m_new
        acc_scr[...] = alpha * acc_scr[...] + pv_of(p)

    for i in range(pps):
        k_scr[:, i * PAGE_SIZE:(i + 1) * PAGE_SIZE] = buf[slot, i, 0:128, :].astype(BF16)
        v_scr[:, i * PAGE_SIZE:(i + 1) * PAGE_SIZE] = buf[slot, i, 128:256, :].astype(BF16)
    s = _dot(q.astype(BF16), k_scr[...]) + biasp_ref[...]
    update(s, selp_ref[...] > 0.5, lambda p: _dot_nt(p.astype(BF16), v_scr[...]))

    @pl.when(c == n_chunks - 1)
    def _():
        kvn = kvn_ref[...]
        kn = kvn[:, 0:128].astype(BF16).astype(F32)
        vn = kvn[:, 128:256].astype(BF16).astype(F32)
        s_new = jnp.sum(q.astype(F32) * kn, axis=-1, keepdims=True) + biasn_ref[...]
        update(s_new, seln_ref[:, 0:1] > 0.5, lambda p: p.astype(BF16).astype(F32) * vn)
        o = acc_scr[...] / l_scr[...]
        lane = lax.broadcasted_iota(I32, o.shape, 1)
        row = lax.broadcasted_iota(I32, o.shape, 0)
        own = (lane // DSA_HEAD_DIM) == (row // DSA_GROUP)
        o_ref[...] = jnp.where(own, o, 0.0)


def _dsa_sample(page_table, qs, kv_new, selp, seln, bias_past, bias_new, cache_kv4, pps):
    dec, n_pages = page_table.shape
    n_chunks = n_pages // pps
    chunk = pps * PAGE_SIZE
    grid_spec = pltpu.PrefetchScalarGridSpec(
        num_scalar_prefetch=1,
        grid=(dec, n_chunks),
        in_specs=[pl.BlockSpec((None, DSA_HEADS, LANES), lambda b, c, pt: (b, 0, 0)),
                  pl.BlockSpec((None, 1, 256), lambda b, c, pt: (b, 0, 0)),
                  pl.BlockSpec((None, 1, chunk), lambda b, c, pt: (b, 0, c)),
                  pl.BlockSpec((None, 1, LANES), lambda b, c, pt: (b, 0, 0)),
                  pl.BlockSpec((DSA_HEADS, chunk), lambda b, c, pt: (0, c)),
                  pl.BlockSpec((DSA_HEADS, 1), lambda b, c, pt: (0, 0)),
                  pl.BlockSpec(memory_space=pl.ANY)],
        out_specs=pl.BlockSpec((None, DSA_HEADS, LANES), lambda b, c, pt: (b, 0, 0)),
        scratch_shapes=_page_scratch(pps, 256) + [
            pltpu.VMEM((128, chunk), BF16), pltpu.VMEM((128, chunk), BF16),
            pltpu.VMEM((DSA_HEADS, 1), F32), pltpu.VMEM((DSA_HEADS, 1), F32),
            pltpu.VMEM((DSA_HEADS, LANES), F32)],
    )
    return pl.pallas_call(
        functools.partial(_dsa_sample_kernel, pps=pps), grid_spec=grid_spec,
        out_shape=jax.ShapeDtypeStruct((dec, DSA_HEADS, LANES), F32),
        compiler_params=_cparams(("arbitrary", "arbitrary")),
    )(page_table, qs, kv_new, selp, seln, bias_past, bias_new, cache_kv4)


def _out_kernel(x_ref, oa_ref, ob_ref, gm_ref, sf_ref, scf_ref, gf_ref, gout_ref, gffn_ref,
                wout_ref, wg_ref, wu_ref, wdown_ref, y_ref, *, ff_chunk):
    def rms(v, g):
        return v * lax.rsqrt(jnp.mean(v * v, axis=-1, keepdims=True) + EPS) * g

    half = oa_ref.shape[1]
    na = rms(oa_ref[...], gout_ref[:, 0:half]).astype(BF16)
    nb = rms(ob_ref[...], gout_ref[:, half:2 * half]).astype(BF16)
    mix = _dot(na, wout_ref[0:half, :]) + _dot(nb, wout_ref[half:2 * half, :])
    x1 = x_ref[...] + gm_ref[...] * mix
    hb = (rms(x1, gffn_ref[...]) * (1.0 + scf_ref[...]) + sf_ref[...]).astype(BF16)
    acc = jnp.zeros(x1.shape, F32)
    for j in range(D_FF // ff_chunk):
        g = _dot(hb, wg_ref[:, j * ff_chunk:(j + 1) * ff_chunk])
        u = _dot(hb, wu_ref[:, j * ff_chunk:(j + 1) * ff_chunk])
        a = (g / (1.0 + jnp.exp(-g))) * u
        acc = acc + _dot(a.astype(BF16), wdown_ref[j * ff_chunk:(j + 1) * ff_chunk, :])
    y_ref[...] = x1 + gf_ref[...] * acc


def _out_stage(x2d, oa, ob, gate_m, shift_f, scale_f, gate_f, wp, tm, per_token, tiles_per_seq):
    t_total = x2d.shape[0]
    if per_token:
        mod_spec = pl.BlockSpec((tm, D_MODEL), lambda i: (i, 0))
    else:
        mod_spec = pl.BlockSpec((None, 1, D_MODEL), lambda i: (i // tiles_per_seq, 0, 0))
    consts = [wp["g_out"], wp["g_ffn"], wp["w_out"], wp["w_gate"], wp["w_up"], wp["w_down"]]
    tok = lambda w: pl.BlockSpec((tm, w), lambda i: (i, 0))
    return pl.pallas_call(
        functools.partial(_out_kernel, ff_chunk=256),
        grid=(t_total // tm,),
        in_specs=[tok(D_MODEL), tok(512), tok(512), mod_spec, mod_spec, mod_spec, mod_spec]
                 + [_full(c.shape) for c in consts],
        out_specs=tok(D_MODEL),
        out_shape=jax.ShapeDtypeStruct((t_total, D_MODEL), F32),
        compiler_params=_cparams(("parallel",)),
    )(x2d, oa, ob, gate_m, shift_f, scale_f, gate_f, *consts)


def _prep_weights(w_in, g_norm_mix, g_norm_ffn, g_q_lat, w_uq, g_kv_lat, w_ukv, g_mla_q_nope, g_mla_q_rope,
                  g_mla_k_nope, g_mla_k_rope, g_dsa_q, g_dsa_k, g_out, w_out, w_ffn_in, w_ffn_out):
    splits = np.cumsum([MLA_Q_RANK, MLA_KV_RANK, MLA_ROPE, 512, 128, 128, 512, IDX_DIM, IDX_HEADS])
    s = [0] + splits.tolist()
    col = lambda i: w_in[:, s[i]:s[i + 1]]
    pad = jnp.zeros((D_MODEL, C_END - C_TAIL - IDX_DIM - MLA_ROPE - IDX_HEADS), w_in.dtype)
    w_in_r = jnp.concatenate([col(0), col(1), col(3), col(4), col(5), col(6), col(7), col(2), col(8), pad], axis=1)
    zq = jnp.zeros((MLA_Q_RANK, MLA_HEADS, LANES - MLA_NOPE - MLA_ROPE), w_uq.dtype)
    w_uq_cat = jnp.concatenate([w_uq, zq], axis=2).reshape(MLA_Q_RANK, MLA_HEADS * LANES)
    zk = jnp.zeros((MLA_KV_RANK, MLA_HEADS, LANES - MLA_NOPE), w_ukv.dtype)
    w_uk_cat = jnp.concatenate([w_ukv[:, :, :MLA_NOPE], zk], axis=2).reshape(MLA_KV_RANK, MLA_HEADS * LANES)
    w_uv = w_ukv[:, :, MLA_NOPE:].reshape(MLA_KV_RANK, MLA_HEADS * MLA_V)
    zv = jnp.zeros((MLA_KV_RANK, MLA_HEADS, LANES - MLA_V), w_ukv.dtype)
    w_uv_pad = jnp.concatenate([w_ukv[:, :, MLA_NOPE:], zv], axis=2).reshape(MLA_KV_RANK, MLA_HEADS * LANES)
    w_uk_t = w_ukv[:, :, :MLA_NOPE].reshape(MLA_KV_RANK, MLA_HEADS * MLA_NOPE).T
    z32 = jnp.zeros((LANES - MLA_NOPE - MLA_ROPE,), F32)
    z64 = jnp.zeros((LANES - MLA_NOPE,), F32)
    g_qcat = jnp.tile(jnp.concatenate([g_mla_q_nope, g_mla_q_rope, z32]), MLA_HEADS)
    g_kcat = jnp.tile(jnp.concatenate([g_mla_k_nope, z64]), MLA_HEADS)
    g_tail = jnp.concatenate([jnp.zeros((T_ROPE,), F32), g_mla_k_rope, z32])
    row = lambda v: v.reshape(1, -1).astype(F32)
    return {
        "g_mix": row(g_norm_mix), "g_ffn": row(g_norm_ffn), "w_in": w_in_r.astype(BF16),
        "g_qlat": row(g_q_lat), "w_uq": w_uq_cat.astype(BF16), "g_qcat": row(g_qcat),
        "g_kvlat": row(g_kv_lat), "w_ukv": jnp.concatenate([w_uk_cat, w_uv_pad], axis=1).astype(BF16),
        "g_kcat": row(g_kcat), "g_tail": row(g_tail),
        "g_q": row(jnp.tile(g_dsa_q, DSA_HEADS)), "g_k": row(jnp.tile(g_dsa_k, DSA_KV_HEADS)),
        "m_cat": _group_matrix(256, [(0, 64), (64, 96), (128, 192), (192, 224)]),
        "m_64": _group_matrix(256, [(0, 64), (64, 128), (128, 192), (192, 256)]),
        "w_uk_t": w_uk_t.astype(BF16), "w_uv": w_uv.astype(BF16), "g_k_nope": g_mla_k_nope,
        "g_out": row(g_out), "w_out": w_out.astype(BF16),
        "w_gate": w_ffn_in[:, :D_FF].astype(BF16), "w_up": w_ffn_in[:, D_FF:].astype(BF16),
        "w_down": w_ffn_out.astype(BF16),
    }


def _rope_tables(pos):
    freq = ROPE_THETA ** (-jnp.arange(HALF_ROPE, dtype=F32) / HALF_ROPE)
    ang = pos.astype(F32)[:, None] * freq[None, :]
    cos, sin = jnp.cos(ang), jnp.sin(ang)
    n = pos.shape[0]
    ones = jnp.ones((n, T_ROPE), F32)
    zeros = jnp.zeros((n, T_ROPE), F32)
    tail1 = jnp.ones((n, LANES - T_ROPE - MLA_ROPE), F32)
    tail0 = jnp.zeros((n, LANES - T_ROPE - MLA_ROPE), F32)
    return (jnp.concatenate([ones, cos, cos, tail1], axis=1),
            jnp.concatenate([zeros, -sin, sin, tail0], axis=1))


def _pick_tile(n, prefs):
    for t in prefs:
        if n % t == 0:
            return t
    return n


def kernel(x_prompt, x_sample, c_prompt, c_sample, cache_mla, cache_kv, cache_idx, page_table, rel_bias, w_ada, b_ada, g_norm_mix, g_norm_ffn, w_in, g_q_lat, w_uq, g_kv_lat, w_ukv, g_mla_q_nope, g_mla_q_rope, g_mla_k_nope, g_mla_k_rope, g_dsa_q, g_dsa_k, g_out, w_out, w_ffn_in, w_ffn_out):
    assert w_ada.shape[0] == 1 and x_sample.shape[1] == 1, "one layer, one new token per sample"
    batch, seq, _ = x_prompt.shape
    dec = x_sample.shape[0]
    n_pages = page_table.shape[1]
    past = n_pages * PAGE_SIZE
    wp = _prep_weights(w_in[0], g_norm_mix[0], g_norm_ffn[0], g_q_lat[0], w_uq[0], g_kv_lat[0], w_ukv[0],
                       g_mla_q_nope[0], g_mla_q_rope[0], g_mla_k_nope[0], g_mla_k_rope[0], g_dsa_q[0], g_dsa_k[0],
                       g_out[0], w_out[0], w_ffn_in[0], w_ffn_out[0])

    mod = _ada(jnp.concatenate([c_prompt, c_sample], axis=0), w_ada[0], b_ada[0])
    mods_p = [m.reshape(batch, 1, D_MODEL) for m in jnp.split(mod[:batch], 6, axis=-1)]
    mods_s = jnp.split(mod[batch:], 6, axis=-1)

    tm = _pick_tile(seq, (512, 256, 128))
    xp2 = x_prompt.reshape(batch * seq, D_MODEL)
    cos_p, sin_p = _rope_tables(jnp.arange(seq))
    pin = _in_stage(xp2, mods_p[1], mods_p[0], cos_p, sin_p, wp, tm, per_token=False)
    o_mla_p = _mla_prompt(pin["qcat"], pin["kcat"], pin["vmla"], batch, seq,
                          _pick_tile(seq, (256, 128)), _pick_tile(seq, (512, 256, 128)))
    o_dsa_p = _dsa_prompt(rel_bias, pin["qidx"], pin["tail"], pin["kidxdup"], pin["qdsa"], pin["kdup"], pin["vone"],
                          batch, seq, _pick_tile(seq, (256, 128)))
    y_p = _out_stage(xp2, o_mla_p, o_dsa_p, mods_p[2], mods_p[3], mods_p[4], mods_p[5], wp, tm,
                     per_token=False, tiles_per_seq=seq // tm)

    xs2 = x_sample.reshape(dec, D_MODEL)
    cos_s, sin_s = _rope_tables(jnp.full((1,), past))
    sin_ = _in_stage(xs2, mods_s[1], mods_s[0], cos_s, sin_s, wp, dec, per_token=True)
    pps = _pick_tile(n_pages, (16, 8, 4, 2))
    pps_wide = _pick_tile(n_pages, (32, 16, 8, 4, 2))
    qidx3 = sin_["qidx"].reshape(dec, IDX_HEADS, IDX_DIM)
    widx3 = sin_["tail"][:, T_WIDX:T_WIDX + IDX_HEADS].reshape(dec, IDX_HEADS, 1)
    scores = _sample_scores(page_table, qidx3, widx3, _pages_feature_major(cache_idx), pps_wide)
    selp, seln = _sample_select(scores, sin_["qidx"], sin_["tail"])
    qc = sin_["qcat"].reshape(dec, MLA_HEADS, LANES)
    qn_row = qc[:, :, 0:MLA_NOPE].astype(F32).reshape(dec, 1, MLA_HEADS * MLA_NOPE)
    qr = qc[:, :, T_ROPE:T_ROPE + MLA_ROPE]
    gk_row = jnp.tile(wp["g_k_nope"], MLA_HEADS).reshape(1, MLA_HEADS * MLA_NOPE)
    o_mla_s = _mla_sample(page_table, qn_row, qr, sin_["row"].reshape(dec, MLA_ROW, 1), wp["w_uk_t"], wp["w_uv"],
                          gk_row, _pages_feature_major(cache_mla), pps).reshape(dec, MLA_HEADS * MLA_V)
    qd = sin_["qdsa"].astype(F32).reshape(dec, DSA_HEADS, DSA_HEAD_DIM)
    on_c = (jnp.arange(DSA_KV_HEADS)[None, :] == (jnp.arange(DSA_HEADS) // DSA_GROUP)[:, None]).astype(F32)
    qd2 = (qd[:, :, None, :] * on_c[None, :, :, None]).reshape(dec, DSA_HEADS, LANES)
    bias_past = _bias_table(rel_bias, 1, past, 0, -1, past).reshape(DSA_HEADS, past)
    bias_new = _bias_table(rel_bias, 1, LANES, 0, 0, 0)[:, 0, 0:1]
    o8 = _dsa_sample(page_table, qd2, sin_["kv"].reshape(dec, 1, 256), selp.reshape(dec, 1, past),
                     seln.reshape(dec, 1, LANES), bias_past, bias_new, _pages_feature_major(cache_kv), pps_wide)
    o_dsa_s = o8.reshape(dec, DSA_HEADS, DSA_KV_HEADS, DSA_HEAD_DIM).sum(axis=2).reshape(dec, DSA_HEADS * DSA_HEAD_DIM)
    y_s = _out_stage(xs2, o_mla_s, o_dsa_s, mods_s[2], mods_s[3], mods_s[4], mods_s[5], wp, dec,
                     per_token=True, tiles_per_seq=1)

    return (y_p.reshape(batch, seq, D_MODEL), y_s.reshape(dec, 1, D_MODEL),
            pin["row"].reshape(1, batch, seq, MLA_ROW),
            pin["kv"].reshape(1, batch, seq, 2, DSA_KV_HEADS, DSA_HEAD_DIM),
            pin["kidx"].reshape(1, batch, seq, IDX_DIM),
            sin_["row"].reshape(1, dec, 1, MLA_ROW),
            sin_["kv"].reshape(1, dec, 1, 2, DSA_KV_HEADS, DSA_HEAD_DIM),
            sin_["kidx"].reshape(1, dec, 1, IDX_DIM))
```

```python
import functools
import math

import numpy as np
import jax
import jax.numpy as jnp
from jax import lax
from jax.experimental import pallas as pl
from jax.experimental.pallas import tpu as pltpu

D_MODEL = 1024
PAGE_SIZE = 128
MLA_HEADS = 8
MLA_NOPE = 64
MLA_ROPE = 32
MLA_V = 64
MLA_Q_RANK = 384
MLA_KV_RANK = 256
MLA_ROW = MLA_KV_RANK + MLA_ROPE
DSA_HEADS = 8
DSA_KV_HEADS = 2
DSA_HEAD_DIM = 64
DSA_GROUP = DSA_HEADS // DSA_KV_HEADS
IDX_HEADS = 8
IDX_DIM = 64
TOPK_MAX = 256
N_BUCKETS = 32
MAX_DISTANCE = 128
ROPE_THETA = 10000.0
D_FF = ((8 * D_MODEL + 3 * 256 - 1) // (3 * 256)) * 256
EPS = 1e-6
MLA_SCALE = (MLA_NOPE + MLA_ROPE) ** -0.5
DSA_SCALE = DSA_HEAD_DIM ** -0.5
IDX_SCALE = (IDX_DIM * IDX_HEADS) ** -0.5

LANES = 128
VMEM_LIMIT = 56 * 1024 * 1024

F32 = jnp.float32
BF16 = jnp.bfloat16
I32 = jnp.int32
NEG_BIG = -1e30
INT_MIN = -(2 ** 31)

C_QLAT = 0
C_KVLAT = C_QLAT + MLA_Q_RANK
C_Q = C_KVLAT + MLA_KV_RANK
C_K = C_Q + DSA_HEADS * DSA_HEAD_DIM
C_V = C_K + DSA_KV_HEADS * DSA_HEAD_DIM
C_QIDX = C_V + DSA_KV_HEADS * DSA_HEAD_DIM
C_TAIL = C_QIDX + IDX_HEADS * IDX_DIM
C_END = C_TAIL + LANES
T_ROPE = IDX_DIM
T_WIDX = IDX_DIM + MLA_ROPE
HALF_ROPE = MLA_ROPE // 2
KB = 128


def _dot(a, b):
    return jnp.dot(a, b, preferred_element_type=F32)


def _dot_nt(a, b):
    return lax.dot_general(a, b, (((1,), (1,)), ((), ())), preferred_element_type=F32)


def _split(a):
    hi = a.astype(BF16)
    lo = (a - hi.astype(F32)).astype(BF16)
    return hi, lo


def _dot3(a, b):
    ah, al = _split(a)
    bh, bl = _split(b)
    return _dot(ah, bh) + (_dot(al, bh) + _dot(ah, bl))


def _cparams(sem):
    return pltpu.CompilerParams(dimension_semantics=sem, vmem_limit_bytes=VMEM_LIMIT)


def _full(shape):
    n = len(shape)
    return pl.BlockSpec(shape, lambda *a, _n=n: (0,) * _n, pipeline_mode=pl.Buffered(1))


def _ada_kernel(c_ref, w_ref, b_ref, o_ref):
    c = c_ref[...]
    s = c / (1.0 + jnp.exp(-c))
    o_ref[...] = _dot3(s, w_ref[...]) + b_ref[...]


def _ada(c, w, b):
    n = c.shape[0]
    return pl.pallas_call(
        _ada_kernel,
        grid=(6,),
        in_specs=[pl.BlockSpec((n, D_MODEL), lambda j: (0, 0)),
                  pl.BlockSpec((D_MODEL, D_MODEL), lambda j: (0, j)),
                  pl.BlockSpec((1, D_MODEL), lambda j: (0, j))],
        out_specs=pl.BlockSpec((n, D_MODEL), lambda j: (0, j)),
        out_shape=jax.ShapeDtypeStruct((n, 6 * D_MODEL), F32),
        compiler_params=_cparams(("arbitrary",)),
    )(c, w, b.reshape(1, -1))


def _group_matrix(width, groups):
    g = np.zeros((width, width), np.float32)
    for lo, hi in groups:
        g[lo:hi, lo:hi] = 1.0 / (hi - lo)
    return jnp.asarray(g, BF16)


def _in_kernel(x_ref, sc_ref, sh_ref, cos_ref, sin_ref, gmix_ref, win_ref, gqlat_ref, wuq_ref, gqcat_ref,
               gkvlat_ref, wukv_ref, gkcat_ref, gtail_ref, gq_ref, gk_ref, mcat_ref, m64_ref,
               qcat_o, kcat_o, vmla_o, row_o, kv_o, kidx_o, tail_o, kidxdup_o, qdsa_o, kdup_o, vone_o, qidx_o):
    tm = x_ref.shape[0]
    x = x_ref[...]
    h = x * lax.rsqrt(jnp.mean(x * x, axis=-1, keepdims=True) + EPS) * gmix_ref[...]
    hb = (h * (1.0 + sc_ref[...]) + sh_ref[...]).astype(BF16)

    def proj(lo, hi):
        return _dot(hb, win_ref[:, lo:hi])

    cos = cos_ref[...]
    sin = sin_ref[...]
    lane = lax.broadcasted_iota(I32, (tm, LANES), 1)
    first_half = lane < T_ROPE + HALF_ROPE
    rope_lanes = (lane >= T_ROPE) & (lane < T_ROPE + MLA_ROPE)
    low64 = lane < 64

    def rope(xh):
        rot = jnp.where(first_half, pltpu.roll(xh, LANES - HALF_ROPE, 1), pltpu.roll(xh, HALF_ROPE, 1))
        return xh * cos + rot * sin

    def gnorm(blk, m_ref):
        ms = _dot((blk * blk).astype(BF16), m_ref[...])
        return blk * lax.rsqrt(ms + EPS)

    ql = proj(C_QLAT, C_KVLAT)
    qln = ql * lax.rsqrt(jnp.mean(ql * ql, axis=-1, keepdims=True) + EPS) * gqlat_ref[...]
    qm = _dot(qln.astype(BF16), wuq_ref[...])
    for j in range(4):
        y = gnorm(qm[:, 256 * j:256 * j + 256], mcat_ref) * gqcat_ref[:, 256 * j:256 * j + 256]
        for t in range(2):
            qcat_o[:, 256 * j + 128 * t:256 * j + 128 * t + 128] = rope(y[:, 128 * t:128 * t + 128]).astype(BF16)

    tail = proj(C_TAIL, C_END)
    ssq = jnp.sum(jnp.where(rope_lanes, tail * tail, 0.0), axis=-1, keepdims=True) * (1.0 / MLA_ROPE)
    tn = jnp.where(rope_lanes, tail * lax.rsqrt(ssq + EPS) * gtail_ref[...], tail)
    tr = rope(tn)
    kidx_o[...] = tail[:, 0:IDX_DIM]
    tail_o[...] = tail
    kidxdup_o[...] = jnp.where(low64, tail, pltpu.roll(tail, 64, 1)).astype(BF16)
    krope = jnp.where(rope_lanes, tr, 0.0)

    kvl = proj(C_KVLAT, C_Q)
    lat = kvl * lax.rsqrt(jnp.mean(kvl * kvl, axis=-1, keepdims=True) + EPS) * gkvlat_ref[...]
    row_o[:, 0:MLA_KV_RANK] = lat
    row_o[:, MLA_KV_RANK:MLA_ROW] = tr[:, T_ROPE:T_ROPE + MLA_ROPE]
    latb = lat.astype(BF16)
    kx = _dot(latb, wukv_ref[:, 0:1024])
    for j in range(4):
        y = gnorm(kx[:, 256 * j:256 * j + 256], mcat_ref) * gkcat_ref[:, 256 * j:256 * j + 256]
        for t in range(2):
            kcat_o[:, 256 * j + 128 * t:256 * j + 128 * t + 128] = (y[:, 128 * t:128 * t + 128] + krope).astype(BF16)
    ones_half = (lax.broadcasted_iota(I32, (tm, 1024), 1) % LANES) >= MLA_V
    vmla_o[...] = jnp.where(ones_half, 1.0, _dot(latb, wukv_ref[:, 1024:2048])).astype(BF16)

    q = proj(C_Q, C_K)
    for j in range(2):
        qdsa_o[:, 256 * j:256 * j + 256] = (gnorm(q[:, 256 * j:256 * j + 256], m64_ref)
                                            * gq_ref[:, 256 * j:256 * j + 256] * DSA_SCALE).astype(BF16)
    k = proj(C_K, C_V)
    ms = _dot((k * k).astype(BF16), m64_ref[0:128, 0:128])
    kn = k * lax.rsqrt(ms + EPS) * gk_ref[...]
    v = proj(C_V, C_QIDX)
    kv_o[:, 0:128] = kn
    kv_o[:, 128:256] = v
    kr = pltpu.roll(kn, 64, 1)
    kdup_o[:, 0:128] = jnp.where(low64, kn, kr).astype(BF16)
    kdup_o[:, 128:256] = jnp.where(low64, kr, kn).astype(BF16)
    vone_o[:, 0:128] = jnp.where(low64, v, 1.0).astype(BF16)
    vone_o[:, 128:256] = jnp.where(low64, pltpu.roll(v, 64, 1), 1.0).astype(BF16)
    qidx_o[...] = proj(C_QIDX, C_TAIL).astype(BF16)


_IN_OUT_WIDTHS = (("qcat", 1024, BF16), ("kcat", 1024, BF16), ("vmla", 1024, BF16), ("row", MLA_ROW, F32),
                  ("kv", 256, F32), ("kidx", IDX_DIM, F32), ("tail", LANES, F32), ("kidxdup", LANES, BF16),
                  ("qdsa", 512, BF16), ("kdup", 256, BF16), ("vone", 256, BF16), ("qidx", 512, BF16))


def _in_stage(x2d, scale, shift, cos, sin, wp, tm, per_token):
    t_total = x2d.shape[0]
    n_tiles = t_total // tm
    if per_token:
        mod_spec = pl.BlockSpec((tm, D_MODEL), lambda i: (i, 0))
        tab_spec = pl.BlockSpec((1, LANES), lambda i: (0, 0))
    else:
        tiles_per_seq = cos.shape[0] // tm
        mod_spec = pl.BlockSpec((None, 1, D_MODEL), lambda i: (i // tiles_per_seq, 0, 0))
        tab_spec = pl.BlockSpec((tm, LANES), lambda i: (i % tiles_per_seq, 0))
    consts = [wp["g_mix"], wp["w_in"], wp["g_qlat"], wp["w_uq"], wp["g_qcat"], wp["g_kvlat"], wp["w_ukv"],
              wp["g_kcat"], wp["g_tail"], wp["g_q"], wp["g_k"], wp["m_cat"], wp["m_64"]]
    in_specs = ([pl.BlockSpec((tm, D_MODEL), lambda i: (i, 0)), mod_spec, mod_spec, tab_spec, tab_spec]
                + [_full(c.shape) for c in consts])
    out_specs = [pl.BlockSpec((tm, w), lambda i: (i, 0)) for _, w, _ in _IN_OUT_WIDTHS]
    out_shape = [jax.ShapeDtypeStruct((t_total, w), dt) for _, w, dt in _IN_OUT_WIDTHS]
    outs = pl.pallas_call(
        _in_kernel, grid=(n_tiles,), in_specs=in_specs, out_specs=out_specs, out_shape=out_shape,
        compiler_params=_cparams(("parallel",)),
    )(x2d, scale, shift, cos, sin, *consts)
    return {name: o for (name, _, _), o in zip(_IN_OUT_WIDTHS, outs)}


def _mla_prompt_kernel(q_ref, k_ref, v_ref, o_ref, m_scr, acc_scr, s_scr, *, tk):
    tq = q_ref.shape[0]
    qi = pl.program_id(1)
    n_chunks = (qi * tq + tq + tk - 1) // tk
    m_scr[...] = jnp.full(m_scr.shape, -jnp.inf, F32)
    acc_scr[...] = jnp.zeros(acc_scr.shape, F32)
    low64 = lax.broadcasted_iota(I32, (tq, LANES), 1) < 64
    qpos = qi * tq + lax.broadcasted_iota(I32, (tq, tk), 0)
    kcol = lax.broadcasted_iota(I32, (tq, tk), 1)
    exp_scale = MLA_SCALE * math.log2(math.e)

    def raw_scores(c, h, masked):
        off = pl.multiple_of(c * tk, tk)
        s = _dot_nt(q_ref[:, h * 128:(h + 1) * 128], k_ref[pl.ds(off, tk), h * 128:(h + 1) * 128])
        if masked:
            s = jnp.where(c * tk + kcol <= qpos, s, -jnp.inf)
        return s

    def pass_max(c, masked):
        for h in range(MLA_HEADS):
            s = raw_scores(c, h, masked)
            s_scr[c, h] = s
            m = m_scr[h]
            for j in range(tk // LANES):
                m = jnp.maximum(m, s[:, j * LANES:(j + 1) * LANES])
            m_scr[h] = m

    def pass_sum(c, masked):
        off = pl.multiple_of(c * tk, tk)
        for h in range(MLA_HEADS):
            s = s_scr[c, h]
            m = m_scr[h]
            p = jnp.concatenate([jnp.exp2((s[:, j * LANES:(j + 1) * LANES] - m) * exp_scale)
                                 for j in range(tk // LANES)], axis=1).astype(BF16)
            acc_scr[h] += _dot(p, v_ref[pl.ds(off, tk), h * 128:(h + 1) * 128])

    def run(step):
        def body(c, carry):
            step(c, False)
            return carry

        lax.fori_loop(0, n_chunks - 1, body, 0)
        step(n_chunks - 1, True)

    run(pass_max)
    for h in range(MLA_HEADS):
        m_scr[h] = jnp.broadcast_to(jnp.max(m_scr[h], axis=-1, keepdims=True), (tq, LANES))
    run(pass_sum)
    for p in range(MLA_HEADS // 2):
        even, odd = acc_scr[2 * p], acc_scr[2 * p + 1]
        o_ref[:, 128 * p:128 * p + 128] = jnp.where(low64, even / pltpu.roll(even, 64, 1),
                                                    pltpu.roll(odd, 64, 1) / odd)


def _mla_prompt(qcat, kcat, vmla, batch, seq, tq, tk):
    nq = seq // tq
    return pl.pallas_call(
        functools.partial(_mla_prompt_kernel, tk=tk),
        grid=(batch, nq),
        in_specs=[pl.BlockSpec((tq, 1024), lambda b, i: (b * nq + i, 0)),
                  pl.BlockSpec((seq, 1024), lambda b, i: (b, 0)),
                  pl.BlockSpec((seq, 1024), lambda b, i: (b, 0))],
        out_specs=pl.BlockSpec((tq, 512), lambda b, i: (b * nq + i, 0)),
        out_shape=jax.ShapeDtypeStruct((batch * seq, 512), F32),
        scratch_shapes=[pltpu.VMEM((MLA_HEADS, tq, LANES), F32), pltpu.VMEM((MLA_HEADS, tq, LANES), F32),
                        pltpu.VMEM((seq // tk, MLA_HEADS, tq, tk), F32)],
        compiler_params=_cparams(("parallel", "arbitrary")),
    )(qcat, kcat, vmla)


def _sort_key(score):
    bits = lax.bitcast_convert_type(score + 0.0, I32)
    return bits ^ ((bits >> 31) & 0x7FFFFFFF)


def _kth_key(count_ge, n_sel, shape):
    def body(i, t):
        cand = t + (jnp.int32(1) << (31 - i))
        return jnp.where(count_ge(cand) >= n_sel, cand, t)

    return lax.fori_loop(0, 32, body, jnp.full(shape, INT_MIN, I32))


def _bucket(dist):
    max_exact = N_BUCKETS // 2
    d = jnp.maximum(dist, 0)
    log_ratio = jnp.log(jnp.maximum(d, max_exact).astype(F32) / max_exact) / math.log(MAX_DISTANCE / max_exact)
    large = jnp.minimum(max_exact + (log_ratio * (N_BUCKETS - max_exact)).astype(I32), N_BUCKETS - 1)
    return jnp.where(d < max_exact, d, large)


def _bias_kernel(rb_ref, o_ref, *, a, b, c):
    rows, cols = o_ref.shape[1], o_ref.shape[2]
    dist = (a * lax.broadcasted_iota(I32, (rows, cols), 0) + b * lax.broadcasted_iota(I32, (rows, cols), 1) + c)
    bucket = _bucket(dist)
    for h in range(DSA_HEADS):
        acc = jnp.zeros((rows, cols), F32)
        for n in range(N_BUCKETS):
            acc = jnp.where(bucket == n, rb_ref[n, h], acc)
        o_ref[h] = acc


def _bias_table(rel_bias, rows, cols, a, b, c):
    return pl.pallas_call(
        functools.partial(_bias_kernel, a=a, b=b, c=c),
        in_specs=[pl.BlockSpec(memory_space=pltpu.SMEM)],
        out_specs=pl.BlockSpec(memory_space=pltpu.VMEM),
        out_shape=jax.ShapeDtypeStruct((DSA_HEADS, rows, cols), F32),
    )(rel_bias)


def _loop2(n, body):
    n = jnp.maximum(n, 0)

    def pair(j, carry):
        body(2 * j)
        body(2 * j + 1)
        return carry

    lax.fori_loop(0, n // 2, pair, 0)

    @pl.when(n % 2 == 1)
    def _():
        body(n - 1)


def _colreduce(x, op):
    return op(x.reshape(x.shape[0] // 8, 8, x.shape[1]), axis=0)


def _dsa_prompt_kernel(rb_ref, qidx_ref, tail_ref, kidx_ref, q_ref, k_ref, v_ref, bias_ref, o_ref,
                       qs_scr, qd_scr, wb_scr, key_scr, sbq_scr, tie_scr, m_scr, p_scr, acc_scr, s_scr, *, n_sel, idx_bits):
    tq = q_ref.shape[0]
    kq = tq // KB
    qi = pl.program_id(1)
    krow = lax.broadcasted_iota(I32, (KB, tq), 0)
    qcol = lax.broadcasted_iota(I32, (KB, tq), 1)
    low64 = lax.broadcasted_iota(I32, (tq, LANES), 1) < 64

    tail = tail_ref[...]
    for h in range(IDX_HEADS):
        mine = low64 if h % 2 == 0 else jnp.logical_not(low64)
        zero = jnp.zeros((tq, LANES), BF16)
        qs_scr[h * tq:(h + 1) * tq, :] = jnp.where(mine, qidx_ref[:, (h // 2) * 128:(h // 2 + 1) * 128], zero)
        qd_scr[h * tq:(h + 1) * tq, :] = jnp.where(mine, q_ref[:, (h // 2) * 128:(h // 2 + 1) * 128], zero)
        wb_scr[h] = jnp.broadcast_to(tail[:, T_WIDX + h:T_WIDX + h + 1], (tq, LANES))

    def rows_of(c):
        return pl.ds(pl.multiple_of(c * tq, tq), tq)

    def score_chunk(c, in_tile):
        d = _dot_nt(qs_scr[...], kidx_ref[rows_of(c), :])
        for t in range(kq):
            sc = jnp.zeros((tq, KB), F32)
            for h in range(IDX_HEADS):
                sc = sc + jnp.maximum(d[h * tq:(h + 1) * tq, t * KB:(t + 1) * KB], 0.0) * wb_scr[h]
            sc_t = (sc * IDX_SCALE).T
            if in_tile:
                sc_t = jnp.where(t * KB + krow <= qcol, sc_t, -jnp.inf)
            key_scr[c * kq + t] = _sort_key(sc_t)

    _loop2(qi, lambda c: score_chunk(c, False))
    score_chunk(qi, True)
    n_chunks = qi + 1

    def count(pred):
        def body(c, acc):
            for t in range(kq):
                kb = c * kq + t
                acc = acc + jnp.where(pred(key_scr[kb], kb), 1.0, 0.0)
            return acc
        acc = lax.fori_loop(0, n_chunks, body, jnp.zeros((KB, tq), F32))
        return jnp.sum(_colreduce(acc, jnp.sum), axis=0, keepdims=True)

    thr = _kth_key(lambda cand: count(lambda k, kb: k >= cand), float(n_sel), (1, tq))
    cnt_gt = count(lambda k, kb: k > thr)
    cnt_ge = count(lambda k, kb: k >= thr)
    tie_scr[...] = jnp.full((1, tq), 2 ** 30, I32)
    excess = jnp.where((cnt_ge > float(n_sel)) & (thr > INT_MIN), 1.0, 0.0)

    @pl.when(jnp.max(excess) > 0.0)
    def _():
        need = float(n_sel) - cnt_gt

        def body(i, p):
            cand = p + (jnp.int32(1) << (idx_bits - 1 - i))
            c = count(lambda k, kb: (k == thr) & (kb * KB + krow < cand))
            return jnp.where(c < need, cand, p)

        tie_scr[...] = lax.fori_loop(0, idx_bits, body, jnp.zeros((1, tq), I32))

    last_tie = tie_scr[...]

    def mask_chunk(c, in_tile):
        for t in range(kq):
            kb = c * kq + t
            key = key_scr[kb]
            sel = (key > thr) | ((key == thr) & (kb * KB + krow <= last_tie))
            if in_tile:
                sel = sel & (t * KB + krow <= qcol)
            sbq_scr[c, :, t * KB:(t + 1) * KB] = jnp.where(sel, 0.0, NEG_BIG).T

    _loop2(qi, lambda c: mask_chunk(c, False))
    mask_chunk(qi, True)

    far_bias = [rb_ref[N_BUCKETS - 1, h] for h in range(DSA_HEADS)]

    def scores(c, g, kind):
        sg = _dot_nt(qd_scr[g * DSA_GROUP * tq:(g + 1) * DSA_GROUP * tq, :], k_ref[rows_of(c), g * 128:(g + 1) * 128])
        sb = sbq_scr[c]
        out = []
        for hh in range(DSA_GROUP):
            h = g * DSA_GROUP + hh
            cols = []
            for t in range(kq):
                s = sg[hh * tq:(hh + 1) * tq, t * KB:(t + 1) * KB] + sb[:, t * KB:(t + 1) * KB]
                if kind == "tile":
                    s = s + (bias_ref[1 + t, h] - far_bias[h])
                elif kind == "before" and t == kq - 1:
                    s = s + (bias_ref[0, h] - far_bias[h])
                cols.append(s)
            out.append(cols)
        return out

    m_scr[...] = jnp.full(m_scr.shape, NEG_BIG, F32)
    acc_scr[...] = jnp.zeros(acc_scr.shape, F32)

    def pass_max(c, kind):
        for g in range(DSA_KV_HEADS):
            for hh, cols in enumerate(scores(c, g, kind)):
                h = g * DSA_GROUP + hh
                m = m_scr[h]
                for t, s in enumerate(cols):
                    s_scr[c, h, :, t * KB:(t + 1) * KB] = s
                    m = jnp.maximum(m, s)
                m_scr[h] = m

    def pass_sum(c, kind):
        for g in range(DSA_KV_HEADS):
            for hh in range(DSA_GROUP):
                h = g * DSA_GROUP + hh
                m = m_scr[h]
                for t in range(kq):
                    s = s_scr[c, h, :, t * KB:(t + 1) * KB]
                    p_scr[g, hh * tq:(hh + 1) * tq, t * KB:(t + 1) * KB] = jnp.exp(s - m).astype(BF16)
            acc_scr[g] += _dot(p_scr[g], v_ref[rows_of(c), g * 128:(g + 1) * 128])

    def run(step):
        _loop2(qi - 1, lambda c: step(c, "far"))

        @pl.when(qi >= 1)
        def _():
            step(qi - 1, "before")

        step(qi, "tile")

    run(pass_max)
    for h in range(DSA_HEADS):
        m_scr[h] = jnp.broadcast_to(jnp.max(m_scr[h], axis=-1, keepdims=True), (tq, LANES))
    run(pass_sum)

    for p in range(DSA_HEADS // 2):
        g, hh = (2 * p) // DSA_GROUP, (2 * p) % DSA_GROUP
        even = acc_scr[g, hh * tq:(hh + 1) * tq, :]
        odd = acc_scr[g, (hh + 1) * tq:(hh + 2) * tq, :]
        o_ref[:, 128 * p:128 * p + 128] = jnp.where(low64, even / pltpu.roll(even, 64, 1),
                                                    pltpu.roll(odd, 64, 1) / odd)


def _dsa_prompt(rel_bias, qidx, tail, kidxdup, qdsa, kdup, vone, batch, seq, tq):
    assert MAX_DISTANCE <= KB, "keys two or more blocks back must all fall in the last bucket"
    nq = seq // tq
    nkb = seq // KB
    kq = tq // KB
    n_sel = min(TOPK_MAX, seq // 4)
    idx_bits = max(1, (seq - 1).bit_length())
    bias = jnp.stack([_bias_table(rel_bias, tq, KB, 1, -1, KB - t * KB) for t in range(kq + 1)])
    kern = functools.partial(_dsa_prompt_kernel, n_sel=n_sel, idx_bits=idx_bits)
    qblk = lambda w: pl.BlockSpec((tq, w), lambda b, i: (b * nq + i, 0))
    seqblk = lambda w: pl.BlockSpec((seq, w), lambda b, i: (b, 0))
    return pl.pallas_call(
        kern,
        grid=(batch, nq),
        in_specs=[pl.BlockSpec(memory_space=pltpu.SMEM),
                  qblk(512), qblk(LANES), seqblk(LANES), qblk(512), seqblk(256), seqblk(256), _full(bias.shape)],
        out_specs=qblk(512),
        out_shape=jax.ShapeDtypeStruct((batch * seq, 512), F32),
        scratch_shapes=[pltpu.VMEM((IDX_HEADS * tq, LANES), BF16), pltpu.VMEM((DSA_HEADS * tq, LANES), BF16),
                        pltpu.VMEM((IDX_HEADS, tq, LANES), F32),
                        pltpu.VMEM((nkb, KB, tq), I32), pltpu.VMEM((nq, tq, tq), F32),
                        pltpu.VMEM((1, tq), I32), pltpu.VMEM((DSA_HEADS, tq, LANES), F32),
                        pltpu.VMEM((DSA_KV_HEADS, DSA_GROUP * tq, tq), BF16),
                        pltpu.VMEM((DSA_KV_HEADS, DSA_GROUP * tq, LANES), F32),
                        pltpu.VMEM((nq, DSA_HEADS, tq, tq), F32)],
        compiler_params=_cparams(("parallel", "arbitrary")),
    )(rel_bias, qidx, tail, kidxdup, qdsa, kdup, vone, bias)


def _fetch_pages(cache_ref, pt_ref, buf, sem, step, n_steps, locate, pps):
    def copy(slot, i, page):
        return pltpu.make_async_copy(cache_ref.at[0, page], buf.at[slot, i], sem.at[slot])

    def start(st, slot):
        b, first = locate(st)
        for i in range(pps):
            copy(slot, i, pt_ref[b, first + i]).start()

    @pl.when(step == 0)
    def _():
        start(step, 0)

    @pl.when(step + 1 < n_steps)
    def _():
        start(step + 1, (step + 1) % 2)

    slot = step % 2
    for i in range(pps):
        copy(slot, i, 0).wait()
    return slot


def _page_scratch(pps, width):
    return [pltpu.VMEM((2, pps, width, PAGE_SIZE), F32), pltpu.SemaphoreType.DMA((2,))]


def _pages_feature_major(cache):
    c = jnp.moveaxis(cache, 2, -1)
    return c.reshape(cache.shape[0], cache.shape[1], -1, PAGE_SIZE)


def _sample_score_kernel(pt_ref, q_ref, w_ref, cache_ref, o_ref, buf, sem, kb_scr, *, pps, grp):
    bo, c, bi = pl.program_id(0), pl.program_id(1), pl.program_id(2)
    n_chunks = pl.num_programs(1)
    step = (bo * n_chunks + c) * grp + bi

    def locate(st):
        return (st // (n_chunks * grp)) * grp + st % grp, ((st // grp) % n_chunks) * pps

    slot = _fetch_pages(cache_ref, pt_ref, buf, sem, step, pl.num_programs(0) * n_chunks * grp, locate, pps)
    for i in range(pps):
        kb_scr[:, i * PAGE_SIZE:(i + 1) * PAGE_SIZE] = buf[slot, i].astype(BF16)
    d = _dot(q_ref[...], kb_scr[...])
    sc = jnp.sum(jnp.maximum(d, 0.0) * w_ref[...], axis=0, keepdims=True) * IDX_SCALE
    o_ref[pl.ds(bi, 1), :] = sc


def _sample_scores(page_table, qidx3, widx3, cache_idx_t, pps):
    dec, n_pages = page_table.shape
    n_chunks = n_pages // pps
    grp = 8
    grid_spec = pltpu.PrefetchScalarGridSpec(
        num_scalar_prefetch=1,
        grid=(dec // grp, n_chunks, grp),
        in_specs=[pl.BlockSpec((None, IDX_HEADS, IDX_DIM), lambda bo, c, bi, pt: (bo * grp + bi, 0, 0)),
                  pl.BlockSpec((None, IDX_HEADS, 1), lambda bo, c, bi, pt: (bo * grp + bi, 0, 0)),
                  pl.BlockSpec(memory_space=pl.ANY)],
        out_specs=pl.BlockSpec((grp, pps * PAGE_SIZE), lambda bo, c, bi, pt: (bo, c)),
        scratch_shapes=_page_scratch(pps, IDX_DIM) + [pltpu.VMEM((IDX_DIM, pps * PAGE_SIZE), BF16)],
    )
    return pl.pallas_call(
        functools.partial(_sample_score_kernel, pps=pps, grp=grp), grid_spec=grid_spec,
        out_shape=jax.ShapeDtypeStruct((dec, n_pages * PAGE_SIZE), F32),
        compiler_params=_cparams(("arbitrary", "arbitrary", "arbitrary")),
    )(page_table, qidx3, widx3, cache_idx_t)


def _sample_select_kernel(sc_ref, qidx_ref, tail_ref, selp_ref, seln_ref, key_scr, *, n_sel, idx_bits):
    rows, past = sc_ref.shape
    nblk = past // LANES
    lane = lax.broadcasted_iota(I32, (rows, LANES), 1)
    tail = tail_ref[...]
    kidx2 = jnp.where(lane < 64, tail, pltpu.roll(tail, 64, 1))
    sc_new = jnp.zeros((rows, 1), F32)
    for h in range(IDX_HEADS):
        prod = qidx_ref[:, (h // 2) * 128:(h // 2 + 1) * 128].astype(F32) * kidx2.astype(BF16).astype(F32)
        mine = (lane < 64) if h % 2 == 0 else (lane >= 64)
        dot = jnp.sum(jnp.where(mine, prod, 0.0), axis=-1, keepdims=True)
        sc_new = sc_new + jnp.maximum(dot, 0.0) * tail[:, T_WIDX + h:T_WIDX + h + 1]
    key_new = _sort_key(sc_new * IDX_SCALE)

    for j in range(nblk):
        key_scr[j] = _sort_key(sc_ref[:, j * LANES:(j + 1) * LANES])

    def count(pred, pred_new):
        def body(j, c):
            return c + jnp.where(pred(key_scr[j], j), 1.0, 0.0)
        c = lax.fori_loop(0, nblk, body, jnp.zeros((rows, LANES), F32), unroll=4)
        return jnp.sum(c, axis=-1, keepdims=True) + jnp.where(pred_new, 1.0, 0.0)

    thr = _kth_key(lambda cand: count(lambda k, j: k >= cand, key_new >= cand), float(n_sel), (rows, 1))
    cnt_gt = count(lambda k, j: k > thr, key_new > thr)
    need = float(n_sel) - cnt_gt

    def body(i, p):
        cand = p + (jnp.int32(1) << (idx_bits - 1 - i))
        c = count(lambda k, j: (k == thr) & (j * LANES + lane < cand), (key_new == thr) & (past < cand))
        return jnp.where(c < need, cand, p)

    last_tie = lax.fori_loop(0, idx_bits, body, jnp.zeros((rows, 1), I32))

    for j in range(nblk):
        k = key_scr[j]
        sel = (k > thr) | ((k == thr) & (j * LANES + lane <= last_tie))
        selp_ref[:, j * LANES:(j + 1) * LANES] = jnp.where(sel, 1.0, 0.0)
    sel_new = (key_new > thr) | ((key_new == thr) & (past <= last_tie))
    seln_ref[...] = jnp.broadcast_to(jnp.where(sel_new, 1.0, 0.0), (rows, LANES))


def _sample_select(scores, qidx, tail):
    dec, past = scores.shape
    n_sel = min(TOPK_MAX, (past + 1) // 4)
    idx_bits = max(1, past.bit_length())
    rows = 64 if dec % 64 == 0 else dec
    kern = functools.partial(_sample_select_kernel, n_sel=n_sel, idx_bits=idx_bits)
    return pl.pallas_call(
        kern,
        grid=(dec // rows,),
        in_specs=[pl.BlockSpec((rows, past), lambda i: (i, 0)), pl.BlockSpec((rows, 512), lambda i: (i, 0)),
                  pl.BlockSpec((rows, LANES), lambda i: (i, 0))],
        out_specs=[pl.BlockSpec((rows, past), lambda i: (i, 0)), pl.BlockSpec((rows, LANES), lambda i: (i, 0))],
        out_shape=[jax.ShapeDtypeStruct((dec, past), F32), jax.ShapeDtypeStruct((dec, LANES), F32)],
        scratch_shapes=[pltpu.VMEM((past // LANES, rows, LANES), I32)],
        compiler_params=_cparams(("parallel",)),
    )(scores, qidx, tail)


def _mla_sample_kernel(pt_ref, qn_ref, qr_ref, rown_ref, wuk_ref, wuv_ref, gk_ref, cache_ref, o_ref,
                       buf, sem, lat_scr, kr_scr, a_scr, m_scr, l_scr, acc_scr, *, pps):
    c = pl.program_id(1)
    n_chunks = pl.num_programs(1)
    heads = MLA_HEADS
    slot = _fetch_pages(cache_ref, pt_ref, buf, sem, pl.program_id(0) * n_chunks + c,
                        pl.num_programs(0) * n_chunks, lambda st: (st // n_chunks, (st % n_chunks) * pps), pps)

    @pl.when(c == 0)
    def _():
        m_scr[...] = jnp.full(m_scr.shape, -jnp.inf, F32)
        l_scr[...] = jnp.zeros(l_scr.shape, F32)
        acc_scr[...] = jnp.zeros(acc_scr.shape, F32)
        qg = qn_ref[...] * gk_ref[...]
        own = (lax.broadcasted_iota(I32, (heads, heads * MLA_NOPE), 1) // MLA_NOPE
               == lax.broadcasted_iota(I32, (heads, heads * MLA_NOPE), 0))
        a_scr[...] = _dot(jnp.where(own, qg, 0.0).astype(BF16), wuk_ref[...]).astype(BF16)

    qr = qr_ref[...]

    def scores(latb, krb):
        knt = _dot(wuk_ref[...], latb)
        ms = jnp.mean((knt * knt).reshape(heads, MLA_NOPE, latb.shape[1]), axis=1)
        return (_dot(a_scr[...], latb) * lax.rsqrt(ms + EPS) + _dot(qr, krb)) * MLA_SCALE

    def update(latb, krb, n_valid):
        n = latb.shape[1]
        sub = min(n, 512)
        s = jnp.concatenate([scores(latb[:, j:j + sub], krb[:, j:j + sub]) for j in range(0, n, sub)], axis=1)
        if n_valid < n:
            s = jnp.where(lax.broadcasted_iota(I32, s.shape, 1) < n_valid, s, -jnp.inf)
        m_old = m_scr[...]
        m_new = jnp.maximum(m_old, jnp.max(s, axis=-1, keepdims=True))
        alpha = jnp.exp(m_old - m_new)
        p = jnp.exp(s - m_new)
        l_scr[...] = alpha * l_scr[...] + jnp.sum(p, axis=-1, keepdims=True)
        m_scr[...] = m_new
        acc_scr[...] = alpha * acc_scr[...] + _dot_nt(p.astype(BF16), latb)

    for i in range(pps):
        lat_scr[:, i * PAGE_SIZE:(i + 1) * PAGE_SIZE] = buf[slot, i, 0:MLA_KV_RANK, :].astype(BF16)
        kr_scr[:, i * PAGE_SIZE:(i + 1) * PAGE_SIZE] = buf[slot, i, MLA_KV_RANK:MLA_ROW, :].astype(BF16)
    update(lat_scr[...], kr_scr[...], lat_scr.shape[1])

    @pl.when(c == n_chunks - 1)
    def _():
        rn = jnp.broadcast_to(rown_ref[...], (MLA_ROW, PAGE_SIZE))
        update(rn[0:MLA_KV_RANK, :].astype(BF16), rn[MLA_KV_RANK:MLA_ROW, :].astype(BF16), 1)
        o_lat = (acc_scr[...] / l_scr[...]).astype(BF16)
        full = _dot(o_lat, wuv_ref[...])
        hd = lax.broadcasted_iota(I32, full.shape, 1) // MLA_V
        own = hd == lax.broadcasted_iota(I32, full.shape, 0)
        o_ref[...] = jnp.sum(jnp.where(own, full, 0.0), axis=0, keepdims=True)


def _mla_sample(page_table, qn_row, qr, row_new, wuk_t, wuv, gk_row, cache_mla, pps):
    dec, n_pages = page_table.shape
    n_chunks = n_pages // pps
    grid_spec = pltpu.PrefetchScalarGridSpec(
        num_scalar_prefetch=1,
        grid=(dec, n_chunks),
        in_specs=[pl.BlockSpec((None, 1, MLA_HEADS * MLA_NOPE), lambda b, c, pt: (b, 0, 0)),
                  pl.BlockSpec((None, MLA_HEADS, MLA_ROPE), lambda b, c, pt: (b, 0, 0)),
                  pl.BlockSpec((None, MLA_ROW, 1), lambda b, c, pt: (b, 0, 0)),
                  pl.BlockSpec(wuk_t.shape, lambda b, c, pt: (0, 0)),
                  pl.BlockSpec(wuv.shape, lambda b, c, pt: (0, 0)),
                  pl.BlockSpec(gk_row.shape, lambda b, c, pt: (0, 0)),
                  pl.BlockSpec(memory_space=pl.ANY)],
        out_specs=pl.BlockSpec((None, 1, MLA_HEADS * MLA_V), lambda b, c, pt: (b, 0, 0)),
        scratch_shapes=_page_scratch(pps, MLA_ROW) + [
            pltpu.VMEM((MLA_KV_RANK, pps * PAGE_SIZE), BF16), pltpu.VMEM((MLA_ROPE, pps * PAGE_SIZE), BF16),
            pltpu.VMEM((MLA_HEADS, MLA_KV_RANK), BF16),
            pltpu.VMEM((MLA_HEADS, 1), F32), pltpu.VMEM((MLA_HEADS, 1), F32),
            pltpu.VMEM((MLA_HEADS, MLA_KV_RANK), F32)],
    )
    return pl.pallas_call(
        functools.partial(_mla_sample_kernel, pps=pps), grid_spec=grid_spec,
        out_shape=jax.ShapeDtypeStruct((dec, 1, MLA_HEADS * MLA_V), F32),
        compiler_params=_cparams(("arbitrary", "arbitrary")),
    )(page_table, qn_row, qr, row_new, wuk_t, wuv, gk_row, cache_mla)


def _dsa_sample_kernel(pt_ref, q_ref, kvn_ref, selp_ref, seln_ref, biasp_ref, biasn_ref, cache_ref, o_ref,
                       buf, sem, k_scr, v_scr, m_scr, l_scr, acc_scr, *, pps):
    c = pl.program_id(1)
    n_chunks = pl.num_programs(1)
    slot = _fetch_pages(cache_ref, pt_ref, buf, sem, pl.program_id(0) * n_chunks + c,
                        pl.num_programs(0) * n_chunks, lambda st: (st // n_chunks, (st % n_chunks) * pps), pps)

    @pl.when(c == 0)
    def _():
        m_scr[...] = jnp.full(m_scr.shape, NEG_BIG, F32)
        l_scr[...] = jnp.zeros(l_scr.shape, F32)
        acc_scr[...] = jnp.zeros(acc_scr.shape, F32)

    q = q_ref[...]

    def update(s, sel, pv_of):
        sh = jnp.where(sel, s, NEG_BIG)
        m_old = m_scr[...]
        m_new = jnp.maximum(m_old, jnp.max(sh, axis=-1, keepdims=True))
        alpha = jnp.exp(m_old - m_new)
        p = jnp.where(sel, jnp.exp(sh - m_new), 0.0)
        l_scr[...] = alpha * l_scr[...] + jnp.sum(p, axis=-1, keepdims=True)
        m_scr[...] = m_new
        acc_scr[...] = alpha * acc_scr[...] + pv_of(p)

    for i in range(pps):
        k_scr[:, i * PAGE_SIZE:(i + 1) * PAGE_SIZE] = buf[slot, i, 0:128, :].astype(BF16)
        v_scr[:, i * PAGE_SIZE:(i + 1) * PAGE_SIZE] = buf[slot, i, 128:256, :].astype(BF16)
    s = _dot(q.astype(BF16), k_scr[...]) + biasp_ref[...]
    update(s, selp_ref[...] > 0.5, lambda p: _dot_nt(p.astype(BF16), v_scr[...]))

    @pl.when(c == n_chunks - 1)
    def _():
        kvn = kvn_ref[...]
        kn = kvn[:, 0:128].astype(BF16).astype(F32)
        vn = kvn[:, 128:256].astype(BF16).astype(F32)
        s_new = jnp.sum(q.astype(F32) * kn, axis=-1, keepdims=True) + biasn_ref[...]
        update(s_new, seln_ref[:, 0:1] > 0.5, lambda p: p.astype(BF16).astype(F32) * vn)
        o = acc_scr[...] / l_scr[...]
        lane = lax.broadcasted_iota(I32, o.shape, 1)
        row = lax.broadcasted_iota(I32, o.shape, 0)
        own = (lane // DSA_HEAD_DIM) == (row // DSA_GROUP)
        o_ref[...] = jnp.where(own, o, 0.0)


def _dsa_sample(page_table, qs, kv_new, selp, seln, bias_past, bias_new, cache_kv4, pps):
    dec, n_pages = page_table.shape
    n_chunks = n_pages // pps
    chunk = pps * PAGE_SIZE
    grid_spec = pltpu.PrefetchScalarGridSpec(
        num_scalar_prefetch=1,
        grid=(dec, n_chunks),
        in_specs=[pl.BlockSpec((None, DSA_HEADS, LANES), lambda b, c, pt: (b, 0, 0)),
                  pl.BlockSpec((None, 1, 256), lambda b, c, pt: (b, 0, 0)),
                  pl.BlockSpec((None, 1, chunk), lambda b, c, pt: (b, 0, c)),
                  pl.BlockSpec((None, 1, LANES), lambda b, c, pt: (b, 0, 0)),
                  pl.BlockSpec((DSA_HEADS, chunk), lambda b, c, pt: (0, c)),
                  pl.BlockSpec((DSA_HEADS, 1), lambda b, c, pt: (0, 0)),
                  pl.BlockSpec(memory_space=pl.ANY)],
        out_specs=pl.BlockSpec((None, DSA_HEADS, LANES), lambda b, c, pt: (b, 0, 0)),
        scratch_shapes=_page_scratch(pps, 256) + [
            pltpu.VMEM((128, chunk), BF16), pltpu.VMEM((128, chunk), BF16),
            pltpu.VMEM((DSA_HEADS, 1), F32), pltpu.VMEM((DSA_HEADS, 1), F32),
            pltpu.VMEM((DSA_HEADS, LANES), F32)],
    )
    return pl.pallas_call(
        functools.partial(_dsa_sample_kernel, pps=pps), grid_spec=grid_spec,
        out_shape=jax.ShapeDtypeStruct((dec, DSA_HEADS, LANES), F32),
        compiler_params=_cparams(("arbitrary", "arbitrary")),
    )(page_table, qs, kv_new, selp, seln, bias_past, bias_new, cache_kv4)


def _out_kernel(x_ref, oa_ref, ob_ref, gm_ref, sf_ref, scf_ref, gf_ref, gout_ref, gffn_ref,
                wout_ref, wg_ref, wu_ref, wdown_ref, y_ref, *, ff_chunk):
    def rms(v, g):
        return v * lax.rsqrt(jnp.mean(v * v, axis=-1, keepdims=True) + EPS) * g

    half = oa_ref.shape[1]
    na = rms(oa_ref[...], gout_ref[:, 0:half]).astype(BF16)
    nb = rms(ob_ref[...], gout_ref[:, half:2 * half]).astype(BF16)
    mix = _dot(na, wout_ref[0:half, :]) + _dot(nb, wout_ref[half:2 * half, :])
    x1 = x_ref[...] + gm_ref[...] * mix
    hb = (rms(x1, gffn_ref[...]) * (1.0 + scf_ref[...]) + sf_ref[...]).astype(BF16)
    acc = jnp.zeros(x1.shape, F32)
    for j in range(D_FF // ff_chunk):
        g = _dot(hb, wg_ref[:, j * ff_chunk:(j + 1) * ff_chunk])
        u = _dot(hb, wu_ref[:, j * ff_chunk:(j + 1) * ff_chunk])
        a = (g / (1.0 + jnp.exp(-g))) * u
        acc = acc + _dot(a.astype(BF16), wdown_ref[j * ff_chunk:(j + 1) * ff_chunk, :])
    y_ref[...] = x1 + gf_ref[...] * acc


def _out_stage(x2d, oa, ob, gate_m, shift_f, scale_f, gate_f, wp, tm, per_token, tiles_per_seq):
    t_total = x2d.shape[0]
    if per_token:
        mod_spec = pl.BlockSpec((tm, D_MODEL), lambda i: (i, 0))
    else:
        mod_spec = pl.BlockSpec((None, 1, D_MODEL), lambda i: (i // tiles_per_seq, 0, 0))
    consts = [wp["g_out"], wp["g_ffn"], wp["w_out"], wp["w_gate"], wp["w_up"], wp["w_down"]]
    tok = lambda w: pl.BlockSpec((tm, w), lambda i: (i, 0))
    return pl.pallas_call(
        functools.partial(_out_kernel, ff_chunk=256),
        grid=(t_total // tm,),
        in_specs=[tok(D_MODEL), tok(512), tok(512), mod_spec, mod_spec, mod_spec, mod_spec]
                 + [_full(c.shape) for c in consts],
        out_specs=tok(D_MODEL),
        out_shape=jax.ShapeDtypeStruct((t_total, D_MODEL), F32),
        compiler_params=_cparams(("parallel",)),
    )(x2d, oa, ob, gate_m, shift_f, scale_f, gate_f, *consts)


def _prep_weights(w_in, g_norm_mix, g_norm_ffn, g_q_lat, w_uq, g_kv_lat, w_ukv, g_mla_q_nope, g_mla_q_rope,
                  g_mla_k_nope, g_mla_k_rope, g_dsa_q, g_dsa_k, g_out, w_out, w_ffn_in, w_ffn_out):
    splits = np.cumsum([MLA_Q_RANK, MLA_KV_RANK, MLA_ROPE, 512, 128, 128, 512, IDX_DIM, IDX_HEADS])
    s = [0] + splits.tolist()
    col = lambda i: w_in[:, s[i]:s[i + 1]]
    pad = jnp.zeros((D_MODEL, C_END - C_TAIL - IDX_DIM - MLA_ROPE - IDX_HEADS), w_in.dtype)
    w_in_r = jnp.concatenate([col(0), col(1), col(3), col(4), col(5), col(6), col(7), col(2), col(8), pad], axis=1)
    zq = jnp.zeros((MLA_Q_RANK, MLA_HEADS, LANES - MLA_NOPE - MLA_ROPE), w_uq.dtype)
    w_uq_cat = jnp.concatenate([w_uq, zq], axis=2).reshape(MLA_Q_RANK, MLA_HEADS * LANES)
    zk = jnp.zeros((MLA_KV_RANK, MLA_HEADS, LANES - MLA_NOPE), w_ukv.dtype)
    w_uk_cat = jnp.concatenate([w_ukv[:, :, :MLA_NOPE], zk], axis=2).reshape(MLA_KV_RANK, MLA_HEADS * LANES)
    w_uv = w_ukv[:, :, MLA_NOPE:].reshape(MLA_KV_RANK, MLA_HEADS * MLA_V)
    zv = jnp.zeros((MLA_KV_RANK, MLA_HEADS, LANES - MLA_V), w_ukv.dtype)
    w_uv_pad = jnp.concatenate([w_ukv[:, :, MLA_NOPE:], zv], axis=2).reshape(MLA_KV_RANK, MLA_HEADS * LANES)
    w_uk_t = w_ukv[:, :, :MLA_NOPE].reshape(MLA_KV_RANK, MLA_HEADS * MLA_NOPE).T
    z32 = jnp.zeros((LANES - MLA_NOPE - MLA_ROPE,), F32)
    z64 = jnp.zeros((LANES - MLA_NOPE,), F32)
    g_qcat = jnp.tile(jnp.concatenate([g_mla_q_nope, g_mla_q_rope, z32]), MLA_HEADS)
    g_kcat = jnp.tile(jnp.concatenate([g_mla_k_nope, z64]), MLA_HEADS)
    g_tail = jnp.concatenate([jnp.zeros((T_ROPE,), F32), g_mla_k_rope, z32])
    row = lambda v: v.reshape(1, -1).astype(F32)
    return {
        "g_mix": row(g_norm_mix), "g_ffn": row(g_norm_ffn), "w_in": w_in_r.astype(BF16),
        "g_qlat": row(g_q_lat), "w_uq": w_uq_cat.astype(BF16), "g_qcat": row(g_qcat),
        "g_kvlat": row(g_kv_lat), "w_ukv": jnp.concatenate([w_uk_cat, w_uv_pad], axis=1).astype(BF16),
        "g_kcat": row(g_kcat), "g_tail": row(g_tail),
        "g_q": row(jnp.tile(g_dsa_q, DSA_HEADS)), "g_k": row(jnp.tile(g_dsa_k, DSA_KV_HEADS)),
        "m_cat": _group_matrix(256, [(0, 64), (64, 96), (128, 192), (192, 224)]),
        "m_64": _group_matrix(256, [(0, 64), (64, 128), (128, 192), (192, 256)]),
        "w_uk_t": w_uk_t.astype(BF16), "w_uv": w_uv.astype(BF16), "g_k_nope": g_mla_k_nope,
        "g_out": row(g_out), "w_out": w_out.astype(BF16),
        "w_gate": w_ffn_in[:, :D_FF].astype(BF16), "w_up": w_ffn_in[:, D_FF:].astype(BF16),
        "w_down": w_ffn_out.astype(BF16),
    }


def _rope_tables(pos):
    freq = ROPE_THETA ** (-jnp.arange(HALF_ROPE, dtype=F32) / HALF_ROPE)
    ang = pos.astype(F32)[:, None] * freq[None, :]
    cos, sin = jnp.cos(ang), jnp.sin(ang)
    n = pos.shape[0]
    ones = jnp.ones((n, T_ROPE), F32)
    zeros = jnp.zeros((n, T_ROPE), F32)
    tail1 = jnp.ones((n, LANES - T_ROPE - MLA_ROPE), F32)
    tail0 = jnp.zeros((n, LANES - T_ROPE - MLA_ROPE), F32)
    return (jnp.concatenate([ones, cos, cos, tail1], axis=1),
            jnp.concatenate([zeros, -sin, sin, tail0], axis=1))


def _pick_tile(n, prefs):
    for t in prefs:
        if n % t == 0:
            return t
    return n


def kernel(x_prompt, x_sample, c_prompt, c_sample, cache_mla, cache_kv, cache_idx, page_table, rel_bias, w_ada, b_ada, g_norm_mix, g_norm_ffn, w_in, g_q_lat, w_uq, g_kv_lat, w_ukv, g_mla_q_nope, g_mla_q_rope, g_mla_k_nope, g_mla_k_rope, g_dsa_q, g_dsa_k, g_out, w_out, w_ffn_in, w_ffn_out):
    assert w_ada.shape[0] == 1 and x_sample.shape[1] == 1, "one layer, one new token per sample"
    batch, seq, _ = x_prompt.shape
    dec = x_sample.shape[0]
    n_pages = page_table.shape[1]
    past = n_pages * PAGE_SIZE
    wp = _prep_weights(w_in[0], g_norm_mix[0], g_norm_ffn[0], g_q_lat[0], w_uq[0], g_kv_lat[0], w_ukv[0],
                       g_mla_q_nope[0], g_mla_q_rope[0], g_mla_k_nope[0], g_mla_k_rope[0], g_dsa_q[0], g_dsa_k[0],
                       g_out[0], w_out[0], w_ffn_in[0], w_ffn_out[0])

    mod = _ada(jnp.concatenate([c_prompt, c_sample], axis=0), w_ada[0], b_ada[0])
    mods_p = [m.reshape(batch, 1, D_MODEL) for m in jnp.split(mod[:batch], 6, axis=-1)]
    mods_s = jnp.split(mod[batch:], 6, axis=-1)

    tm = _pick_tile(seq, (512, 256, 128))
    xp2 = x_prompt.reshape(batch * seq, D_MODEL)
    cos_p, sin_p = _rope_tables(jnp.arange(seq))
    pin = _in_stage(xp2, mods_p[1], mods_p[0], cos_p, sin_p, wp, tm, per_token=False)
    o_mla_p = _mla_prompt(pin["qcat"], pin["kcat"], pin["vmla"], batch, seq,
                          _pick_tile(seq, (256, 128)), _pick_tile(seq, (512, 256, 128)))
    o_dsa_p = _dsa_prompt(rel_bias, pin["qidx"], pin["tail"], pin["kidxdup"], pin["qdsa"], pin["kdup"], pin["vone"],
                          batch, seq, _pick_tile(seq, (256, 128)))
    y_p = _out_stage(xp2, o_mla_p, o_dsa_p, mods_p[2], mods_p[3], mods_p[4], mods_p[5], wp, tm,
                     per_token=False, tiles_per_seq=seq // tm)

    xs2 = x_sample.reshape(dec, D_MODEL)
    cos_s, sin_s = _rope_tables(jnp.full((1,), past))
    sin_ = _in_stage(xs2, mods_s[1], mods_s[0], cos_s, sin_s, wp, dec, per_token=True)
    pps = _pick_tile(n_pages, (16, 8, 4, 2))
    pps_wide = _pick_tile(n_pages, (32, 16, 8, 4, 2))
    qidx3 = sin_["qidx"].reshape(dec, IDX_HEADS, IDX_DIM)
    widx3 = sin_["tail"][:, T_WIDX:T_WIDX + IDX_HEADS].reshape(dec, IDX_HEADS, 1)
    scores = _sample_scores(page_table, qidx3, widx3, _pages_feature_major(cache_idx), pps_wide)
    selp, seln = _sample_select(scores, sin_["qidx"], sin_["tail"])
    qc = sin_["qcat"].reshape(dec, MLA_HEADS, LANES)
    qn_row = qc[:, :, 0:MLA_NOPE].astype(F32).reshape(dec, 1, MLA_HEADS * MLA_NOPE)
    qr = qc[:, :, T_ROPE:T_ROPE + MLA_ROPE]
    gk_row = jnp.tile(wp["g_k_nope"], MLA_HEADS).reshape(1, MLA_HEADS * MLA_NOPE)
    o_mla_s = _mla_sample(page_table, qn_row, qr, sin_["row"].reshape(dec, MLA_ROW, 1), wp["w_uk_t"], wp["w_uv"],
                          gk_row, _pages_feature_major(cache_mla), pps).reshape(dec, MLA_HEADS * MLA_V)
    qd = sin_["qdsa"].astype(F32).reshape(dec, DSA_HEADS, DSA_HEAD_DIM)
    on_c = (jnp.arange(DSA_KV_HEADS)[None, :] == (jnp.arange(DSA_HEADS) // DSA_GROUP)[:, None]).astype(F32)
    qd2 = (qd[:, :, None, :] * on_c[None, :, :, None]).reshape(dec, DSA_HEADS, LANES)
    bias_past = _bias_table(rel_bias, 1, past, 0, -1, past).reshape(DSA_HEADS, past)
    bias_new = _bias_table(rel_bias, 1, LANES, 0, 0, 0)[:, 0, 0:1]
    o8 = _dsa_sample(page_table, qd2, sin_["kv"].reshape(dec, 1, 256), selp.reshape(dec, 1, past),
                     seln.reshape(dec, 1, LANES), bias_past, bias_new, _pages_feature_major(cache_kv), pps_wide)
    o_dsa_s = o8.reshape(dec, DSA_HEADS, DSA_KV_HEADS, DSA_HEAD_DIM).sum(axis=2).reshape(dec, DSA_HEADS * DSA_HEAD_DIM)
    y_s = _out_stage(xs2, o_mla_s, o_dsa_s, mods_s[2], mods_s[3], mods_s[4], mods_s[5], wp, dec,
                     per_token=True, tiles_per_seq=1)

    return (y_p.reshape(batch, seq, D_MODEL), y_s.reshape(dec, 1, D_MODEL),
            pin["row"].reshape(1, batch, seq, MLA_ROW),
            pin["kv"].reshape(1, batch, seq, 2, DSA_KV_HEADS, DSA_HEAD_DIM),
            pin["kidx"].reshape(1, batch, seq, IDX_DIM),
            sin_["row"].reshape(1, dec, 1, MLA_ROW),
            sin_["kv"].reshape(1, dec, 1, 2, DSA_KV_HEADS, DSA_HEAD_DIM),
            sin_["kidx"].reshape(1, dec, 1, IDX_DIM))
```

```python
import functools
import math

import numpy as np
import jax
import jax.numpy as jnp
from jax import lax
from jax.experimental import pallas as pl
from jax.experimental.pallas import tpu as pltpu

D_MODEL = 1024
PAGE_SIZE = 128
MLA_HEADS = 8
MLA_NOPE = 64
MLA_ROPE = 32
MLA_V = 64
MLA_Q_RANK = 384
MLA_KV_RANK = 256
MLA_ROW = MLA_KV_RANK + MLA_ROPE
DSA_HEADS = 8
DSA_KV_HEADS = 2
DSA_HEAD_DIM = 64
DSA_GROUP = DSA_HEADS // DSA_KV_HEADS
IDX_HEADS = 8
IDX_DIM = 64
TOPK_MAX = 256
N_BUCKETS = 32
MAX_DISTANCE = 128
ROPE_THETA = 10000.0
D_FF = ((8 * D_MODEL + 3 * 256 - 1) // (3 * 256)) * 256
EPS = 1e-6
MLA_SCALE = (MLA_NOPE + MLA_ROPE) ** -0.5
DSA_SCALE = DSA_HEAD_DIM ** -0.5
IDX_SCALE = (IDX_DIM * IDX_HEADS) ** -0.5

LANES = 128
VMEM_LIMIT = 56 * 1024 * 1024

F32 = jnp.float32
BF16 = jnp.bfloat16
I32 = jnp.int32
NEG_BIG = -1e30
INT_MIN = -(2 ** 31)

C_QLAT = 0
C_KVLAT = C_QLAT + MLA_Q_RANK
C_Q = C_KVLAT + MLA_KV_RANK
C_K = C_Q + DSA_HEADS * DSA_HEAD_DIM
C_V = C_K + DSA_KV_HEADS * DSA_HEAD_DIM
C_QIDX = C_V + DSA_KV_HEADS * DSA_HEAD_DIM
C_TAIL = C_QIDX + IDX_HEADS * IDX_DIM
C_END = C_TAIL + LANES
T_ROPE = IDX_DIM
T_WIDX = IDX_DIM + MLA_ROPE
HALF_ROPE = MLA_ROPE // 2
KB = 128


def _dot(a, b):
    return jnp.dot(a, b, preferred_element_type=F32)


def _dot_nt(a, b):
    return lax.dot_general(a, b, (((1,), (1,)), ((), ())), preferred_element_type=F32)


def _split(a):
    hi = a.astype(BF16)
    lo = (a - hi.astype(F32)).astype(BF16)
    return hi, lo


def _dot3(a, b):
    ah, al = _split(a)
    bh, bl = _split(b)
    return _dot(ah, bh) + (_dot(al, bh) + _dot(ah, bl))


def _cparams(sem):
    return pltpu.CompilerParams(dimension_semantics=sem, vmem_limit_bytes=VMEM_LIMIT)


def _full(shape):
    n = len(shape)
    return pl.BlockSpec(shape, lambda *a, _n=n: (0,) * _n, pipeline_mode=pl.Buffered(1))


def _ada_kernel(c_ref, w_ref, b_ref, o_ref):
    c = c_ref[...]
    s = c / (1.0 + jnp.exp(-c))
    o_ref[...] = _dot3(s, w_ref[...]) + b_ref[...]


def _ada(c, w, b):
    n = c.shape[0]
    return pl.pallas_call(
        _ada_kernel,
        grid=(6,),
        in_specs=[pl.BlockSpec((n, D_MODEL), lambda j: (0, 0)),
                  pl.BlockSpec((D_MODEL, D_MODEL), lambda j: (0, j)),
                  pl.BlockSpec((1, D_MODEL), lambda j: (0, j))],
        out_specs=pl.BlockSpec((n, D_MODEL), lambda j: (0, j)),
        out_shape=jax.ShapeDtypeStruct((n, 6 * D_MODEL), F32),
        compiler_params=_cparams(("arbitrary",)),
    )(c, w, b.reshape(1, -1))


def _group_matrix(width, groups):
    g = np.zeros((width, width), np.float32)
    for lo, hi in groups:
        g[lo:hi, lo:hi] = 1.0 / (hi - lo)
    return jnp.asarray(g, BF16)


def _in_kernel(x_ref, sc_ref, sh_ref, cos_ref, sin_ref, gmix_ref, win_ref, gqlat_ref, wuq_ref, gqcat_ref,
               gkvlat_ref, wukv_ref, gkcat_ref, gtail_ref, gq_ref, gk_ref, mcat_ref, m64_ref,
               qcat_o, kcat_o, vmla_o, row_o, kv_o, kidx_o, tail_o, kidxdup_o, qdsa_o, kdup_o, vone_o, qidx_o,
               *, rows_feature_major):
    tm = x_ref.shape[0]
    x = x_ref[...]
    h = x * lax.rsqrt(jnp.mean(x * x, axis=-1, keepdims=True) + EPS) * gmix_ref[...]
    hb = (h * (1.0 + sc_ref[...]) + sh_ref[...]).astype(BF16)

    def proj(lo, hi):
        return _dot(hb, win_ref[:, lo:hi])

    cos = cos_ref[...]
    sin = sin_ref[...]
    lane = lax.broadcasted_iota(I32, (tm, LANES), 1)
    first_half = lane < T_ROPE + HALF_ROPE
    rope_lanes = (lane >= T_ROPE) & (lane < T_ROPE + MLA_ROPE)
    low64 = lane < 64

    def rope(xh):
        rot = jnp.where(first_half, pltpu.roll(xh, LANES - HALF_ROPE, 1), pltpu.roll(xh, HALF_ROPE, 1))
        return xh * cos + rot * sin

    def gnorm(blk, m_ref):
        ms = _dot((blk * blk).astype(BF16), m_ref[...])
        return blk * lax.rsqrt(ms + EPS)

    ql = proj(C_QLAT, C_KVLAT)
    qln = ql * lax.rsqrt(jnp.mean(ql * ql, axis=-1, keepdims=True) + EPS) * gqlat_ref[...]
    qm = _dot(qln.astype(BF16), wuq_ref[...])
    for j in range(4):
        y = gnorm(qm[:, 256 * j:256 * j + 256], mcat_ref) * gqcat_ref[:, 256 * j:256 * j + 256]
        for t in range(2):
            qcat_o[:, 256 * j + 128 * t:256 * j + 128 * t + 128] = rope(y[:, 128 * t:128 * t + 128]).astype(BF16)

    tail = proj(C_TAIL, C_END)
    ssq = jnp.sum(jnp.where(rope_lanes, tail * tail, 0.0), axis=-1, keepdims=True) * (1.0 / MLA_ROPE)
    tn = jnp.where(rope_lanes, tail * lax.rsqrt(ssq + EPS) * gtail_ref[...], tail)
    tr = rope(tn)
    if rows_feature_major:
        kidx_o[...] = tail.T[0:IDX_DIM, :]
    else:
        kidx_o[...] = tail[:, 0:IDX_DIM]
    tail_o[...] = tail
    kidxdup_o[...] = jnp.where(low64, tail, pltpu.roll(tail, 64, 1)).astype(BF16)
    krope = jnp.where(rope_lanes, tr, 0.0)

    kvl = proj(C_KVLAT, C_Q)
    lat = kvl * lax.rsqrt(jnp.mean(kvl * kvl, axis=-1, keepdims=True) + EPS) * gkvlat_ref[...]
    if rows_feature_major:
        row_o[0:MLA_KV_RANK, :] = lat.T
        row_o[MLA_KV_RANK:MLA_ROW, :] = tr.T[T_ROPE:T_ROPE + MLA_ROPE, :]
    else:
        row_o[:, 0:MLA_KV_RANK] = lat
        row_o[:, MLA_KV_RANK:MLA_ROW] = tr[:, T_ROPE:T_ROPE + MLA_ROPE]
    latb = lat.astype(BF16)
    kx = _dot(latb, wukv_ref[:, 0:1024])
    for j in range(4):
        y = gnorm(kx[:, 256 * j:256 * j + 256], mcat_ref) * gkcat_ref[:, 256 * j:256 * j + 256]
        for t in range(2):
            kcat_o[:, 256 * j + 128 * t:256 * j + 128 * t + 128] = (y[:, 128 * t:128 * t + 128] + krope).astype(BF16)
    ones_half = (lax.broadcasted_iota(I32, (tm, 1024), 1) % LANES) >= MLA_V
    vmla_o[...] = jnp.where(ones_half, 1.0, _dot(latb, wukv_ref[:, 1024:2048])).astype(BF16)

    q = proj(C_Q, C_K)
    for j in range(2):
        qdsa_o[:, 256 * j:256 * j + 256] = (gnorm(q[:, 256 * j:256 * j + 256], m64_ref)
                                            * gq_ref[:, 256 * j:256 * j + 256] * DSA_SCALE).astype(BF16)
    k = proj(C_K, C_V)
    ms = _dot((k * k).astype(BF16), m64_ref[0:128, 0:128])
    kn = k * lax.rsqrt(ms + EPS) * gk_ref[...]
    v = proj(C_V, C_QIDX)
    if rows_feature_major:
        kv_o[0:128, :] = kn.T
        kv_o[128:256, :] = v.T
    else:
        kv_o[:, 0:128] = kn
        kv_o[:, 128:256] = v
    kr = pltpu.roll(kn, 64, 1)
    kdup_o[:, 0:128] = jnp.where(low64, kn, kr).astype(BF16)
    kdup_o[:, 128:256] = jnp.where(low64, kr, kn).astype(BF16)
    vone_o[:, 0:128] = jnp.where(low64, v, 1.0).astype(BF16)
    vone_o[:, 128:256] = jnp.where(low64, pltpu.roll(v, 64, 1), 1.0).astype(BF16)
    qidx_o[...] = proj(C_QIDX, C_TAIL).astype(BF16)


_IN_OUT_WIDTHS = (("qcat", 1024, BF16), ("kcat", 1024, BF16), ("vmla", 1024, BF16), ("row", MLA_ROW, F32),
                  ("kv", 256, F32), ("kidx", IDX_DIM, F32), ("tail", LANES, F32), ("kidxdup", LANES, BF16),
                  ("qdsa", 512, BF16), ("kdup", 256, BF16), ("vone", 256, BF16), ("qidx", 512, BF16))


def _in_stage(x2d, scale, shift, cos, sin, wp, tm, per_token):
    t_total = x2d.shape[0]
    n_tiles = t_total // tm
    if per_token:
        mod_spec = pl.BlockSpec((tm, D_MODEL), lambda i: (i, 0))
        tab_spec = pl.BlockSpec((1, LANES), lambda i: (0, 0))
    else:
        tiles_per_seq = cos.shape[0] // tm
        mod_spec = pl.BlockSpec((None, 1, D_MODEL), lambda i: (i // tiles_per_seq, 0, 0))
        tab_spec = pl.BlockSpec((tm, LANES), lambda i: (i % tiles_per_seq, 0))
    consts = [wp["g_mix"], wp["w_in"], wp["g_qlat"], wp["w_uq"], wp["g_qcat"], wp["g_kvlat"], wp["w_ukv"],
              wp["g_kcat"], wp["g_tail"], wp["g_q"], wp["g_k"], wp["m_cat"], wp["m_64"]]
    in_specs = ([pl.BlockSpec((tm, D_MODEL), lambda i: (i, 0)), mod_spec, mod_spec, tab_spec, tab_spec]
                + [_full(c.shape) for c in consts])
    out_specs = [pl.BlockSpec((tm, w), lambda i: (i, 0)) for _, w, _ in _IN_OUT_WIDTHS]
    out_shape = [jax.ShapeDtypeStruct((t_total, w), dt) for _, w, dt in _IN_OUT_WIDTHS]
    if not per_token:
        for j, (name, w, dt) in enumerate(_IN_OUT_WIDTHS):
            if name in ("row", "kv", "kidx"):
                out_specs[j] = pl.BlockSpec((None, w, tm), lambda i: (i // tiles_per_seq, 0, i % tiles_per_seq))
                out_shape[j] = jax.ShapeDtypeStruct((n_tiles // tiles_per_seq, w, tiles_per_seq * tm), dt)
    outs = pl.pallas_call(
        functools.partial(_in_kernel, rows_feature_major=not per_token),
        grid=(n_tiles,), in_specs=in_specs, out_specs=out_specs, out_shape=out_shape,
        compiler_params=_cparams(("parallel",)),
    )(x2d, scale, shift, cos, sin, *consts)
    return {name: o for (name, _, _), o in zip(_IN_OUT_WIDTHS, outs)}


def _mla_prompt_kernel(q_ref, k_ref, v_ref, o_ref, m_scr, acc_scr, s_scr, *, tk):
    tq = q_ref.shape[0]
    qi = pl.program_id(1)
    n_chunks = (qi * tq + tq + tk - 1) // tk
    m_scr[...] = jnp.full(m_scr.shape, -jnp.inf, F32)
    acc_scr[...] = jnp.zeros(acc_scr.shape, F32)
    low64 = lax.broadcasted_iota(I32, (tq, LANES), 1) < 64
    qpos = qi * tq + lax.broadcasted_iota(I32, (tq, tk), 0)
    kcol = lax.broadcasted_iota(I32, (tq, tk), 1)
    exp_scale = MLA_SCALE * math.log2(math.e)

    def raw_scores(c, h, masked):
        off = pl.multiple_of(c * tk, tk)
        s = _dot_nt(q_ref[:, h * 128:(h + 1) * 128], k_ref[pl.ds(off, tk), h * 128:(h + 1) * 128])
        if masked:
            s = jnp.where(c * tk + kcol <= qpos, s, -jnp.inf)
        return s

    def pass_max(c, masked):
        for h in range(MLA_HEADS):
            s = raw_scores(c, h, masked)
            s_scr[c, h] = s
            m = m_scr[h]
            for j in range(tk // LANES):
                m = jnp.maximum(m, s[:, j * LANES:(j + 1) * LANES])
            m_scr[h] = m

    def pass_sum(c, masked):
        off = pl.multiple_of(c * tk, tk)
        for h in range(MLA_HEADS):
            s = s_scr[c, h]
            m = m_scr[h]
            p = jnp.concatenate([jnp.exp2((s[:, j * LANES:(j + 1) * LANES] - m) * exp_scale)
                                 for j in range(tk // LANES)], axis=1).astype(BF16)
            acc_scr[h] += _dot(p, v_ref[pl.ds(off, tk), h * 128:(h + 1) * 128])

    def run(step):
        def body(c, carry):
            step(c, False)
            return carry

        lax.fori_loop(0, n_chunks - 1, body, 0)
        step(n_chunks - 1, True)

    run(pass_max)
    for h in range(MLA_HEADS):
        m_scr[h] = jnp.broadcast_to(jnp.max(m_scr[h], axis=-1, keepdims=True), (tq, LANES))
    run(pass_sum)
    for p in range(MLA_HEADS // 2):
        even, odd = acc_scr[2 * p], acc_scr[2 * p + 1]
        o_ref[:, 128 * p:128 * p + 128] = jnp.where(low64, even / pltpu.roll(even, 64, 1),
                                                    pltpu.roll(odd, 64, 1) / odd)


def _mla_prompt(qcat, kcat, vmla, batch, seq, tq, tk):
    nq = seq // tq
    return pl.pallas_call(
        functools.partial(_mla_prompt_kernel, tk=tk),
        grid=(batch, nq),
        in_specs=[pl.BlockSpec((tq, 1024), lambda b, i: (b * nq + i, 0)),
                  pl.BlockSpec((seq, 1024), lambda b, i: (b, 0)),
                  pl.BlockSpec((seq, 1024), lambda b, i: (b, 0))],
        out_specs=pl.BlockSpec((tq, 512), lambda b, i: (b * nq + i, 0)),
        out_shape=jax.ShapeDtypeStruct((batch * seq, 512), F32),
        scratch_shapes=[pltpu.VMEM((MLA_HEADS, tq, LANES), F32), pltpu.VMEM((MLA_HEADS, tq, LANES), F32),
                        pltpu.VMEM((seq // tk, MLA_HEADS, tq, tk), F32)],
        compiler_params=_cparams(("parallel", "arbitrary")),
    )(qcat, kcat, vmla)


def _sort_key(score):
    bits = lax.bitcast_convert_type(score + 0.0, I32)
    return bits ^ ((bits >> 31) & 0x7FFFFFFF)


def _kth_key(count_ge, n_sel, shape):
    def body(i, t):
        cand = t + (jnp.int32(1) << (31 - i))
        return jnp.where(count_ge(cand) >= n_sel, cand, t)

    return lax.fori_loop(0, 32, body, jnp.full(shape, INT_MIN, I32))


def _bucket(dist):
    max_exact = N_BUCKETS // 2
    d = jnp.maximum(dist, 0)
    log_ratio = jnp.log(jnp.maximum(d, max_exact).astype(F32) / max_exact) / math.log(MAX_DISTANCE / max_exact)
    large = jnp.minimum(max_exact + (log_ratio * (N_BUCKETS - max_exact)).astype(I32), N_BUCKETS - 1)
    return jnp.where(d < max_exact, d, large)


def _bias_kernel(rb_ref, o_ref, *, a, b, c):
    rows, cols = o_ref.shape[1], o_ref.shape[2]
    dist = (a * lax.broadcasted_iota(I32, (rows, cols), 0) + b * lax.broadcasted_iota(I32, (rows, cols), 1) + c)
    bucket = _bucket(dist)
    for h in range(DSA_HEADS):
        acc = jnp.zeros((rows, cols), F32)
        for n in range(N_BUCKETS):
            acc = jnp.where(bucket == n, rb_ref[n, h], acc)
        o_ref[h] = acc


def _bias_table(rel_bias, rows, cols, a, b, c):
    return pl.pallas_call(
        functools.partial(_bias_kernel, a=a, b=b, c=c),
        in_specs=[pl.BlockSpec(memory_space=pltpu.SMEM)],
        out_specs=pl.BlockSpec(memory_space=pltpu.VMEM),
        out_shape=jax.ShapeDtypeStruct((DSA_HEADS, rows, cols), F32),
    )(rel_bias)


def _loop2(n, body):
    n = jnp.maximum(n, 0)

    def pair(j, carry):
        body(2 * j)
        body(2 * j + 1)
        return carry

    lax.fori_loop(0, n // 2, pair, 0)

    @pl.when(n % 2 == 1)
    def _():
        body(n - 1)


def _colreduce(x, op):
    return op(x.reshape(x.shape[0] // 8, 8, x.shape[1]), axis=0)


def _dsa_prompt_kernel(rb_ref, qidx_ref, tail_ref, kidx_ref, q_ref, k_ref, v_ref, bias_ref, o_ref,
                       qs_scr, qd_scr, wb_scr, key_scr, sbq_scr, tie_scr, m_scr, p_scr, acc_scr, s_scr, *, n_sel, idx_bits):
    tq = q_ref.shape[0]
    kq = tq // KB
    qi = pl.program_id(1)
    krow = lax.broadcasted_iota(I32, (KB, tq), 0)
    qcol = lax.broadcasted_iota(I32, (KB, tq), 1)
    low64 = lax.broadcasted_iota(I32, (tq, LANES), 1) < 64

    tail = tail_ref[...]
    for h in range(IDX_HEADS):
        mine = low64 if h % 2 == 0 else jnp.logical_not(low64)
        zero = jnp.zeros((tq, LANES), BF16)
        qs_scr[h * tq:(h + 1) * tq, :] = jnp.where(mine, qidx_ref[:, (h // 2) * 128:(h // 2 + 1) * 128], zero)
        qd_scr[h * tq:(h + 1) * tq, :] = jnp.where(mine, q_ref[:, (h // 2) * 128:(h // 2 + 1) * 128], zero)
        wb_scr[h] = jnp.broadcast_to(tail[:, T_WIDX + h:T_WIDX + h + 1], (tq, LANES))

    def rows_of(c):
        return pl.ds(pl.multiple_of(c * tq, tq), tq)

    def score_chunk(c, in_tile):
        d = _dot_nt(qs_scr[...], kidx_ref[rows_of(c), :])
        for t in range(kq):
            sc = jnp.zeros((tq, KB), F32)
            for h in range(IDX_HEADS):
                sc = sc + jnp.maximum(d[h * tq:(h + 1) * tq, t * KB:(t + 1) * KB], 0.0) * wb_scr[h]
            sc_t = (sc * IDX_SCALE).T
            if in_tile:
                sc_t = jnp.where(t * KB + krow <= qcol, sc_t, -jnp.inf)
            key_scr[c * kq + t] = _sort_key(sc_t)

    _loop2(qi, lambda c: score_chunk(c, False))
    score_chunk(qi, True)
    n_chunks = qi + 1

    def count(pred):
        def body(c, acc):
            for t in range(kq):
                kb = c * kq + t
                acc = acc + jnp.where(pred(key_scr[kb], kb), 1.0, 0.0)
            return acc
        acc = lax.fori_loop(0, n_chunks, body, jnp.zeros((KB, tq), F32))
        return jnp.sum(_colreduce(acc, jnp.sum), axis=0, keepdims=True)

    thr = _kth_key(lambda cand: count(lambda k, kb: k >= cand), float(n_sel), (1, tq))
    cnt_gt = count(lambda k, kb: k > thr)
    cnt_ge = count(lambda k, kb: k >= thr)
    tie_scr[...] = jnp.full((1, tq), 2 ** 30, I32)
    excess = jnp.where((cnt_ge > float(n_sel)) & (thr > INT_MIN), 1.0, 0.0)

    @pl.when(jnp.max(excess) > 0.0)
    def _():
        need = float(n_sel) - cnt_gt

        def body(i, p):
            cand = p + (jnp.int32(1) << (idx_bits - 1 - i))
            c = count(lambda k, kb: (k == thr) & (kb * KB + krow < cand))
            return jnp.where(c < need, cand, p)

        tie_scr[...] = lax.fori_loop(0, idx_bits, body, jnp.zeros((1, tq), I32))

    last_tie = tie_scr[...]

    def mask_chunk(c, in_tile):
        for t in range(kq):
            kb = c * kq + t
            key = key_scr[kb]
            sel = (key > thr) | ((key == thr) & (kb * KB + krow <= last_tie))
            if in_tile:
                sel = sel & (t * KB + krow <= qcol)
            sbq_scr[c, :, t * KB:(t + 1) * KB] = jnp.where(sel, 0.0, NEG_BIG).T

    _loop2(qi, lambda c: mask_chunk(c, False))
    mask_chunk(qi, True)

    far_bias = [rb_ref[N_BUCKETS - 1, h] for h in range(DSA_HEADS)]

    def scores(c, g, kind):
        sg = _dot_nt(qd_scr[g * DSA_GROUP * tq:(g + 1) * DSA_GROUP * tq, :], k_ref[rows_of(c), g * 128:(g + 1) * 128])
        sb = sbq_scr[c]
        out = []
        for hh in range(DSA_GROUP):
            h = g * DSA_GROUP + hh
            cols = []
            for t in range(kq):
                s = sg[hh * tq:(hh + 1) * tq, t * KB:(t + 1) * KB] + sb[:, t * KB:(t + 1) * KB]
                if kind == "tile":
                    s = s + (bias_ref[1 + t, h] - far_bias[h])
                elif kind == "before" and t == kq - 1:
                    s = s + (bias_ref[0, h] - far_bias[h])
                cols.append(s)
            out.append(cols)
        return out

    m_scr[...] = jnp.full(m_scr.shape, NEG_BIG, F32)
    acc_scr[...] = jnp.zeros(acc_scr.shape, F32)

    def pass_max(c, kind):
        for g in range(DSA_KV_HEADS):
            for hh, cols in enumerate(scores(c, g, kind)):
                h = g * DSA_GROUP + hh
                m = m_scr[h]
                for t, s in enumerate(cols):
                    s_scr[c, h, :, t * KB:(t + 1) * KB] = s
                    m = jnp.maximum(m, s)
                m_scr[h] = m

    def pass_sum(c, kind):
        for g in range(DSA_KV_HEADS):
            for hh in range(DSA_GROUP):
                h = g * DSA_GROUP + hh
                m = m_scr[h]
                for t in range(kq):
                    s = s_scr[c, h, :, t * KB:(t + 1) * KB]
                    p_scr[g, hh * tq:(hh + 1) * tq, t * KB:(t + 1) * KB] = jnp.exp(s - m).astype(BF16)
            acc_scr[g] += _dot(p_scr[g], v_ref[rows_of(c), g * 128:(g + 1) * 128])

    def run(step):
        _loop2(qi - 1, lambda c: step(c, "far"))

        @pl.when(qi >= 1)
        def _():
            step(qi - 1, "before")

        step(qi, "tile")

    run(pass_max)
    for h in range(DSA_HEADS):
        m_scr[h] = jnp.broadcast_to(jnp.max(m_scr[h], axis=-1, keepdims=True), (tq, LANES))
    run(pass_sum)

    for p in range(DSA_HEADS // 2):
        g, hh = (2 * p) // DSA_GROUP, (2 * p) % DSA_GROUP
        even = acc_scr[g, hh * tq:(hh + 1) * tq, :]
        odd = acc_scr[g, (hh + 1) * tq:(hh + 2) * tq, :]
        o_ref[:, 128 * p:128 * p + 128] = jnp.where(low64, even / pltpu.roll(even, 64, 1),
                                                    pltpu.roll(odd, 64, 1) / odd)


def _dsa_prompt(rel_bias, qidx, tail, kidxdup, qdsa, kdup, vone, batch, seq, tq):
    assert MAX_DISTANCE <= KB, "keys two or more blocks back must all fall in the last bucket"
    nq = seq // tq
    nkb = seq // KB
    kq = tq // KB
    n_sel = min(TOPK_MAX, seq // 4)
    idx_bits = max(1, (seq - 1).bit_length())
    bias = jnp.stack([_bias_table(rel_bias, tq, KB, 1, -1, KB - t * KB) for t in range(kq + 1)])
    kern = functools.partial(_dsa_prompt_kernel, n_sel=n_sel, idx_bits=idx_bits)
    qblk = lambda w: pl.BlockSpec((tq, w), lambda b, i: (b * nq + i, 0))
    seqblk = lambda w: pl.BlockSpec((seq, w), lambda b, i: (b, 0))
    return pl.pallas_call(
        kern,
        grid=(batch, nq),
        in_specs=[pl.BlockSpec(memory_space=pltpu.SMEM),
                  qblk(512), qblk(LANES), seqblk(LANES), qblk(512), seqblk(256), seqblk(256), _full(bias.shape)],
        out_specs=qblk(512),
        out_shape=jax.ShapeDtypeStruct((batch * seq, 512), F32),
        scratch_shapes=[pltpu.VMEM((IDX_HEADS * tq, LANES), BF16), pltpu.VMEM((DSA_HEADS * tq, LANES), BF16),
                        pltpu.VMEM((IDX_HEADS, tq, LANES), F32),
                        pltpu.VMEM((nkb, KB, tq), I32), pltpu.VMEM((nq, tq, tq), F32),
                        pltpu.VMEM((1, tq), I32), pltpu.VMEM((DSA_HEADS, tq, LANES), F32),
                        pltpu.VMEM((DSA_KV_HEADS, DSA_GROUP * tq, tq), BF16),
                        pltpu.VMEM((DSA_KV_HEADS, DSA_GROUP * tq, LANES), F32),
                        pltpu.VMEM((nq, DSA_HEADS, tq, tq), F32)],
        compiler_params=_cparams(("parallel", "arbitrary")),
    )(rel_bias, qidx, tail, kidxdup, qdsa, kdup, vone, bias)


def _fetch_pages(cache_ref, pt_ref, buf, sem, step, n_steps, locate, pps):
    def copy(slot, i, page):
        return pltpu.make_async_copy(cache_ref.at[0, page], buf.at[slot, i], sem.at[slot])

    def start(st, slot):
        b, first = locate(st)
        for i in range(pps):
            copy(slot, i, pt_ref[b, first + i]).start()

    @pl.when(step == 0)
    def _():
        start(step, 0)

    @pl.when(step + 1 < n_steps)
    def _():
        start(step + 1, (step + 1) % 2)

    slot = step % 2
    for i in range(pps):
        copy(slot, i, 0).wait()
    return slot


def _page_scratch(pps, width):
    return [pltpu.VMEM((2, pps, width, PAGE_SIZE), F32), pltpu.SemaphoreType.DMA((2,))]


def _pages_feature_major(cache):
    c = jnp.moveaxis(cache, 2, -1)
    return c.reshape(cache.shape[0], cache.shape[1], -1, PAGE_SIZE)


def _sample_score_kernel(pt_ref, q_ref, w_ref, cache_ref, o_ref, buf, sem, kb_scr, *, pps, grp):
    bo, c, bi = pl.program_id(0), pl.program_id(1), pl.program_id(2)
    n_chunks = pl.num_programs(1)
    step = (bo * n_chunks + c) * grp + bi

    def locate(st):
        return (st // (n_chunks * grp)) * grp + st % grp, ((st // grp) % n_chunks) * pps

    slot = _fetch_pages(cache_ref, pt_ref, buf, sem, step, pl.num_programs(0) * n_chunks * grp, locate, pps)
    for i in range(pps):
        kb_scr[:, i * PAGE_SIZE:(i + 1) * PAGE_SIZE] = buf[slot, i].astype(BF16)
    d = _dot(q_ref[...], kb_scr[...])
    sc = jnp.sum(jnp.maximum(d, 0.0) * w_ref[...], axis=0, keepdims=True) * IDX_SCALE
    o_ref[pl.ds(bi, 1), :] = sc


def _sample_scores(page_table, qidx3, widx3, cache_idx_t, pps):
    dec, n_pages = page_table.shape
    n_chunks = n_pages // pps
    grp = 8
    grid_spec = pltpu.PrefetchScalarGridSpec(
        num_scalar_prefetch=1,
        grid=(dec // grp, n_chunks, grp),
        in_specs=[pl.BlockSpec((None, IDX_HEADS, IDX_DIM), lambda bo, c, bi, pt: (bo * grp + bi, 0, 0)),
                  pl.BlockSpec((None, IDX_HEADS, 1), lambda bo, c, bi, pt: (bo * grp + bi, 0, 0)),
                  pl.BlockSpec(memory_space=pl.ANY)],
        out_specs=pl.BlockSpec((grp, pps * PAGE_SIZE), lambda bo, c, bi, pt: (bo, c)),
        scratch_shapes=_page_scratch(pps, IDX_DIM) + [pltpu.VMEM((IDX_DIM, pps * PAGE_SIZE), BF16)],
    )
    return pl.pallas_call(
        functools.partial(_sample_score_kernel, pps=pps, grp=grp), grid_spec=grid_spec,
        out_shape=jax.ShapeDtypeStruct((dec, n_pages * PAGE_SIZE), F32),
        compiler_params=_cparams(("arbitrary", "arbitrary", "arbitrary")),
    )(page_table, qidx3, widx3, cache_idx_t)


def _sample_select_kernel(sc_ref, qidx_ref, tail_ref, selp_ref, seln_ref, key_scr, *, n_sel, idx_bits):
    rows, past = sc_ref.shape
    nblk = past // LANES
    lane = lax.broadcasted_iota(I32, (rows, LANES), 1)
    tail = tail_ref[...]
    kidx2 = jnp.where(lane < 64, tail, pltpu.roll(tail, 64, 1))
    sc_new = jnp.zeros((rows, 1), F32)
    for h in range(IDX_HEADS):
        prod = qidx_ref[:, (h // 2) * 128:(h // 2 + 1) * 128].astype(F32) * kidx2.astype(BF16).astype(F32)
        mine = (lane < 64) if h % 2 == 0 else (lane >= 64)
        dot = jnp.sum(jnp.where(mine, prod, 0.0), axis=-1, keepdims=True)
        sc_new = sc_new + jnp.maximum(dot, 0.0) * tail[:, T_WIDX + h:T_WIDX + h + 1]
    key_new = _sort_key(sc_new * IDX_SCALE)

    for j in range(nblk):
        key_scr[j] = _sort_key(sc_ref[:, j * LANES:(j + 1) * LANES])

    def count(pred, pred_new):
        def body(j, c):
            return c + jnp.where(pred(key_scr[j], j), 1.0, 0.0)
        c = lax.fori_loop(0, nblk, body, jnp.zeros((rows, LANES), F32), unroll=4)
        return jnp.sum(c, axis=-1, keepdims=True) + jnp.where(pred_new, 1.0, 0.0)

    thr = _kth_key(lambda cand: count(lambda k, j: k >= cand, key_new >= cand), float(n_sel), (rows, 1))
    cnt_gt = count(lambda k, j: k > thr, key_new > thr)
    need = float(n_sel) - cnt_gt

    def body(i, p):
        cand = p + (jnp.int32(1) << (idx_bits - 1 - i))
        c = count(lambda k, j: (k == thr) & (j * LANES + lane < cand), (key_new == thr) & (past < cand))
        return jnp.where(c < need, cand, p)

    last_tie = lax.fori_loop(0, idx_bits, body, jnp.zeros((rows, 1), I32))

    for j in range(nblk):
        k = key_scr[j]
        sel = (k > thr) | ((k == thr) & (j * LANES + lane <= last_tie))
        selp_ref[:, j * LANES:(j + 1) * LANES] = jnp.where(sel, 1.0, 0.0)
    sel_new = (key_new > thr) | ((key_new == thr) & (past <= last_tie))
    seln_ref[...] = jnp.broadcast_to(jnp.where(sel_new, 1.0, 0.0), (rows, LANES))


def _sample_select(scores, qidx, tail):
    dec, past = scores.shape
    n_sel = min(TOPK_MAX, (past + 1) // 4)
    idx_bits = max(1, past.bit_length())
    rows = 64 if dec % 64 == 0 else dec
    kern = functools.partial(_sample_select_kernel, n_sel=n_sel, idx_bits=idx_bits)
    return pl.pallas_call(
        kern,
        grid=(dec // rows,),
        in_specs=[pl.BlockSpec((rows, past), lambda i: (i, 0)), pl.BlockSpec((rows, 512), lambda i: (i, 0)),
                  pl.BlockSpec((rows, LANES), lambda i: (i, 0))],
        out_specs=[pl.BlockSpec((rows, past), lambda i: (i, 0)), pl.BlockSpec((rows, LANES), lambda i: (i, 0))],
        out_shape=[jax.ShapeDtypeStruct((dec, past), F32), jax.ShapeDtypeStruct((dec, LANES), F32)],
        scratch_shapes=[pltpu.VMEM((past // LANES, rows, LANES), I32)],
        compiler_params=_cparams(("parallel",)),
    )(scores, qidx, tail)


def _mla_sample_kernel(pt_ref, qn_ref, qr_ref, rown_ref, wuk_ref, wuv_ref, gk_ref, cache_ref, o_ref,
                       buf, sem, lat_scr, kr_scr, a_scr, m_scr, l_scr, acc_scr, *, pps):
    c = pl.program_id(1)
    n_chunks = pl.num_programs(1)
    heads = MLA_HEADS
    slot = _fetch_pages(cache_ref, pt_ref, buf, sem, pl.program_id(0) * n_chunks + c,
                        pl.num_programs(0) * n_chunks, lambda st: (st // n_chunks, (st % n_chunks) * pps), pps)

    @pl.when(c == 0)
    def _():
        m_scr[...] = jnp.full(m_scr.shape, -jnp.inf, F32)
        l_scr[...] = jnp.zeros(l_scr.shape, F32)
        acc_scr[...] = jnp.zeros(acc_scr.shape, F32)
        qg = qn_ref[...] * gk_ref[...]
        own = (lax.broadcasted_iota(I32, (heads, heads * MLA_NOPE), 1) // MLA_NOPE
               == lax.broadcasted_iota(I32, (heads, heads * MLA_NOPE), 0))
        a_scr[...] = _dot(jnp.where(own, qg, 0.0).astype(BF16), wuk_ref[...]).astype(BF16)

    qr = qr_ref[...]

    def scores(latb, krb):
        knt = _dot(wuk_ref[...], latb)
        ms = jnp.mean((knt * knt).reshape(heads, MLA_NOPE, latb.shape[1]), axis=1)
        return (_dot(a_scr[...], latb) * lax.rsqrt(ms + EPS) + _dot(qr, krb)) * MLA_SCALE

    def update(latb, krb, n_valid):
        n = latb.shape[1]
        sub = min(n, 512)
        s = jnp.concatenate([scores(latb[:, j:j + sub], krb[:, j:j + sub]) for j in range(0, n, sub)], axis=1)
        if n_valid < n:
            s = jnp.where(lax.broadcasted_iota(I32, s.shape, 1) < n_valid, s, -jnp.inf)
        m_old = m_scr[...]
        m_new = jnp.maximum(m_old, jnp.max(s, axis=-1, keepdims=True))
        alpha = jnp.exp(m_old - m_new)
        p = jnp.exp(s - m_new)
        l_scr[...] = alpha * l_scr[...] + jnp.sum(p, axis=-1, keepdims=True)
        m_scr[...] = m_new
        acc_scr[...] = alpha * acc_scr[...] + _dot_nt(p.astype(BF16), latb)

    for i in range(pps):
        lat_scr[:, i * PAGE_SIZE:(i + 1) * PAGE_SIZE] = buf[slot, i, 0:MLA_KV_RANK, :].astype(BF16)
        kr_scr[:, i * PAGE_SIZE:(i + 1) * PAGE_SIZE] = buf[slot, i, MLA_KV_RANK:MLA_ROW, :].astype(BF16)
    update(lat_scr[...], kr_scr[...], lat_scr.shape[1])

    @pl.when(c == n_chunks - 1)
    def _():
        rn = jnp.broadcast_to(rown_ref[...], (MLA_ROW, PAGE_SIZE))
        update(rn[0:MLA_KV_RANK, :].astype(BF16), rn[MLA_KV_RANK:MLA_ROW, :].astype(BF16), 1)
        o_lat = (acc_scr[...] / l_scr[...]).astype(BF16)
        full = _dot(o_lat, wuv_ref[...])
        hd = lax.broadcasted_iota(I32, full.shape, 1) // MLA_V
        own = hd == lax.broadcasted_iota(I32, full.shape, 0)
        o_ref[...] = jnp.sum(jnp.where(own, full, 0.0), axis=0, keepdims=True)


def _mla_sample(page_table, qn_row, qr, row_new, wuk_t, wuv, gk_row, cache_mla, pps):
    dec, n_pages = page_table.shape
    n_chunks = n_pages // pps
    grid_spec = pltpu.PrefetchScalarGridSpec(
        num_scalar_prefetch=1,
        grid=(dec, n_chunks),
        in_specs=[pl.BlockSpec((None, 1, MLA_HEADS * MLA_NOPE), lambda b, c, pt: (b, 0, 0)),
                  pl.BlockSpec((None, MLA_HEADS, MLA_ROPE), lambda b, c, pt: (b, 0, 0)),
                  pl.BlockSpec((None, MLA_ROW, 1), lambda b, c, pt: (b, 0, 0)),
                  pl.BlockSpec(wuk_t.shape, lambda b, c, pt: (0, 0)),
                  pl.BlockSpec(wuv.shape, lambda b, c, pt: (0, 0)),
                  pl.BlockSpec(gk_row.shape, lambda b, c, pt: (0, 0)),
                  pl.BlockSpec(memory_space=pl.ANY)],
        out_specs=pl.BlockSpec((None, 1, MLA_HEADS * MLA_V), lambda b, c, pt: (b, 0, 0)),
        scratch_shapes=_page_scratch(pps, MLA_ROW) + [
            pltpu.VMEM((MLA_KV_RANK, pps * PAGE_SIZE), BF16), pltpu.VMEM((MLA_ROPE, pps * PAGE_SIZE), BF16),
            pltpu.VMEM((MLA_HEADS, MLA_KV_RANK), BF16),
            pltpu.VMEM((MLA_HEADS, 1), F32), pltpu.VMEM((MLA_HEADS, 1), F32),
            pltpu.VMEM((MLA_HEADS, MLA_KV_RANK), F32)],
    )
    return pl.pallas_call(
        functools.partial(_mla_sample_kernel, pps=pps), grid_spec=grid_spec,
        out_shape=jax.ShapeDtypeStruct((dec, 1, MLA_HEADS * MLA_V), F32),
        compiler_params=_cparams(("arbitrary", "arbitrary")),
    )(page_table, qn_row, qr, row_new, wuk_t, wuv, gk_row, cache_mla)


def _dsa_sample_kernel(pt_ref, q_ref, kvn_ref, selp_ref, seln_ref, biasp_ref, biasn_ref, cache_ref, o_ref,
                       buf, sem, k_scr, v_scr, m_scr, l_scr, acc_scr, *, pps):
    c = pl.program_id(1)
    n_chunks = pl.num_programs(1)
    slot = _fetch_pages(cache_ref, pt_ref, buf, sem, pl.program_id(0) * n_chunks + c,
                        pl.num_programs(0) * n_chunks, lambda st: (st // n_chunks, (st % n_chunks) * pps), pps)

    @pl.when(c == 0)
    def _():
        m_scr[...] = jnp.full(m_scr.shape, NEG_BIG, F32)
        l_scr[...] = jnp.zeros(l_scr.shape, F32)
        acc_scr[...] = jnp.zeros(acc_scr.shape, F32)

    q = q_ref[...]

    def update(s, sel, pv_of):
        sh = jnp.where(sel, s, NEG_BIG)
        m_old = m_scr[...]
        m_new = jnp.maximum(m_old, jnp.max(sh, axis=-1, keepdims=True))
        alpha = jnp.exp(m_old - m_new)
        p = jnp.where(sel, jnp.exp(sh - m_new), 0.0)
        l_scr[...] = alpha * l_scr[...] + jnp.sum(p, axis=-1, keepdims=True)
        m_scr[...] = m_new
        acc_scr[...] = alpha * acc_scr[...] + pv_of(p)

    for i in range(pps):
        k_scr[:, i * PAGE_SIZE:(i + 1) * PAGE_SIZE] = buf[slot, i, 0:128, :].astype(BF16)
        v_scr[:, i * PAGE_SIZE:(i + 1) * PAGE_SIZE] = buf[slot, i, 128:256, :].astype(BF16)
    s = _dot(q.astype(BF16), k_scr[...]) + biasp_ref[...]
    update(s, selp_ref[...] > 0.5, lambda p: _dot_nt(p.astype(BF16), v_scr[...]))

    @pl.when(c == n_chunks - 1)
    def _():
        kvn = kvn_ref[...]
        kn = kvn[:, 0:128].astype(BF16).astype(F32)
        vn = kvn[:, 128:256].astype(BF16).astype(F32)
        s_new = jnp.sum(q.astype(F32) * kn, axis=-1, keepdims=True) + biasn_ref[...]
        update(s_new, seln_ref[:, 0:1] > 0.5, lambda p: p.astype(BF16).astype(F32) * vn)
        o = acc_scr[...] / l_scr[...]
        lane = lax.broadcasted_iota(I32, o.shape, 1)
        row = lax.broadcasted_iota(I32, o.shape, 0)
        own = (lane // DSA_HEAD_DIM) == (row // DSA_GROUP)
        o_ref[...] = jnp.where(own, o, 0.0)


def _dsa_sample(page_table, qs, kv_new, selp, seln, bias_past, bias_new, cache_kv4, pps):
    dec, n_pages = page_table.shape
    n_chunks = n_pages // pps
    chunk = pps * PAGE_SIZE
    grid_spec = pltpu.PrefetchScalarGridSpec(
        num_scalar_prefetch=1,
        grid=(dec, n_chunks),
        in_specs=[pl.BlockSpec((None, DSA_HEADS, LANES), lambda b, c, pt: (b, 0, 0)),
                  pl.BlockSpec((None, 1, 256), lambda b, c, pt: (b, 0, 0)),
                  pl.BlockSpec((None, 1, chunk), lambda b, c, pt: (b, 0, c)),
                  pl.BlockSpec((None, 1, LANES), lambda b, c, pt: (b, 0, 0)),
                  pl.BlockSpec((DSA_HEADS, chunk), lambda b, c, pt: (0, c)),
                  pl.BlockSpec((DSA_HEADS, 1), lambda b, c, pt: (0, 0)),
                  pl.BlockSpec(memory_space=pl.ANY)],
        out_specs=pl.BlockSpec((None, DSA_HEADS, LANES), lambda b, c, pt: (b, 0, 0)),
        scratch_shapes=_page_scratch(pps, 256) + [
            pltpu.VMEM((128, chunk), BF16), pltpu.VMEM((128, chunk), BF16),
            pltpu.VMEM((DSA_HEADS, 1), F32), pltpu.VMEM((DSA_HEADS, 1), F32),
            pltpu.VMEM((DSA_HEADS, LANES), F32)],
    )
    return pl.pallas_call(
        functools.partial(_dsa_sample_kernel, pps=pps), grid_spec=grid_spec,
        out_shape=jax.ShapeDtypeStruct((dec, DSA_HEADS, LANES), F32),
        compiler_params=_cparams(("arbitrary", "arbitrary")),
    )(page_table, qs, kv_new, selp, seln, bias_past, bias_new, cache_kv4)


def _out_kernel(x_ref, oa_ref, ob_ref, gm_ref, sf_ref, scf_ref, gf_ref, gout_ref, gffn_ref,
                wout_ref, wg_ref, wu_ref, wdown_ref, y_ref, *, ff_chunk):
    def rms(v, g):
        return v * lax.rsqrt(jnp.mean(v * v, axis=-1, keepdims=True) + EPS) * g

    half = oa_ref.shape[1]
    na = rms(oa_ref[...], gout_ref[:, 0:half]).astype(BF16)
    nb = rms(ob_ref[...], gout_ref[:, half:2 * half]).astype(BF16)
    mix = _dot(na, wout_ref[0:half, :]) + _dot(nb, wout_ref[half:2 * half, :])
    x1 = x_ref[...] + gm_ref[...] * mix
    hb = (rms(x1, gffn_ref[...]) * (1.0 + scf_ref[...]) + sf_ref[...]).astype(BF16)
    acc = jnp.zeros(x1.shape, F32)
    for j in range(D_FF // ff_chunk):
        g = _dot(hb, wg_ref[:, j * ff_chunk:(j + 1) * ff_chunk])
        u = _dot(hb, wu_ref[:, j * ff_chunk:(j + 1) * ff_chunk])
        a = (g / (1.0 + jnp.exp(-g))) * u
        acc = acc + _dot(a.astype(BF16), wdown_ref[j * ff_chunk:(j + 1) * ff_chunk, :])
    y_ref[...] = x1 + gf_ref[...] * acc


def _out_stage(x2d, oa, ob, gate_m, shift_f, scale_f, gate_f, wp, tm, per_token, tiles_per_seq):
    t_total = x2d.shape[0]
    if per_token:
        mod_spec = pl.BlockSpec((tm, D_MODEL), lambda i: (i, 0))
    else:
        mod_spec = pl.BlockSpec((None, 1, D_MODEL), lambda i: (i // tiles_per_seq, 0, 0))
    consts = [wp["g_out"], wp["g_ffn"], wp["w_out"], wp["w_gate"], wp["w_up"], wp["w_down"]]
    tok = lambda w: pl.BlockSpec((tm, w), lambda i: (i, 0))
    return pl.pallas_call(
        functools.partial(_out_kernel, ff_chunk=256),
        grid=(t_total // tm,),
        in_specs=[tok(D_MODEL), tok(512), tok(512), mod_spec, mod_spec, mod_spec, mod_spec]
                 + [_full(c.shape) for c in consts],
        out_specs=tok(D_MODEL),
        out_shape=jax.ShapeDtypeStruct((t_total, D_MODEL), F32),
        compiler_params=_cparams(("parallel",)),
    )(x2d, oa, ob, gate_m, shift_f, scale_f, gate_f, *consts)


def _prep_weights(w_in, g_norm_mix, g_norm_ffn, g_q_lat, w_uq, g_kv_lat, w_ukv, g_mla_q_nope, g_mla_q_rope,
                  g_mla_k_nope, g_mla_k_rope, g_dsa_q, g_dsa_k, g_out, w_out, w_ffn_in, w_ffn_out):
    splits = np.cumsum([MLA_Q_RANK, MLA_KV_RANK, MLA_ROPE, 512, 128, 128, 512, IDX_DIM, IDX_HEADS])
    s = [0] + splits.tolist()
    col = lambda i: w_in[:, s[i]:s[i + 1]]
    pad = jnp.zeros((D_MODEL, C_END - C_TAIL - IDX_DIM - MLA_ROPE - IDX_HEADS), w_in.dtype)
    w_in_r = jnp.concatenate([col(0), col(1), col(3), col(4), col(5), col(6), col(7), col(2), col(8), pad], axis=1)
    zq = jnp.zeros((MLA_Q_RANK, MLA_HEADS, LANES - MLA_NOPE - MLA_ROPE), w_uq.dtype)
    w_uq_cat = jnp.concatenate([w_uq, zq], axis=2).reshape(MLA_Q_RANK, MLA_HEADS * LANES)
    zk = jnp.zeros((MLA_KV_RANK, MLA_HEADS, LANES - MLA_NOPE), w_ukv.dtype)
    w_uk_cat = jnp.concatenate([w_ukv[:, :, :MLA_NOPE], zk], axis=2).reshape(MLA_KV_RANK, MLA_HEADS * LANES)
    w_uv = w_ukv[:, :, MLA_NOPE:].reshape(MLA_KV_RANK, MLA_HEADS * MLA_V)
    zv = jnp.zeros((MLA_KV_RANK, MLA_HEADS, LANES - MLA_V), w_ukv.dtype)
    w_uv_pad = jnp.concatenate([w_ukv[:, :, MLA_NOPE:], zv], axis=2).reshape(MLA_KV_RANK, MLA_HEADS * LANES)
    w_uk_t = w_ukv[:, :, :MLA_NOPE].reshape(MLA_KV_RANK, MLA_HEADS * MLA_NOPE).T
    z32 = jnp.zeros((LANES - MLA_NOPE - MLA_ROPE,), F32)
    z64 = jnp.zeros((LANES - MLA_NOPE,), F32)
    g_qcat = jnp.tile(jnp.concatenate([g_mla_q_nope, g_mla_q_rope, z32]), MLA_HEADS)
    g_kcat = jnp.tile(jnp.concatenate([g_mla_k_nope, z64]), MLA_HEADS)
    g_tail = jnp.concatenate([jnp.zeros((T_ROPE,), F32), g_mla_k_rope, z32])
    row = lambda v: v.reshape(1, -1).astype(F32)
    return {
        "g_mix": row(g_norm_mix), "g_ffn": row(g_norm_ffn), "w_in": w_in_r.astype(BF16),
        "g_qlat": row(g_q_lat), "w_uq": w_uq_cat.astype(BF16), "g_qcat": row(g_qcat),
        "g_kvlat": row(g_kv_lat), "w_ukv": jnp.concatenate([w_uk_cat, w_uv_pad], axis=1).astype(BF16),
        "g_kcat": row(g_kcat), "g_tail": row(g_tail),
        "g_q": row(jnp.tile(g_dsa_q, DSA_HEADS)), "g_k": row(jnp.tile(g_dsa_k, DSA_KV_HEADS)),
        "m_cat": _group_matrix(256, [(0, 64), (64, 96), (128, 192), (192, 224)]),
        "m_64": _group_matrix(256, [(0, 64), (64, 128), (128, 192), (192, 256)]),
        "w_uk_t": w_uk_t.astype(BF16), "w_uv": w_uv.astype(BF16), "g_k_nope": g_mla_k_nope,
        "g_out": row(g_out), "w_out": w_out.astype(BF16),
        "w_gate": w_ffn_in[:, :D_FF].astype(BF16), "w_up": w_ffn_in[:, D_FF:].astype(BF16),
        "w_down": w_ffn_out.astype(BF16),
    }


def _rope_tables(pos):
    freq = ROPE_THETA ** (-jnp.arange(HALF_ROPE, dtype=F32) / HALF_ROPE)
    ang = pos.astype(F32)[:, None] * freq[None, :]
    cos, sin = jnp.cos(ang), jnp.sin(ang)
    n = pos.shape[0]
    ones = jnp.ones((n, T_ROPE), F32)
    zeros = jnp.zeros((n, T_ROPE), F32)
    tail1 = jnp.ones((n, LANES - T_ROPE - MLA_ROPE), F32)
    tail0 = jnp.zeros((n, LANES - T_ROPE - MLA_ROPE), F32)
    return (jnp.concatenate([ones, cos, cos, tail1], axis=1),
            jnp.concatenate([zeros, -sin, sin, tail0], axis=1))


def _pick_tile(n, prefs):
    for t in prefs:
        if n % t == 0:
            return t
    return n


def kernel(x_prompt, x_sample, c_prompt, c_sample, cache_mla, cache_kv, cache_idx, page_table, rel_bias, w_ada, b_ada, g_norm_mix, g_norm_ffn, w_in, g_q_lat, w_uq, g_kv_lat, w_ukv, g_mla_q_nope, g_mla_q_rope, g_mla_k_nope, g_mla_k_rope, g_dsa_q, g_dsa_k, g_out, w_out, w_ffn_in, w_ffn_out):
    assert w_ada.shape[0] == 1 and x_sample.shape[1] == 1, "one layer, one new token per sample"
    batch, seq, _ = x_prompt.shape
    dec = x_sample.shape[0]
    n_pages = page_table.shape[1]
    past = n_pages * PAGE_SIZE
    wp = _prep_weights(w_in[0], g_norm_mix[0], g_norm_ffn[0], g_q_lat[0], w_uq[0], g_kv_lat[0], w_ukv[0],
                       g_mla_q_nope[0], g_mla_q_rope[0], g_mla_k_nope[0], g_mla_k_rope[0], g_dsa_q[0], g_dsa_k[0],
                       g_out[0], w_out[0], w_ffn_in[0], w_ffn_out[0])

    mod = _ada(jnp.concatenate([c_prompt, c_sample], axis=0), w_ada[0], b_ada[0])
    mods_p = [m.reshape(batch, 1, D_MODEL) for m in jnp.split(mod[:batch], 6, axis=-1)]
    mods_s = jnp.split(mod[batch:], 6, axis=-1)

    tm = _pick_tile(seq, (512, 256, 128))
    xp2 = x_prompt.reshape(batch * seq, D_MODEL)
    cos_p, sin_p = _rope_tables(jnp.arange(seq))
    pin = _in_stage(xp2, mods_p[1], mods_p[0], cos_p, sin_p, wp, tm, per_token=False)
    o_mla_p = _mla_prompt(pin["qcat"], pin["kcat"], pin["vmla"], batch, seq,
                          _pick_tile(seq, (256, 128)), _pick_tile(seq, (512, 256, 128)))
    o_dsa_p = _dsa_prompt(rel_bias, pin["qidx"], pin["tail"], pin["kidxdup"], pin["qdsa"], pin["kdup"], pin["vone"],
                          batch, seq, _pick_tile(seq, (256, 128)))
    y_p = _out_stage(xp2, o_mla_p, o_dsa_p, mods_p[2], mods_p[3], mods_p[4], mods_p[5], wp, tm,
                     per_token=False, tiles_per_seq=seq // tm)

    xs2 = x_sample.reshape(dec, D_MODEL)
    cos_s, sin_s = _rope_tables(jnp.full((1,), past))
    sin_ = _in_stage(xs2, mods_s[1], mods_s[0], cos_s, sin_s, wp, dec, per_token=True)
    pps = _pick_tile(n_pages, (16, 8, 4, 2))
    pps_wide = _pick_tile(n_pages, (32, 16, 8, 4, 2))
    qidx3 = sin_["qidx"].reshape(dec, IDX_HEADS, IDX_DIM)
    widx3 = sin_["tail"][:, T_WIDX:T_WIDX + IDX_HEADS].reshape(dec, IDX_HEADS, 1)
    scores = _sample_scores(page_table, qidx3, widx3, _pages_feature_major(cache_idx), pps_wide)
    selp, seln = _sample_select(scores, sin_["qidx"], sin_["tail"])
    qc = sin_["qcat"].reshape(dec, MLA_HEADS, LANES)
    qn_row = qc[:, :, 0:MLA_NOPE].astype(F32).reshape(dec, 1, MLA_HEADS * MLA_NOPE)
    qr = qc[:, :, T_ROPE:T_ROPE + MLA_ROPE]
    gk_row = jnp.tile(wp["g_k_nope"], MLA_HEADS).reshape(1, MLA_HEADS * MLA_NOPE)
    o_mla_s = _mla_sample(page_table, qn_row, qr, sin_["row"].reshape(dec, MLA_ROW, 1), wp["w_uk_t"], wp["w_uv"],
                          gk_row, _pages_feature_major(cache_mla), pps_wide).reshape(dec, MLA_HEADS * MLA_V)
    qd = sin_["qdsa"].astype(F32).reshape(dec, DSA_HEADS, DSA_HEAD_DIM)
    on_c = (jnp.arange(DSA_KV_HEADS)[None, :] == (jnp.arange(DSA_HEADS) // DSA_GROUP)[:, None]).astype(F32)
    qd2 = (qd[:, :, None, :] * on_c[None, :, :, None]).reshape(dec, DSA_HEADS, LANES)
    bias_past = _bias_table(rel_bias, 1, past, 0, -1, past).reshape(DSA_HEADS, past)
    bias_new = _bias_table(rel_bias, 1, LANES, 0, 0, 0)[:, 0, 0:1]
    o8 = _dsa_sample(page_table, qd2, sin_["kv"].reshape(dec, 1, 256), selp.reshape(dec, 1, past),
                     seln.reshape(dec, 1, LANES), bias_past, bias_new, _pages_feature_major(cache_kv), pps_wide)
    o_dsa_s = o8.reshape(dec, DSA_HEADS, DSA_KV_HEADS, DSA_HEAD_DIM).sum(axis=2).reshape(dec, DSA_HEADS * DSA_HEAD_DIM)
    y_s = _out_stage(xs2, o_mla_s, o_dsa_s, mods_s[2], mods_s[3], mods_s[4], mods_s[5], wp, dec,
                     per_token=True, tiles_per_seq=1)

    return (y_p.reshape(batch, seq, D_MODEL), y_s.reshape(dec, 1, D_MODEL),
            jnp.swapaxes(pin["row"], 1, 2)[None],
            jnp.moveaxis(pin["kv"].reshape(batch, 2, DSA_KV_HEADS, DSA_HEAD_DIM, seq), -1, 1)[None],
            jnp.swapaxes(pin["kidx"], 1, 2)[None],
            sin_["row"].reshape(1, dec, 1, MLA_ROW),
            sin_["kv"].reshape(1, dec, 1, 2, DSA_KV_HEADS, DSA_HEAD_DIM),
            sin_["kidx"].reshape(1, dec, 1, IDX_DIM))
```

```python
import functools
import math

import numpy as np
import jax
import jax.numpy as jnp
from jax import lax
from jax.experimental import pallas as pl
from jax.experimental.pallas import tpu as pltpu

D_MODEL = 1024
PAGE_SIZE = 128
MLA_HEADS = 8
MLA_NOPE = 64
MLA_ROPE = 32
MLA_V = 64
MLA_Q_RANK = 384
MLA_KV_RANK = 256
MLA_ROW = MLA_KV_RANK + MLA_ROPE
DSA_HEADS = 8
DSA_KV_HEADS = 2
DSA_HEAD_DIM = 64
DSA_GROUP = DSA_HEADS // DSA_KV_HEADS
IDX_HEADS = 8
IDX_DIM = 64
TOPK_MAX = 256
N_BUCKETS = 32
MAX_DISTANCE = 128
ROPE_THETA = 10000.0
D_FF = ((8 * D_MODEL + 3 * 256 - 1) // (3 * 256)) * 256
EPS = 1e-6
MLA_SCALE = (MLA_NOPE + MLA_ROPE) ** -0.5
DSA_SCALE = DSA_HEAD_DIM ** -0.5
IDX_SCALE = (IDX_DIM * IDX_HEADS) ** -0.5

LANES = 128
VMEM_LIMIT = 56 * 1024 * 1024

F32 = jnp.float32
BF16 = jnp.bfloat16
I32 = jnp.int32
NEG_BIG = -1e30
INT_MIN = -(2 ** 31)

C_QLAT = 0
C_KVLAT = C_QLAT + MLA_Q_RANK
C_Q = C_KVLAT + MLA_KV_RANK
C_K = C_Q + DSA_HEADS * DSA_HEAD_DIM
C_V = C_K + DSA_KV_HEADS * DSA_HEAD_DIM
C_QIDX = C_V + DSA_KV_HEADS * DSA_HEAD_DIM
C_TAIL = C_QIDX + IDX_HEADS * IDX_DIM
C_END = C_TAIL + LANES
T_ROPE = IDX_DIM
T_WIDX = IDX_DIM + MLA_ROPE
HALF_ROPE = MLA_ROPE // 2
KB = 128


def _dot(a, b):
    return jnp.dot(a, b, preferred_element_type=F32)


def _dot_nt(a, b):
    return lax.dot_general(a, b, (((1,), (1,)), ((), ())), preferred_element_type=F32)


def _split(a):
    hi = a.astype(BF16)
    lo = (a - hi.astype(F32)).astype(BF16)
    return hi, lo


def _dot3(a, b):
    ah, al = _split(a)
    bh, bl = _split(b)
    return _dot(ah, bh) + (_dot(al, bh) + _dot(ah, bl))


def _cparams(sem):
    return pltpu.CompilerParams(dimension_semantics=sem, vmem_limit_bytes=VMEM_LIMIT)


def _full(shape):
    n = len(shape)
    return pl.BlockSpec(shape, lambda *a, _n=n: (0,) * _n, pipeline_mode=pl.Buffered(1))


def _ada_kernel(c_ref, w_ref, b_ref, o_ref):
    c = c_ref[...]
    s = c / (1.0 + jnp.exp(-c))
    o_ref[...] = _dot3(s, w_ref[...]) + b_ref[...]


def _ada(c, w, b):
    n = c.shape[0]
    return pl.pallas_call(
        _ada_kernel,
        grid=(6,),
        in_specs=[pl.BlockSpec((n, D_MODEL), lambda j: (0, 0)),
                  pl.BlockSpec((D_MODEL, D_MODEL), lambda j: (0, j)),
                  pl.BlockSpec((1, D_MODEL), lambda j: (0, j))],
        out_specs=pl.BlockSpec((n, D_MODEL), lambda j: (0, j)),
        out_shape=jax.ShapeDtypeStruct((n, 6 * D_MODEL), F32),
        compiler_params=_cparams(("arbitrary",)),
    )(c, w, b.reshape(1, -1))


def _group_matrix(width, groups):
    g = np.zeros((width, width), np.float32)
    for lo, hi in groups:
        g[lo:hi, lo:hi] = 1.0 / (hi - lo)
    return jnp.asarray(g, BF16)


def _in_kernel(x_ref, sc_ref, sh_ref, cos_ref, sin_ref, gmix_ref, win_ref, gqlat_ref, wuq_ref, gqcat_ref,
               gkvlat_ref, wukv_ref, gkcat_ref, gtail_ref, gq_ref, gk_ref, mcat_ref, m64_ref,
               qcat_o, kcat_o, vmla_o, row_o, kv_o, kidx_o, tail_o, kidxdup_o, qdsa_o, kdup_o, vone_o, qidx_o,
               *, rows_feature_major):
    tm = x_ref.shape[0]
    x = x_ref[...]
    h = x * lax.rsqrt(jnp.mean(x * x, axis=-1, keepdims=True) + EPS) * gmix_ref[...]
    hb = (h * (1.0 + sc_ref[...]) + sh_ref[...]).astype(BF16)

    def proj(lo, hi):
        return _dot(hb, win_ref[:, lo:hi])

    cos = cos_ref[...]
    sin = sin_ref[...]
    lane = lax.broadcasted_iota(I32, (tm, LANES), 1)
    first_half = lane < T_ROPE + HALF_ROPE
    rope_lanes = (lane >= T_ROPE) & (lane < T_ROPE + MLA_ROPE)
    low64 = lane < 64

    def rope(xh):
        rot = jnp.where(first_half, pltpu.roll(xh, LANES - HALF_ROPE, 1), pltpu.roll(xh, HALF_ROPE, 1))
        return xh * cos + rot * sin

    def gnorm(blk, m_ref):
        ms = _dot((blk * blk).astype(BF16), m_ref[...])
        return blk * lax.rsqrt(ms + EPS)

    ql = proj(C_QLAT, C_KVLAT)
    qln = ql * lax.rsqrt(jnp.mean(ql * ql, axis=-1, keepdims=True) + EPS) * gqlat_ref[...]
    qm = _dot(qln.astype(BF16), wuq_ref[...])
    for j in range(4):
        y = gnorm(qm[:, 256 * j:256 * j + 256], mcat_ref) * gqcat_ref[:, 256 * j:256 * j + 256]
        for t in range(2):
            qcat_o[:, 256 * j + 128 * t:256 * j + 128 * t + 128] = rope(y[:, 128 * t:128 * t + 128]).astype(BF16)

    tail = proj(C_TAIL, C_END)
    ssq = jnp.sum(jnp.where(rope_lanes, tail * tail, 0.0), axis=-1, keepdims=True) * (1.0 / MLA_ROPE)
    tn = jnp.where(rope_lanes, tail * lax.rsqrt(ssq + EPS) * gtail_ref[...], tail)
    tr = rope(tn)
    if rows_feature_major:
        kidx_o[...] = tail.T[0:IDX_DIM, :]
    else:
        kidx_o[...] = tail[:, 0:IDX_DIM]
    tail_o[...] = tail
    kidxdup_o[...] = jnp.where(low64, tail, pltpu.roll(tail, 64, 1)).astype(BF16)
    krope = jnp.where(rope_lanes, tr, 0.0)

    kvl = proj(C_KVLAT, C_Q)
    lat = kvl * lax.rsqrt(jnp.mean(kvl * kvl, axis=-1, keepdims=True) + EPS) * gkvlat_ref[...]
    if rows_feature_major:
        row_o[0:MLA_KV_RANK, :] = lat.T
        row_o[MLA_KV_RANK:MLA_ROW, :] = tr.T[T_ROPE:T_ROPE + MLA_ROPE, :]
    else:
        row_o[:, 0:MLA_KV_RANK] = lat
        row_o[:, MLA_KV_RANK:MLA_ROW] = tr[:, T_ROPE:T_ROPE + MLA_ROPE]
    latb = lat.astype(BF16)
    kx = _dot(latb, wukv_ref[:, 0:1024])
    for j in range(4):
        y = gnorm(kx[:, 256 * j:256 * j + 256], mcat_ref) * gkcat_ref[:, 256 * j:256 * j + 256]
        for t in range(2):
            kcat_o[:, 256 * j + 128 * t:256 * j + 128 * t + 128] = (y[:, 128 * t:128 * t + 128] + krope).astype(BF16)
    ones_half = (lax.broadcasted_iota(I32, (tm, 1024), 1) % LANES) >= MLA_V
    vmla_o[...] = jnp.where(ones_half, 1.0, _dot(latb, wukv_ref[:, 1024:2048])).astype(BF16)

    q = proj(C_Q, C_K)
    for j in range(2):
        qdsa_o[:, 256 * j:256 * j + 256] = (gnorm(q[:, 256 * j:256 * j + 256], m64_ref)
                                            * gq_ref[:, 256 * j:256 * j + 256] * DSA_SCALE).astype(BF16)
    k = proj(C_K, C_V)
    ms = _dot((k * k).astype(BF16), m64_ref[0:128, 0:128])
    kn = k * lax.rsqrt(ms + EPS) * gk_ref[...]
    v = proj(C_V, C_QIDX)
    if rows_feature_major:
        kv_o[0:128, :] = kn.T
        kv_o[128:256, :] = v.T
    else:
        kv_o[:, 0:128] = kn
        kv_o[:, 128:256] = v
    kr = pltpu.roll(kn, 64, 1)
    kdup_o[:, 0:128] = jnp.where(low64, kn, kr).astype(BF16)
    kdup_o[:, 128:256] = jnp.where(low64, kr, kn).astype(BF16)
    vone_o[:, 0:128] = jnp.where(low64, v, 1.0).astype(BF16)
    vone_o[:, 128:256] = jnp.where(low64, pltpu.roll(v, 64, 1), 1.0).astype(BF16)
    qidx_o[...] = proj(C_QIDX, C_TAIL).astype(BF16)


_IN_OUT_WIDTHS = (("qcat", 1024, BF16), ("kcat", 1024, BF16), ("vmla", 1024, BF16), ("row", MLA_ROW, F32),
                  ("kv", 256, F32), ("kidx", IDX_DIM, F32), ("tail", LANES, F32), ("kidxdup", LANES, BF16),
                  ("qdsa", 512, BF16), ("kdup", 256, BF16), ("vone", 256, BF16), ("qidx", 512, BF16))


def _in_stage(x2d, scale, shift, cos, sin, wp, tm, per_token):
    t_total = x2d.shape[0]
    n_tiles = t_total // tm
    if per_token:
        mod_spec = pl.BlockSpec((tm, D_MODEL), lambda i: (i, 0))
        tab_spec = pl.BlockSpec((1, LANES), lambda i: (0, 0))
    else:
        tiles_per_seq = cos.shape[0] // tm
        mod_spec = pl.BlockSpec((None, 1, D_MODEL), lambda i: (i // tiles_per_seq, 0, 0))
        tab_spec = pl.BlockSpec((tm, LANES), lambda i: (i % tiles_per_seq, 0))
    consts = [wp["g_mix"], wp["w_in"], wp["g_qlat"], wp["w_uq"], wp["g_qcat"], wp["g_kvlat"], wp["w_ukv"],
              wp["g_kcat"], wp["g_tail"], wp["g_q"], wp["g_k"], wp["m_cat"], wp["m_64"]]
    in_specs = ([pl.BlockSpec((tm, D_MODEL), lambda i: (i, 0)), mod_spec, mod_spec, tab_spec, tab_spec]
                + [_full(c.shape) for c in consts])
    out_specs = [pl.BlockSpec((tm, w), lambda i: (i, 0)) for _, w, _ in _IN_OUT_WIDTHS]
    out_shape = [jax.ShapeDtypeStruct((t_total, w), dt) for _, w, dt in _IN_OUT_WIDTHS]
    if not per_token:
        for j, (name, w, dt) in enumerate(_IN_OUT_WIDTHS):
            if name in ("row", "kv", "kidx"):
                out_specs[j] = pl.BlockSpec((None, w, tm), lambda i: (i // tiles_per_seq, 0, i % tiles_per_seq))
                out_shape[j] = jax.ShapeDtypeStruct((n_tiles // tiles_per_seq, w, tiles_per_seq * tm), dt)
    outs = pl.pallas_call(
        functools.partial(_in_kernel, rows_feature_major=not per_token),
        grid=(n_tiles,), in_specs=in_specs, out_specs=out_specs, out_shape=out_shape,
        compiler_params=_cparams(("parallel",)),
    )(x2d, scale, shift, cos, sin, *consts)
    return {name: o for (name, _, _), o in zip(_IN_OUT_WIDTHS, outs)}


def _mla_prompt_kernel(q_ref, k_ref, v_ref, o_ref, m_scr, acc_scr, s_scr, *, tk):
    tq = q_ref.shape[0]
    qi = pl.program_id(1)
    n_chunks = (qi * tq + tq + tk - 1) // tk
    m_scr[...] = jnp.full(m_scr.shape, -jnp.inf, F32)
    acc_scr[...] = jnp.zeros(acc_scr.shape, F32)
    low64 = lax.broadcasted_iota(I32, (tq, LANES), 1) < 64
    qpos = qi * tq + lax.broadcasted_iota(I32, (tq, tk), 0)
    kcol = lax.broadcasted_iota(I32, (tq, tk), 1)
    exp_scale = MLA_SCALE * math.log2(math.e)

    def raw_scores(c, h, masked):
        off = pl.multiple_of(c * tk, tk)
        s = _dot_nt(q_ref[:, h * 128:(h + 1) * 128], k_ref[pl.ds(off, tk), h * 128:(h + 1) * 128])
        if masked:
            s = jnp.where(c * tk + kcol <= qpos, s, -jnp.inf)
        return s

    def pass_max(c, masked):
        for h in range(MLA_HEADS):
            s = raw_scores(c, h, masked)
            s_scr[c, h] = s
            m = m_scr[h]
            for j in range(tk // LANES):
                m = jnp.maximum(m, s[:, j * LANES:(j + 1) * LANES])
            m_scr[h] = m

    def pass_sum(c, masked):
        off = pl.multiple_of(c * tk, tk)
        for h in range(MLA_HEADS):
            s = s_scr[c, h]
            m = m_scr[h]
            p = jnp.concatenate([jnp.exp2((s[:, j * LANES:(j + 1) * LANES] - m) * exp_scale)
                                 for j in range(tk // LANES)], axis=1).astype(BF16)
            acc_scr[h] += _dot(p, v_ref[pl.ds(off, tk), h * 128:(h + 1) * 128])

    def run(step):
        def body(c, carry):
            step(c, False)
            return carry

        lax.fori_loop(0, n_chunks - 1, body, 0)
        step(n_chunks - 1, True)

    run(pass_max)
    for h in range(MLA_HEADS):
        m_scr[h] = jnp.broadcast_to(jnp.max(m_scr[h], axis=-1, keepdims=True), (tq, LANES))
    run(pass_sum)
    for p in range(MLA_HEADS // 2):
        even, odd = acc_scr[2 * p], acc_scr[2 * p + 1]
        o_ref[:, 128 * p:128 * p + 128] = jnp.where(low64, even / pltpu.roll(even, 64, 1),
                                                    pltpu.roll(odd, 64, 1) / odd)


def _mla_prompt(qcat, kcat, vmla, batch, seq, tq, tk):
    nq = seq // tq
    return pl.pallas_call(
        functools.partial(_mla_prompt_kernel, tk=tk),
        grid=(batch, nq),
        in_specs=[pl.BlockSpec((tq, 1024), lambda b, i: (b * nq + i, 0)),
                  pl.BlockSpec((seq, 1024), lambda b, i: (b, 0)),
                  pl.BlockSpec((seq, 1024), lambda b, i: (b, 0))],
        out_specs=pl.BlockSpec((tq, 512), lambda b, i: (b * nq + i, 0)),
        out_shape=jax.ShapeDtypeStruct((batch * seq, 512), F32),
        scratch_shapes=[pltpu.VMEM((MLA_HEADS, tq, LANES), F32), pltpu.VMEM((MLA_HEADS, tq, LANES), F32),
                        pltpu.VMEM((seq // tk, MLA_HEADS, tq, tk), F32)],
        compiler_params=_cparams(("parallel", "arbitrary")),
    )(qcat, kcat, vmla)


def _sort_key(score):
    bits = lax.bitcast_convert_type(score + 0.0, I32)
    return bits ^ ((bits >> 31) & 0x7FFFFFFF)


def _kth_key(count_ge, n_sel, shape):
    def body(i, t):
        cand = t + (jnp.int32(1) << (31 - i))
        return jnp.where(count_ge(cand) >= n_sel, cand, t)

    return lax.fori_loop(0, 32, body, jnp.full(shape, INT_MIN, I32))


def _bucket(dist):
    max_exact = N_BUCKETS // 2
    d = jnp.maximum(dist, 0)
    log_ratio = jnp.log(jnp.maximum(d, max_exact).astype(F32) / max_exact) / math.log(MAX_DISTANCE / max_exact)
    large = jnp.minimum(max_exact + (log_ratio * (N_BUCKETS - max_exact)).astype(I32), N_BUCKETS - 1)
    return jnp.where(d < max_exact, d, large)


def _bias_kernel(rb_ref, o_ref, *, a, b, c):
    rows, cols = o_ref.shape[1], o_ref.shape[2]
    dist = (a * lax.broadcasted_iota(I32, (rows, cols), 0) + b * lax.broadcasted_iota(I32, (rows, cols), 1) + c)
    bucket = _bucket(dist)
    for h in range(DSA_HEADS):
        acc = jnp.zeros((rows, cols), F32)
        for n in range(N_BUCKETS):
            acc = jnp.where(bucket == n, rb_ref[n, h], acc)
        o_ref[h] = acc


def _bias_table(rel_bias, rows, cols, a, b, c):
    return pl.pallas_call(
        functools.partial(_bias_kernel, a=a, b=b, c=c),
        in_specs=[pl.BlockSpec(memory_space=pltpu.SMEM)],
        out_specs=pl.BlockSpec(memory_space=pltpu.VMEM),
        out_shape=jax.ShapeDtypeStruct((DSA_HEADS, rows, cols), F32),
    )(rel_bias)


def _loop2(n, body):
    n = jnp.maximum(n, 0)

    def pair(j, carry):
        body(2 * j)
        body(2 * j + 1)
        return carry

    lax.fori_loop(0, n // 2, pair, 0)

    @pl.when(n % 2 == 1)
    def _():
        body(n - 1)


def _colreduce(x, op):
    return op(x.reshape(x.shape[0] // 8, 8, x.shape[1]), axis=0)


def _dsa_prompt_kernel(rb_ref, qidx_ref, tail_ref, kidx_ref, q_ref, k_ref, v_ref, bias_ref, o_ref,
                       qs_scr, qd_scr, wb_scr, key_scr, sbq_scr, tie_scr, m_scr, p_scr, acc_scr, s_scr, *, n_sel, idx_bits):
    tq = q_ref.shape[0]
    kq = tq // KB
    qi = pl.program_id(1)
    krow = lax.broadcasted_iota(I32, (KB, tq), 0)
    qcol = lax.broadcasted_iota(I32, (KB, tq), 1)
    low64 = lax.broadcasted_iota(I32, (tq, LANES), 1) < 64

    tail = tail_ref[...]
    for h in range(IDX_HEADS):
        mine = low64 if h % 2 == 0 else jnp.logical_not(low64)
        zero = jnp.zeros((tq, LANES), BF16)
        qs_scr[h * tq:(h + 1) * tq, :] = jnp.where(mine, qidx_ref[:, (h // 2) * 128:(h // 2 + 1) * 128], zero)
        qd_scr[h * tq:(h + 1) * tq, :] = jnp.where(mine, q_ref[:, (h // 2) * 128:(h // 2 + 1) * 128], zero)
        wb_scr[h] = jnp.broadcast_to(tail[:, T_WIDX + h:T_WIDX + h + 1], (tq, LANES))

    def rows_of(c):
        return pl.ds(pl.multiple_of(c * tq, tq), tq)

    def score_chunk(c, in_tile):
        d = _dot_nt(qs_scr[...], kidx_ref[rows_of(c), :])
        for t in range(kq):
            sc = jnp.zeros((tq, KB), F32)
            for h in range(IDX_HEADS):
                sc = sc + jnp.maximum(d[h * tq:(h + 1) * tq, t * KB:(t + 1) * KB], 0.0) * wb_scr[h]
            sc_t = (sc * IDX_SCALE).T
            if in_tile:
                sc_t = jnp.where(t * KB + krow <= qcol, sc_t, -jnp.inf)
            key_scr[c * kq + t] = _sort_key(sc_t)

    _loop2(qi, lambda c: score_chunk(c, False))
    score_chunk(qi, True)
    n_chunks = qi + 1

    def count(pred):
        def body(c, acc):
            for t in range(kq):
                kb = c * kq + t
                acc = acc + jnp.where(pred(key_scr[kb], kb), 1.0, 0.0)
            return acc
        acc = lax.fori_loop(0, n_chunks, body, jnp.zeros((KB, tq), F32))
        return jnp.sum(_colreduce(acc, jnp.sum), axis=0, keepdims=True)

    thr = _kth_key(lambda cand: count(lambda k, kb: k >= cand), float(n_sel), (1, tq))
    cnt_gt = count(lambda k, kb: k > thr)
    cnt_ge = count(lambda k, kb: k >= thr)
    tie_scr[...] = jnp.full((1, tq), 2 ** 30, I32)
    excess = jnp.where((cnt_ge > float(n_sel)) & (thr > INT_MIN), 1.0, 0.0)

    @pl.when(jnp.max(excess) > 0.0)
    def _():
        need = float(n_sel) - cnt_gt

        def body(i, p):
            cand = p + (jnp.int32(1) << (idx_bits - 1 - i))
            c = count(lambda k, kb: (k == thr) & (kb * KB + krow < cand))
            return jnp.where(c < need, cand, p)

        tie_scr[...] = lax.fori_loop(0, idx_bits, body, jnp.zeros((1, tq), I32))

    last_tie = tie_scr[...]

    def mask_chunk(c, in_tile):
        for t in range(kq):
            kb = c * kq + t
            key = key_scr[kb]
            sel = (key > thr) | ((key == thr) & (kb * KB + krow <= last_tie))
            if in_tile:
                sel = sel & (t * KB + krow <= qcol)
            sbq_scr[c, :, t * KB:(t + 1) * KB] = jnp.where(sel, 0.0, NEG_BIG).T

    _loop2(qi, lambda c: mask_chunk(c, False))
    mask_chunk(qi, True)

    far_bias = [rb_ref[N_BUCKETS - 1, h] for h in range(DSA_HEADS)]

    def scores(c, g, kind):
        sg = _dot_nt(qd_scr[g * DSA_GROUP * tq:(g + 1) * DSA_GROUP * tq, :], k_ref[rows_of(c), g * 128:(g + 1) * 128])
        sb = sbq_scr[c]
        out = []
        for hh in range(DSA_GROUP):
            h = g * DSA_GROUP + hh
            cols = []
            for t in range(kq):
                s = sg[hh * tq:(hh + 1) * tq, t * KB:(t + 1) * KB] + sb[:, t * KB:(t + 1) * KB]
                if kind == "tile":
                    s = s + (bias_ref[1 + t, h] - far_bias[h])
                elif kind == "before" and t == kq - 1:
                    s = s + (bias_ref[0, h] - far_bias[h])
                cols.append(s)
            out.append(cols)
        return out

    m_scr[...] = jnp.full(m_scr.shape, NEG_BIG, F32)
    acc_scr[...] = jnp.zeros(acc_scr.shape, F32)

    def pass_max(c, kind):
        for g in range(DSA_KV_HEADS):
            for hh, cols in enumerate(scores(c, g, kind)):
                h = g * DSA_GROUP + hh
                m = m_scr[h]
                for t, s in enumerate(cols):
                    s_scr[c, h, :, t * KB:(t + 1) * KB] = s
                    m = jnp.maximum(m, s)
                m_scr[h] = m

    def pass_sum(c, kind):
        for g in range(DSA_KV_HEADS):
            for hh in range(DSA_GROUP):
                h = g * DSA_GROUP + hh
                m = m_scr[h]
                for t in range(kq):
                    s = s_scr[c, h, :, t * KB:(t + 1) * KB]
                    p_scr[g, hh * tq:(hh + 1) * tq, t * KB:(t + 1) * KB] = jnp.exp(s - m).astype(BF16)
            acc_scr[g] += _dot(p_scr[g], v_ref[rows_of(c), g * 128:(g + 1) * 128])

    def run(step):
        _loop2(qi - 1, lambda c: step(c, "far"))

        @pl.when(qi >= 1)
        def _():
            step(qi - 1, "before")

        step(qi, "tile")

    run(pass_max)
    for h in range(DSA_HEADS):
        m_scr[h] = jnp.broadcast_to(jnp.max(m_scr[h], axis=-1, keepdims=True), (tq, LANES))
    run(pass_sum)

    for p in range(DSA_HEADS // 2):
        g, hh = (2 * p) // DSA_GROUP, (2 * p) % DSA_GROUP
        even = acc_scr[g, hh * tq:(hh + 1) * tq, :]
        odd = acc_scr[g, (hh + 1) * tq:(hh + 2) * tq, :]
        o_ref[:, 128 * p:128 * p + 128] = jnp.where(low64, even / pltpu.roll(even, 64, 1),
                                                    pltpu.roll(odd, 64, 1) / odd)


def _dsa_prompt(rel_bias, qidx, tail, kidxdup, qdsa, kdup, vone, batch, seq, tq):
    assert MAX_DISTANCE <= KB, "keys two or more blocks back must all fall in the last bucket"
    nq = seq // tq
    nkb = seq // KB
    kq = tq // KB
    n_sel = min(TOPK_MAX, seq // 4)
    idx_bits = max(1, (seq - 1).bit_length())
    bias = jnp.stack([_bias_table(rel_bias, tq, KB, 1, -1, KB - t * KB) for t in range(kq + 1)])
    kern = functools.partial(_dsa_prompt_kernel, n_sel=n_sel, idx_bits=idx_bits)
    qblk = lambda w: pl.BlockSpec((tq, w), lambda b, i: (b * nq + i, 0))
    seqblk = lambda w: pl.BlockSpec((seq, w), lambda b, i: (b, 0))
    return pl.pallas_call(
        kern,
        grid=(batch, nq),
        in_specs=[pl.BlockSpec(memory_space=pltpu.SMEM),
                  qblk(512), qblk(LANES), seqblk(LANES), qblk(512), seqblk(256), seqblk(256), _full(bias.shape)],
        out_specs=qblk(512),
        out_shape=jax.ShapeDtypeStruct((batch * seq, 512), F32),
        scratch_shapes=[pltpu.VMEM((IDX_HEADS * tq, LANES), BF16), pltpu.VMEM((DSA_HEADS * tq, LANES), BF16),
                        pltpu.VMEM((IDX_HEADS, tq, LANES), F32),
                        pltpu.VMEM((nkb, KB, tq), I32), pltpu.VMEM((nq, tq, tq), F32),
                        pltpu.VMEM((1, tq), I32), pltpu.VMEM((DSA_HEADS, tq, LANES), F32),
                        pltpu.VMEM((DSA_KV_HEADS, DSA_GROUP * tq, tq), BF16),
                        pltpu.VMEM((DSA_KV_HEADS, DSA_GROUP * tq, LANES), F32),
                        pltpu.VMEM((nq, DSA_HEADS, tq, tq), F32)],
        compiler_params=_cparams(("parallel", "arbitrary")),
    )(rel_bias, qidx, tail, kidxdup, qdsa, kdup, vone, bias)


def _fetch_pages(cache_ref, pt_ref, buf, sem, step, n_steps, locate, pps):
    def copy(slot, i, page):
        return pltpu.make_async_copy(cache_ref.at[0, page], buf.at[slot, i], sem.at[slot])

    def start(st, slot):
        b, first = locate(st)
        for i in range(pps):
            copy(slot, i, pt_ref[b, first + i]).start()

    @pl.when(step == 0)
    def _():
        start(step, 0)

    @pl.when(step + 1 < n_steps)
    def _():
        start(step + 1, (step + 1) % 2)

    slot = step % 2
    for i in range(pps):
        copy(slot, i, 0).wait()
    return slot


def _page_scratch(pps, width):
    return [pltpu.VMEM((2, pps, width, PAGE_SIZE), F32), pltpu.SemaphoreType.DMA((2,))]


def _pages_feature_major(cache):
    c = jnp.moveaxis(cache, 2, -1)
    return c.reshape(cache.shape[0], cache.shape[1], -1, PAGE_SIZE)


def _sample_score_kernel(pt_ref, q_ref, w_ref, cache_ref, o_ref, buf, sem, kb_scr, *, pps, grp):
    bo, c, bi = pl.program_id(0), pl.program_id(1), pl.program_id(2)
    n_chunks = pl.num_programs(1)
    step = (bo * n_chunks + c) * grp + bi

    def locate(st):
        return (st // (n_chunks * grp)) * grp + st % grp, ((st // grp) % n_chunks) * pps

    slot = _fetch_pages(cache_ref, pt_ref, buf, sem, step, pl.num_programs(0) * n_chunks * grp, locate, pps)
    for i in range(pps):
        kb_scr[:, i * PAGE_SIZE:(i + 1) * PAGE_SIZE] = buf[slot, i].astype(BF16)
    d = _dot(q_ref[...], kb_scr[...])
    sc = jnp.sum(jnp.maximum(d, 0.0) * w_ref[...], axis=0, keepdims=True) * IDX_SCALE
    o_ref[pl.ds(bi, 1), :] = sc


def _sample_scores(page_table, qidx3, widx3, cache_idx_t, pps):
    dec, n_pages = page_table.shape
    n_chunks = n_pages // pps
    grp = 8
    grid_spec = pltpu.PrefetchScalarGridSpec(
        num_scalar_prefetch=1,
        grid=(dec // grp, n_chunks, grp),
        in_specs=[pl.BlockSpec((None, IDX_HEADS, IDX_DIM), lambda bo, c, bi, pt: (bo * grp + bi, 0, 0)),
                  pl.BlockSpec((None, IDX_HEADS, 1), lambda bo, c, bi, pt: (bo * grp + bi, 0, 0)),
                  pl.BlockSpec(memory_space=pl.ANY)],
        out_specs=pl.BlockSpec((grp, pps * PAGE_SIZE), lambda bo, c, bi, pt: (bo, c)),
        scratch_shapes=_page_scratch(pps, IDX_DIM) + [pltpu.VMEM((IDX_DIM, pps * PAGE_SIZE), BF16)],
    )
    return pl.pallas_call(
        functools.partial(_sample_score_kernel, pps=pps, grp=grp), grid_spec=grid_spec,
        out_shape=jax.ShapeDtypeStruct((dec, n_pages * PAGE_SIZE), F32),
        compiler_params=_cparams(("arbitrary", "arbitrary", "arbitrary")),
    )(page_table, qidx3, widx3, cache_idx_t)


def _sample_select_kernel(sc_ref, qidx_ref, tail_ref, selp_ref, seln_ref, key_scr, *, n_sel, idx_bits):
    rows, past = sc_ref.shape
    nblk = past // LANES
    lane = lax.broadcasted_iota(I32, (rows, LANES), 1)
    tail = tail_ref[...]
    kidx2 = jnp.where(lane < 64, tail, pltpu.roll(tail, 64, 1))
    sc_new = jnp.zeros((rows, 1), F32)
    for h in range(IDX_HEADS):
        prod = qidx_ref[:, (h // 2) * 128:(h // 2 + 1) * 128].astype(F32) * kidx2.astype(BF16).astype(F32)
        mine = (lane < 64) if h % 2 == 0 else (lane >= 64)
        dot = jnp.sum(jnp.where(mine, prod, 0.0), axis=-1, keepdims=True)
        sc_new = sc_new + jnp.maximum(dot, 0.0) * tail[:, T_WIDX + h:T_WIDX + h + 1]
    key_new = _sort_key(sc_new * IDX_SCALE)

    for j in range(nblk):
        key_scr[j] = _sort_key(sc_ref[:, j * LANES:(j + 1) * LANES])

    def count(pred, pred_new):
        def body(j, c):
            return c + jnp.where(pred(key_scr[j], j), 1.0, 0.0)
        c = lax.fori_loop(0, nblk, body, jnp.zeros((rows, LANES), F32), unroll=4)
        return jnp.sum(c, axis=-1, keepdims=True) + jnp.where(pred_new, 1.0, 0.0)

    def wide(col):
        return jnp.broadcast_to(col, (rows, LANES))

    def count_ge(cand):
        cand_w = wide(cand)
        return count(lambda k, j: k >= cand_w, key_new >= cand)

    thr = _kth_key(count_ge, float(n_sel), (rows, 1))
    thr_w = wide(thr)
    cnt_gt = count(lambda k, j: k > thr_w, key_new > thr)
    need = float(n_sel) - cnt_gt

    def body(i, p):
        cand = p + (jnp.int32(1) << (idx_bits - 1 - i))
        cand_w = wide(cand)
        c = count(lambda k, j: (k == thr_w) & (j * LANES + lane < cand_w), (key_new == thr) & (past < cand))
        return jnp.where(c < need, cand, p)

    last_tie = lax.fori_loop(0, idx_bits, body, jnp.zeros((rows, 1), I32))
    last_w = wide(last_tie)

    for j in range(nblk):
        k = key_scr[j]
        sel = (k > thr_w) | ((k == thr_w) & (j * LANES + lane <= last_w))
        selp_ref[:, j * LANES:(j + 1) * LANES] = jnp.where(sel, 1.0, 0.0)
    sel_new = (key_new > thr) | ((key_new == thr) & (past <= last_tie))
    seln_ref[...] = jnp.broadcast_to(jnp.where(sel_new, 1.0, 0.0), (rows, LANES))


def _sample_select(scores, qidx, tail):
    dec, past = scores.shape
    n_sel = min(TOPK_MAX, (past + 1) // 4)
    idx_bits = max(1, past.bit_length())
    rows = 64 if dec % 64 == 0 else dec
    kern = functools.partial(_sample_select_kernel, n_sel=n_sel, idx_bits=idx_bits)
    return pl.pallas_call(
        kern,
        grid=(dec // rows,),
        in_specs=[pl.BlockSpec((rows, past), lambda i: (i, 0)), pl.BlockSpec((rows, 512), lambda i: (i, 0)),
                  pl.BlockSpec((rows, LANES), lambda i: (i, 0))],
        out_specs=[pl.BlockSpec((rows, past), lambda i: (i, 0)), pl.BlockSpec((rows, LANES), lambda i: (i, 0))],
        out_shape=[jax.ShapeDtypeStruct((dec, past), F32), jax.ShapeDtypeStruct((dec, LANES), F32)],
        scratch_shapes=[pltpu.VMEM((past // LANES, rows, LANES), I32)],
        compiler_params=_cparams(("parallel",)),
    )(scores, qidx, tail)


def _mla_sample_kernel(pt_ref, qn_ref, qr_ref, rown_ref, wuk_ref, wuv_ref, gk_ref, cache_ref, o_ref,
                       buf, sem, lat_scr, kr_scr, a_scr, m_scr, l_scr, acc_scr, *, pps):
    c = pl.program_id(1)
    n_chunks = pl.num_programs(1)
    heads = MLA_HEADS
    slot = _fetch_pages(cache_ref, pt_ref, buf, sem, pl.program_id(0) * n_chunks + c,
                        pl.num_programs(0) * n_chunks, lambda st: (st // n_chunks, (st % n_chunks) * pps), pps)

    @pl.when(c == 0)
    def _():
        m_scr[...] = jnp.full(m_scr.shape, -jnp.inf, F32)
        l_scr[...] = jnp.zeros(l_scr.shape, F32)
        acc_scr[...] = jnp.zeros(acc_scr.shape, F32)
        qg = qn_ref[...] * gk_ref[...]
        own = (lax.broadcasted_iota(I32, (heads, heads * MLA_NOPE), 1) // MLA_NOPE
               == lax.broadcasted_iota(I32, (heads, heads * MLA_NOPE), 0))
        a_scr[...] = _dot(jnp.where(own, qg, 0.0).astype(BF16), wuk_ref[...]).astype(BF16)

    qr = qr_ref[...]

    def scores(latb, krb):
        knt = _dot(wuk_ref[...], latb)
        ms = jnp.mean((knt * knt).reshape(heads, MLA_NOPE, latb.shape[1]), axis=1)
        return (_dot(a_scr[...], latb) * lax.rsqrt(ms + EPS) + _dot(qr, krb)) * MLA_SCALE

    def update(latb, krb, n_valid):
        n = latb.shape[1]
        sub = min(n, 512)
        s = jnp.concatenate([scores(latb[:, j:j + sub], krb[:, j:j + sub]) for j in range(0, n, sub)], axis=1)
        if n_valid < n:
            s = jnp.where(lax.broadcasted_iota(I32, s.shape, 1) < n_valid, s, -jnp.inf)
        m_old = m_scr[...]
        m_new = jnp.maximum(m_old, jnp.max(s, axis=-1, keepdims=True))
        alpha = jnp.exp(m_old - m_new)
        p = jnp.exp(s - m_new)
        l_scr[...] = alpha * l_scr[...] + jnp.sum(p, axis=-1, keepdims=True)
        m_scr[...] = m_new
        acc_scr[...] = alpha * acc_scr[...] + _dot_nt(p.astype(BF16), latb)

    for i in range(pps):
        lat_scr[:, i * PAGE_SIZE:(i + 1) * PAGE_SIZE] = buf[slot, i, 0:MLA_KV_RANK, :].astype(BF16)
        kr_scr[:, i * PAGE_SIZE:(i + 1) * PAGE_SIZE] = buf[slot, i, MLA_KV_RANK:MLA_ROW, :].astype(BF16)
    update(lat_scr[...], kr_scr[...], lat_scr.shape[1])

    @pl.when(c == n_chunks - 1)
    def _():
        rn = jnp.broadcast_to(rown_ref[...], (MLA_ROW, PAGE_SIZE))
        update(rn[0:MLA_KV_RANK, :].astype(BF16), rn[MLA_KV_RANK:MLA_ROW, :].astype(BF16), 1)
        o_lat = (acc_scr[...] / l_scr[...]).astype(BF16)
        full = _dot(o_lat, wuv_ref[...])
        hd = lax.broadcasted_iota(I32, full.shape, 1) // MLA_V
        own = hd == lax.broadcasted_iota(I32, full.shape, 0)
        o_ref[...] = jnp.sum(jnp.where(own, full, 0.0), axis=0, keepdims=True)


def _mla_sample(page_table, qn_row, qr, row_new, wuk_t, wuv, gk_row, cache_mla, pps):
    dec, n_pages = page_table.shape
    n_chunks = n_pages // pps
    grid_spec = pltpu.PrefetchScalarGridSpec(
        num_scalar_prefetch=1,
        grid=(dec, n_chunks),
        in_specs=[pl.BlockSpec((None, 1, MLA_HEADS * MLA_NOPE), lambda b, c, pt: (b, 0, 0)),
                  pl.BlockSpec((None, MLA_HEADS, MLA_ROPE), lambda b, c, pt: (b, 0, 0)),
                  pl.BlockSpec((None, MLA_ROW, 1), lambda b, c, pt: (b, 0, 0)),
                  pl.BlockSpec(wuk_t.shape, lambda b, c, pt: (0, 0)),
                  pl.BlockSpec(wuv.shape, lambda b, c, pt: (0, 0)),
                  pl.BlockSpec(gk_row.shape, lambda b, c, pt: (0, 0)),
                  pl.BlockSpec(memory_space=pl.ANY)],
        out_specs=pl.BlockSpec((None, 1, MLA_HEADS * MLA_V), lambda b, c, pt: (b, 0, 0)),
        scratch_shapes=_page_scratch(pps, MLA_ROW) + [
            pltpu.VMEM((MLA_KV_RANK, pps * PAGE_SIZE), BF16), pltpu.VMEM((MLA_ROPE, pps * PAGE_SIZE), BF16),
            pltpu.VMEM((MLA_HEADS, MLA_KV_RANK), BF16),
            pltpu.VMEM((MLA_HEADS, 1), F32), pltpu.VMEM((MLA_HEADS, 1), F32),
            pltpu.VMEM((MLA_HEADS, MLA_KV_RANK), F32)],
    )
    return pl.pallas_call(
        functools.partial(_mla_sample_kernel, pps=pps), grid_spec=grid_spec,
        out_shape=jax.ShapeDtypeStruct((dec, 1, MLA_HEADS * MLA_V), F32),
        compiler_params=_cparams(("arbitrary", "arbitrary")),
    )(page_table, qn_row, qr, row_new, wuk_t, wuv, gk_row, cache_mla)


def _dsa_sample_kernel(pt_ref, q_ref, kvn_ref, selp_ref, seln_ref, biasp_ref, biasn_ref, cache_ref, o_ref,
                       buf, sem, k_scr, v_scr, m_scr, l_scr, acc_scr, *, pps):
    c = pl.program_id(1)
    n_chunks = pl.num_programs(1)
    slot = _fetch_pages(cache_ref, pt_ref, buf, sem, pl.program_id(0) * n_chunks + c,
                        pl.num_programs(0) * n_chunks, lambda st: (st // n_chunks, (st % n_chunks) * pps), pps)

    @pl.when(c == 0)
    def _():
        m_scr[...] = jnp.full(m_scr.shape, NEG_BIG, F32)
        l_scr[...] = jnp.zeros(l_scr.shape, F32)
        acc_scr[...] = jnp.zeros(acc_scr.shape, F32)

    q = q_ref[...]

    def update(s, sel, pv_of):
        sh = jnp.where(sel, s, NEG_BIG)
        m_old = m_scr[...]
        m_new = jnp.maximum(m_old, jnp.max(sh, axis=-1, keepdims=True))
        alpha = jnp.exp(m_old - m_new)
        p = jnp.where(sel, jnp.exp(sh - m_new), 0.0)
        l_scr[...] = alpha * l_scr[...] + jnp.sum(p, axis=-1, keepdims=True)
        m_scr[...] = m_new
        acc_scr[...] = alpha * acc_scr[...] + pv_of(p)

    for i in range(pps):
        k_scr[:, i * PAGE_SIZE:(i + 1) * PAGE_SIZE] = buf[slot, i, 0:128, :].astype(BF16)
        v_scr[:, i * PAGE_SIZE:(i + 1) * PAGE_SIZE] = buf[slot, i, 128:256, :].astype(BF16)
    s = _dot(q.astype(BF16), k_scr[...]) + biasp_ref[...]
    update(s, selp_ref[...] > 0.5, lambda p: _dot_nt(p.astype(BF16), v_scr[...]))

    @pl.when(c == n_chunks - 1)
    def _():
        kvn = kvn_ref[...]
        kn = kvn[:, 0:128].astype(BF16).astype(F32)
        vn = kvn[:, 128:256].astype(BF16).astype(F32)
        s_new = jnp.sum(q.astype(F32) * kn, axis=-1, keepdims=True) + biasn_ref[...]
        update(s_new, seln_ref[:, 0:1] > 0.5, lambda p: p.astype(BF16).astype(F32) * vn)
        o = acc_scr[...] / l_scr[...]
        lane = lax.broadcasted_iota(I32, o.shape, 1)
        row = lax.broadcasted_iota(I32, o.shape, 0)
        own = (lane // DSA_HEAD_DIM) == (row // DSA_GROUP)
        o_ref[...] = jnp.where(own, o, 0.0)


def _dsa_sample(page_table, qs, kv_new, selp, seln, bias_past, bias_new, cache_kv4, pps):
    dec, n_pages = page_table.shape
    n_chunks = n_pages // pps
    chunk = pps * PAGE_SIZE
    grid_spec = pltpu.PrefetchScalarGridSpec(
        num_scalar_prefetch=1,
        grid=(dec, n_chunks),
        in_specs=[pl.BlockSpec((None, DSA_HEADS, LANES), lambda b, c, pt: (b, 0, 0)),
                  pl.BlockSpec((None, 1, 256), lambda b, c, pt: (b, 0, 0)),
                  pl.BlockSpec((None, 1, chunk), lambda b, c, pt: (b, 0, c)),
                  pl.BlockSpec((None, 1, LANES), lambda b, c, pt: (b, 0, 0)),
                  pl.BlockSpec((DSA_HEADS, chunk), lambda b, c, pt: (0, c)),
                  pl.BlockSpec((DSA_HEADS, 1), lambda b, c, pt: (0, 0)),
                  pl.BlockSpec(memory_space=pl.ANY)],
        out_specs=pl.BlockSpec((None, DSA_HEADS, LANES), lambda b, c, pt: (b, 0, 0)),
        scratch_shapes=_page_scratch(pps, 256) + [
            pltpu.VMEM((128, chunk), BF16), pltpu.VMEM((128, chunk), BF16),
            pltpu.VMEM((DSA_HEADS, 1), F32), pltpu.VMEM((DSA_HEADS, 1), F32),
            pltpu.VMEM((DSA_HEADS, LANES), F32)],
    )
    return pl.pallas_call(
        functools.partial(_dsa_sample_kernel, pps=pps), grid_spec=grid_spec,
        out_shape=jax.ShapeDtypeStruct((dec, DSA_HEADS, LANES), F32),
        compiler_params=_cparams(("arbitrary", "arbitrary")),
    )(page_table, qs, kv_new, selp, seln, bias_past, bias_new, cache_kv4)


def _out_kernel(x_ref, oa_ref, ob_ref, gm_ref, sf_ref, scf_ref, gf_ref, gout_ref, gffn_ref,
                wout_ref, wg_ref, wu_ref, wdown_ref, y_ref, *, ff_chunk):
    def rms(v, g):
        return v * lax.rsqrt(jnp.mean(v * v, axis=-1, keepdims=True) + EPS) * g

    half = oa_ref.shape[1]
    na = rms(oa_ref[...], gout_ref[:, 0:half]).astype(BF16)
    nb = rms(ob_ref[...], gout_ref[:, half:2 * half]).astype(BF16)
    mix = _dot(na, wout_ref[0:half, :]) + _dot(nb, wout_ref[half:2 * half, :])
    x1 = x_ref[...] + gm_ref[...] * mix
    hb = (rms(x1, gffn_ref[...]) * (1.0 + scf_ref[...]) + sf_ref[...]).astype(BF16)
    acc = jnp.zeros(x1.shape, F32)
    for j in range(D_FF // ff_chunk):
        g = _dot(hb, wg_ref[:, j * ff_chunk:(j + 1) * ff_chunk])
        u = _dot(hb, wu_ref[:, j * ff_chunk:(j + 1) * ff_chunk])
        a = (g / (1.0 + jnp.exp(-g))) * u
        acc = acc + _dot(a.astype(BF16), wdown_ref[j * ff_chunk:(j + 1) * ff_chunk, :])
    y_ref[...] = x1 + gf_ref[...] * acc


def _out_stage(x2d, oa, ob, gate_m, shift_f, scale_f, gate_f, wp, tm, per_token, tiles_per_seq):
    t_total = x2d.shape[0]
    if per_token:
        mod_spec = pl.BlockSpec((tm, D_MODEL), lambda i: (i, 0))
    else:
        mod_spec = pl.BlockSpec((None, 1, D_MODEL), lambda i: (i // tiles_per_seq, 0, 0))
    consts = [wp["g_out"], wp["g_ffn"], wp["w_out"], wp["w_gate"], wp["w_up"], wp["w_down"]]
    tok = lambda w: pl.BlockSpec((tm, w), lambda i: (i, 0))
    return pl.pallas_call(
        functools.partial(_out_kernel, ff_chunk=256),
        grid=(t_total // tm,),
        in_specs=[tok(D_MODEL), tok(512), tok(512), mod_spec, mod_spec, mod_spec, mod_spec]
                 + [_full(c.shape) for c in consts],
        out_specs=tok(D_MODEL),
        out_shape=jax.ShapeDtypeStruct((t_total, D_MODEL), F32),
        compiler_params=_cparams(("parallel",)),
    )(x2d, oa, ob, gate_m, shift_f, scale_f, gate_f, *consts)


def _prep_weights(w_in, g_norm_mix, g_norm_ffn, g_q_lat, w_uq, g_kv_lat, w_ukv, g_mla_q_nope, g_mla_q_rope,
                  g_mla_k_nope, g_mla_k_rope, g_dsa_q, g_dsa_k, g_out, w_out, w_ffn_in, w_ffn_out):
    splits = np.cumsum([MLA_Q_RANK, MLA_KV_RANK, MLA_ROPE, 512, 128, 128, 512, IDX_DIM, IDX_HEADS])
    s = [0] + splits.tolist()
    col = lambda i: w_in[:, s[i]:s[i + 1]]
    pad = jnp.zeros((D_MODEL, C_END - C_TAIL - IDX_DIM - MLA_ROPE - IDX_HEADS), w_in.dtype)
    w_in_r = jnp.concatenate([col(0), col(1), col(3), col(4), col(5), col(6), col(7), col(2), col(8), pad], axis=1)
    zq = jnp.zeros((MLA_Q_RANK, MLA_HEADS, LANES - MLA_NOPE - MLA_ROPE), w_uq.dtype)
    w_uq_cat = jnp.concatenate([w_uq, zq], axis=2).reshape(MLA_Q_RANK, MLA_HEADS * LANES)
    zk = jnp.zeros((MLA_KV_RANK, MLA_HEADS, LANES - MLA_NOPE), w_ukv.dtype)
    w_uk_cat = jnp.concatenate([w_ukv[:, :, :MLA_NOPE], zk], axis=2).reshape(MLA_KV_RANK, MLA_HEADS * LANES)
    w_uv = w_ukv[:, :, MLA_NOPE:].reshape(MLA_KV_RANK, MLA_HEADS * MLA_V)
    zv = jnp.zeros((MLA_KV_RANK, MLA_HEADS, LANES - MLA_V), w_ukv.dtype)
    w_uv_pad = jnp.concatenate([w_ukv[:, :, MLA_NOPE:], zv], axis=2).reshape(MLA_KV_RANK, MLA_HEADS * LANES)
    w_uk_t = w_ukv[:, :, :MLA_NOPE].reshape(MLA_KV_RANK, MLA_HEADS * MLA_NOPE).T
    z32 = jnp.zeros((LANES - MLA_NOPE - MLA_ROPE,), F32)
    z64 = jnp.zeros((LANES - MLA_NOPE,), F32)
    g_qcat = jnp.tile(jnp.concatenate([g_mla_q_nope, g_mla_q_rope, z32]), MLA_HEADS)
    g_kcat = jnp.tile(jnp.concatenate([g_mla_k_nope, z64]), MLA_HEADS)
    g_tail = jnp.concatenate([jnp.zeros((T_ROPE,), F32), g_mla_k_rope, z32])
    row = lambda v: v.reshape(1, -1).astype(F32)
    return {
        "g_mix": row(g_norm_mix), "g_ffn": row(g_norm_ffn), "w_in": w_in_r.astype(BF16),
        "g_qlat": row(g_q_lat), "w_uq": w_uq_cat.astype(BF16), "g_qcat": row(g_qcat),
        "g_kvlat": row(g_kv_lat), "w_ukv": jnp.concatenate([w_uk_cat, w_uv_pad], axis=1).astype(BF16),
        "g_kcat": row(g_kcat), "g_tail": row(g_tail),
        "g_q": row(jnp.tile(g_dsa_q, DSA_HEADS)), "g_k": row(jnp.tile(g_dsa_k, DSA_KV_HEADS)),
        "m_cat": _group_matrix(256, [(0, 64), (64, 96), (128, 192), (192, 224)]),
        "m_64": _group_matrix(256, [(0, 64), (64, 128), (128, 192), (192, 256)]),
        "w_uk_t": w_uk_t.astype(BF16), "w_uv": w_uv.astype(BF16), "g_k_nope": g_mla_k_nope,
        "g_out": row(g_out), "w_out": w_out.astype(BF16),
        "w_gate": w_ffn_in[:, :D_FF].astype(BF16), "w_up": w_ffn_in[:, D_FF:].astype(BF16),
        "w_down": w_ffn_out.astype(BF16),
    }


def _rope_tables(pos):
    freq = ROPE_THETA ** (-jnp.arange(HALF_ROPE, dtype=F32) / HALF_ROPE)
    ang = pos.astype(F32)[:, None] * freq[None, :]
    cos, sin = jnp.cos(ang), jnp.sin(ang)
    n = pos.shape[0]
    ones = jnp.ones((n, T_ROPE), F32)
    zeros = jnp.zeros((n, T_ROPE), F32)
    tail1 = jnp.ones((n, LANES - T_ROPE - MLA_ROPE), F32)
    tail0 = jnp.zeros((n, LANES - T_ROPE - MLA_ROPE), F32)
    return (jnp.concatenate([ones, cos, cos, tail1], axis=1),
            jnp.concatenate([zeros, -sin, sin, tail0], axis=1))


def _pick_tile(n, prefs):
    for t in prefs:
        if n % t == 0:
            return t
    return n


def kernel(x_prompt, x_sample, c_prompt, c_sample, cache_mla, cache_kv, cache_idx, page_table, rel_bias, w_ada, b_ada, g_norm_mix, g_norm_ffn, w_in, g_q_lat, w_uq, g_kv_lat, w_ukv, g_mla_q_nope, g_mla_q_rope, g_mla_k_nope, g_mla_k_rope, g_dsa_q, g_dsa_k, g_out, w_out, w_ffn_in, w_ffn_out):
    assert w_ada.shape[0] == 1 and x_sample.shape[1] == 1, "one layer, one new token per sample"
    batch, seq, _ = x_prompt.shape
    dec = x_sample.shape[0]
    n_pages = page_table.shape[1]
    past = n_pages * PAGE_SIZE
    wp = _prep_weights(w_in[0], g_norm_mix[0], g_norm_ffn[0], g_q_lat[0], w_uq[0], g_kv_lat[0], w_ukv[0],
                       g_mla_q_nope[0], g_mla_q_rope[0], g_mla_k_nope[0], g_mla_k_rope[0], g_dsa_q[0], g_dsa_k[0],
                       g_out[0], w_out[0], w_ffn_in[0], w_ffn_out[0])

    mod = _ada(jnp.concatenate([c_prompt, c_sample], axis=0), w_ada[0], b_ada[0])
    mods_p = [m.reshape(batch, 1, D_MODEL) for m in jnp.split(mod[:batch], 6, axis=-1)]
    mods_s = jnp.split(mod[batch:], 6, axis=-1)

    tm = _pick_tile(seq, (512, 256, 128))
    xp2 = x_prompt.reshape(batch * seq, D_MODEL)
    cos_p, sin_p = _rope_tables(jnp.arange(seq))
    pin = _in_stage(xp2, mods_p[1], mods_p[0], cos_p, sin_p, wp, tm, per_token=False)
    o_mla_p = _mla_prompt(pin["qcat"], pin["kcat"], pin["vmla"], batch, seq,
                          _pick_tile(seq, (256, 128)), _pick_tile(seq, (512, 256, 128)))
    o_dsa_p = _dsa_prompt(rel_bias, pin["qidx"], pin["tail"], pin["kidxdup"], pin["qdsa"], pin["kdup"], pin["vone"],
                          batch, seq, _pick_tile(seq, (256, 128)))
    y_p = _out_stage(xp2, o_mla_p, o_dsa_p, mods_p[2], mods_p[3], mods_p[4], mods_p[5], wp, tm,
                     per_token=False, tiles_per_seq=seq // tm)

    xs2 = x_sample.reshape(dec, D_MODEL)
    cos_s, sin_s = _rope_tables(jnp.full((1,), past))
    sin_ = _in_stage(xs2, mods_s[1], mods_s[0], cos_s, sin_s, wp, dec, per_token=True)
    pps_wide = _pick_tile(n_pages, (64, 32, 16, 8, 4, 2))
    qidx3 = sin_["qidx"].reshape(dec, IDX_HEADS, IDX_DIM)
    widx3 = sin_["tail"][:, T_WIDX:T_WIDX + IDX_HEADS].reshape(dec, IDX_HEADS, 1)
    scores = _sample_scores(page_table, qidx3, widx3, _pages_feature_major(cache_idx), pps_wide)
    selp, seln = _sample_select(scores, sin_["qidx"], sin_["tail"])
    qc = sin_["qcat"].reshape(dec, MLA_HEADS, LANES)
    qn_row = qc[:, :, 0:MLA_NOPE].astype(F32).reshape(dec, 1, MLA_HEADS * MLA_NOPE)
    qr = qc[:, :, T_ROPE:T_ROPE + MLA_ROPE]
    gk_row = jnp.tile(wp["g_k_nope"], MLA_HEADS).reshape(1, MLA_HEADS * MLA_NOPE)
    o_mla_s = _mla_sample(page_table, qn_row, qr, sin_["row"].reshape(dec, MLA_ROW, 1), wp["w_uk_t"], wp["w_uv"],
                          gk_row, _pages_feature_major(cache_mla), pps_wide).reshape(dec, MLA_HEADS * MLA_V)
    qd = sin_["qdsa"].astype(F32).reshape(dec, DSA_HEADS, DSA_HEAD_DIM)
    on_c = (jnp.arange(DSA_KV_HEADS)[None, :] == (jnp.arange(DSA_HEADS) // DSA_GROUP)[:, None]).astype(F32)
    qd2 = (qd[:, :, None, :] * on_c[None, :, :, None]).reshape(dec, DSA_HEADS, LANES)
    bias_past = _bias_table(rel_bias, 1, past, 0, -1, past).reshape(DSA_HEADS, past)
    bias_new = _bias_table(rel_bias, 1, LANES, 0, 0, 0)[:, 0, 0:1]
    o8 = _dsa_sample(page_table, qd2, sin_["kv"].reshape(dec, 1, 256), selp.reshape(dec, 1, past),
                     seln.reshape(dec, 1, LANES), bias_past, bias_new, _pages_feature_major(cache_kv), pps_wide)
    o_dsa_s = o8.reshape(dec, DSA_HEADS, DSA_KV_HEADS, DSA_HEAD_DIM).sum(axis=2).reshape(dec, DSA_HEADS * DSA_HEAD_DIM)
    y_s = _out_stage(xs2, o_mla_s, o_dsa_s, mods_s[2], mods_s[3], mods_s[4], mods_s[5], wp, dec,
                     per_token=True, tiles_per_seq=1)

    return (y_p.reshape(batch, seq, D_MODEL), y_s.reshape(dec, 1, D_MODEL),
            jnp.swapaxes(pin["row"], 1, 2)[None],
            jnp.moveaxis(pin["kv"].reshape(batch, 2, DSA_KV_HEADS, DSA_HEAD_DIM, seq), -1, 1)[None],
            jnp.swapaxes(pin["kidx"], 1, 2)[None],
            sin_["row"].reshape(1, dec, 1, MLA_ROW),
            sin_["kv"].reshape(1, dec, 1, 2, DSA_KV_HEADS, DSA_HEAD_DIM),
            sin_["kidx"].reshape(1, dec, 1, IDX_DIM))
```

```python
import functools
import math

import numpy as np
import jax
import jax.numpy as jnp
from jax import lax
from jax.experimental import pallas as pl
from jax.experimental.pallas import tpu as pltpu

D_MODEL = 1024
PAGE_SIZE = 128
MLA_HEADS = 8
MLA_NOPE = 64
MLA_ROPE = 32
MLA_V = 64
MLA_Q_RANK = 384
MLA_KV_RANK = 256
MLA_ROW = MLA_KV_RANK + MLA_ROPE
DSA_HEADS = 8
DSA_KV_HEADS = 2
DSA_HEAD_DIM = 64
DSA_GROUP = DSA_HEADS // DSA_KV_HEADS
IDX_HEADS = 8
IDX_DIM = 64
TOPK_MAX = 256
N_BUCKETS = 32
MAX_DISTANCE = 128
ROPE_THETA = 10000.0
D_FF = ((8 * D_MODEL + 3 * 256 - 1) // (3 * 256)) * 256
EPS = 1e-6
MLA_SCALE = (MLA_NOPE + MLA_ROPE) ** -0.5
DSA_SCALE = DSA_HEAD_DIM ** -0.5
IDX_SCALE = (IDX_DIM * IDX_HEADS) ** -0.5

LANES = 128
VMEM_LIMIT = 56 * 1024 * 1024

F32 = jnp.float32
BF16 = jnp.bfloat16
I32 = jnp.int32
NEG_BIG = -1e30
INT_MIN = -(2 ** 31)

C_QLAT = 0
C_KVLAT = C_QLAT + MLA_Q_RANK
C_Q = C_KVLAT + MLA_KV_RANK
C_K = C_Q + DSA_HEADS * DSA_HEAD_DIM
C_V = C_K + DSA_KV_HEADS * DSA_HEAD_DIM
C_QIDX = C_V + DSA_KV_HEADS * DSA_HEAD_DIM
C_TAIL = C_QIDX + IDX_HEADS * IDX_DIM
C_END = C_TAIL + LANES
T_ROPE = IDX_DIM
T_WIDX = IDX_DIM + MLA_ROPE
HALF_ROPE = MLA_ROPE // 2
KB = 128


def _dot(a, b):
    return jnp.dot(a, b, preferred_element_type=F32)


def _dot_nt(a, b):
    return lax.dot_general(a, b, (((1,), (1,)), ((), ())), preferred_element_type=F32)


def _split(a):
    hi = a.astype(BF16)
    lo = (a - hi.astype(F32)).astype(BF16)
    return hi, lo


def _dot3(a, b):
    ah, al = _split(a)
    bh, bl = _split(b)
    return _dot(ah, bh) + (_dot(al, bh) + _dot(ah, bl))


def _cparams(sem):
    return pltpu.CompilerParams(dimension_semantics=sem, vmem_limit_bytes=VMEM_LIMIT)


def _full(shape):
    n = len(shape)
    return pl.BlockSpec(shape, lambda *a, _n=n: (0,) * _n, pipeline_mode=pl.Buffered(1))


def _ada_kernel(c_ref, w_ref, b_ref, o_ref):
    c = c_ref[...]
    s = c / (1.0 + jnp.exp(-c))
    o_ref[...] = _dot3(s, w_ref[...]) + b_ref[...]


def _ada(c, w, b):
    n = c.shape[0]
    return pl.pallas_call(
        _ada_kernel,
        grid=(6,),
        in_specs=[pl.BlockSpec((n, D_MODEL), lambda j: (0, 0)),
                  pl.BlockSpec((D_MODEL, D_MODEL), lambda j: (0, j)),
                  pl.BlockSpec((1, D_MODEL), lambda j: (0, j))],
        out_specs=pl.BlockSpec((n, D_MODEL), lambda j: (0, j)),
        out_shape=jax.ShapeDtypeStruct((n, 6 * D_MODEL), F32),
        compiler_params=_cparams(("arbitrary",)),
    )(c, w, b.reshape(1, -1))


def _group_matrix(width, groups):
    g = np.zeros((width, width), np.float32)
    for lo, hi in groups:
        g[lo:hi, lo:hi] = 1.0 / (hi - lo)
    return jnp.asarray(g, BF16)


def _in_kernel(x_ref, sc_ref, sh_ref, cos_ref, sin_ref, gmix_ref, win_ref, gqlat_ref, wuq_ref, gqcat_ref,
               gkvlat_ref, wukv_ref, gkcat_ref, gtail_ref, gq_ref, gk_ref, mcat_ref, m64_ref,
               qcat_o, kcat_o, vmla_o, row_o, kv_o, kidx_o, tail_o, kidxdup_o, qdsa_o, kdup_o, vone_o, qidx_o,
               *, rows_feature_major):
    tm = x_ref.shape[0]
    x = x_ref[...]
    h = x * lax.rsqrt(jnp.mean(x * x, axis=-1, keepdims=True) + EPS) * gmix_ref[...]
    hb = (h * (1.0 + sc_ref[...]) + sh_ref[...]).astype(BF16)

    def proj(lo, hi):
        return _dot(hb, win_ref[:, lo:hi])

    cos = cos_ref[...]
    sin = sin_ref[...]
    lane = lax.broadcasted_iota(I32, (tm, LANES), 1)
    first_half = lane < T_ROPE + HALF_ROPE
    rope_lanes = (lane >= T_ROPE) & (lane < T_ROPE + MLA_ROPE)
    low64 = lane < 64

    def rope(xh):
        rot = jnp.where(first_half, pltpu.roll(xh, LANES - HALF_ROPE, 1), pltpu.roll(xh, HALF_ROPE, 1))
        return xh * cos + rot * sin

    def gnorm(blk, m_ref):
        ms = _dot((blk * blk).astype(BF16), m_ref[...])
        return blk * lax.rsqrt(ms + EPS)

    ql = proj(C_QLAT, C_KVLAT)
    qln = ql * lax.rsqrt(jnp.mean(ql * ql, axis=-1, keepdims=True) + EPS) * gqlat_ref[...]
    qm = _dot(qln.astype(BF16), wuq_ref[...])
    for j in range(4):
        y = gnorm(qm[:, 256 * j:256 * j + 256], mcat_ref) * gqcat_ref[:, 256 * j:256 * j + 256]
        for t in range(2):
            qcat_o[:, 256 * j + 128 * t:256 * j + 128 * t + 128] = rope(y[:, 128 * t:128 * t + 128]).astype(BF16)

    tail = proj(C_TAIL, C_END)
    ssq = jnp.sum(jnp.where(rope_lanes, tail * tail, 0.0), axis=-1, keepdims=True) * (1.0 / MLA_ROPE)
    tn = jnp.where(rope_lanes, tail * lax.rsqrt(ssq + EPS) * gtail_ref[...], tail)
    tr = rope(tn)
    if rows_feature_major:
        kidx_o[...] = tail.T[0:IDX_DIM, :]
    else:
        kidx_o[...] = tail[:, 0:IDX_DIM]
    tail_o[...] = tail
    kidxdup_o[...] = jnp.where(low64, tail, pltpu.roll(tail, 64, 1)).astype(BF16)
    krope = jnp.where(rope_lanes, tr, 0.0)

    kvl = proj(C_KVLAT, C_Q)
    lat = kvl * lax.rsqrt(jnp.mean(kvl * kvl, axis=-1, keepdims=True) + EPS) * gkvlat_ref[...]
    if rows_feature_major:
        row_o[0:MLA_KV_RANK, :] = lat.T
        row_o[MLA_KV_RANK:MLA_ROW, :] = tr.T[T_ROPE:T_ROPE + MLA_ROPE, :]
    else:
        row_o[:, 0:MLA_KV_RANK] = lat
        row_o[:, MLA_KV_RANK:MLA_ROW] = tr[:, T_ROPE:T_ROPE + MLA_ROPE]
    latb = lat.astype(BF16)
    kx = _dot(latb, wukv_ref[:, 0:1024])
    for j in range(4):
        y = gnorm(kx[:, 256 * j:256 * j + 256], mcat_ref) * gkcat_ref[:, 256 * j:256 * j + 256]
        for t in range(2):
            kcat_o[:, 256 * j + 128 * t:256 * j + 128 * t + 128] = (y[:, 128 * t:128 * t + 128] + krope).astype(BF16)
    ones_half = (lax.broadcasted_iota(I32, (tm, 1024), 1) % LANES) >= MLA_V
    vmla_o[...] = jnp.where(ones_half, 1.0, _dot(latb, wukv_ref[:, 1024:2048])).astype(BF16)

    q = proj(C_Q, C_K)
    for j in range(2):
        qdsa_o[:, 256 * j:256 * j + 256] = (gnorm(q[:, 256 * j:256 * j + 256], m64_ref)
                                            * gq_ref[:, 256 * j:256 * j + 256] * DSA_SCALE).astype(BF16)
    k = proj(C_K, C_V)
    ms = _dot((k * k).astype(BF16), m64_ref[0:128, 0:128])
    kn = k * lax.rsqrt(ms + EPS) * gk_ref[...]
    v = proj(C_V, C_QIDX)
    if rows_feature_major:
        kv_o[0:128, :] = kn.T
        kv_o[128:256, :] = v.T
    else:
        kv_o[:, 0:128] = kn
        kv_o[:, 128:256] = v
    kr = pltpu.roll(kn, 64, 1)
    kdup_o[:, 0:128] = jnp.where(low64, kn, kr).astype(BF16)
    kdup_o[:, 128:256] = jnp.where(low64, kr, kn).astype(BF16)
    vone_o[:, 0:128] = jnp.where(low64, v, 1.0).astype(BF16)
    vone_o[:, 128:256] = jnp.where(low64, pltpu.roll(v, 64, 1), 1.0).astype(BF16)
    qidx_o[...] = proj(C_QIDX, C_TAIL).astype(BF16)


_IN_OUT_WIDTHS = (("qcat", 1024, BF16), ("kcat", 1024, BF16), ("vmla", 1024, BF16), ("row", MLA_ROW, F32),
                  ("kv", 256, F32), ("kidx", IDX_DIM, F32), ("tail", LANES, F32), ("kidxdup", LANES, BF16),
                  ("qdsa", 512, BF16), ("kdup", 256, BF16), ("vone", 256, BF16), ("qidx", 512, BF16))


def _in_stage(x2d, scale, shift, cos, sin, wp, tm, per_token):
    t_total = x2d.shape[0]
    n_tiles = t_total // tm
    if per_token:
        mod_spec = pl.BlockSpec((tm, D_MODEL), lambda i: (i, 0))
        tab_spec = pl.BlockSpec((1, LANES), lambda i: (0, 0))
    else:
        tiles_per_seq = cos.shape[0] // tm
        mod_spec = pl.BlockSpec((None, 1, D_MODEL), lambda i: (i // tiles_per_seq, 0, 0))
        tab_spec = pl.BlockSpec((tm, LANES), lambda i: (i % tiles_per_seq, 0))
    consts = [wp["g_mix"], wp["w_in"], wp["g_qlat"], wp["w_uq"], wp["g_qcat"], wp["g_kvlat"], wp["w_ukv"],
              wp["g_kcat"], wp["g_tail"], wp["g_q"], wp["g_k"], wp["m_cat"], wp["m_64"]]
    in_specs = ([pl.BlockSpec((tm, D_MODEL), lambda i: (i, 0)), mod_spec, mod_spec, tab_spec, tab_spec]
                + [_full(c.shape) for c in consts])
    out_specs = [pl.BlockSpec((tm, w), lambda i: (i, 0)) for _, w, _ in _IN_OUT_WIDTHS]
    out_shape = [jax.ShapeDtypeStruct((t_total, w), dt) for _, w, dt in _IN_OUT_WIDTHS]
    if not per_token:
        for j, (name, w, dt) in enumerate(_IN_OUT_WIDTHS):
            if name in ("row", "kv", "kidx"):
                out_specs[j] = pl.BlockSpec((None, w, tm), lambda i: (i // tiles_per_seq, 0, i % tiles_per_seq))
                out_shape[j] = jax.ShapeDtypeStruct((n_tiles // tiles_per_seq, w, tiles_per_seq * tm), dt)
    outs = pl.pallas_call(
        functools.partial(_in_kernel, rows_feature_major=not per_token),
        grid=(n_tiles,), in_specs=in_specs, out_specs=out_specs, out_shape=out_shape,
        compiler_params=_cparams(("parallel",)),
    )(x2d, scale, shift, cos, sin, *consts)
    return {name: o for (name, _, _), o in zip(_IN_OUT_WIDTHS, outs)}


def _mla_prompt_kernel(q_ref, k_ref, v_ref, o_ref, m_scr, acc_scr, s_scr, *, tk):
    tq = q_ref.shape[0]
    qi = pl.program_id(1)
    n_chunks = (qi * tq + tq + tk - 1) // tk
    m_scr[...] = jnp.full(m_scr.shape, -jnp.inf, F32)
    acc_scr[...] = jnp.zeros(acc_scr.shape, F32)
    low64 = lax.broadcasted_iota(I32, (tq, LANES), 1) < 64
    qpos = qi * tq + lax.broadcasted_iota(I32, (tq, tk), 0)
    kcol = lax.broadcasted_iota(I32, (tq, tk), 1)
    exp_scale = MLA_SCALE * math.log2(math.e)

    def raw_scores(c, h, masked):
        off = pl.multiple_of(c * tk, tk)
        s = _dot_nt(q_ref[:, h * 128:(h + 1) * 128], k_ref[pl.ds(off, tk), h * 128:(h + 1) * 128])
        if masked:
            s = jnp.where(c * tk + kcol <= qpos, s, -jnp.inf)
        return s

    def pass_max(c, masked):
        for h in range(MLA_HEADS):
            s = raw_scores(c, h, masked)
            s_scr[c, h] = s
            m = m_scr[h]
            for j in range(tk // LANES):
                m = jnp.maximum(m, s[:, j * LANES:(j + 1) * LANES])
            m_scr[h] = m

    def pass_sum(c, masked):
        off = pl.multiple_of(c * tk, tk)
        for h in range(MLA_HEADS):
            s = s_scr[c, h]
            m = m_scr[h]
            p = jnp.concatenate([jnp.exp2((s[:, j * LANES:(j + 1) * LANES] - m) * exp_scale)
                                 for j in range(tk // LANES)], axis=1).astype(BF16)
            acc_scr[h] += _dot(p, v_ref[pl.ds(off, tk), h * 128:(h + 1) * 128])

    def run(step):
        def body(c, carry):
            step(c, False)
            return carry

        lax.fori_loop(0, n_chunks - 1, body, 0)
        step(n_chunks - 1, True)

    run(pass_max)
    for h in range(MLA_HEADS):
        m_scr[h] = jnp.broadcast_to(jnp.max(m_scr[h], axis=-1, keepdims=True), (tq, LANES))
    run(pass_sum)
    for p in range(MLA_HEADS // 2):
        even, odd = acc_scr[2 * p], acc_scr[2 * p + 1]
        o_ref[:, 128 * p:128 * p + 128] = jnp.where(low64, even / pltpu.roll(even, 64, 1),
                                                    pltpu.roll(odd, 64, 1) / odd)


def _mla_prompt(qcat, kcat, vmla, batch, seq, tq, tk):
    nq = seq // tq
    return pl.pallas_call(
        functools.partial(_mla_prompt_kernel, tk=tk),
        grid=(batch, nq),
        in_specs=[pl.BlockSpec((tq, 1024), lambda b, i: (b * nq + i, 0)),
                  pl.BlockSpec((seq, 1024), lambda b, i: (b, 0)),
                  pl.BlockSpec((seq, 1024), lambda b, i: (b, 0))],
        out_specs=pl.BlockSpec((tq, 512), lambda b, i: (b * nq + i, 0)),
        out_shape=jax.ShapeDtypeStruct((batch * seq, 512), F32),
        scratch_shapes=[pltpu.VMEM((MLA_HEADS, tq, LANES), F32), pltpu.VMEM((MLA_HEADS, tq, LANES), F32),
                        pltpu.VMEM((seq // tk, MLA_HEADS, tq, tk), F32)],
        compiler_params=_cparams(("parallel", "arbitrary")),
    )(qcat, kcat, vmla)


def _sort_key(score):
    bits = lax.bitcast_convert_type(score + 0.0, I32)
    return bits ^ ((bits >> 31) & 0x7FFFFFFF)


def _kth_key(count_ge, n_sel, shape):
    def body(i, t):
        cand = t + (jnp.int32(1) << (31 - i))
        return jnp.where(count_ge(cand) >= n_sel, cand, t)

    return lax.fori_loop(0, 32, body, jnp.full(shape, INT_MIN, I32))


def _bucket(dist):
    max_exact = N_BUCKETS // 2
    d = jnp.maximum(dist, 0)
    log_ratio = jnp.log(jnp.maximum(d, max_exact).astype(F32) / max_exact) / math.log(MAX_DISTANCE / max_exact)
    large = jnp.minimum(max_exact + (log_ratio * (N_BUCKETS - max_exact)).astype(I32), N_BUCKETS - 1)
    return jnp.where(d < max_exact, d, large)


def _bias_kernel(rb_ref, o_ref, *, a, b, c):
    rows, cols = o_ref.shape[1], o_ref.shape[2]
    dist = (a * lax.broadcasted_iota(I32, (rows, cols), 0) + b * lax.broadcasted_iota(I32, (rows, cols), 1) + c)
    bucket = _bucket(dist)
    for h in range(DSA_HEADS):
        acc = jnp.zeros((rows, cols), F32)
        for n in range(N_BUCKETS):
            acc = jnp.where(bucket == n, rb_ref[n, h], acc)
        o_ref[h] = acc


def _bias_table(rel_bias, rows, cols, a, b, c):
    return pl.pallas_call(
        functools.partial(_bias_kernel, a=a, b=b, c=c),
        in_specs=[pl.BlockSpec(memory_space=pltpu.SMEM)],
        out_specs=pl.BlockSpec(memory_space=pltpu.VMEM),
        out_shape=jax.ShapeDtypeStruct((DSA_HEADS, rows, cols), F32),
    )(rel_bias)


def _loop2(n, body):
    n = jnp.maximum(n, 0)

    def pair(j, carry):
        body(2 * j)
        body(2 * j + 1)
        return carry

    lax.fori_loop(0, n // 2, pair, 0)

    @pl.when(n % 2 == 1)
    def _():
        body(n - 1)


def _colreduce(x, op):
    return op(x.reshape(x.shape[0] // 8, 8, x.shape[1]), axis=0)


def _dsa_prompt_kernel(rb_ref, qidx_ref, tail_ref, kidx_ref, q_ref, k_ref, v_ref, bias_ref, o_ref,
                       qs_scr, qd_scr, wb_scr, key_scr, sbq_scr, tie_scr, m_scr, p_scr, acc_scr, s_scr, *, n_sel, idx_bits):
    tq = q_ref.shape[0]
    kq = tq // KB
    qi = pl.program_id(1)
    krow = lax.broadcasted_iota(I32, (KB, tq), 0)
    qcol = lax.broadcasted_iota(I32, (KB, tq), 1)
    low64 = lax.broadcasted_iota(I32, (tq, LANES), 1) < 64

    tail = tail_ref[...]
    for h in range(IDX_HEADS):
        mine = low64 if h % 2 == 0 else jnp.logical_not(low64)
        zero = jnp.zeros((tq, LANES), BF16)
        qs_scr[h * tq:(h + 1) * tq, :] = jnp.where(mine, qidx_ref[:, (h // 2) * 128:(h // 2 + 1) * 128], zero)
        qd_scr[h * tq:(h + 1) * tq, :] = jnp.where(mine, q_ref[:, (h // 2) * 128:(h // 2 + 1) * 128], zero)
        wb_scr[h] = jnp.broadcast_to(tail[:, T_WIDX + h:T_WIDX + h + 1], (tq, LANES))

    def rows_of(c):
        return pl.ds(pl.multiple_of(c * tq, tq), tq)

    def score_chunk(c, in_tile):
        d = _dot_nt(qs_scr[...], kidx_ref[rows_of(c), :])
        for t in range(kq):
            sc = jnp.zeros((tq, KB), F32)
            for h in range(IDX_HEADS):
                sc = sc + jnp.maximum(d[h * tq:(h + 1) * tq, t * KB:(t + 1) * KB], 0.0) * wb_scr[h]
            sc_t = (sc * IDX_SCALE).T
            if in_tile:
                sc_t = jnp.where(t * KB + krow <= qcol, sc_t, -jnp.inf)
            key_scr[c * kq + t] = _sort_key(sc_t)

    _loop2(qi, lambda c: score_chunk(c, False))
    score_chunk(qi, True)
    n_chunks = qi + 1

    def count(pred):
        part = 32
        def body(c, acc):
            for t in range(kq):
                kb = c * kq + t
                hit = jnp.where(pred(key_scr[kb], kb), 1.0, 0.0)
                acc = acc + jnp.sum(hit.reshape(KB // part, part, tq), axis=0)
            return acc
        acc = lax.fori_loop(0, n_chunks, body, jnp.zeros((part, tq), F32))
        return jnp.sum(_colreduce(acc, jnp.sum), axis=0, keepdims=True)

    thr = _kth_key(lambda cand: count(lambda k, kb: k >= cand), float(n_sel), (1, tq))
    cnt_gt = count(lambda k, kb: k > thr)
    cnt_ge = count(lambda k, kb: k >= thr)
    tie_scr[...] = jnp.full((1, tq), 2 ** 30, I32)
    excess = jnp.where((cnt_ge > float(n_sel)) & (thr > INT_MIN), 1.0, 0.0)

    @pl.when(jnp.max(excess) > 0.0)
    def _():
        need = float(n_sel) - cnt_gt

        def body(i, p):
            cand = p + (jnp.int32(1) << (idx_bits - 1 - i))
            c = count(lambda k, kb: (k == thr) & (kb * KB + krow < cand))
            return jnp.where(c < need, cand, p)

        tie_scr[...] = lax.fori_loop(0, idx_bits, body, jnp.zeros((1, tq), I32))

    last_tie = tie_scr[...]

    def mask_chunk(c, in_tile):
        for t in range(kq):
            kb = c * kq + t
            key = key_scr[kb]
            sel = (key > thr) | ((key == thr) & (kb * KB + krow <= last_tie))
            if in_tile:
                sel = sel & (t * KB + krow <= qcol)
            sbq_scr[c, :, t * KB:(t + 1) * KB] = jnp.where(sel, 0.0, NEG_BIG).T

    _loop2(qi, lambda c: mask_chunk(c, False))
    mask_chunk(qi, True)

    far_bias = [rb_ref[N_BUCKETS - 1, h] for h in range(DSA_HEADS)]

    def scores(c, g, kind):
        sg = _dot_nt(qd_scr[g * DSA_GROUP * tq:(g + 1) * DSA_GROUP * tq, :], k_ref[rows_of(c), g * 128:(g + 1) * 128])
        sb = sbq_scr[c]
        out = []
        for hh in range(DSA_GROUP):
            h = g * DSA_GROUP + hh
            cols = []
            for t in range(kq):
                s = sg[hh * tq:(hh + 1) * tq, t * KB:(t + 1) * KB] + sb[:, t * KB:(t + 1) * KB]
                if kind == "tile":
                    s = s + (bias_ref[1 + t, h] - far_bias[h])
                elif kind == "before" and t == kq - 1:
                    s = s + (bias_ref[0, h] - far_bias[h])
                cols.append(s)
            out.append(cols)
        return out

    m_scr[...] = jnp.full(m_scr.shape, NEG_BIG, F32)
    acc_scr[...] = jnp.zeros(acc_scr.shape, F32)

    def pass_max(c, kind):
        for g in range(DSA_KV_HEADS):
            for hh, cols in enumerate(scores(c, g, kind)):
                h = g * DSA_GROUP + hh
                m = m_scr[h]
                for t, s in enumerate(cols):
                    s_scr[c, h, :, t * KB:(t + 1) * KB] = s
                    m = jnp.maximum(m, s)
                m_scr[h] = m

    def pass_sum(c, kind):
        for g in range(DSA_KV_HEADS):
            for hh in range(DSA_GROUP):
                h = g * DSA_GROUP + hh
                m = m_scr[h]
                for t in range(kq):
                    s = s_scr[c, h, :, t * KB:(t + 1) * KB]
                    p_scr[g, hh * tq:(hh + 1) * tq, t * KB:(t + 1) * KB] = jnp.exp(s - m).astype(BF16)
            acc_scr[g] += _dot(p_scr[g], v_ref[rows_of(c), g * 128:(g + 1) * 128])

    def run(step):
        _loop2(qi - 1, lambda c: step(c, "far"))

        @pl.when(qi >= 1)
        def _():
            step(qi - 1, "before")

        step(qi, "tile")

    run(pass_max)
    for h in range(DSA_HEADS):
        m_scr[h] = jnp.broadcast_to(jnp.max(m_scr[h], axis=-1, keepdims=True), (tq, LANES))
    run(pass_sum)

    for p in range(DSA_HEADS // 2):
        g, hh = (2 * p) // DSA_GROUP, (2 * p) % DSA_GROUP
        even = acc_scr[g, hh * tq:(hh + 1) * tq, :]
        odd = acc_scr[g, (hh + 1) * tq:(hh + 2) * tq, :]
        o_ref[:, 128 * p:128 * p + 128] = jnp.where(low64, even / pltpu.roll(even, 64, 1),
                                                    pltpu.roll(odd, 64, 1) / odd)


def _dsa_prompt(rel_bias, qidx, tail, kidxdup, qdsa, kdup, vone, batch, seq, tq):
    assert MAX_DISTANCE <= KB, "keys two or more blocks back must all fall in the last bucket"
    nq = seq // tq
    nkb = seq // KB
    kq = tq // KB
    n_sel = min(TOPK_MAX, seq // 4)
    idx_bits = max(1, (seq - 1).bit_length())
    bias = jnp.stack([_bias_table(rel_bias, tq, KB, 1, -1, KB - t * KB) for t in range(kq + 1)])
    kern = functools.partial(_dsa_prompt_kernel, n_sel=n_sel, idx_bits=idx_bits)
    qblk = lambda w: pl.BlockSpec((tq, w), lambda b, i: (b * nq + i, 0))
    seqblk = lambda w: pl.BlockSpec((seq, w), lambda b, i: (b, 0))
    return pl.pallas_call(
        kern,
        grid=(batch, nq),
        in_specs=[pl.BlockSpec(memory_space=pltpu.SMEM),
                  qblk(512), qblk(LANES), seqblk(LANES), qblk(512), seqblk(256), seqblk(256), _full(bias.shape)],
        out_specs=qblk(512),
        out_shape=jax.ShapeDtypeStruct((batch * seq, 512), F32),
        scratch_shapes=[pltpu.VMEM((IDX_HEADS * tq, LANES), BF16), pltpu.VMEM((DSA_HEADS * tq, LANES), BF16),
                        pltpu.VMEM((IDX_HEADS, tq, LANES), F32),
                        pltpu.VMEM((nkb, KB, tq), I32), pltpu.VMEM((nq, tq, tq), F32),
                        pltpu.VMEM((1, tq), I32), pltpu.VMEM((DSA_HEADS, tq, LANES), F32),
                        pltpu.VMEM((DSA_KV_HEADS, DSA_GROUP * tq, tq), BF16),
                        pltpu.VMEM((DSA_KV_HEADS, DSA_GROUP * tq, LANES), F32),
                        pltpu.VMEM((nq, DSA_HEADS, tq, tq), F32)],
        compiler_params=_cparams(("parallel", "arbitrary")),
    )(rel_bias, qidx, tail, kidxdup, qdsa, kdup, vone, bias)


def _fetch_pages(cache_ref, pt_ref, buf, sem, step, n_steps, locate, pps):
    def copy(slot, i, page):
        return pltpu.make_async_copy(cache_ref.at[0, page], buf.at[slot, i], sem.at[slot])

    def start(st, slot):
        b, first = locate(st)
        for i in range(pps):
            copy(slot, i, pt_ref[b, first + i]).start()

    @pl.when(step == 0)
    def _():
        start(step, 0)

    @pl.when(step + 1 < n_steps)
    def _():
        start(step + 1, (step + 1) % 2)

    slot = step % 2
    for i in range(pps):
        copy(slot, i, 0).wait()
    return slot


def _page_scratch(pps, width):
    return [pltpu.VMEM((2, pps, width, PAGE_SIZE), F32), pltpu.SemaphoreType.DMA((2,))]


def _pages_feature_major(cache):
    c = jnp.moveaxis(cache, 2, -1)
    return c.reshape(cache.shape[0], cache.shape[1], -1, PAGE_SIZE)


def _sample_score_kernel(pt_ref, q_ref, w_ref, cache_ref, o_ref, buf, sem, kb_scr, *, pps, grp):
    bo, c, bi = pl.program_id(0), pl.program_id(1), pl.program_id(2)
    n_chunks = pl.num_programs(1)
    step = (bo * n_chunks + c) * grp + bi

    def locate(st):
        return (st // (n_chunks * grp)) * grp + st % grp, ((st // grp) % n_chunks) * pps

    slot = _fetch_pages(cache_ref, pt_ref, buf, sem, step, pl.num_programs(0) * n_chunks * grp, locate, pps)
    for i in range(pps):
        kb_scr[:, i * PAGE_SIZE:(i + 1) * PAGE_SIZE] = buf[slot, i].astype(BF16)
    d = _dot(q_ref[...], kb_scr[...])
    sc = jnp.sum(jnp.maximum(d, 0.0) * w_ref[...], axis=0, keepdims=True) * IDX_SCALE
    o_ref[pl.ds(bi, 1), :] = sc


def _sample_scores(page_table, qidx3, widx3, cache_idx_t, pps):
    dec, n_pages = page_table.shape
    n_chunks = n_pages // pps
    grp = 8
    grid_spec = pltpu.PrefetchScalarGridSpec(
        num_scalar_prefetch=1,
        grid=(dec // grp, n_chunks, grp),
        in_specs=[pl.BlockSpec((None, IDX_HEADS, IDX_DIM), lambda bo, c, bi, pt: (bo * grp + bi, 0, 0)),
                  pl.BlockSpec((None, IDX_HEADS, 1), lambda bo, c, bi, pt: (bo * grp + bi, 0, 0)),
                  pl.BlockSpec(memory_space=pl.ANY)],
        out_specs=pl.BlockSpec((grp, pps * PAGE_SIZE), lambda bo, c, bi, pt: (bo, c)),
        scratch_shapes=_page_scratch(pps, IDX_DIM) + [pltpu.VMEM((IDX_DIM, pps * PAGE_SIZE), BF16)],
    )
    return pl.pallas_call(
        functools.partial(_sample_score_kernel, pps=pps, grp=grp), grid_spec=grid_spec,
        out_shape=jax.ShapeDtypeStruct((dec, n_pages * PAGE_SIZE), F32),
        compiler_params=_cparams(("arbitrary", "arbitrary", "arbitrary")),
    )(page_table, qidx3, widx3, cache_idx_t)


def _sample_select_kernel(sc_ref, qidx_ref, tail_ref, selp_ref, seln_ref, key_scr, *, n_sel, idx_bits):
    rows, past = sc_ref.shape
    nblk = past // LANES
    lane = lax.broadcasted_iota(I32, (rows, LANES), 1)
    tail = tail_ref[...]
    kidx2 = jnp.where(lane < 64, tail, pltpu.roll(tail, 64, 1))
    sc_new = jnp.zeros((rows, 1), F32)
    for h in range(IDX_HEADS):
        prod = qidx_ref[:, (h // 2) * 128:(h // 2 + 1) * 128].astype(F32) * kidx2.astype(BF16).astype(F32)
        mine = (lane < 64) if h % 2 == 0 else (lane >= 64)
        dot = jnp.sum(jnp.where(mine, prod, 0.0), axis=-1, keepdims=True)
        sc_new = sc_new + jnp.maximum(dot, 0.0) * tail[:, T_WIDX + h:T_WIDX + h + 1]
    key_new = _sort_key(sc_new * IDX_SCALE)

    for j in range(nblk):
        key_scr[j] = _sort_key(sc_ref[:, j * LANES:(j + 1) * LANES])

    def count(pred, pred_new):
        def body(j, c):
            return c + jnp.where(pred(key_scr[j], j), 1.0, 0.0)
        c = lax.fori_loop(0, nblk, body, jnp.zeros((rows, LANES), F32), unroll=4)
        return jnp.sum(c, axis=-1, keepdims=True) + jnp.where(pred_new, 1.0, 0.0)

    def wide(col):
        return jnp.broadcast_to(col, (rows, LANES))

    def count_ge(cand):
        cand_w = wide(cand)
        return count(lambda k, j: k >= cand_w, key_new >= cand)

    thr = _kth_key(count_ge, float(n_sel), (rows, 1))
    thr_w = wide(thr)
    cnt_gt = count(lambda k, j: k > thr_w, key_new > thr)
    need = float(n_sel) - cnt_gt

    def body(i, p):
        cand = p + (jnp.int32(1) << (idx_bits - 1 - i))
        cand_w = wide(cand)
        c = count(lambda k, j: (k == thr_w) & (j * LANES + lane < cand_w), (key_new == thr) & (past < cand))
        return jnp.where(c < need, cand, p)

    last_tie = lax.fori_loop(0, idx_bits, body, jnp.zeros((rows, 1), I32))
    last_w = wide(last_tie)

    for j in range(nblk):
        k = key_scr[j]
        sel = (k > thr_w) | ((k == thr_w) & (j * LANES + lane <= last_w))
        selp_ref[:, j * LANES:(j + 1) * LANES] = jnp.where(sel, 1.0, 0.0)
    sel_new = (key_new > thr) | ((key_new == thr) & (past <= last_tie))
    seln_ref[...] = jnp.broadcast_to(jnp.where(sel_new, 1.0, 0.0), (rows, LANES))


def _sample_select(scores, qidx, tail):
    dec, past = scores.shape
    n_sel = min(TOPK_MAX, (past + 1) // 4)
    idx_bits = max(1, past.bit_length())
    rows = 64 if dec % 64 == 0 else dec
    kern = functools.partial(_sample_select_kernel, n_sel=n_sel, idx_bits=idx_bits)
    return pl.pallas_call(
        kern,
        grid=(dec // rows,),
        in_specs=[pl.BlockSpec((rows, past), lambda i: (i, 0)), pl.BlockSpec((rows, 512), lambda i: (i, 0)),
                  pl.BlockSpec((rows, LANES), lambda i: (i, 0))],
        out_specs=[pl.BlockSpec((rows, past), lambda i: (i, 0)), pl.BlockSpec((rows, LANES), lambda i: (i, 0))],
        out_shape=[jax.ShapeDtypeStruct((dec, past), F32), jax.ShapeDtypeStruct((dec, LANES), F32)],
        scratch_shapes=[pltpu.VMEM((past // LANES, rows, LANES), I32)],
        compiler_params=_cparams(("parallel",)),
    )(scores, qidx, tail)


def _mla_sample_kernel(pt_ref, qn_ref, qr_ref, rown_ref, wuk_ref, wuv_ref, gk_ref, cache_ref, o_ref,
                       buf, sem, lat_scr, kr_scr, a_scr, m_scr, l_scr, acc_scr, *, pps):
    c = pl.program_id(1)
    n_chunks = pl.num_programs(1)
    heads = MLA_HEADS
    slot = _fetch_pages(cache_ref, pt_ref, buf, sem, pl.program_id(0) * n_chunks + c,
                        pl.num_programs(0) * n_chunks, lambda st: (st // n_chunks, (st % n_chunks) * pps), pps)

    @pl.when(c == 0)
    def _():
        m_scr[...] = jnp.full(m_scr.shape, -jnp.inf, F32)
        l_scr[...] = jnp.zeros(l_scr.shape, F32)
        acc_scr[...] = jnp.zeros(acc_scr.shape, F32)
        qg = qn_ref[...] * gk_ref[...]
        own = (lax.broadcasted_iota(I32, (heads, heads * MLA_NOPE), 1) // MLA_NOPE
               == lax.broadcasted_iota(I32, (heads, heads * MLA_NOPE), 0))
        a_scr[...] = _dot(jnp.where(own, qg, 0.0).astype(BF16), wuk_ref[...]).astype(BF16)

    qr = qr_ref[...]

    def scores(latb, krb):
        knt = _dot(wuk_ref[...], latb)
        ms = jnp.mean((knt * knt).reshape(heads, MLA_NOPE, latb.shape[1]), axis=1)
        return (_dot(a_scr[...], latb) * lax.rsqrt(ms + EPS) + _dot(qr, krb)) * MLA_SCALE

    def update(latb, krb, n_valid):
        n = latb.shape[1]
        sub = min(n, 512)
        s = jnp.concatenate([scores(latb[:, j:j + sub], krb[:, j:j + sub]) for j in range(0, n, sub)], axis=1)
        if n_valid < n:
            s = jnp.where(lax.broadcasted_iota(I32, s.shape, 1) < n_valid, s, -jnp.inf)
        m_old = m_scr[...]
        m_new = jnp.maximum(m_old, jnp.max(s, axis=-1, keepdims=True))
        alpha = jnp.exp(m_old - m_new)
        p = jnp.exp(s - m_new)
        l_scr[...] = alpha * l_scr[...] + jnp.sum(p, axis=-1, keepdims=True)
        m_scr[...] = m_new
        acc_scr[...] = alpha * acc_scr[...] + _dot_nt(p.astype(BF16), latb)

    for i in range(pps):
        lat_scr[:, i * PAGE_SIZE:(i + 1) * PAGE_SIZE] = buf[slot, i, 0:MLA_KV_RANK, :].astype(BF16)
        kr_scr[:, i * PAGE_SIZE:(i + 1) * PAGE_SIZE] = buf[slot, i, MLA_KV_RANK:MLA_ROW, :].astype(BF16)
    update(lat_scr[...], kr_scr[...], lat_scr.shape[1])

    @pl.when(c == n_chunks - 1)
    def _():
        rn = jnp.broadcast_to(rown_ref[...], (MLA_ROW, PAGE_SIZE))
        update(rn[0:MLA_KV_RANK, :].astype(BF16), rn[MLA_KV_RANK:MLA_ROW, :].astype(BF16), 1)
        o_lat = (acc_scr[...] / l_scr[...]).astype(BF16)
        full = _dot(o_lat, wuv_ref[...])
        hd = lax.broadcasted_iota(I32, full.shape, 1) // MLA_V
        own = hd == lax.broadcasted_iota(I32, full.shape, 0)
        o_ref[...] = jnp.sum(jnp.where(own, full, 0.0), axis=0, keepdims=True)


def _mla_sample(page_table, qn_row, qr, row_new, wuk_t, wuv, gk_row, cache_mla, pps):
    dec, n_pages = page_table.shape
    n_chunks = n_pages // pps
    grid_spec = pltpu.PrefetchScalarGridSpec(
        num_scalar_prefetch=1,
        grid=(dec, n_chunks),
        in_specs=[pl.BlockSpec((None, 1, MLA_HEADS * MLA_NOPE), lambda b, c, pt: (b, 0, 0)),
                  pl.BlockSpec((None, MLA_HEADS, MLA_ROPE), lambda b, c, pt: (b, 0, 0)),
                  pl.BlockSpec((None, MLA_ROW, 1), lambda b, c, pt: (b, 0, 0)),
                  pl.BlockSpec(wuk_t.shape, lambda b, c, pt: (0, 0)),
                  pl.BlockSpec(wuv.shape, lambda b, c, pt: (0, 0)),
                  pl.BlockSpec(gk_row.shape, lambda b, c, pt: (0, 0)),
                  pl.BlockSpec(memory_space=pl.ANY)],
        out_specs=pl.BlockSpec((None, 1, MLA_HEADS * MLA_V), lambda b, c, pt: (b, 0, 0)),
        scratch_shapes=_page_scratch(pps, MLA_ROW) + [
            pltpu.VMEM((MLA_KV_RANK, pps * PAGE_SIZE), BF16), pltpu.VMEM((MLA_ROPE, pps * PAGE_SIZE), BF16),
            pltpu.VMEM((MLA_HEADS, MLA_KV_RANK), BF16),
            pltpu.VMEM((MLA_HEADS, 1), F32), pltpu.VMEM((MLA_HEADS, 1), F32),
            pltpu.VMEM((MLA_HEADS, MLA_KV_RANK), F32)],
    )
    return pl.pallas_call(
        functools.partial(_mla_sample_kernel, pps=pps), grid_spec=grid_spec,
        out_shape=jax.ShapeDtypeStruct((dec, 1, MLA_HEADS * MLA_V), F32),
        compiler_params=_cparams(("arbitrary", "arbitrary")),
    )(page_table, qn_row, qr, row_new, wuk_t, wuv, gk_row, cache_mla)


def _dsa_sample_kernel(pt_ref, q_ref, kvn_ref, selp_ref, seln_ref, biasp_ref, biasn_ref, cache_ref, o_ref,
                       buf, sem, k_scr, v_scr, m_scr, l_scr, acc_scr, *, pps):
    c = pl.program_id(1)
    n_chunks = pl.num_programs(1)
    slot = _fetch_pages(cache_ref, pt_ref, buf, sem, pl.program_id(0) * n_chunks + c,
                        pl.num_programs(0) * n_chunks, lambda st: (st // n_chunks, (st % n_chunks) * pps), pps)

    @pl.when(c == 0)
    def _():
        m_scr[...] = jnp.full(m_scr.shape, NEG_BIG, F32)
        l_scr[...] = jnp.zeros(l_scr.shape, F32)
        acc_scr[...] = jnp.zeros(acc_scr.shape, F32)

    q = q_ref[...]

    def update(s, sel, pv_of):
        sh = jnp.where(sel, s, NEG_BIG)
        m_old = m_scr[...]
        m_new = jnp.maximum(m_old, jnp.max(sh, axis=-1, keepdims=True))
        alpha = jnp.exp(m_old - m_new)
        p = jnp.where(sel, jnp.exp(sh - m_new), 0.0)
        l_scr[...] = alpha * l_scr[...] + jnp.sum(p, axis=-1, keepdims=True)
        m_scr[...] = m_new
        acc_scr[...] = alpha * acc_scr[...] + pv_of(p)

    for i in range(pps):
        k_scr[:, i * PAGE_SIZE:(i + 1) * PAGE_SIZE] = buf[slot, i, 0:128, :].astype(BF16)
        v_scr[:, i * PAGE_SIZE:(i + 1) * PAGE_SIZE] = buf[slot, i, 128:256, :].astype(BF16)
    s = _dot(q.astype(BF16), k_scr[...]) + biasp_ref[...]
    update(s, selp_ref[...] > 0.5, lambda p: _dot_nt(p.astype(BF16), v_scr[...]))

    @pl.when(c == n_chunks - 1)
    def _():
        kvn = kvn_ref[...]
        kn = kvn[:, 0:128].astype(BF16).astype(F32)
        vn = kvn[:, 128:256].astype(BF16).astype(F32)
        s_new = jnp.sum(q.astype(F32) * kn, axis=-1, keepdims=True) + biasn_ref[...]
        update(s_new, seln_ref[:, 0:1] > 0.5, lambda p: p.astype(BF16).astype(F32) * vn)
        o = acc_scr[...] / l_scr[...]
        lane = lax.broadcasted_iota(I32, o.shape, 1)
        row = lax.broadcasted_iota(I32, o.shape, 0)
        own = (lane // DSA_HEAD_DIM) == (row // DSA_GROUP)
        o_ref[...] = jnp.where(own, o, 0.0)


def _dsa_sample(page_table, qs, kv_new, selp, seln, bias_past, bias_new, cache_kv4, pps):
    dec, n_pages = page_table.shape
    n_chunks = n_pages // pps
    chunk = pps * PAGE_SIZE
    grid_spec = pltpu.PrefetchScalarGridSpec(
        num_scalar_prefetch=1,
        grid=(dec, n_chunks),
        in_specs=[pl.BlockSpec((None, DSA_HEADS, LANES), lambda b, c, pt: (b, 0, 0)),
                  pl.BlockSpec((None, 1, 256), lambda b, c, pt: (b, 0, 0)),
                  pl.BlockSpec((None, 1, chunk), lambda b, c, pt: (b, 0, c)),
                  pl.BlockSpec((None, 1, LANES), lambda b, c, pt: (b, 0, 0)),
                  pl.BlockSpec((DSA_HEADS, chunk), lambda b, c, pt: (0, c)),
                  pl.BlockSpec((DSA_HEADS, 1), lambda b, c, pt: (0, 0)),
                  pl.BlockSpec(memory_space=pl.ANY)],
        out_specs=pl.BlockSpec((None, DSA_HEADS, LANES), lambda b, c, pt: (b, 0, 0)),
        scratch_shapes=_page_scratch(pps, 256) + [
            pltpu.VMEM((128, chunk), BF16), pltpu.VMEM((128, chunk), BF16),
            pltpu.VMEM((DSA_HEADS, 1), F32), pltpu.VMEM((DSA_HEADS, 1), F32),
            pltpu.VMEM((DSA_HEADS, LANES), F32)],
    )
    return pl.pallas_call(
        functools.partial(_dsa_sample_kernel, pps=pps), grid_spec=grid_spec,
        out_shape=jax.ShapeDtypeStruct((dec, DSA_HEADS, LANES), F32),
        compiler_params=_cparams(("arbitrary", "arbitrary")),
    )(page_table, qs, kv_new, selp, seln, bias_past, bias_new, cache_kv4)


def _out_kernel(x_ref, oa_ref, ob_ref, gm_ref, sf_ref, scf_ref, gf_ref, gout_ref, gffn_ref,
                wout_ref, wg_ref, wu_ref, wdown_ref, y_ref, *, ff_chunk):
    def rms(v, g):
        return v * lax.rsqrt(jnp.mean(v * v, axis=-1, keepdims=True) + EPS) * g

    half = oa_ref.shape[1]
    na = rms(oa_ref[...], gout_ref[:, 0:half]).astype(BF16)
    nb = rms(ob_ref[...], gout_ref[:, half:2 * half]).astype(BF16)
    mix = _dot(na, wout_ref[0:half, :]) + _dot(nb, wout_ref[half:2 * half, :])
    x1 = x_ref[...] + gm_ref[...] * mix
    hb = (rms(x1, gffn_ref[...]) * (1.0 + scf_ref[...]) + sf_ref[...]).astype(BF16)
    acc = jnp.zeros(x1.shape, F32)
    for j in range(D_FF // ff_chunk):
        g = _dot(hb, wg_ref[:, j * ff_chunk:(j + 1) * ff_chunk])
        u = _dot(hb, wu_ref[:, j * ff_chunk:(j + 1) * ff_chunk])
        a = (g / (1.0 + jnp.exp(-g))) * u
        acc = acc + _dot(a.astype(BF16), wdown_ref[j * ff_chunk:(j + 1) * ff_chunk, :])
    y_ref[...] = x1 + gf_ref[...] * acc


def _out_stage(x2d, oa, ob, gate_m, shift_f, scale_f, gate_f, wp, tm, per_token, tiles_per_seq):
    t_total = x2d.shape[0]
    if per_token:
        mod_spec = pl.BlockSpec((tm, D_MODEL), lambda i: (i, 0))
    else:
        mod_spec = pl.BlockSpec((None, 1, D_MODEL), lambda i: (i // tiles_per_seq, 0, 0))
    consts = [wp["g_out"], wp["g_ffn"], wp["w_out"], wp["w_gate"], wp["w_up"], wp["w_down"]]
    tok = lambda w: pl.BlockSpec((tm, w), lambda i: (i, 0))
    return pl.pallas_call(
        functools.partial(_out_kernel, ff_chunk=256),
        grid=(t_total // tm,),
        in_specs=[tok(D_MODEL), tok(512), tok(512), mod_spec, mod_spec, mod_spec, mod_spec]
                 + [_full(c.shape) for c in consts],
        out_specs=tok(D_MODEL),
        out_shape=jax.ShapeDtypeStruct((t_total, D_MODEL), F32),
        compiler_params=_cparams(("parallel",)),
    )(x2d, oa, ob, gate_m, shift_f, scale_f, gate_f, *consts)


def _prep_weights(w_in, g_norm_mix, g_norm_ffn, g_q_lat, w_uq, g_kv_lat, w_ukv, g_mla_q_nope, g_mla_q_rope,
                  g_mla_k_nope, g_mla_k_rope, g_dsa_q, g_dsa_k, g_out, w_out, w_ffn_in, w_ffn_out):
    splits = np.cumsum([MLA_Q_RANK, MLA_KV_RANK, MLA_ROPE, 512, 128, 128, 512, IDX_DIM, IDX_HEADS])
    s = [0] + splits.tolist()
    col = lambda i: w_in[:, s[i]:s[i + 1]]
    pad = jnp.zeros((D_MODEL, C_END - C_TAIL - IDX_DIM - MLA_ROPE - IDX_HEADS), w_in.dtype)
    w_in_r = jnp.concatenate([col(0), col(1), col(3), col(4), col(5), col(6), col(7), col(2), col(8), pad], axis=1)
    zq = jnp.zeros((MLA_Q_RANK, MLA_HEADS, LANES - MLA_NOPE - MLA_ROPE), w_uq.dtype)
    w_uq_cat = jnp.concatenate([w_uq, zq], axis=2).reshape(MLA_Q_RANK, MLA_HEADS * LANES)
    zk = jnp.zeros((MLA_KV_RANK, MLA_HEADS, LANES - MLA_NOPE), w_ukv.dtype)
    w_uk_cat = jnp.concatenate([w_ukv[:, :, :MLA_NOPE], zk], axis=2).reshape(MLA_KV_RANK, MLA_HEADS * LANES)
    w_uv = w_ukv[:, :, MLA_NOPE:].reshape(MLA_KV_RANK, MLA_HEADS * MLA_V)
    zv = jnp.zeros((MLA_KV_RANK, MLA_HEADS, LANES - MLA_V), w_ukv.dtype)
    w_uv_pad = jnp.concatenate([w_ukv[:, :, MLA_NOPE:], zv], axis=2).reshape(MLA_KV_RANK, MLA_HEADS * LANES)
    w_uk_t = w_ukv[:, :, :MLA_NOPE].reshape(MLA_KV_RANK, MLA_HEADS * MLA_NOPE).T
    z32 = jnp.zeros((LANES - MLA_NOPE - MLA_ROPE,), F32)
    z64 = jnp.zeros((LANES - MLA_NOPE,), F32)
    g_qcat = jnp.tile(jnp.concatenate([g_mla_q_nope, g_mla_q_rope, z32]), MLA_HEADS)
    g_kcat = jnp.tile(jnp.concatenate([g_mla_k_nope, z64]), MLA_HEADS)
    g_tail = jnp.concatenate([jnp.zeros((T_ROPE,), F32), g_mla_k_rope, z32])
    row = lambda v: v.reshape(1, -1).astype(F32)
    return {
        "g_mix": row(g_norm_mix), "g_ffn": row(g_norm_ffn), "w_in": w_in_r.astype(BF16),
        "g_qlat": row(g_q_lat), "w_uq": w_uq_cat.astype(BF16), "g_qcat": row(g_qcat),
        "g_kvlat": row(g_kv_lat), "w_ukv": jnp.concatenate([w_uk_cat, w_uv_pad], axis=1).astype(BF16),
        "g_kcat": row(g_kcat), "g_tail": row(g_tail),
        "g_q": row(jnp.tile(g_dsa_q, DSA_HEADS)), "g_k": row(jnp.tile(g_dsa_k, DSA_KV_HEADS)),
        "m_cat": _group_matrix(256, [(0, 64), (64, 96), (128, 192), (192, 224)]),
        "m_64": _group_matrix(256, [(0, 64), (64, 128), (128, 192), (192, 256)]),
        "w_uk_t": w_uk_t.astype(BF16), "w_uv": w_uv.astype(BF16), "g_k_nope": g_mla_k_nope,
        "g_out": row(g_out), "w_out": w_out.astype(BF16),
        "w_gate": w_ffn_in[:, :D_FF].astype(BF16), "w_up": w_ffn_in[:, D_FF:].astype(BF16),
        "w_down": w_ffn_out.astype(BF16),
    }


def _rope_tables(pos):
    freq = ROPE_THETA ** (-jnp.arange(HALF_ROPE, dtype=F32) / HALF_ROPE)
    ang = pos.astype(F32)[:, None] * freq[None, :]
    cos, sin = jnp.cos(ang), jnp.sin(ang)
    n = pos.shape[0]
    ones = jnp.ones((n, T_ROPE), F32)
    zeros = jnp.zeros((n, T_ROPE), F32)
    tail1 = jnp.ones((n, LANES - T_ROPE - MLA_ROPE), F32)
    tail0 = jnp.zeros((n, LANES - T_ROPE - MLA_ROPE), F32)
    return (jnp.concatenate([ones, cos, cos, tail1], axis=1),
            jnp.concatenate([zeros, -sin, sin, tail0], axis=1))


def _pick_tile(n, prefs):
    for t in prefs:
        if n % t == 0:
            return t
    return n


def kernel(x_prompt, x_sample, c_prompt, c_sample, cache_mla, cache_kv, cache_idx, page_table, rel_bias, w_ada, b_ada, g_norm_mix, g_norm_ffn, w_in, g_q_lat, w_uq, g_kv_lat, w_ukv, g_mla_q_nope, g_mla_q_rope, g_mla_k_nope, g_mla_k_rope, g_dsa_q, g_dsa_k, g_out, w_out, w_ffn_in, w_ffn_out):
    assert w_ada.shape[0] == 1 and x_sample.shape[1] == 1, "one layer, one new token per sample"
    batch, seq, _ = x_prompt.shape
    dec = x_sample.shape[0]
    n_pages = page_table.shape[1]
    past = n_pages * PAGE_SIZE
    wp = _prep_weights(w_in[0], g_norm_mix[0], g_norm_ffn[0], g_q_lat[0], w_uq[0], g_kv_lat[0], w_ukv[0],
                       g_mla_q_nope[0], g_mla_q_rope[0], g_mla_k_nope[0], g_mla_k_rope[0], g_dsa_q[0], g_dsa_k[0],
                       g_out[0], w_out[0], w_ffn_in[0], w_ffn_out[0])

    mod = _ada(jnp.concatenate([c_prompt, c_sample], axis=0), w_ada[0], b_ada[0])
    mods_p = [m.reshape(batch, 1, D_MODEL) for m in jnp.split(mod[:batch], 6, axis=-1)]
    mods_s = jnp.split(mod[batch:], 6, axis=-1)

    tm = _pick_tile(seq, (512, 256, 128))
    xp2 = x_prompt.reshape(batch * seq, D_MODEL)
    cos_p, sin_p = _rope_tables(jnp.arange(seq))
    pin = _in_stage(xp2, mods_p[1], mods_p[0], cos_p, sin_p, wp, tm, per_token=False)
    o_mla_p = _mla_prompt(pin["qcat"], pin["kcat"], pin["vmla"], batch, seq,
                          _pick_tile(seq, (256, 128)), _pick_tile(seq, (512, 256, 128)))
    o_dsa_p = _dsa_prompt(rel_bias, pin["qidx"], pin["tail"], pin["kidxdup"], pin["qdsa"], pin["kdup"], pin["vone"],
                          batch, seq, _pick_tile(seq, (256, 128)))
    y_p = _out_stage(xp2, o_mla_p, o_dsa_p, mods_p[2], mods_p[3], mods_p[4], mods_p[5], wp, tm,
                     per_token=False, tiles_per_seq=seq // tm)

    xs2 = x_sample.reshape(dec, D_MODEL)
    cos_s, sin_s = _rope_tables(jnp.full((1,), past))
    sin_ = _in_stage(xs2, mods_s[1], mods_s[0], cos_s, sin_s, wp, dec, per_token=True)
    pps_wide = _pick_tile(n_pages, (64, 32, 16, 8, 4, 2))
    qidx3 = sin_["qidx"].reshape(dec, IDX_HEADS, IDX_DIM)
    widx3 = sin_["tail"][:, T_WIDX:T_WIDX + IDX_HEADS].reshape(dec, IDX_HEADS, 1)
    scores = _sample_scores(page_table, qidx3, widx3, _pages_feature_major(cache_idx), pps_wide)
    selp, seln = _sample_select(scores, sin_["qidx"], sin_["tail"])
    qc = sin_["qcat"].reshape(dec, MLA_HEADS, LANES)
    qn_row = qc[:, :, 0:MLA_NOPE].astype(F32).reshape(dec, 1, MLA_HEADS * MLA_NOPE)
    qr = qc[:, :, T_ROPE:T_ROPE + MLA_ROPE]
    gk_row = jnp.tile(wp["g_k_nope"], MLA_HEADS).reshape(1, MLA_HEADS * MLA_NOPE)
    o_mla_s = _mla_sample(page_table, qn_row, qr, sin_["row"].reshape(dec, MLA_ROW, 1), wp["w_uk_t"], wp["w_uv"],
                          gk_row, _pages_feature_major(cache_mla), pps_wide).reshape(dec, MLA_HEADS * MLA_V)
    qd = sin_["qdsa"].astype(F32).reshape(dec, DSA_HEADS, DSA_HEAD_DIM)
    on_c = (jnp.arange(DSA_KV_HEADS)[None, :] == (jnp.arange(DSA_HEADS) // DSA_GROUP)[:, None]).astype(F32)
    qd2 = (qd[:, :, None, :] * on_c[None, :, :, None]).reshape(dec, DSA_HEADS, LANES)
    bias_past = _bias_table(rel_bias, 1, past, 0, -1, past).reshape(DSA_HEADS, past)
    bias_new = _bias_table(rel_bias, 1, LANES, 0, 0, 0)[:, 0, 0:1]
    o8 = _dsa_sample(page_table, qd2, sin_["kv"].reshape(dec, 1, 256), selp.reshape(dec, 1, past),
                     seln.reshape(dec, 1, LANES), bias_past, bias_new, _pages_feature_major(cache_kv), pps_wide)
    o_dsa_s = o8.reshape(dec, DSA_HEADS, DSA_KV_HEADS, DSA_HEAD_DIM).sum(axis=2).reshape(dec, DSA_HEADS * DSA_HEAD_DIM)
    y_s = _out_stage(xs2, o_mla_s, o_dsa_s, mods_s[2], mods_s[3], mods_s[4], mods_s[5], wp, dec,
                     per_token=True, tiles_per_seq=1)

    return (y_p.reshape(batch, seq, D_MODEL), y_s.reshape(dec, 1, D_MODEL),
            jnp.swapaxes(pin["row"], 1, 2)[None],
            jnp.moveaxis(pin["kv"].reshape(batch, 2, DSA_KV_HEADS, DSA_HEAD_DIM, seq), -1, 1)[None],
            jnp.swapaxes(pin["kidx"], 1, 2)[None],
            sin_["row"].reshape(1, dec, 1, MLA_ROW),
            sin_["kv"].reshape(1, dec, 1, 2, DSA_KV_HEADS, DSA_HEAD_DIM),
            sin_["kidx"].reshape(1, dec, 1, IDX_DIM))
```

```python
import functools
import math

import numpy as np
import jax
import jax.numpy as jnp
from jax import lax
from jax.experimental import pallas as pl
from jax.experimental.pallas import tpu as pltpu

D_MODEL = 1024
PAGE_SIZE = 128
MLA_HEADS = 8
MLA_NOPE = 64
MLA_ROPE = 32
MLA_V = 64
MLA_Q_RANK = 384
MLA_KV_RANK = 256
MLA_ROW = MLA_KV_RANK + MLA_ROPE
DSA_HEADS = 8
DSA_KV_HEADS = 2
DSA_HEAD_DIM = 64
DSA_GROUP = DSA_HEADS // DSA_KV_HEADS
IDX_HEADS = 8
IDX_DIM = 64
TOPK_MAX = 256
N_BUCKETS = 32
MAX_DISTANCE = 128
ROPE_THETA = 10000.0
D_FF = ((8 * D_MODEL + 3 * 256 - 1) // (3 * 256)) * 256
EPS = 1e-6
MLA_SCALE = (MLA_NOPE + MLA_ROPE) ** -0.5
DSA_SCALE = DSA_HEAD_DIM ** -0.5
IDX_SCALE = (IDX_DIM * IDX_HEADS) ** -0.5

LANES = 128
VMEM_LIMIT = 56 * 1024 * 1024

F32 = jnp.float32
BF16 = jnp.bfloat16
I32 = jnp.int32
NEG_BIG = -1e30
INT_MIN = -(2 ** 31)

C_QLAT = 0
C_KVLAT = C_QLAT + MLA_Q_RANK
C_Q = C_KVLAT + MLA_KV_RANK
C_K = C_Q + DSA_HEADS * DSA_HEAD_DIM
C_V = C_K + DSA_KV_HEADS * DSA_HEAD_DIM
C_QIDX = C_V + DSA_KV_HEADS * DSA_HEAD_DIM
C_TAIL = C_QIDX + IDX_HEADS * IDX_DIM
C_END = C_TAIL + LANES
T_ROPE = IDX_DIM
T_WIDX = IDX_DIM + MLA_ROPE
HALF_ROPE = MLA_ROPE // 2
KB = 128


def _dot(a, b):
    return jnp.dot(a, b, preferred_element_type=F32)


def _dot_nt(a, b):
    return lax.dot_general(a, b, (((1,), (1,)), ((), ())), preferred_element_type=F32)


def _split(a):
    hi = a.astype(BF16)
    lo = (a - hi.astype(F32)).astype(BF16)
    return hi, lo


def _dot3(a, b):
    ah, al = _split(a)
    bh, bl = _split(b)
    return _dot(ah, bh) + (_dot(al, bh) + _dot(ah, bl))


def _cparams(sem):
    return pltpu.CompilerParams(dimension_semantics=sem, vmem_limit_bytes=VMEM_LIMIT)


def _full(shape):
    n = len(shape)
    return pl.BlockSpec(shape, lambda *a, _n=n: (0,) * _n, pipeline_mode=pl.Buffered(1))


def _ada_kernel(c_ref, w_ref, b_ref, o_ref):
    c = c_ref[...]
    s = c / (1.0 + jnp.exp(-c))
    o_ref[...] = _dot3(s, w_ref[...]) + b_ref[...]


def _ada(c, w, b):
    n = c.shape[0]
    return pl.pallas_call(
        _ada_kernel,
        grid=(6,),
        in_specs=[pl.BlockSpec((n, D_MODEL), lambda j: (0, 0)),
                  pl.BlockSpec((D_MODEL, D_MODEL), lambda j: (0, j)),
                  pl.BlockSpec((1, D_MODEL), lambda j: (0, j))],
        out_specs=pl.BlockSpec((n, D_MODEL), lambda j: (0, j)),
        out_shape=jax.ShapeDtypeStruct((n, 6 * D_MODEL), F32),
        compiler_params=_cparams(("arbitrary",)),
    )(c, w, b.reshape(1, -1))


def _group_matrix(width, groups):
    g = np.zeros((width, width), np.float32)
    for lo, hi in groups:
        g[lo:hi, lo:hi] = 1.0 / (hi - lo)
    return jnp.asarray(g, BF16)


def _in_kernel(x_ref, sc_ref, sh_ref, cos_ref, sin_ref, gmix_ref, win_ref, gqlat_ref, wuq_ref, gqcat_ref,
               gkvlat_ref, wukv_ref, gkcat_ref, gtail_ref, gq_ref, gk_ref, mcat_ref, m64_ref,
               qcat_o, kcat_o, vmla_o, row_o, kv_o, kidx_o, tail_o, kidxdup_o, qdsa_o, kdup_o, vone_o, qidx_o,
               *, rows_feature_major):
    tm = x_ref.shape[0]
    x = x_ref[...]
    h = x * lax.rsqrt(jnp.mean(x * x, axis=-1, keepdims=True) + EPS) * gmix_ref[...]
    hb = (h * (1.0 + sc_ref[...]) + sh_ref[...]).astype(BF16)

    def proj(lo, hi):
        return _dot(hb, win_ref[:, lo:hi])

    cos = cos_ref[...]
    sin = sin_ref[...]
    lane = lax.broadcasted_iota(I32, (tm, LANES), 1)
    first_half = lane < T_ROPE + HALF_ROPE
    rope_lanes = (lane >= T_ROPE) & (lane < T_ROPE + MLA_ROPE)
    low64 = lane < 64

    def rope(xh):
        rot = jnp.where(first_half, pltpu.roll(xh, LANES - HALF_ROPE, 1), pltpu.roll(xh, HALF_ROPE, 1))
        return xh * cos + rot * sin

    def gnorm(blk, m_ref):
        ms = _dot((blk * blk).astype(BF16), m_ref[...])
        return blk * lax.rsqrt(ms + EPS)

    ql = proj(C_QLAT, C_KVLAT)
    qln = ql * lax.rsqrt(jnp.mean(ql * ql, axis=-1, keepdims=True) + EPS) * gqlat_ref[...]
    qm = _dot(qln.astype(BF16), wuq_ref[...])
    for j in range(4):
        y = gnorm(qm[:, 256 * j:256 * j + 256], mcat_ref) * gqcat_ref[:, 256 * j:256 * j + 256]
        for t in range(2):
            qcat_o[:, 256 * j + 128 * t:256 * j + 128 * t + 128] = rope(y[:, 128 * t:128 * t + 128]).astype(BF16)

    tail = proj(C_TAIL, C_END)
    ssq = jnp.sum(jnp.where(rope_lanes, tail * tail, 0.0), axis=-1, keepdims=True) * (1.0 / MLA_ROPE)
    tn = jnp.where(rope_lanes, tail * lax.rsqrt(ssq + EPS) * gtail_ref[...], tail)
    tr = rope(tn)
    if rows_feature_major:
        kidx_o[...] = tail.T[0:IDX_DIM, :]
    else:
        kidx_o[...] = tail[:, 0:IDX_DIM]
    tail_o[...] = tail
    kidxdup_o[...] = jnp.where(low64, tail, pltpu.roll(tail, 64, 1)).astype(BF16)
    krope = jnp.where(rope_lanes, tr, 0.0)

    kvl = proj(C_KVLAT, C_Q)
    lat = kvl * lax.rsqrt(jnp.mean(kvl * kvl, axis=-1, keepdims=True) + EPS) * gkvlat_ref[...]
    if rows_feature_major:
        row_o[0:MLA_KV_RANK, :] = lat.T
        row_o[MLA_KV_RANK:MLA_ROW, :] = tr.T[T_ROPE:T_ROPE + MLA_ROPE, :]
    else:
        row_o[:, 0:MLA_KV_RANK] = lat
        row_o[:, MLA_KV_RANK:MLA_ROW] = tr[:, T_ROPE:T_ROPE + MLA_ROPE]
    latb = lat.astype(BF16)
    kx = _dot(latb, wukv_ref[:, 0:1024])
    for j in range(4):
        y = gnorm(kx[:, 256 * j:256 * j + 256], mcat_ref) * gkcat_ref[:, 256 * j:256 * j + 256]
        for t in range(2):
            kcat_o[:, 256 * j + 128 * t:256 * j + 128 * t + 128] = (y[:, 128 * t:128 * t + 128] + krope).astype(BF16)
    ones_half = (lax.broadcasted_iota(I32, (tm, 1024), 1) % LANES) >= MLA_V
    vmla_o[...] = jnp.where(ones_half, 1.0, _dot(latb, wukv_ref[:, 1024:2048])).astype(BF16)

    q = proj(C_Q, C_K)
    for j in range(2):
        qdsa_o[:, 256 * j:256 * j + 256] = (gnorm(q[:, 256 * j:256 * j + 256], m64_ref)
                                            * gq_ref[:, 256 * j:256 * j + 256] * DSA_SCALE).astype(BF16)
    k = proj(C_K, C_V)
    ms = _dot((k * k).astype(BF16), m64_ref[0:128, 0:128])
    kn = k * lax.rsqrt(ms + EPS) * gk_ref[...]
    v = proj(C_V, C_QIDX)
    if rows_feature_major:
        kv_o[0:128, :] = kn.T
        kv_o[128:256, :] = v.T
    else:
        kv_o[:, 0:128] = kn
        kv_o[:, 128:256] = v
    kr = pltpu.roll(kn, 64, 1)
    kdup_o[:, 0:128] = jnp.where(low64, kn, kr).astype(BF16)
    kdup_o[:, 128:256] = jnp.where(low64, kr, kn).astype(BF16)
    vone_o[:, 0:128] = jnp.where(low64, v, 1.0).astype(BF16)
    vone_o[:, 128:256] = jnp.where(low64, pltpu.roll(v, 64, 1), 1.0).astype(BF16)
    qidx_o[...] = proj(C_QIDX, C_TAIL).astype(BF16)


_IN_OUT_WIDTHS = (("qcat", 1024, BF16), ("kcat", 1024, BF16), ("vmla", 1024, BF16), ("row", MLA_ROW, F32),
                  ("kv", 256, F32), ("kidx", IDX_DIM, F32), ("tail", LANES, F32), ("kidxdup", LANES, BF16),
                  ("qdsa", 512, BF16), ("kdup", 256, BF16), ("vone", 256, BF16), ("qidx", 512, BF16))


def _in_stage(x2d, scale, shift, cos, sin, wp, tm, per_token):
    t_total = x2d.shape[0]
    n_tiles = t_total // tm
    if per_token:
        mod_spec = pl.BlockSpec((tm, D_MODEL), lambda i: (i, 0))
        tab_spec = pl.BlockSpec((1, LANES), lambda i: (0, 0))
    else:
        tiles_per_seq = cos.shape[0] // tm
        mod_spec = pl.BlockSpec((None, 1, D_MODEL), lambda i: (i // tiles_per_seq, 0, 0))
        tab_spec = pl.BlockSpec((tm, LANES), lambda i: (i % tiles_per_seq, 0))
    consts = [wp["g_mix"], wp["w_in"], wp["g_qlat"], wp["w_uq"], wp["g_qcat"], wp["g_kvlat"], wp["w_ukv"],
              wp["g_kcat"], wp["g_tail"], wp["g_q"], wp["g_k"], wp["m_cat"], wp["m_64"]]
    in_specs = ([pl.BlockSpec((tm, D_MODEL), lambda i: (i, 0)), mod_spec, mod_spec, tab_spec, tab_spec]
                + [_full(c.shape) for c in consts])
    out_specs = [pl.BlockSpec((tm, w), lambda i: (i, 0)) for _, w, _ in _IN_OUT_WIDTHS]
    out_shape = [jax.ShapeDtypeStruct((t_total, w), dt) for _, w, dt in _IN_OUT_WIDTHS]
    if not per_token:
        for j, (name, w, dt) in enumerate(_IN_OUT_WIDTHS):
            if name in ("row", "kv", "kidx"):
                out_specs[j] = pl.BlockSpec((None, w, tm), lambda i: (i // tiles_per_seq, 0, i % tiles_per_seq))
                out_shape[j] = jax.ShapeDtypeStruct((n_tiles // tiles_per_seq, w, tiles_per_seq * tm), dt)
    outs = pl.pallas_call(
        functools.partial(_in_kernel, rows_feature_major=not per_token),
        grid=(n_tiles,), in_specs=in_specs, out_specs=out_specs, out_shape=out_shape,
        compiler_params=_cparams(("parallel",)),
    )(x2d, scale, shift, cos, sin, *consts)
    return {name: o for (name, _, _), o in zip(_IN_OUT_WIDTHS, outs)}


def _mla_prompt_kernel(q_ref, k_ref, v_ref, o_ref, m_scr, acc_scr, s_scr, *, tk):
    tq = q_ref.shape[0]
    qi = pl.program_id(1)
    n_chunks = (qi * tq + tq + tk - 1) // tk
    m_scr[...] = jnp.full(m_scr.shape, -jnp.inf, F32)
    acc_scr[...] = jnp.zeros(acc_scr.shape, F32)
    low64 = lax.broadcasted_iota(I32, (tq, LANES), 1) < 64
    qpos = qi * tq + lax.broadcasted_iota(I32, (tq, tk), 0)
    kcol = lax.broadcasted_iota(I32, (tq, tk), 1)
    exp_scale = MLA_SCALE * math.log2(math.e)

    def raw_scores(c, h, masked):
        off = pl.multiple_of(c * tk, tk)
        s = _dot_nt(q_ref[:, h * 128:(h + 1) * 128], k_ref[pl.ds(off, tk), h * 128:(h + 1) * 128])
        if masked:
            s = jnp.where(c * tk + kcol <= qpos, s, -jnp.inf)
        return s

    def pass_max(c, masked):
        for h in range(MLA_HEADS):
            s = raw_scores(c, h, masked)
            s_scr[c, h] = s
            m = m_scr[h]
            for j in range(tk // LANES):
                m = jnp.maximum(m, s[:, j * LANES:(j + 1) * LANES])
            m_scr[h] = m

    def pass_sum(c, masked):
        off = pl.multiple_of(c * tk, tk)
        for h in range(MLA_HEADS):
            s = s_scr[c, h]
            m = m_scr[h]
            p = jnp.concatenate([jnp.exp2((s[:, j * LANES:(j + 1) * LANES] - m) * exp_scale)
                                 for j in range(tk // LANES)], axis=1).astype(BF16)
            acc_scr[h] += _dot(p, v_ref[pl.ds(off, tk), h * 128:(h + 1) * 128])

    def run(step):
        def body(c, carry):
            step(c, False)
            return carry

        lax.fori_loop(0, n_chunks - 1, body, 0)
        step(n_chunks - 1, True)

    run(pass_max)
    for h in range(MLA_HEADS):
        m_scr[h] = jnp.broadcast_to(jnp.max(m_scr[h], axis=-1, keepdims=True), (tq, LANES))
    run(pass_sum)
    for p in range(MLA_HEADS // 2):
        even, odd = acc_scr[2 * p], acc_scr[2 * p + 1]
        o_ref[:, 128 * p:128 * p + 128] = jnp.where(low64, even / pltpu.roll(even, 64, 1),
                                                    pltpu.roll(odd, 64, 1) / odd)


def _mla_prompt(qcat, kcat, vmla, batch, seq, tq, tk):
    nq = seq // tq
    return pl.pallas_call(
        functools.partial(_mla_prompt_kernel, tk=tk),
        grid=(batch, nq),
        in_specs=[pl.BlockSpec((tq, 1024), lambda b, i: (b * nq + i, 0)),
                  pl.BlockSpec((seq, 1024), lambda b, i: (b, 0)),
                  pl.BlockSpec((seq, 1024), lambda b, i: (b, 0))],
        out_specs=pl.BlockSpec((tq, 512), lambda b, i: (b * nq + i, 0)),
        out_shape=jax.ShapeDtypeStruct((batch * seq, 512), F32),
        scratch_shapes=[pltpu.VMEM((MLA_HEADS, tq, LANES), F32), pltpu.VMEM((MLA_HEADS, tq, LANES), F32),
                        pltpu.VMEM((seq // tk, MLA_HEADS, tq, tk), F32)],
        compiler_params=_cparams(("parallel", "arbitrary")),
    )(qcat, kcat, vmla)


def _sort_key(score):
    bits = lax.bitcast_convert_type(score + 0.0, I32)
    return bits ^ ((bits >> 31) & 0x7FFFFFFF)


def _kth_key(count_ge, n_sel, shape):
    def body(i, t):
        cand = t + (jnp.int32(1) << (31 - i))
        return jnp.where(count_ge(cand) >= n_sel, cand, t)

    return lax.fori_loop(0, 32, body, jnp.full(shape, INT_MIN, I32))


def _bucket(dist):
    max_exact = N_BUCKETS // 2
    d = jnp.maximum(dist, 0)
    log_ratio = jnp.log(jnp.maximum(d, max_exact).astype(F32) / max_exact) / math.log(MAX_DISTANCE / max_exact)
    large = jnp.minimum(max_exact + (log_ratio * (N_BUCKETS - max_exact)).astype(I32), N_BUCKETS - 1)
    return jnp.where(d < max_exact, d, large)


def _bias_kernel(rb_ref, o_ref, *, a, b, c):
    rows, cols = o_ref.shape[1], o_ref.shape[2]
    dist = (a * lax.broadcasted_iota(I32, (rows, cols), 0) + b * lax.broadcasted_iota(I32, (rows, cols), 1) + c)
    bucket = _bucket(dist)
    for h in range(DSA_HEADS):
        acc = jnp.zeros((rows, cols), F32)
        for n in range(N_BUCKETS):
            acc = jnp.where(bucket == n, rb_ref[n, h], acc)
        o_ref[h] = acc


def _bias_table(rel_bias, rows, cols, a, b, c):
    return pl.pallas_call(
        functools.partial(_bias_kernel, a=a, b=b, c=c),
        in_specs=[pl.BlockSpec(memory_space=pltpu.SMEM)],
        out_specs=pl.BlockSpec(memory_space=pltpu.VMEM),
        out_shape=jax.ShapeDtypeStruct((DSA_HEADS, rows, cols), F32),
    )(rel_bias)


def _loop2(n, body):
    n = jnp.maximum(n, 0)

    def pair(j, carry):
        body(2 * j)
        body(2 * j + 1)
        return carry

    lax.fori_loop(0, n // 2, pair, 0)

    @pl.when(n % 2 == 1)
    def _():
        body(n - 1)


def _colreduce(x, op):
    return op(x.reshape(x.shape[0] // 8, 8, x.shape[1]), axis=0)


def _dsa_prompt_kernel(rb_ref, qidx_ref, tail_ref, kidx_ref, q_ref, k_ref, v_ref, bias_ref, o_ref,
                       qs_scr, qd_scr, wb_scr, key_scr, sbq_scr, tie_scr, m_scr, p_scr, acc_scr, s_scr, *, n_sel, idx_bits):
    tq = q_ref.shape[0]
    kq = tq // KB
    qi = pl.program_id(1)
    krow = lax.broadcasted_iota(I32, (KB, tq), 0)
    qcol = lax.broadcasted_iota(I32, (KB, tq), 1)
    low64 = lax.broadcasted_iota(I32, (tq, LANES), 1) < 64

    tail = tail_ref[...]
    for h in range(IDX_HEADS):
        mine = low64 if h % 2 == 0 else jnp.logical_not(low64)
        zero = jnp.zeros((tq, LANES), BF16)
        qs_scr[h * tq:(h + 1) * tq, :] = jnp.where(mine, qidx_ref[:, (h // 2) * 128:(h // 2 + 1) * 128], zero)
        qd_scr[h * tq:(h + 1) * tq, :] = jnp.where(mine, q_ref[:, (h // 2) * 128:(h // 2 + 1) * 128], zero)
        wb_scr[h] = jnp.broadcast_to(tail[:, T_WIDX + h:T_WIDX + h + 1], (tq, LANES))

    def rows_of(c):
        return pl.ds(pl.multiple_of(c * tq, tq), tq)

    def score_chunk(c, in_tile):
        d = _dot_nt(qs_scr[...], kidx_ref[rows_of(c), :])
        rb = 64
        for t in range(kq):
            parts = []
            for r in range(0, tq, rb):
                sc = jnp.zeros((rb, KB), F32)
                for h in range(IDX_HEADS):
                    sc = sc + (jnp.maximum(d[h * tq + r:h * tq + r + rb, t * KB:(t + 1) * KB], 0.0)
                               * wb_scr[h, r:r + rb, :])
                parts.append(sc)
            sc_t = (jnp.concatenate(parts, axis=0) * IDX_SCALE).T
            if in_tile:
                sc_t = jnp.where(t * KB + krow <= qcol, sc_t, -jnp.inf)
            key_scr[c * kq + t] = _sort_key(sc_t)

    _loop2(qi, lambda c: score_chunk(c, False))
    score_chunk(qi, True)
    n_chunks = qi + 1

    def count(pred):
        part = 32
        def body(c, acc):
            for t in range(kq):
                kb = c * kq + t
                hit = jnp.where(pred(key_scr[kb], kb), 1.0, 0.0)
                acc = acc + jnp.sum(hit.reshape(KB // part, part, tq), axis=0)
            return acc
        acc = lax.fori_loop(0, n_chunks, body, jnp.zeros((part, tq), F32))
        return jnp.sum(_colreduce(acc, jnp.sum), axis=0, keepdims=True)

    thr = _kth_key(lambda cand: count(lambda k, kb: k >= cand), float(n_sel), (1, tq))
    cnt_gt = count(lambda k, kb: k > thr)
    cnt_ge = count(lambda k, kb: k >= thr)
    tie_scr[...] = jnp.full((1, tq), 2 ** 30, I32)
    excess = jnp.where((cnt_ge > float(n_sel)) & (thr > INT_MIN), 1.0, 0.0)

    @pl.when(jnp.max(excess) > 0.0)
    def _():
        need = float(n_sel) - cnt_gt

        def body(i, p):
            cand = p + (jnp.int32(1) << (idx_bits - 1 - i))
            c = count(lambda k, kb: (k == thr) & (kb * KB + krow < cand))
            return jnp.where(c < need, cand, p)

        tie_scr[...] = lax.fori_loop(0, idx_bits, body, jnp.zeros((1, tq), I32))

    last_tie = tie_scr[...]

    def mask_chunk(c, in_tile):
        for t in range(kq):
            kb = c * kq + t
            key = key_scr[kb]
            sel = (key > thr) | ((key == thr) & (kb * KB + krow <= last_tie))
            if in_tile:
                sel = sel & (t * KB + krow <= qcol)
            sbq_scr[c, :, t * KB:(t + 1) * KB] = jnp.where(sel, 0.0, NEG_BIG).T

    _loop2(qi, lambda c: mask_chunk(c, False))
    mask_chunk(qi, True)

    far_bias = [rb_ref[N_BUCKETS - 1, h] for h in range(DSA_HEADS)]

    def scores(c, g, kind):
        sg = _dot_nt(qd_scr[g * DSA_GROUP * tq:(g + 1) * DSA_GROUP * tq, :], k_ref[rows_of(c), g * 128:(g + 1) * 128])
        sb = sbq_scr[c]
        out = []
        for hh in range(DSA_GROUP):
            h = g * DSA_GROUP + hh
            cols = []
            for t in range(kq):
                s = sg[hh * tq:(hh + 1) * tq, t * KB:(t + 1) * KB] + sb[:, t * KB:(t + 1) * KB]
                if kind == "tile":
                    s = s + (bias_ref[1 + t, h] - far_bias[h])
                elif kind == "before" and t == kq - 1:
                    s = s + (bias_ref[0, h] - far_bias[h])
                cols.append(s)
            out.append(cols)
        return out

    m_scr[...] = jnp.full(m_scr.shape, NEG_BIG, F32)
    acc_scr[...] = jnp.zeros(acc_scr.shape, F32)

    def pass_max(c, kind):
        for g in range(DSA_KV_HEADS):
            for hh, cols in enumerate(scores(c, g, kind)):
                h = g * DSA_GROUP + hh
                m = m_scr[h]
                for t, s in enumerate(cols):
                    s_scr[c, h, :, t * KB:(t + 1) * KB] = s
                    m = jnp.maximum(m, s)
                m_scr[h] = m

    def pass_sum(c, kind):
        for g in range(DSA_KV_HEADS):
            for hh in range(DSA_GROUP):
                h = g * DSA_GROUP + hh
                m = m_scr[h]
                for t in range(kq):
                    s = s_scr[c, h, :, t * KB:(t + 1) * KB]
                    p_scr[g, hh * tq:(hh + 1) * tq, t * KB:(t + 1) * KB] = jnp.exp(s - m).astype(BF16)
            acc_scr[g] += _dot(p_scr[g], v_ref[rows_of(c), g * 128:(g + 1) * 128])

    def run(step):
        _loop2(qi - 1, lambda c: step(c, "far"))

        @pl.when(qi >= 1)
        def _():
            step(qi - 1, "before")

        step(qi, "tile")

    run(pass_max)
    for h in range(DSA_HEADS):
        m_scr[h] = jnp.broadcast_to(jnp.max(m_scr[h], axis=-1, keepdims=True), (tq, LANES))
    run(pass_sum)

    for p in range(DSA_HEADS // 2):
        g, hh = (2 * p) // DSA_GROUP, (2 * p) % DSA_GROUP
        even = acc_scr[g, hh * tq:(hh + 1) * tq, :]
        odd = acc_scr[g, (hh + 1) * tq:(hh + 2) * tq, :]
        o_ref[:, 128 * p:128 * p + 128] = jnp.where(low64, even / pltpu.roll(even, 64, 1),
                                                    pltpu.roll(odd, 64, 1) / odd)


def _dsa_prompt(rel_bias, qidx, tail, kidxdup, qdsa, kdup, vone, batch, seq, tq):
    assert MAX_DISTANCE <= KB, "keys two or more blocks back must all fall in the last bucket"
    nq = seq // tq
    nkb = seq // KB
    kq = tq // KB
    n_sel = min(TOPK_MAX, seq // 4)
    idx_bits = max(1, (seq - 1).bit_length())
    bias = jnp.stack([_bias_table(rel_bias, tq, KB, 1, -1, KB - t * KB) for t in range(kq + 1)])
    kern = functools.partial(_dsa_prompt_kernel, n_sel=n_sel, idx_bits=idx_bits)
    qblk = lambda w: pl.BlockSpec((tq, w), lambda b, i: (b * nq + i, 0))
    seqblk = lambda w: pl.BlockSpec((seq, w), lambda b, i: (b, 0))
    return pl.pallas_call(
        kern,
        grid=(batch, nq),
        in_specs=[pl.BlockSpec(memory_space=pltpu.SMEM),
                  qblk(512), qblk(LANES), seqblk(LANES), qblk(512), seqblk(256), seqblk(256), _full(bias.shape)],
        out_specs=qblk(512),
        out_shape=jax.ShapeDtypeStruct((batch * seq, 512), F32),
        scratch_shapes=[pltpu.VMEM((IDX_HEADS * tq, LANES), BF16), pltpu.VMEM((DSA_HEADS * tq, LANES), BF16),
                        pltpu.VMEM((IDX_HEADS, tq, LANES), F32),
                        pltpu.VMEM((nkb, KB, tq), I32), pltpu.VMEM((nq, tq, tq), F32),
                        pltpu.VMEM((1, tq), I32), pltpu.VMEM((DSA_HEADS, tq, LANES), F32),
                        pltpu.VMEM((DSA_KV_HEADS, DSA_GROUP * tq, tq), BF16),
                        pltpu.VMEM((DSA_KV_HEADS, DSA_GROUP * tq, LANES), F32),
                        pltpu.VMEM((nq, DSA_HEADS, tq, tq), F32)],
        compiler_params=_cparams(("parallel", "arbitrary")),
    )(rel_bias, qidx, tail, kidxdup, qdsa, kdup, vone, bias)


def _fetch_pages(cache_ref, pt_ref, buf, sem, step, n_steps, locate, pps):
    def copy(slot, i, page):
        return pltpu.make_async_copy(cache_ref.at[0, page], buf.at[slot, i], sem.at[slot])

    def start(st, slot):
        b, first = locate(st)
        for i in range(pps):
            copy(slot, i, pt_ref[b, first + i]).start()

    @pl.when(step == 0)
    def _():
        start(step, 0)

    @pl.when(step + 1 < n_steps)
    def _():
        start(step + 1, (step + 1) % 2)

    slot = step % 2
    for i in range(pps):
        copy(slot, i, 0).wait()
    return slot


def _page_scratch(pps, width):
    return [pltpu.VMEM((2, pps, width, PAGE_SIZE), F32), pltpu.SemaphoreType.DMA((2,))]


def _pages_feature_major(cache):
    c = jnp.moveaxis(cache, 2, -1)
    return c.reshape(cache.shape[0], cache.shape[1], -1, PAGE_SIZE)


def _sample_score_kernel(pt_ref, q_ref, w_ref, cache_ref, o_ref, buf, sem, kb_scr, *, pps, grp):
    bo, c, bi = pl.program_id(0), pl.program_id(1), pl.program_id(2)
    n_chunks = pl.num_programs(1)
    step = (bo * n_chunks + c) * grp + bi

    def locate(st):
        return (st // (n_chunks * grp)) * grp + st % grp, ((st // grp) % n_chunks) * pps

    slot = _fetch_pages(cache_ref, pt_ref, buf, sem, step, pl.num_programs(0) * n_chunks * grp, locate, pps)
    for i in range(pps):
        kb_scr[:, i * PAGE_SIZE:(i + 1) * PAGE_SIZE] = buf[slot, i].astype(BF16)
    d = _dot(q_ref[...], kb_scr[...])
    sc = jnp.sum(jnp.maximum(d, 0.0) * w_ref[...], axis=0, keepdims=True) * IDX_SCALE
    o_ref[pl.ds(bi, 1), :] = sc


def _sample_scores(page_table, qidx3, widx3, cache_idx_t, pps):
    dec, n_pages = page_table.shape
    n_chunks = n_pages // pps
    grp = 8
    grid_spec = pltpu.PrefetchScalarGridSpec(
        num_scalar_prefetch=1,
        grid=(dec // grp, n_chunks, grp),
        in_specs=[pl.BlockSpec((None, IDX_HEADS, IDX_DIM), lambda bo, c, bi, pt: (bo * grp + bi, 0, 0)),
                  pl.BlockSpec((None, IDX_HEADS, 1), lambda bo, c, bi, pt: (bo * grp + bi, 0, 0)),
                  pl.BlockSpec(memory_space=pl.ANY)],
        out_specs=pl.BlockSpec((grp, pps * PAGE_SIZE), lambda bo, c, bi, pt: (bo, c)),
        scratch_shapes=_page_scratch(pps, IDX_DIM) + [pltpu.VMEM((IDX_DIM, pps * PAGE_SIZE), BF16)],
    )
    return pl.pallas_call(
        functools.partial(_sample_score_kernel, pps=pps, grp=grp), grid_spec=grid_spec,
        out_shape=jax.ShapeDtypeStruct((dec, n_pages * PAGE_SIZE), F32),
        compiler_params=_cparams(("arbitrary", "arbitrary", "arbitrary")),
    )(page_table, qidx3, widx3, cache_idx_t)


def _sample_select_kernel(sc_ref, qidx_ref, tail_ref, selp_ref, seln_ref, key_scr, *, n_sel, idx_bits):
    rows, past = sc_ref.shape
    nblk = past // LANES
    lane = lax.broadcasted_iota(I32, (rows, LANES), 1)
    tail = tail_ref[...]
    kidx2 = jnp.where(lane < 64, tail, pltpu.roll(tail, 64, 1))
    sc_new = jnp.zeros((rows, 1), F32)
    for h in range(IDX_HEADS):
        prod = qidx_ref[:, (h // 2) * 128:(h // 2 + 1) * 128].astype(F32) * kidx2.astype(BF16).astype(F32)
        mine = (lane < 64) if h % 2 == 0 else (lane >= 64)
        dot = jnp.sum(jnp.where(mine, prod, 0.0), axis=-1, keepdims=True)
        sc_new = sc_new + jnp.maximum(dot, 0.0) * tail[:, T_WIDX + h:T_WIDX + h + 1]
    key_new = _sort_key(sc_new * IDX_SCALE)

    for j in range(nblk):
        key_scr[j] = _sort_key(sc_ref[:, j * LANES:(j + 1) * LANES])

    def count(pred, pred_new):
        def body(j, c):
            return c + jnp.where(pred(key_scr[j], j), 1.0, 0.0)
        c = lax.fori_loop(0, nblk, body, jnp.zeros((rows, LANES), F32), unroll=4)
        return jnp.sum(c, axis=-1, keepdims=True) + jnp.where(pred_new, 1.0, 0.0)

    def wide(col):
        return jnp.broadcast_to(col, (rows, LANES))

    def count_ge(cand):
        cand_w = wide(cand)
        return count(lambda k, j: k >= cand_w, key_new >= cand)

    thr = _kth_key(count_ge, float(n_sel), (rows, 1))
    thr_w = wide(thr)
    cnt_gt = count(lambda k, j: k > thr_w, key_new > thr)
    need = float(n_sel) - cnt_gt

    def body(i, p):
        cand = p + (jnp.int32(1) << (idx_bits - 1 - i))
        cand_w = wide(cand)
        c = count(lambda k, j: (k == thr_w) & (j * LANES + lane < cand_w), (key_new == thr) & (past < cand))
        return jnp.where(c < need, cand, p)

    last_tie = lax.fori_loop(0, idx_bits, body, jnp.zeros((rows, 1), I32))
    last_w = wide(last_tie)

    for j in range(nblk):
        k = key_scr[j]
        sel = (k > thr_w) | ((k == thr_w) & (j * LANES + lane <= last_w))
        selp_ref[:, j * LANES:(j + 1) * LANES] = jnp.where(sel, 1.0, 0.0)
    sel_new = (key_new > thr) | ((key_new == thr) & (past <= last_tie))
    seln_ref[...] = jnp.broadcast_to(jnp.where(sel_new, 1.0, 0.0), (rows, LANES))


def _sample_select(scores, qidx, tail):
    dec, past = scores.shape
    n_sel = min(TOPK_MAX, (past + 1) // 4)
    idx_bits = max(1, past.bit_length())
    rows = 64 if dec % 64 == 0 else dec
    kern = functools.partial(_sample_select_kernel, n_sel=n_sel, idx_bits=idx_bits)
    return pl.pallas_call(
        kern,
        grid=(dec // rows,),
        in_specs=[pl.BlockSpec((rows, past), lambda i: (i, 0)), pl.BlockSpec((rows, 512), lambda i: (i, 0)),
                  pl.BlockSpec((rows, LANES), lambda i: (i, 0))],
        out_specs=[pl.BlockSpec((rows, past), lambda i: (i, 0)), pl.BlockSpec((rows, LANES), lambda i: (i, 0))],
        out_shape=[jax.ShapeDtypeStruct((dec, past), F32), jax.ShapeDtypeStruct((dec, LANES), F32)],
        scratch_shapes=[pltpu.VMEM((past // LANES, rows, LANES), I32)],
        compiler_params=_cparams(("parallel",)),
    )(scores, qidx, tail)


def _mla_sample_kernel(pt_ref, qn_ref, qr_ref, rown_ref, wuk_ref, wuv_ref, gk_ref, cache_ref, o_ref,
                       buf, sem, lat_scr, kr_scr, a_scr, m_scr, l_scr, acc_scr, *, pps):
    c = pl.program_id(1)
    n_chunks = pl.num_programs(1)
    heads = MLA_HEADS
    slot = _fetch_pages(cache_ref, pt_ref, buf, sem, pl.program_id(0) * n_chunks + c,
                        pl.num_programs(0) * n_chunks, lambda st: (st // n_chunks, (st % n_chunks) * pps), pps)

    @pl.when(c == 0)
    def _():
        m_scr[...] = jnp.full(m_scr.shape, -jnp.inf, F32)
        l_scr[...] = jnp.zeros(l_scr.shape, F32)
        acc_scr[...] = jnp.zeros(acc_scr.shape, F32)
        qg = qn_ref[...] * gk_ref[...]
        own = (lax.broadcasted_iota(I32, (heads, heads * MLA_NOPE), 1) // MLA_NOPE
               == lax.broadcasted_iota(I32, (heads, heads * MLA_NOPE), 0))
        n_up = heads * MLA_NOPE
        a_scr[0:n_up, :] = wuk_ref[...]
        a = _dot(jnp.where(own, qg, 0.0).astype(BF16), wuk_ref[...]).astype(BF16)
        a_scr[n_up:n_up + 16, :] = jnp.concatenate([a, jnp.zeros_like(a)], axis=0)

    qr = qr_ref[...]

    def scores(latb, krb):
        n_up = heads * MLA_NOPE
        up = _dot(a_scr[...], latb)
        knt = up[0:n_up, :]
        ms = jnp.mean((knt * knt).reshape(heads, MLA_NOPE, latb.shape[1]), axis=1)
        return (up[n_up:n_up + heads, :] * lax.rsqrt(ms + EPS) + _dot(qr, krb)) * MLA_SCALE

    def update(latb, krb, n_valid):
        n = latb.shape[1]
        sub = min(n, 512)
        s = jnp.concatenate([scores(latb[:, j:j + sub], krb[:, j:j + sub]) for j in range(0, n, sub)], axis=1)
        if n_valid < n:
            s = jnp.where(lax.broadcasted_iota(I32, s.shape, 1) < n_valid, s, -jnp.inf)
        m_old = m_scr[...]
        m_new = jnp.maximum(m_old, jnp.max(s, axis=-1, keepdims=True))
        alpha = jnp.exp(m_old - m_new)
        p = jnp.exp(s - m_new)
        l_scr[...] = alpha * l_scr[...] + jnp.sum(p, axis=-1, keepdims=True)
        m_scr[...] = m_new
        acc_scr[...] = alpha * acc_scr[...] + _dot_nt(p.astype(BF16), latb)

    for i in range(pps):
        lat_scr[:, i * PAGE_SIZE:(i + 1) * PAGE_SIZE] = buf[slot, i, 0:MLA_KV_RANK, :].astype(BF16)
        kr_scr[:, i * PAGE_SIZE:(i + 1) * PAGE_SIZE] = buf[slot, i, MLA_KV_RANK:MLA_ROW, :].astype(BF16)
    update(lat_scr[...], kr_scr[...], lat_scr.shape[1])

    @pl.when(c == n_chunks - 1)
    def _():
        rn = jnp.broadcast_to(rown_ref[...], (MLA_ROW, PAGE_SIZE))
        update(rn[0:MLA_KV_RANK, :].astype(BF16), rn[MLA_KV_RANK:MLA_ROW, :].astype(BF16), 1)
        o_lat = (acc_scr[...] / l_scr[...]).astype(BF16)
        full = _dot(o_lat, wuv_ref[...])
        hd = lax.broadcasted_iota(I32, full.shape, 1) // MLA_V
        own = hd == lax.broadcasted_iota(I32, full.shape, 0)
        o_ref[...] = jnp.sum(jnp.where(own, full, 0.0), axis=0, keepdims=True)


def _dsa_sample_kernel(pt_ref, q_ref, kvn_ref, selp_ref, seln_ref, biasp_ref, biasn_ref, cache_ref, o_ref,
                       buf, sem, k_scr, v_scr, m_scr, l_scr, acc_scr, *, pps):
    c = pl.program_id(1)
    n_chunks = pl.num_programs(1)
    slot = _fetch_pages(cache_ref, pt_ref, buf, sem, pl.program_id(0) * n_chunks + c,
                        pl.num_programs(0) * n_chunks, lambda st: (st // n_chunks, (st % n_chunks) * pps), pps)

    @pl.when(c == 0)
    def _():
        m_scr[...] = jnp.full(m_scr.shape, NEG_BIG, F32)
        l_scr[...] = jnp.zeros(l_scr.shape, F32)
        acc_scr[...] = jnp.zeros(acc_scr.shape, F32)

    q = q_ref[...]

    def update(s, sel, pv_of):
        sh = jnp.where(sel, s, NEG_BIG)
        m_old = m_scr[...]
        m_new = jnp.maximum(m_old, jnp.max(sh, axis=-1, keepdims=True))
        alpha = jnp.exp(m_old - m_new)
        p = jnp.where(sel, jnp.exp(sh - m_new), 0.0)
        l_scr[...] = alpha * l_scr[...] + jnp.sum(p, axis=-1, keepdims=True)
        m_scr[...] = m_new
        acc_scr[...] = alpha * acc_scr[...] + pv_of(p)

    for i in range(pps):
        k_scr[:, i * PAGE_SIZE:(i + 1) * PAGE_SIZE] = buf[slot, i, 0:128, :].astype(BF16)
        v_scr[:, i * PAGE_SIZE:(i + 1) * PAGE_SIZE] = buf[slot, i, 128:256, :].astype(BF16)
    s = _dot(q.astype(BF16), k_scr[...]) + biasp_ref[...]
    update(s, selp_ref[...] > 0.5, lambda p: _dot_nt(p.astype(BF16), v_scr[...]))

    @pl.when(c == n_chunks - 1)
    def _():
        kvn = kvn_ref[...]
        kn = kvn[:, 0:128].astype(BF16).astype(F32)
        vn = kvn[:, 128:256].astype(BF16).astype(F32)
        s_new = jnp.sum(q.astype(F32) * kn, axis=-1, keepdims=True) + biasn_ref[...]
        update(s_new, seln_ref[:, 0:1] > 0.5, lambda p: p.astype(BF16).astype(F32) * vn)
        o = acc_scr[...] / l_scr[...]
        lane = lax.broadcasted_iota(I32, o.shape, 1)
        row = lax.broadcasted_iota(I32, o.shape, 0)
        own = (lane // DSA_HEAD_DIM) == (row // DSA_GROUP)
        o_ref[...] = jnp.where(own, o, 0.0)


_N_MLA_IN, _N_MLA_SCRATCH = 7, 8
_N_DSA_IN = 7


def _decode_kernel(pt_ref, *refs, pps):
    mla_in = refs[:_N_MLA_IN]
    dsa_in = refs[_N_MLA_IN:_N_MLA_IN + _N_DSA_IN]
    o_mla, o_dsa = refs[_N_MLA_IN + _N_DSA_IN:_N_MLA_IN + _N_DSA_IN + 2]
    scratch = refs[_N_MLA_IN + _N_DSA_IN + 2:]
    _mla_sample_kernel(pt_ref, *mla_in, o_mla, *scratch[:_N_MLA_SCRATCH], pps=pps)
    _dsa_sample_kernel(pt_ref, *dsa_in, o_dsa, *scratch[_N_MLA_SCRATCH:], pps=pps)


def _decode_attention(page_table, qn_row, qr, row_new, wuk_t, wuv, gk_row, cache_mla,
                      qs, kv_new, selp, seln, bias_past, bias_new, cache_kv4, pps):
    dec, n_pages = page_table.shape
    n_chunks = n_pages // pps
    chunk = pps * PAGE_SIZE
    per_b = lambda *shape: pl.BlockSpec((None,) + shape, lambda b, c, pt: (b,) + (0,) * len(shape))
    const = lambda a: pl.BlockSpec(a.shape, lambda b, c, pt: (0,) * a.ndim)
    grid_spec = pltpu.PrefetchScalarGridSpec(
        num_scalar_prefetch=1,
        grid=(dec, n_chunks),
        in_specs=[per_b(1, MLA_HEADS * MLA_NOPE), per_b(MLA_HEADS, MLA_ROPE), per_b(MLA_ROW, 1),
                  const(wuk_t), const(wuv), const(gk_row), pl.BlockSpec(memory_space=pl.ANY),
                  per_b(DSA_HEADS, LANES), per_b(1, 256),
                  pl.BlockSpec((None, 1, chunk), lambda b, c, pt: (b, 0, c)), per_b(1, LANES),
                  pl.BlockSpec((DSA_HEADS, chunk), lambda b, c, pt: (0, c)), const(bias_new),
                  pl.BlockSpec(memory_space=pl.ANY)],
        out_specs=[per_b(1, MLA_HEADS * MLA_V), per_b(DSA_HEADS, LANES)],
        scratch_shapes=_page_scratch(pps, MLA_ROW) + [
            pltpu.VMEM((MLA_KV_RANK, chunk), BF16), pltpu.VMEM((MLA_ROPE, chunk), BF16),
            pltpu.VMEM((MLA_HEADS * MLA_NOPE + 16, MLA_KV_RANK), BF16),
            pltpu.VMEM((MLA_HEADS, 1), F32), pltpu.VMEM((MLA_HEADS, 1), F32),
            pltpu.VMEM((MLA_HEADS, MLA_KV_RANK), F32)]
        + _page_scratch(pps, 256) + [
            pltpu.VMEM((128, chunk), BF16), pltpu.VMEM((128, chunk), BF16),
            pltpu.VMEM((DSA_HEADS, 1), F32), pltpu.VMEM((DSA_HEADS, 1), F32),
            pltpu.VMEM((DSA_HEADS, LANES), F32)],
    )
    return pl.pallas_call(
        functools.partial(_decode_kernel, pps=pps), grid_spec=grid_spec,
        out_shape=[jax.ShapeDtypeStruct((dec, 1, MLA_HEADS * MLA_V), F32),
                   jax.ShapeDtypeStruct((dec, DSA_HEADS, LANES), F32)],
        compiler_params=_cparams(("arbitrary", "arbitrary")),
    )(page_table, qn_row, qr, row_new, wuk_t, wuv, gk_row, cache_mla,
      qs, kv_new, selp, seln, bias_past, bias_new, cache_kv4)


def _out_kernel(x_ref, oa_ref, ob_ref, gm_ref, sf_ref, scf_ref, gf_ref, gout_ref, gffn_ref,
                wout_ref, wg_ref, wu_ref, wdown_ref, y_ref, *, ff_chunk):
    def rms(v, g):
        return v * lax.rsqrt(jnp.mean(v * v, axis=-1, keepdims=True) + EPS) * g

    half = oa_ref.shape[1]
    na = rms(oa_ref[...], gout_ref[:, 0:half]).astype(BF16)
    nb = rms(ob_ref[...], gout_ref[:, half:2 * half]).astype(BF16)
    mix = _dot(na, wout_ref[0:half, :]) + _dot(nb, wout_ref[half:2 * half, :])
    x1 = x_ref[...] + gm_ref[...] * mix
    hb = (rms(x1, gffn_ref[...]) * (1.0 + scf_ref[...]) + sf_ref[...]).astype(BF16)
    acc = jnp.zeros(x1.shape, F32)
    for j in range(D_FF // ff_chunk):
        g = _dot(hb, wg_ref[:, j * ff_chunk:(j + 1) * ff_chunk])
        u = _dot(hb, wu_ref[:, j * ff_chunk:(j + 1) * ff_chunk])
        a = (g / (1.0 + jnp.exp(-g))) * u
        acc = acc + _dot(a.astype(BF16), wdown_ref[j * ff_chunk:(j + 1) * ff_chunk, :])
    y_ref[...] = x1 + gf_ref[...] * acc


def _out_stage(x2d, oa, ob, gate_m, shift_f, scale_f, gate_f, wp, tm, per_token, tiles_per_seq):
    t_total = x2d.shape[0]
    if per_token:
        mod_spec = pl.BlockSpec((tm, D_MODEL), lambda i: (i, 0))
    else:
        mod_spec = pl.BlockSpec((None, 1, D_MODEL), lambda i: (i // tiles_per_seq, 0, 0))
    consts = [wp["g_out"], wp["g_ffn"], wp["w_out"], wp["w_gate"], wp["w_up"], wp["w_down"]]
    tok = lambda w: pl.BlockSpec((tm, w), lambda i: (i, 0))
    return pl.pallas_call(
        functools.partial(_out_kernel, ff_chunk=256),
        grid=(t_total // tm,),
        in_specs=[tok(D_MODEL), tok(512), tok(512), mod_spec, mod_spec, mod_spec, mod_spec]
                 + [_full(c.shape) for c in consts],
        out_specs=tok(D_MODEL),
        out_shape=jax.ShapeDtypeStruct((t_total, D_MODEL), F32),
        compiler_params=_cparams(("parallel",)),
    )(x2d, oa, ob, gate_m, shift_f, scale_f, gate_f, *consts)


def _prep_weights(w_in, g_norm_mix, g_norm_ffn, g_q_lat, w_uq, g_kv_lat, w_ukv, g_mla_q_nope, g_mla_q_rope,
                  g_mla_k_nope, g_mla_k_rope, g_dsa_q, g_dsa_k, g_out, w_out, w_ffn_in, w_ffn_out):
    splits = np.cumsum([MLA_Q_RANK, MLA_KV_RANK, MLA_ROPE, 512, 128, 128, 512, IDX_DIM, IDX_HEADS])
    s = [0] + splits.tolist()
    col = lambda i: w_in[:, s[i]:s[i + 1]]
    pad = jnp.zeros((D_MODEL, C_END - C_TAIL - IDX_DIM - MLA_ROPE - IDX_HEADS), w_in.dtype)
    w_in_r = jnp.concatenate([col(0), col(1), col(3), col(4), col(5), col(6), col(7), col(2), col(8), pad], axis=1)
    zq = jnp.zeros((MLA_Q_RANK, MLA_HEADS, LANES - MLA_NOPE - MLA_ROPE), w_uq.dtype)
    w_uq_cat = jnp.concatenate([w_uq, zq], axis=2).reshape(MLA_Q_RANK, MLA_HEADS * LANES)
    zk = jnp.zeros((MLA_KV_RANK, MLA_HEADS, LANES - MLA_NOPE), w_ukv.dtype)
    w_uk_cat = jnp.concatenate([w_ukv[:, :, :MLA_NOPE], zk], axis=2).reshape(MLA_KV_RANK, MLA_HEADS * LANES)
    w_uv = w_ukv[:, :, MLA_NOPE:].reshape(MLA_KV_RANK, MLA_HEADS * MLA_V)
    zv = jnp.zeros((MLA_KV_RANK, MLA_HEADS, LANES - MLA_V), w_ukv.dtype)
    w_uv_pad = jnp.concatenate([w_ukv[:, :, MLA_NOPE:], zv], axis=2).reshape(MLA_KV_RANK, MLA_HEADS * LANES)
    w_uk_t = w_ukv[:, :, :MLA_NOPE].reshape(MLA_KV_RANK, MLA_HEADS * MLA_NOPE).T
    z32 = jnp.zeros((LANES - MLA_NOPE - MLA_ROPE,), F32)
    z64 = jnp.zeros((LANES - MLA_NOPE,), F32)
    g_qcat = jnp.tile(jnp.concatenate([g_mla_q_nope, g_mla_q_rope, z32]), MLA_HEADS)
    g_kcat = jnp.tile(jnp.concatenate([g_mla_k_nope, z64]), MLA_HEADS)
    g_tail = jnp.concatenate([jnp.zeros((T_ROPE,), F32), g_mla_k_rope, z32])
    row = lambda v: v.reshape(1, -1).astype(F32)
    return {
        "g_mix": row(g_norm_mix), "g_ffn": row(g_norm_ffn), "w_in": w_in_r.astype(BF16),
        "g_qlat": row(g_q_lat), "w_uq": w_uq_cat.astype(BF16), "g_qcat": row(g_qcat),
        "g_kvlat": row(g_kv_lat), "w_ukv": jnp.concatenate([w_uk_cat, w_uv_pad], axis=1).astype(BF16),
        "g_kcat": row(g_kcat), "g_tail": row(g_tail),
        "g_q": row(jnp.tile(g_dsa_q, DSA_HEADS)), "g_k": row(jnp.tile(g_dsa_k, DSA_KV_HEADS)),
        "m_cat": _group_matrix(256, [(0, 64), (64, 96), (128, 192), (192, 224)]),
        "m_64": _group_matrix(256, [(0, 64), (64, 128), (128, 192), (192, 256)]),
        "w_uk_t": w_uk_t.astype(BF16), "w_uv": w_uv.astype(BF16), "g_k_nope": g_mla_k_nope,
        "g_out": row(g_out), "w_out": w_out.astype(BF16),
        "w_gate": w_ffn_in[:, :D_FF].astype(BF16), "w_up": w_ffn_in[:, D_FF:].astype(BF16),
        "w_down": w_ffn_out.astype(BF16),
    }


def _rope_tables(pos):
    freq = ROPE_THETA ** (-jnp.arange(HALF_ROPE, dtype=F32) / HALF_ROPE)
    ang = pos.astype(F32)[:, None] * freq[None, :]
    cos, sin = jnp.cos(ang), jnp.sin(ang)
    n = pos.shape[0]
    ones = jnp.ones((n, T_ROPE), F32)
    zeros = jnp.zeros((n, T_ROPE), F32)
    tail1 = jnp.ones((n, LANES - T_ROPE - MLA_ROPE), F32)
    tail0 = jnp.zeros((n, LANES - T_ROPE - MLA_ROPE), F32)
    return (jnp.concatenate([ones, cos, cos, tail1], axis=1),
            jnp.concatenate([zeros, -sin, sin, tail0], axis=1))


def _pick_tile(n, prefs):
    for t in prefs:
        if n % t == 0:
            return t
    return n


def kernel(x_prompt, x_sample, c_prompt, c_sample, cache_mla, cache_kv, cache_idx, page_table, rel_bias, w_ada, b_ada, g_norm_mix, g_norm_ffn, w_in, g_q_lat, w_uq, g_kv_lat, w_ukv, g_mla_q_nope, g_mla_q_rope, g_mla_k_nope, g_mla_k_rope, g_dsa_q, g_dsa_k, g_out, w_out, w_ffn_in, w_ffn_out):
    assert w_ada.shape[0] == 1 and x_sample.shape[1] == 1, "one layer, one new token per sample"
    batch, seq, _ = x_prompt.shape
    dec = x_sample.shape[0]
    n_pages = page_table.shape[1]
    past = n_pages * PAGE_SIZE
    wp = _prep_weights(w_in[0], g_norm_mix[0], g_norm_ffn[0], g_q_lat[0], w_uq[0], g_kv_lat[0], w_ukv[0],
                       g_mla_q_nope[0], g_mla_q_rope[0], g_mla_k_nope[0], g_mla_k_rope[0], g_dsa_q[0], g_dsa_k[0],
                       g_out[0], w_out[0], w_ffn_in[0], w_ffn_out[0])

    mod = _ada(jnp.concatenate([c_prompt, c_sample], axis=0), w_ada[0], b_ada[0])
    mods_p = [m.reshape(batch, 1, D_MODEL) for m in jnp.split(mod[:batch], 6, axis=-1)]
    mods_s = jnp.split(mod[batch:], 6, axis=-1)

    tm = _pick_tile(seq, (512, 256, 128))
    xp2 = x_prompt.reshape(batch * seq, D_MODEL)
    cos_p, sin_p = _rope_tables(jnp.arange(seq))
    pin = _in_stage(xp2, mods_p[1], mods_p[0], cos_p, sin_p, wp, tm, per_token=False)
    o_mla_p = _mla_prompt(pin["qcat"], pin["kcat"], pin["vmla"], batch, seq,
                          _pick_tile(seq, (256, 128)), _pick_tile(seq, (512, 256, 128)))
    o_dsa_p = _dsa_prompt(rel_bias, pin["qidx"], pin["tail"], pin["kidxdup"], pin["qdsa"], pin["kdup"], pin["vone"],
                          batch, seq, _pick_tile(seq, (256, 128)))
    y_p = _out_stage(xp2, o_mla_p, o_dsa_p, mods_p[2], mods_p[3], mods_p[4], mods_p[5], wp, tm,
                     per_token=False, tiles_per_seq=seq // tm)

    xs2 = x_sample.reshape(dec, D_MODEL)
    cos_s, sin_s = _rope_tables(jnp.full((1,), past))
    sin_ = _in_stage(xs2, mods_s[1], mods_s[0], cos_s, sin_s, wp, dec, per_token=True)
    pps_wide = _pick_tile(n_pages, (64, 32, 16, 8, 4, 2))
    qidx3 = sin_["qidx"].reshape(dec, IDX_HEADS, IDX_DIM)
    widx3 = sin_["tail"][:, T_WIDX:T_WIDX + IDX_HEADS].reshape(dec, IDX_HEADS, 1)
    scores = _sample_scores(page_table, qidx3, widx3, _pages_feature_major(cache_idx), pps_wide)
    selp, seln = _sample_select(scores, sin_["qidx"], sin_["tail"])
    qc = sin_["qcat"].reshape(dec, MLA_HEADS, LANES)
    qn_row = qc[:, :, 0:MLA_NOPE].astype(F32).reshape(dec, 1, MLA_HEADS * MLA_NOPE)
    qr = qc[:, :, T_ROPE:T_ROPE + MLA_ROPE]
    gk_row = jnp.tile(wp["g_k_nope"], MLA_HEADS).reshape(1, MLA_HEADS * MLA_NOPE)
    qd = sin_["qdsa"].astype(F32).reshape(dec, DSA_HEADS, DSA_HEAD_DIM)
    on_c = (jnp.arange(DSA_KV_HEADS)[None, :] == (jnp.arange(DSA_HEADS) // DSA_GROUP)[:, None]).astype(F32)
    qd2 = (qd[:, :, None, :] * on_c[None, :, :, None]).reshape(dec, DSA_HEADS, LANES)
    bias_past = _bias_table(rel_bias, 1, past, 0, -1, past).reshape(DSA_HEADS, past)
    bias_new = _bias_table(rel_bias, 1, LANES, 0, 0, 0)[:, 0, 0:1]
    o_mla_s, o8 = _decode_attention(
        page_table, qn_row, qr, sin_["row"].reshape(dec, MLA_ROW, 1), wp["w_uk_t"], wp["w_uv"], gk_row,
        _pages_feature_major(cache_mla),
        qd2, sin_["kv"].reshape(dec, 1, 256), selp.reshape(dec, 1, past), seln.reshape(dec, 1, LANES),
        bias_past, bias_new, _pages_feature_major(cache_kv), pps_wide)
    o_mla_s = o_mla_s.reshape(dec, MLA_HEADS * MLA_V)
    o_dsa_s = o8.reshape(dec, DSA_HEADS, DSA_KV_HEADS, DSA_HEAD_DIM).sum(axis=2).reshape(dec, DSA_HEADS * DSA_HEAD_DIM)
    y_s = _out_stage(xs2, o_mla_s, o_dsa_s, mods_s[2], mods_s[3], mods_s[4], mods_s[5], wp, dec,
                     per_token=True, tiles_per_seq=1)

    return (y_p.reshape(batch, seq, D_MODEL), y_s.reshape(dec, 1, D_MODEL),
            jnp.swapaxes(pin["row"], 1, 2)[None],
            jnp.moveaxis(pin["kv"].reshape(batch, 2, DSA_KV_HEADS, DSA_HEAD_DIM, seq), -1, 1)[None],
            jnp.swapaxes(pin["kidx"], 1, 2)[None],
            sin_["row"].reshape(1, dec, 1, MLA_ROW),
            sin_["kv"].reshape(1, dec, 1, 2, DSA_KV_HEADS, DSA_HEAD_DIM),
            sin_["kidx"].reshape(1, dec, 1, IDX_DIM))
```

```python
import functools
import math

import numpy as np
import jax
import jax.numpy as jnp
from jax import lax
from jax.experimental import pallas as pl
from jax.experimental.pallas import tpu as pltpu

D_MODEL = 1024
PAGE_SIZE = 128
MLA_HEADS = 8
MLA_NOPE = 64
MLA_ROPE = 32
MLA_V = 64
MLA_Q_RANK = 384
MLA_KV_RANK = 256
MLA_ROW = MLA_KV_RANK + MLA_ROPE
DSA_HEADS = 8
DSA_KV_HEADS = 2
DSA_HEAD_DIM = 64
DSA_GROUP = DSA_HEADS // DSA_KV_HEADS
IDX_HEADS = 8
IDX_DIM = 64
TOPK_MAX = 256
N_BUCKETS = 32
MAX_DISTANCE = 128
ROPE_THETA = 10000.0
D_FF = ((8 * D_MODEL + 3 * 256 - 1) // (3 * 256)) * 256
EPS = 1e-6
MLA_SCALE = (MLA_NOPE + MLA_ROPE) ** -0.5
DSA_SCALE = DSA_HEAD_DIM ** -0.5
IDX_SCALE = (IDX_DIM * IDX_HEADS) ** -0.5

LANES = 128
VMEM_LIMIT = 56 * 1024 * 1024

F32 = jnp.float32
BF16 = jnp.bfloat16
I32 = jnp.int32
NEG_BIG = -1e30
INT_MIN = -(2 ** 31)

C_QLAT = 0
C_KVLAT = C_QLAT + MLA_Q_RANK
C_Q = C_KVLAT + MLA_KV_RANK
C_K = C_Q + DSA_HEADS * DSA_HEAD_DIM
C_V = C_K + DSA_KV_HEADS * DSA_HEAD_DIM
C_QIDX = C_V + DSA_KV_HEADS * DSA_HEAD_DIM
C_TAIL = C_QIDX + IDX_HEADS * IDX_DIM
C_END = C_TAIL + LANES
T_ROPE = IDX_DIM
T_WIDX = IDX_DIM + MLA_ROPE
HALF_ROPE = MLA_ROPE // 2
KB = 128


def _dot(a, b):
    return jnp.dot(a, b, preferred_element_type=F32)


def _dot_nt(a, b):
    return lax.dot_general(a, b, (((1,), (1,)), ((), ())), preferred_element_type=F32)


def _split(a):
    hi = a.astype(BF16)
    lo = (a - hi.astype(F32)).astype(BF16)
    return hi, lo


def _dot3(a, b):
    ah, al = _split(a)
    bh, bl = _split(b)
    return _dot(ah, bh) + (_dot(al, bh) + _dot(ah, bl))


def _cparams(sem):
    return pltpu.CompilerParams(dimension_semantics=sem, vmem_limit_bytes=VMEM_LIMIT)


def _full(shape):
    n = len(shape)
    return pl.BlockSpec(shape, lambda *a, _n=n: (0,) * _n, pipeline_mode=pl.Buffered(1))


def _ada_kernel(c_ref, w_ref, b_ref, o_ref):
    c = c_ref[...]
    s = c / (1.0 + jnp.exp(-c))
    o_ref[...] = _dot3(s, w_ref[...]) + b_ref[...]


def _ada(c, w, b):
    n = c.shape[0]
    return pl.pallas_call(
        _ada_kernel,
        grid=(6,),
        in_specs=[pl.BlockSpec((n, D_MODEL), lambda j: (0, 0)),
                  pl.BlockSpec((D_MODEL, D_MODEL), lambda j: (0, j)),
                  pl.BlockSpec((1, D_MODEL), lambda j: (0, j))],
        out_specs=pl.BlockSpec((n, D_MODEL), lambda j: (0, j)),
        out_shape=jax.ShapeDtypeStruct((n, 6 * D_MODEL), F32),
        compiler_params=_cparams(("arbitrary",)),
    )(c, w, b.reshape(1, -1))


def _group_matrix(width, groups):
    g = np.zeros((width, width), np.float32)
    for lo, hi in groups:
        g[lo:hi, lo:hi] = 1.0 / (hi - lo)
    return jnp.asarray(g, BF16)


def _in_kernel(x_ref, sc_ref, sh_ref, cos_ref, sin_ref, gmix_ref, win_ref, gqlat_ref, wuq_ref, gqcat_ref,
               gkvlat_ref, wukv_ref, gkcat_ref, gtail_ref, gq_ref, gk_ref, mcat_ref, m64_ref,
               qcat_o, kcat_o, vmla_o, row_o, kv_o, kidx_o, tail_o, kidxdup_o, qdsa_o, kdup_o, vone_o, qidx_o,
               *, rows_feature_major):
    tm = x_ref.shape[0]
    x = x_ref[...]
    h = x * lax.rsqrt(jnp.mean(x * x, axis=-1, keepdims=True) + EPS) * gmix_ref[...]
    hb = (h * (1.0 + sc_ref[...]) + sh_ref[...]).astype(BF16)

    def proj(lo, hi):
        return _dot(hb, win_ref[:, lo:hi])

    cos = cos_ref[...]
    sin = sin_ref[...]
    lane = lax.broadcasted_iota(I32, (tm, LANES), 1)
    first_half = lane < T_ROPE + HALF_ROPE
    rope_lanes = (lane >= T_ROPE) & (lane < T_ROPE + MLA_ROPE)
    low64 = lane < 64

    def rope(xh):
        rot = jnp.where(first_half, pltpu.roll(xh, LANES - HALF_ROPE, 1), pltpu.roll(xh, HALF_ROPE, 1))
        return xh * cos + rot * sin

    def gnorm(blk, m_ref):
        ms = _dot((blk * blk).astype(BF16), m_ref[...])
        return blk * lax.rsqrt(ms + EPS)

    ql = proj(C_QLAT, C_KVLAT)
    qln = ql * lax.rsqrt(jnp.mean(ql * ql, axis=-1, keepdims=True) + EPS) * gqlat_ref[...]
    qm = _dot(qln.astype(BF16), wuq_ref[...])
    for j in range(4):
        y = gnorm(qm[:, 256 * j:256 * j + 256], mcat_ref) * gqcat_ref[:, 256 * j:256 * j + 256]
        for t in range(2):
            qcat_o[:, 256 * j + 128 * t:256 * j + 128 * t + 128] = rope(y[:, 128 * t:128 * t + 128]).astype(BF16)

    tail = proj(C_TAIL, C_END)
    ssq = jnp.sum(jnp.where(rope_lanes, tail * tail, 0.0), axis=-1, keepdims=True) * (1.0 / MLA_ROPE)
    tn = jnp.where(rope_lanes, tail * lax.rsqrt(ssq + EPS) * gtail_ref[...], tail)
    tr = rope(tn)
    if rows_feature_major:
        kidx_o[...] = tail.T[0:IDX_DIM, :]
    else:
        kidx_o[...] = tail[:, 0:IDX_DIM]
    tail_o[...] = tail
    kidxdup_o[...] = jnp.where(low64, tail, pltpu.roll(tail, 64, 1)).astype(BF16)
    krope = jnp.where(rope_lanes, tr, 0.0)

    kvl = proj(C_KVLAT, C_Q)
    lat = kvl * lax.rsqrt(jnp.mean(kvl * kvl, axis=-1, keepdims=True) + EPS) * gkvlat_ref[...]
    if rows_feature_major:
        row_o[0:MLA_KV_RANK, :] = lat.T
        row_o[MLA_KV_RANK:MLA_ROW, :] = tr.T[T_ROPE:T_ROPE + MLA_ROPE, :]
    else:
        row_o[:, 0:MLA_KV_RANK] = lat
        row_o[:, MLA_KV_RANK:MLA_ROW] = tr[:, T_ROPE:T_ROPE + MLA_ROPE]
    latb = lat.astype(BF16)
    kx = _dot(latb, wukv_ref[:, 0:1024])
    for j in range(4):
        y = gnorm(kx[:, 256 * j:256 * j + 256], mcat_ref) * gkcat_ref[:, 256 * j:256 * j + 256]
        for t in range(2):
            kcat_o[:, 256 * j + 128 * t:256 * j + 128 * t + 128] = (y[:, 128 * t:128 * t + 128] + krope).astype(BF16)
    ones_half = (lax.broadcasted_iota(I32, (tm, 1024), 1) % LANES) >= MLA_V
    vmla_o[...] = jnp.where(ones_half, 1.0, _dot(latb, wukv_ref[:, 1024:2048])).astype(BF16)

    q = proj(C_Q, C_K)
    for j in range(2):
        qdsa_o[:, 256 * j:256 * j + 256] = (gnorm(q[:, 256 * j:256 * j + 256], m64_ref)
                                            * gq_ref[:, 256 * j:256 * j + 256] * DSA_SCALE).astype(BF16)
    k = proj(C_K, C_V)
    ms = _dot((k * k).astype(BF16), m64_ref[0:128, 0:128])
    kn = k * lax.rsqrt(ms + EPS) * gk_ref[...]
    v = proj(C_V, C_QIDX)
    if rows_feature_major:
        kv_o[0:128, :] = kn.T
        kv_o[128:256, :] = v.T
    else:
        kv_o[:, 0:128] = kn
        kv_o[:, 128:256] = v
    kr = pltpu.roll(kn, 64, 1)
    kdup_o[:, 0:128] = jnp.where(low64, kn, kr).astype(BF16)
    kdup_o[:, 128:256] = jnp.where(low64, kr, kn).astype(BF16)
    vone_o[:, 0:128] = jnp.where(low64, v, 1.0).astype(BF16)
    vone_o[:, 128:256] = jnp.where(low64, pltpu.roll(v, 64, 1), 1.0).astype(BF16)
    qidx_o[...] = proj(C_QIDX, C_TAIL).astype(BF16)


_IN_OUT_WIDTHS = (("qcat", 1024, BF16), ("kcat", 1024, BF16), ("vmla", 1024, BF16), ("row", MLA_ROW, F32),
                  ("kv", 256, F32), ("kidx", IDX_DIM, F32), ("tail", LANES, F32), ("kidxdup", LANES, BF16),
                  ("qdsa", 512, BF16), ("kdup", 256, BF16), ("vone", 256, BF16), ("qidx", 512, BF16))


def _in_stage(x2d, scale, shift, cos, sin, wp, tm, per_token):
    t_total = x2d.shape[0]
    n_tiles = t_total // tm
    if per_token:
        mod_spec = pl.BlockSpec((tm, D_MODEL), lambda i: (i, 0))
        tab_spec = pl.BlockSpec((1, LANES), lambda i: (0, 0))
    else:
        tiles_per_seq = cos.shape[0] // tm
        mod_spec = pl.BlockSpec((None, 1, D_MODEL), lambda i: (i // tiles_per_seq, 0, 0))
        tab_spec = pl.BlockSpec((tm, LANES), lambda i: (i % tiles_per_seq, 0))
    consts = [wp["g_mix"], wp["w_in"], wp["g_qlat"], wp["w_uq"], wp["g_qcat"], wp["g_kvlat"], wp["w_ukv"],
              wp["g_kcat"], wp["g_tail"], wp["g_q"], wp["g_k"], wp["m_cat"], wp["m_64"]]
    in_specs = ([pl.BlockSpec((tm, D_MODEL), lambda i: (i, 0)), mod_spec, mod_spec, tab_spec, tab_spec]
                + [_full(c.shape) for c in consts])
    out_specs = [pl.BlockSpec((tm, w), lambda i: (i, 0)) for _, w, _ in _IN_OUT_WIDTHS]
    out_shape = [jax.ShapeDtypeStruct((t_total, w), dt) for _, w, dt in _IN_OUT_WIDTHS]
    if not per_token:
        for j, (name, w, dt) in enumerate(_IN_OUT_WIDTHS):
            if name in ("row", "kv", "kidx"):
                out_specs[j] = pl.BlockSpec((None, w, tm), lambda i: (i // tiles_per_seq, 0, i % tiles_per_seq))
                out_shape[j] = jax.ShapeDtypeStruct((n_tiles // tiles_per_seq, w, tiles_per_seq * tm), dt)
    outs = pl.pallas_call(
        functools.partial(_in_kernel, rows_feature_major=not per_token),
        grid=(n_tiles,), in_specs=in_specs, out_specs=out_specs, out_shape=out_shape,
        compiler_params=_cparams(("parallel",)),
    )(x2d, scale, shift, cos, sin, *consts)
    return {name: o for (name, _, _), o in zip(_IN_OUT_WIDTHS, outs)}


def _mla_prompt_kernel(q_ref, k_ref, v_ref, o_ref, m_scr, acc_scr, s_scr, *, tk):
    tq = q_ref.shape[0]
    qi = pl.program_id(1)
    n_chunks = (qi * tq + tq + tk - 1) // tk
    m_scr[...] = jnp.full(m_scr.shape, -jnp.inf, F32)
    acc_scr[...] = jnp.zeros(acc_scr.shape, F32)
    low64 = lax.broadcasted_iota(I32, (tq, LANES), 1) < 64
    qpos = qi * tq + lax.broadcasted_iota(I32, (tq, tk), 0)
    kcol = lax.broadcasted_iota(I32, (tq, tk), 1)
    exp_scale = MLA_SCALE * math.log2(math.e)

    def raw_scores(c, h, masked):
        off = pl.multiple_of(c * tk, tk)
        s = _dot_nt(q_ref[:, h * 128:(h + 1) * 128], k_ref[pl.ds(off, tk), h * 128:(h + 1) * 128])
        if masked:
            s = jnp.where(c * tk + kcol <= qpos, s, -jnp.inf)
        return s

    def pass_max(c, masked):
        for h in range(MLA_HEADS):
            s = raw_scores(c, h, masked)
            s_scr[c, h] = s
            m = m_scr[h]
            for j in range(tk // LANES):
                m = jnp.maximum(m, s[:, j * LANES:(j + 1) * LANES])
            m_scr[h] = m

    def pass_sum(c, masked):
        off = pl.multiple_of(c * tk, tk)
        for h in range(MLA_HEADS):
            s = s_scr[c, h]
            m = m_scr[h]
            p = jnp.concatenate([jnp.exp2((s[:, j * LANES:(j + 1) * LANES] - m) * exp_scale)
                                 for j in range(tk // LANES)], axis=1).astype(BF16)
            acc_scr[h] += _dot(p, v_ref[pl.ds(off, tk), h * 128:(h + 1) * 128])

    def run(step):
        def body(c, carry):
            step(c, False)
            return carry

        lax.fori_loop(0, n_chunks - 1, body, 0)
        step(n_chunks - 1, True)

    run(pass_max)
    for h in range(MLA_HEADS):
        m_scr[h] = jnp.broadcast_to(jnp.max(m_scr[h], axis=-1, keepdims=True), (tq, LANES))
    run(pass_sum)
    for p in range(MLA_HEADS // 2):
        even, odd = acc_scr[2 * p], acc_scr[2 * p + 1]
        o_ref[:, 128 * p:128 * p + 128] = jnp.where(low64, even / pltpu.roll(even, 64, 1),
                                                    pltpu.roll(odd, 64, 1) / odd)


def _mla_prompt(qcat, kcat, vmla, batch, seq, tq, tk):
    nq = seq // tq
    return pl.pallas_call(
        functools.partial(_mla_prompt_kernel, tk=tk),
        grid=(batch, nq),
        in_specs=[pl.BlockSpec((tq, 1024), lambda b, i: (b * nq + i, 0)),
                  pl.BlockSpec((seq, 1024), lambda b, i: (b, 0)),
                  pl.BlockSpec((seq, 1024), lambda b, i: (b, 0))],
        out_specs=pl.BlockSpec((tq, 512), lambda b, i: (b * nq + i, 0)),
        out_shape=jax.ShapeDtypeStruct((batch * seq, 512), F32),
        scratch_shapes=[pltpu.VMEM((MLA_HEADS, tq, LANES), F32), pltpu.VMEM((MLA_HEADS, tq, LANES), F32),
                        pltpu.VMEM((seq // tk, MLA_HEADS, tq, tk), F32)],
        compiler_params=_cparams(("parallel", "arbitrary")),
    )(qcat, kcat, vmla)


def _sort_key(score):
    bits = lax.bitcast_convert_type(score + 0.0, I32)
    return bits ^ ((bits >> 31) & 0x7FFFFFFF)


def _kth_key(count_ge, n_sel, shape):
    def body(i, t):
        cand = t + (jnp.int32(1) << (31 - i))
        return jnp.where(count_ge(cand) >= n_sel, cand, t)

    return lax.fori_loop(0, 32, body, jnp.full(shape, INT_MIN, I32))


def _bucket(dist):
    max_exact = N_BUCKETS // 2
    d = jnp.maximum(dist, 0)
    log_ratio = jnp.log(jnp.maximum(d, max_exact).astype(F32) / max_exact) / math.log(MAX_DISTANCE / max_exact)
    large = jnp.minimum(max_exact + (log_ratio * (N_BUCKETS - max_exact)).astype(I32), N_BUCKETS - 1)
    return jnp.where(d < max_exact, d, large)


def _bias_kernel(rb_ref, o_ref, *, a, b, c):
    rows, cols = o_ref.shape[1], o_ref.shape[2]
    dist = (a * lax.broadcasted_iota(I32, (rows, cols), 0) + b * lax.broadcasted_iota(I32, (rows, cols), 1) + c)
    bucket = _bucket(dist)
    for h in range(DSA_HEADS):
        acc = jnp.zeros((rows, cols), F32)
        for n in range(N_BUCKETS):
            acc = jnp.where(bucket == n, rb_ref[n, h], acc)
        o_ref[h] = acc


def _bias_table(rel_bias, rows, cols, a, b, c):
    return pl.pallas_call(
        functools.partial(_bias_kernel, a=a, b=b, c=c),
        in_specs=[pl.BlockSpec(memory_space=pltpu.SMEM)],
        out_specs=pl.BlockSpec(memory_space=pltpu.VMEM),
        out_shape=jax.ShapeDtypeStruct((DSA_HEADS, rows, cols), F32),
    )(rel_bias)


def _loop2(n, body):
    n = jnp.maximum(n, 0)

    def pair(j, carry):
        body(2 * j)
        body(2 * j + 1)
        return carry

    lax.fori_loop(0, n // 2, pair, 0)

    @pl.when(n % 2 == 1)
    def _():
        body(n - 1)


def _colreduce(x, op):
    return op(x.reshape(x.shape[0] // 8, 8, x.shape[1]), axis=0)


def _dsa_prompt_kernel(rb_ref, qidx_ref, tail_ref, kidx_ref, q_ref, k_ref, v_ref, bias_ref, o_ref,
                       qs_scr, qd_scr, wb_scr, key_scr, tie_scr, m_scr, p_scr, acc_scr, s_scr, *, n_sel, idx_bits):
    tq = q_ref.shape[0]
    kq = tq // KB
    qi = pl.program_id(1)
    krow = lax.broadcasted_iota(I32, (KB, tq), 0)
    qcol = lax.broadcasted_iota(I32, (KB, tq), 1)
    low64 = lax.broadcasted_iota(I32, (tq, LANES), 1) < 64

    tail = tail_ref[...]
    for h in range(IDX_HEADS):
        mine = low64 if h % 2 == 0 else jnp.logical_not(low64)
        zero = jnp.zeros((tq, LANES), BF16)
        qs_scr[h * tq:(h + 1) * tq, :] = jnp.where(mine, qidx_ref[:, (h // 2) * 128:(h // 2 + 1) * 128], zero)
        qd_scr[h * tq:(h + 1) * tq, :] = jnp.where(mine, q_ref[:, (h // 2) * 128:(h // 2 + 1) * 128], zero)
        wb_scr[h] = jnp.broadcast_to(tail[:, T_WIDX + h:T_WIDX + h + 1], (tq, LANES))

    def rows_of(c):
        return pl.ds(pl.multiple_of(c * tq, tq), tq)

    def score_chunk(c, in_tile):
        d = _dot_nt(qs_scr[...], kidx_ref[rows_of(c), :])
        rb = 64
        for t in range(kq):
            parts = []
            for r in range(0, tq, rb):
                sc = jnp.zeros((rb, KB), F32)
                for h in range(IDX_HEADS):
                    sc = sc + (jnp.maximum(d[h * tq + r:h * tq + r + rb, t * KB:(t + 1) * KB], 0.0)
                               * wb_scr[h, r:r + rb, :])
                parts.append(sc)
            sc_t = (jnp.concatenate(parts, axis=0) * IDX_SCALE).T
            if in_tile:
                sc_t = jnp.where(t * KB + krow <= qcol, sc_t, -jnp.inf)
            key_scr[c * kq + t] = _sort_key(sc_t)

    _loop2(qi, lambda c: score_chunk(c, False))
    score_chunk(qi, True)
    n_chunks = qi + 1

    def count(pred):
        part = 32
        def body(c, acc):
            for t in range(kq):
                kb = c * kq + t
                hit = jnp.where(pred(key_scr[kb], kb), 1.0, 0.0)
                acc = acc + jnp.sum(hit.reshape(KB // part, part, tq), axis=0)
            return acc
        acc = lax.fori_loop(0, n_chunks, body, jnp.zeros((part, tq), F32))
        return jnp.sum(_colreduce(acc, jnp.sum), axis=0, keepdims=True)

    thr = _kth_key(lambda cand: count(lambda k, kb: k >= cand), float(n_sel), (1, tq))
    cnt_gt = count(lambda k, kb: k > thr)
    cnt_ge = count(lambda k, kb: k >= thr)
    tie_scr[...] = jnp.full((1, tq), 2 ** 30, I32)
    excess = jnp.where((cnt_ge > float(n_sel)) & (thr > INT_MIN), 1.0, 0.0)

    @pl.when(jnp.max(excess) > 0.0)
    def _():
        need = float(n_sel) - cnt_gt

        def body(i, p):
            cand = p + (jnp.int32(1) << (idx_bits - 1 - i))
            c = count(lambda k, kb: (k == thr) & (kb * KB + krow < cand))
            return jnp.where(c < need, cand, p)

        tie_scr[...] = lax.fori_loop(0, idx_bits, body, jnp.zeros((1, tq), I32))

    last_tie = tie_scr[...]

    def mask_chunk(c, in_tile):
        halves = []
        for t in range(kq):
            kb = c * kq + t
            key = key_scr[kb]
            sel = (key > thr) | ((key == thr) & (kb * KB + krow <= last_tie))
            if in_tile:
                sel = sel & (t * KB + krow <= qcol)
            halves.append(jnp.where(sel, 0.0, NEG_BIG).T)
        return halves

    far_bias = [rb_ref[N_BUCKETS - 1, h] for h in range(DSA_HEADS)]

    def scores(c, g, kind, mask):
        sg = _dot_nt(qd_scr[g * DSA_GROUP * tq:(g + 1) * DSA_GROUP * tq, :], k_ref[rows_of(c), g * 128:(g + 1) * 128])
        out = []
        for hh in range(DSA_GROUP):
            h = g * DSA_GROUP + hh
            cols = []
            for t in range(kq):
                s = sg[hh * tq:(hh + 1) * tq, t * KB:(t + 1) * KB] + mask[t]
                if kind == "tile":
                    s = s + (bias_ref[1 + t, h] - far_bias[h])
                elif kind == "before" and t == kq - 1:
                    s = s + (bias_ref[0, h] - far_bias[h])
                cols.append(s)
            out.append(cols)
        return out

    m_scr[...] = jnp.full(m_scr.shape, NEG_BIG, F32)
    acc_scr[...] = jnp.zeros(acc_scr.shape, F32)

    def pass_max(c, kind):
        mask = mask_chunk(c, kind == "tile")
        for g in range(DSA_KV_HEADS):
            for hh, cols in enumerate(scores(c, g, kind, mask)):
                h = g * DSA_GROUP + hh
                m = m_scr[h]
                for t, s in enumerate(cols):
                    s_scr[c, h, :, t * KB:(t + 1) * KB] = s
                    m = jnp.maximum(m, s)
                m_scr[h] = m

    def pass_sum(c, kind):
        for g in range(DSA_KV_HEADS):
            for hh in range(DSA_GROUP):
                h = g * DSA_GROUP + hh
                m = m_scr[h]
                for t in range(kq):
                    s = s_scr[c, h, :, t * KB:(t + 1) * KB]
                    p_scr[g, hh * tq:(hh + 1) * tq, t * KB:(t + 1) * KB] = jnp.exp(s - m).astype(BF16)
            acc_scr[g] += _dot(p_scr[g], v_ref[rows_of(c), g * 128:(g + 1) * 128])

    def run(step):
        _loop2(qi - 1, lambda c: step(c, "far"))

        @pl.when(qi >= 1)
        def _():
            step(qi - 1, "before")

        step(qi, "tile")

    run(pass_max)
    for h in range(DSA_HEADS):
        m_scr[h] = jnp.broadcast_to(jnp.max(m_scr[h], axis=-1, keepdims=True), (tq, LANES))
    run(pass_sum)

    for p in range(DSA_HEADS // 2):
        g, hh = (2 * p) // DSA_GROUP, (2 * p) % DSA_GROUP
        even = acc_scr[g, hh * tq:(hh + 1) * tq, :]
        odd = acc_scr[g, (hh + 1) * tq:(hh + 2) * tq, :]
        o_ref[:, 128 * p:128 * p + 128] = jnp.where(low64, even / pltpu.roll(even, 64, 1),
                                                    pltpu.roll(odd, 64, 1) / odd)


def _dsa_prompt(rel_bias, qidx, tail, kidxdup, qdsa, kdup, vone, batch, seq, tq):
    assert MAX_DISTANCE <= KB, "keys two or more blocks back must all fall in the last bucket"
    nq = seq // tq
    nkb = seq // KB
    kq = tq // KB
    n_sel = min(TOPK_MAX, seq // 4)
    idx_bits = max(1, (seq - 1).bit_length())
    bias = jnp.stack([_bias_table(rel_bias, tq, KB, 1, -1, KB - t * KB) for t in range(kq + 1)])
    kern = functools.partial(_dsa_prompt_kernel, n_sel=n_sel, idx_bits=idx_bits)
    qblk = lambda w: pl.BlockSpec((tq, w), lambda b, i: (b * nq + i, 0))
    seqblk = lambda w: pl.BlockSpec((seq, w), lambda b, i: (b, 0))
    return pl.pallas_call(
        kern,
        grid=(batch, nq),
        in_specs=[pl.BlockSpec(memory_space=pltpu.SMEM),
                  qblk(512), qblk(LANES), seqblk(LANES), qblk(512), seqblk(256), seqblk(256), _full(bias.shape)],
        out_specs=qblk(512),
        out_shape=jax.ShapeDtypeStruct((batch * seq, 512), F32),
        scratch_shapes=[pltpu.VMEM((IDX_HEADS * tq, LANES), BF16), pltpu.VMEM((DSA_HEADS * tq, LANES), BF16),
                        pltpu.VMEM((IDX_HEADS, tq, LANES), F32),
                        pltpu.VMEM((nkb, KB, tq), I32),
                        pltpu.VMEM((1, tq), I32), pltpu.VMEM((DSA_HEADS, tq, LANES), F32),
                        pltpu.VMEM((DSA_KV_HEADS, DSA_GROUP * tq, tq), BF16),
                        pltpu.VMEM((DSA_KV_HEADS, DSA_GROUP * tq, LANES), F32),
                        pltpu.VMEM((nq, DSA_HEADS, tq, tq), F32)],
        compiler_params=_cparams(("parallel", "arbitrary")),
    )(rel_bias, qidx, tail, kidxdup, qdsa, kdup, vone, bias)


def _fetch_pages(cache_ref, pt_ref, buf, sem, step, n_steps, locate, pps):
    def copy(slot, i, page):
        return pltpu.make_async_copy(cache_ref.at[0, page], buf.at[slot, i], sem.at[slot])

    def start(st, slot):
        b, first = locate(st)
        for i in range(pps):
            copy(slot, i, pt_ref[b, first + i]).start()

    @pl.when(step == 0)
    def _():
        start(step, 0)

    @pl.when(step + 1 < n_steps)
    def _():
        start(step + 1, (step + 1) % 2)

    slot = step % 2
    for i in range(pps):
        copy(slot, i, 0).wait()
    return slot


def _page_scratch(pps, width):
    return [pltpu.VMEM((2, pps, width, PAGE_SIZE), F32), pltpu.SemaphoreType.DMA((2,))]


def _pages_feature_major(cache):
    c = jnp.moveaxis(cache, 2, -1)
    return c.reshape(cache.shape[0], cache.shape[1], -1, PAGE_SIZE)


def _sample_score_kernel(pt_ref, q_ref, w_ref, cache_ref, o_ref, buf, sem, kb_scr, *, pps, grp):
    bo, c, bi = pl.program_id(0), pl.program_id(1), pl.program_id(2)
    n_chunks = pl.num_programs(1)
    step = (bo * n_chunks + c) * grp + bi

    def locate(st):
        return (st // (n_chunks * grp)) * grp + st % grp, ((st // grp) % n_chunks) * pps

    slot = _fetch_pages(cache_ref, pt_ref, buf, sem, step, pl.num_programs(0) * n_chunks * grp, locate, pps)
    for i in range(pps):
        kb_scr[:, i * PAGE_SIZE:(i + 1) * PAGE_SIZE] = buf[slot, i].astype(BF16)
    d = _dot(q_ref[...], kb_scr[...])
    sc = jnp.sum(jnp.maximum(d, 0.0) * w_ref[...], axis=0, keepdims=True) * IDX_SCALE
    o_ref[pl.ds(bi, 1), :] = sc


def _sample_scores(page_table, qidx3, widx3, cache_idx_t, pps):
    dec, n_pages = page_table.shape
    n_chunks = n_pages // pps
    grp = 8
    grid_spec = pltpu.PrefetchScalarGridSpec(
        num_scalar_prefetch=1,
        grid=(dec // grp, n_chunks, grp),
        in_specs=[pl.BlockSpec((None, IDX_HEADS, IDX_DIM), lambda bo, c, bi, pt: (bo * grp + bi, 0, 0)),
                  pl.BlockSpec((None, IDX_HEADS, 1), lambda bo, c, bi, pt: (bo * grp + bi, 0, 0)),
                  pl.BlockSpec(memory_space=pl.ANY)],
        out_specs=pl.BlockSpec((grp, pps * PAGE_SIZE), lambda bo, c, bi, pt: (bo, c)),
        scratch_shapes=_page_scratch(pps, IDX_DIM) + [pltpu.VMEM((IDX_DIM, pps * PAGE_SIZE), BF16)],
    )
    return pl.pallas_call(
        functools.partial(_sample_score_kernel, pps=pps, grp=grp), grid_spec=grid_spec,
        out_shape=jax.ShapeDtypeStruct((dec, n_pages * PAGE_SIZE), F32),
        compiler_params=_cparams(("arbitrary", "arbitrary", "arbitrary")),
    )(page_table, qidx3, widx3, cache_idx_t)


def _sample_select_kernel(sc_ref, qidx_ref, tail_ref, selp_ref, seln_ref, key_scr, *, n_sel, idx_bits):
    rows, past = sc_ref.shape
    nblk = past // LANES
    lane = lax.broadcasted_iota(I32, (rows, LANES), 1)
    tail = tail_ref[...]
    kidx2 = jnp.where(lane < 64, tail, pltpu.roll(tail, 64, 1))
    sc_new = jnp.zeros((rows, 1), F32)
    for h in range(IDX_HEADS):
        prod = qidx_ref[:, (h // 2) * 128:(h // 2 + 1) * 128].astype(F32) * kidx2.astype(BF16).astype(F32)
        mine = (lane < 64) if h % 2 == 0 else (lane >= 64)
        dot = jnp.sum(jnp.where(mine, prod, 0.0), axis=-1, keepdims=True)
        sc_new = sc_new + jnp.maximum(dot, 0.0) * tail[:, T_WIDX + h:T_WIDX + h + 1]
    key_new = _sort_key(sc_new * IDX_SCALE)

    for j in range(nblk):
        key_scr[j] = _sort_key(sc_ref[:, j * LANES:(j + 1) * LANES])

    def count(pred, pred_new):
        def body(j, c):
            return c + jnp.where(pred(key_scr[j], j), 1.0, 0.0)
        c = lax.fori_loop(0, nblk, body, jnp.zeros((rows, LANES), F32), unroll=4)
        return jnp.sum(c, axis=-1, keepdims=True) + jnp.where(pred_new, 1.0, 0.0)

    def wide(col):
        return jnp.broadcast_to(col, (rows, LANES))

    def count_ge(cand):
        cand_w = wide(cand)
        return count(lambda k, j: k >= cand_w, key_new >= cand)

    thr = _kth_key(count_ge, float(n_sel), (rows, 1))
    thr_w = wide(thr)
    cnt_gt = count(lambda k, j: k > thr_w, key_new > thr)
    need = float(n_sel) - cnt_gt

    def body(i, p):
        cand = p + (jnp.int32(1) << (idx_bits - 1 - i))
        cand_w = wide(cand)
        c = count(lambda k, j: (k == thr_w) & (j * LANES + lane < cand_w), (key_new == thr) & (past < cand))
        return jnp.where(c < need, cand, p)

    last_tie = lax.fori_loop(0, idx_bits, body, jnp.zeros((rows, 1), I32))
    last_w = wide(last_tie)

    for j in range(nblk):
        k = key_scr[j]
        sel = (k > thr_w) | ((k == thr_w) & (j * LANES + lane <= last_w))
        selp_ref[:, j * LANES:(j + 1) * LANES] = jnp.where(sel, 1.0, 0.0)
    sel_new = (key_new > thr) | ((key_new == thr) & (past <= last_tie))
    seln_ref[...] = jnp.broadcast_to(jnp.where(sel_new, 1.0, 0.0), (rows, LANES))


def _sample_select(scores, qidx, tail):
    dec, past = scores.shape
    n_sel = min(TOPK_MAX, (past + 1) // 4)
    idx_bits = max(1, past.bit_length())
    rows = 64 if dec % 64 == 0 else dec
    kern = functools.partial(_sample_select_kernel, n_sel=n_sel, idx_bits=idx_bits)
    return pl.pallas_call(
        kern,
        grid=(dec // rows,),
        in_specs=[pl.BlockSpec((rows, past), lambda i: (i, 0)), pl.BlockSpec((rows, 512), lambda i: (i, 0)),
                  pl.BlockSpec((rows, LANES), lambda i: (i, 0))],
        out_specs=[pl.BlockSpec((rows, past), lambda i: (i, 0)), pl.BlockSpec((rows, LANES), lambda i: (i, 0))],
        out_shape=[jax.ShapeDtypeStruct((dec, past), F32), jax.ShapeDtypeStruct((dec, LANES), F32)],
        scratch_shapes=[pltpu.VMEM((past // LANES, rows, LANES), I32)],
        compiler_params=_cparams(("parallel",)),
    )(scores, qidx, tail)


def _mla_sample_kernel(pt_ref, qn_ref, qr_ref, rown_ref, wuk_ref, wuv_ref, gk_ref, cache_ref, o_ref,
                       buf, sem, lat_scr, kr_scr, a_scr, m_scr, l_scr, acc_scr, *, pps):
    c = pl.program_id(1)
    n_chunks = pl.num_programs(1)
    heads = MLA_HEADS
    slot = _fetch_pages(cache_ref, pt_ref, buf, sem, pl.program_id(0) * n_chunks + c,
                        pl.num_programs(0) * n_chunks, lambda st: (st // n_chunks, (st % n_chunks) * pps), pps)

    @pl.when(c == 0)
    def _():
        m_scr[...] = jnp.full(m_scr.shape, -jnp.inf, F32)
        l_scr[...] = jnp.zeros(l_scr.shape, F32)
        acc_scr[...] = jnp.zeros(acc_scr.shape, F32)
        qg = qn_ref[...] * gk_ref[...]
        own = (lax.broadcasted_iota(I32, (heads, heads * MLA_NOPE), 1) // MLA_NOPE
               == lax.broadcasted_iota(I32, (heads, heads * MLA_NOPE), 0))
        n_up = heads * MLA_NOPE
        a_scr[0:n_up, :] = wuk_ref[...]
        a = _dot(jnp.where(own, qg, 0.0).astype(BF16), wuk_ref[...]).astype(BF16)
        a_scr[n_up:n_up + 16, :] = jnp.concatenate([a, jnp.zeros_like(a)], axis=0)

    qr = qr_ref[...]

    def scores(latb, krb):
        n_up = heads * MLA_NOPE
        up = _dot(a_scr[...], latb)
        knt = up[0:n_up, :]
        ms = jnp.mean((knt * knt).reshape(heads, MLA_NOPE, latb.shape[1]), axis=1)
        return (up[n_up:n_up + heads, :] * lax.rsqrt(ms + EPS) + _dot(qr, krb)) * MLA_SCALE

    def update(latb, krb, n_valid):
        n = latb.shape[1]
        sub = min(n, 512)
        s = jnp.concatenate([scores(latb[:, j:j + sub], krb[:, j:j + sub]) for j in range(0, n, sub)], axis=1)
        if n_valid < n:
            s = jnp.where(lax.broadcasted_iota(I32, s.shape, 1) < n_valid, s, -jnp.inf)
        m_old = m_scr[...]
        m_new = jnp.maximum(m_old, jnp.max(s, axis=-1, keepdims=True))
        alpha = jnp.exp(m_old - m_new)
        p = jnp.exp(s - m_new)
        l_scr[...] = alpha * l_scr[...] + jnp.sum(p, axis=-1, keepdims=True)
        m_scr[...] = m_new
        acc_scr[...] = alpha * acc_scr[...] + _dot_nt(p.astype(BF16), latb)

    for i in range(pps):
        lat_scr[:, i * PAGE_SIZE:(i + 1) * PAGE_SIZE] = buf[slot, i, 0:MLA_KV_RANK, :].astype(BF16)
        kr_scr[:, i * PAGE_SIZE:(i + 1) * PAGE_SIZE] = buf[slot, i, MLA_KV_RANK:MLA_ROW, :].astype(BF16)
    update(lat_scr[...], kr_scr[...], lat_scr.shape[1])

    @pl.when(c == n_chunks - 1)
    def _():
        rn = jnp.broadcast_to(rown_ref[...], (MLA_ROW, PAGE_SIZE))
        update(rn[0:MLA_KV_RANK, :].astype(BF16), rn[MLA_KV_RANK:MLA_ROW, :].astype(BF16), 1)
        o_lat = (acc_scr[...] / l_scr[...]).astype(BF16)
        full = _dot(o_lat, wuv_ref[...])
        hd = lax.broadcasted_iota(I32, full.shape, 1) // MLA_V
        own = hd == lax.broadcasted_iota(I32, full.shape, 0)
        o_ref[...] = jnp.sum(jnp.where(own, full, 0.0), axis=0, keepdims=True)


def _dsa_sample_kernel(pt_ref, q_ref, kvn_ref, selp_ref, seln_ref, biasp_ref, biasn_ref, cache_ref, o_ref,
                       buf, sem, k_scr, v_scr, m_scr, l_scr, acc_scr, *, pps):
    c = pl.program_id(1)
    n_chunks = pl.num_programs(1)
    slot = _fetch_pages(cache_ref, pt_ref, buf, sem, pl.program_id(0) * n_chunks + c,
                        pl.num_programs(0) * n_chunks, lambda st: (st // n_chunks, (st % n_chunks) * pps), pps)

    @pl.when(c == 0)
    def _():
        m_scr[...] = jnp.full(m_scr.shape, NEG_BIG, F32)
        l_scr[...] = jnp.zeros(l_scr.shape, F32)
        acc_scr[...] = jnp.zeros(acc_scr.shape, F32)

    q = q_ref[...]

    def update(s, sel, pv_of):
        sh = jnp.where(sel, s, NEG_BIG)
        m_old = m_scr[...]
        m_new = jnp.maximum(m_old, jnp.max(sh, axis=-1, keepdims=True))
        alpha = jnp.exp(m_old - m_new)
        p = jnp.where(sel, jnp.exp(sh - m_new), 0.0)
        l_scr[...] = alpha * l_scr[...] + jnp.sum(p, axis=-1, keepdims=True)
        m_scr[...] = m_new
        acc_scr[...] = alpha * acc_scr[...] + pv_of(p)

    for i in range(pps):
        k_scr[:, i * PAGE_SIZE:(i + 1) * PAGE_SIZE] = buf[slot, i, 0:128, :].astype(BF16)
        v_scr[:, i * PAGE_SIZE:(i + 1) * PAGE_SIZE] = buf[slot, i, 128:256, :].astype(BF16)
    s = _dot(q.astype(BF16), k_scr[...]) + biasp_ref[...]
    update(s, selp_ref[...] > 0.5, lambda p: _dot_nt(p.astype(BF16), v_scr[...]))

    @pl.when(c == n_chunks - 1)
    def _():
        kvn = kvn_ref[...]
        kn = kvn[:, 0:128].astype(BF16).astype(F32)
        vn = kvn[:, 128:256].astype(BF16).astype(F32)
        s_new = jnp.sum(q.astype(F32) * kn, axis=-1, keepdims=True) + biasn_ref[...]
        update(s_new, seln_ref[:, 0:1] > 0.5, lambda p: p.astype(BF16).astype(F32) * vn)
        o = acc_scr[...] / l_scr[...]
        lane = lax.broadcasted_iota(I32, o.shape, 1)
        row = lax.broadcasted_iota(I32, o.shape, 0)
        own = (lane // DSA_HEAD_DIM) == (row // DSA_GROUP)
        o_ref[...] = jnp.where(own, o, 0.0)


_N_MLA_IN, _N_MLA_SCRATCH = 7, 8
_N_DSA_IN = 7


def _decode_kernel(pt_ref, *refs, pps):
    mla_in = refs[:_N_MLA_IN]
    dsa_in = refs[_N_MLA_IN:_N_MLA_IN + _N_DSA_IN]
    o_mla, o_dsa = refs[_N_MLA_IN + _N_DSA_IN:_N_MLA_IN + _N_DSA_IN + 2]
    scratch = refs[_N_MLA_IN + _N_DSA_IN + 2:]
    _mla_sample_kernel(pt_ref, *mla_in, o_mla, *scratch[:_N_MLA_SCRATCH], pps=pps)
    _dsa_sample_kernel(pt_ref, *dsa_in, o_dsa, *scratch[_N_MLA_SCRATCH:], pps=pps)


def _decode_attention(page_table, qn_row, qr, row_new, wuk_t, wuv, gk_row, cache_mla,
                      qs, kv_new, selp, seln, bias_past, bias_new, cache_kv4, pps):
    dec, n_pages = page_table.shape
    n_chunks = n_pages // pps
    chunk = pps * PAGE_SIZE
    per_b = lambda *shape: pl.BlockSpec((None,) + shape, lambda b, c, pt: (b,) + (0,) * len(shape))
    const = lambda a: pl.BlockSpec(a.shape, lambda b, c, pt: (0,) * a.ndim)
    grid_spec = pltpu.PrefetchScalarGridSpec(
        num_scalar_prefetch=1,
        grid=(dec, n_chunks),
        in_specs=[per_b(1, MLA_HEADS * MLA_NOPE), per_b(MLA_HEADS, MLA_ROPE), per_b(MLA_ROW, 1),
                  const(wuk_t), const(wuv), const(gk_row), pl.BlockSpec(memory_space=pl.ANY),
                  per_b(DSA_HEADS, LANES), per_b(1, 256),
                  pl.BlockSpec((None, 1, chunk), lambda b, c, pt: (b, 0, c)), per_b(1, LANES),
                  pl.BlockSpec((DSA_HEADS, chunk), lambda b, c, pt: (0, c)), const(bias_new),
                  pl.BlockSpec(memory_space=pl.ANY)],
        out_specs=[per_b(1, MLA_HEADS * MLA_V), per_b(DSA_HEADS, LANES)],
        scratch_shapes=_page_scratch(pps, MLA_ROW) + [
            pltpu.VMEM((MLA_KV_RANK, chunk), BF16), pltpu.VMEM((MLA_ROPE, chunk), BF16),
            pltpu.VMEM((MLA_HEADS * MLA_NOPE + 16, MLA_KV_RANK), BF16),
            pltpu.VMEM((MLA_HEADS, 1), F32), pltpu.VMEM((MLA_HEADS, 1), F32),
            pltpu.VMEM((MLA_HEADS, MLA_KV_RANK), F32)]
        + _page_scratch(pps, 256) + [
            pltpu.VMEM((128, chunk), BF16), pltpu.VMEM((128, chunk), BF16),
            pltpu.VMEM((DSA_HEADS, 1), F32), pltpu.VMEM((DSA_HEADS, 1), F32),
            pltpu.VMEM((DSA_HEADS, LANES), F32)],
    )
    return pl.pallas_call(
        functools.partial(_decode_kernel, pps=pps), grid_spec=grid_spec,
        out_shape=[jax.ShapeDtypeStruct((dec, 1, MLA_HEADS * MLA_V), F32),
                   jax.ShapeDtypeStruct((dec, DSA_HEADS, LANES), F32)],
        compiler_params=_cparams(("arbitrary", "arbitrary")),
    )(page_table, qn_row, qr, row_new, wuk_t, wuv, gk_row, cache_mla,
      qs, kv_new, selp, seln, bias_past, bias_new, cache_kv4)


def _out_kernel(x_ref, oa_ref, ob_ref, gm_ref, sf_ref, scf_ref, gf_ref, gout_ref, gffn_ref,
                wout_ref, wg_ref, wu_ref, wdown_ref, y_ref, *, ff_chunk):
    def rms(v, g):
        return v * lax.rsqrt(jnp.mean(v * v, axis=-1, keepdims=True) + EPS) * g

    half = oa_ref.shape[1]
    na = rms(oa_ref[...], gout_ref[:, 0:half]).astype(BF16)
    nb = rms(ob_ref[...], gout_ref[:, half:2 * half]).astype(BF16)
    mix = _dot(na, wout_ref[0:half, :]) + _dot(nb, wout_ref[half:2 * half, :])
    x1 = x_ref[...] + gm_ref[...] * mix
    hb = (rms(x1, gffn_ref[...]) * (1.0 + scf_ref[...]) + sf_ref[...]).astype(BF16)
    acc = jnp.zeros(x1.shape, F32)
    for j in range(D_FF // ff_chunk):
        g = _dot(hb, wg_ref[:, j * ff_chunk:(j + 1) * ff_chunk])
        u = _dot(hb, wu_ref[:, j * ff_chunk:(j + 1) * ff_chunk])
        a = (g / (1.0 + jnp.exp(-g))) * u
        acc = acc + _dot(a.astype(BF16), wdown_ref[j * ff_chunk:(j + 1) * ff_chunk, :])
    y_ref[...] = x1 + gf_ref[...] * acc


def _out_stage(x2d, oa, ob, gate_m, shift_f, scale_f, gate_f, wp, tm, per_token, tiles_per_seq):
    t_total = x2d.shape[0]
    if per_token:
        mod_spec = pl.BlockSpec((tm, D_MODEL), lambda i: (i, 0))
    else:
        mod_spec = pl.BlockSpec((None, 1, D_MODEL), lambda i: (i // tiles_per_seq, 0, 0))
    consts = [wp["g_out"], wp["g_ffn"], wp["w_out"], wp["w_gate"], wp["w_up"], wp["w_down"]]
    tok = lambda w: pl.BlockSpec((tm, w), lambda i: (i, 0))
    return pl.pallas_call(
        functools.partial(_out_kernel, ff_chunk=256),
        grid=(t_total // tm,),
        in_specs=[tok(D_MODEL), tok(512), tok(512), mod_spec, mod_spec, mod_spec, mod_spec]
                 + [_full(c.shape) for c in consts],
        out_specs=tok(D_MODEL),
        out_shape=jax.ShapeDtypeStruct((t_total, D_MODEL), F32),
        compiler_params=_cparams(("parallel",)),
    )(x2d, oa, ob, gate_m, shift_f, scale_f, gate_f, *consts)


def _prep_weights(w_in, g_norm_mix, g_norm_ffn, g_q_lat, w_uq, g_kv_lat, w_ukv, g_mla_q_nope, g_mla_q_rope,
                  g_mla_k_nope, g_mla_k_rope, g_dsa_q, g_dsa_k, g_out, w_out, w_ffn_in, w_ffn_out):
    splits = np.cumsum([MLA_Q_RANK, MLA_KV_RANK, MLA_ROPE, 512, 128, 128, 512, IDX_DIM, IDX_HEADS])
    s = [0] + splits.tolist()
    col = lambda i: w_in[:, s[i]:s[i + 1]]
    pad = jnp.zeros((D_MODEL, C_END - C_TAIL - IDX_DIM - MLA_ROPE - IDX_HEADS), w_in.dtype)
    w_in_r = jnp.concatenate([col(0), col(1), col(3), col(4), col(5), col(6), col(7), col(2), col(8), pad], axis=1)
    zq = jnp.zeros((MLA_Q_RANK, MLA_HEADS, LANES - MLA_NOPE - MLA_ROPE), w_uq.dtype)
    w_uq_cat = jnp.concatenate([w_uq, zq], axis=2).reshape(MLA_Q_RANK, MLA_HEADS * LANES)
    zk = jnp.zeros((MLA_KV_RANK, MLA_HEADS, LANES - MLA_NOPE), w_ukv.dtype)
    w_uk_cat = jnp.concatenate([w_ukv[:, :, :MLA_NOPE], zk], axis=2).reshape(MLA_KV_RANK, MLA_HEADS * LANES)
    w_uv = w_ukv[:, :, MLA_NOPE:].reshape(MLA_KV_RANK, MLA_HEADS * MLA_V)
    zv = jnp.zeros((MLA_KV_RANK, MLA_HEADS, LANES - MLA_V), w_ukv.dtype)
    w_uv_pad = jnp.concatenate([w_ukv[:, :, MLA_NOPE:], zv], axis=2).reshape(MLA_KV_RANK, MLA_HEADS * LANES)
    w_uk_t = w_ukv[:, :, :MLA_NOPE].reshape(MLA_KV_RANK, MLA_HEADS * MLA_NOPE).T
    z32 = jnp.zeros((LANES - MLA_NOPE - MLA_ROPE,), F32)
    z64 = jnp.zeros((LANES - MLA_NOPE,), F32)
    g_qcat = jnp.tile(jnp.concatenate([g_mla_q_nope, g_mla_q_rope, z32]), MLA_HEADS)
    g_kcat = jnp.tile(jnp.concatenate([g_mla_k_nope, z64]), MLA_HEADS)
    g_tail = jnp.concatenate([jnp.zeros((T_ROPE,), F32), g_mla_k_rope, z32])
    row = lambda v: v.reshape(1, -1).astype(F32)
    return {
        "g_mix": row(g_norm_mix), "g_ffn": row(g_norm_ffn), "w_in": w_in_r.astype(BF16),
        "g_qlat": row(g_q_lat), "w_uq": w_uq_cat.astype(BF16), "g_qcat": row(g_qcat),
        "g_kvlat": row(g_kv_lat), "w_ukv": jnp.concatenate([w_uk_cat, w_uv_pad], axis=1).astype(BF16),
        "g_kcat": row(g_kcat), "g_tail": row(g_tail),
        "g_q": row(jnp.tile(g_dsa_q, DSA_HEADS)), "g_k": row(jnp.tile(g_dsa_k, DSA_KV_HEADS)),
        "m_cat": _group_matrix(256, [(0, 64), (64, 96), (128, 192), (192, 224)]),
        "m_64": _group_matrix(256, [(0, 64), (64, 128), (128, 192), (192, 256)]),
        "w_uk_t": w_uk_t.astype(BF16), "w_uv": w_uv.astype(BF16), "g_k_nope": g_mla_k_nope,
        "g_out": row(g_out), "w_out": w_out.astype(BF16),
        "w_gate": w_ffn_in[:, :D_FF].astype(BF16), "w_up": w_ffn_in[:, D_FF:].astype(BF16),
        "w_down": w_ffn_out.astype(BF16),
    }


def _rope_tables(pos):
    freq = ROPE_THETA ** (-jnp.arange(HALF_ROPE, dtype=F32) / HALF_ROPE)
    ang = pos.astype(F32)[:, None] * freq[None, :]
    cos, sin = jnp.cos(ang), jnp.sin(ang)
    n = pos.shape[0]
    ones = jnp.ones((n, T_ROPE), F32)
    zeros = jnp.zeros((n, T_ROPE), F32)
    tail1 = jnp.ones((n, LANES - T_ROPE - MLA_ROPE), F32)
    tail0 = jnp.zeros((n, LANES - T_ROPE - MLA_ROPE), F32)
    return (jnp.concatenate([ones, cos, cos, tail1], axis=1),
            jnp.concatenate([zeros, -sin, sin, tail0], axis=1))


def _pick_tile(n, prefs):
    for t in prefs:
        if n % t == 0:
            return t
    return n


def kernel(x_prompt, x_sample, c_prompt, c_sample, cache_mla, cache_kv, cache_idx, page_table, rel_bias, w_ada, b_ada, g_norm_mix, g_norm_ffn, w_in, g_q_lat, w_uq, g_kv_lat, w_ukv, g_mla_q_nope, g_mla_q_rope, g_mla_k_nope, g_mla_k_rope, g_dsa_q, g_dsa_k, g_out, w_out, w_ffn_in, w_ffn_out):
    assert w_ada.shape[0] == 1 and x_sample.shape[1] == 1, "one layer, one new token per sample"
    batch, seq, _ = x_prompt.shape
    dec = x_sample.shape[0]
    n_pages = page_table.shape[1]
    past = n_pages * PAGE_SIZE
    wp = _prep_weights(w_in[0], g_norm_mix[0], g_norm_ffn[0], g_q_lat[0], w_uq[0], g_kv_lat[0], w_ukv[0],
                       g_mla_q_nope[0], g_mla_q_rope[0], g_mla_k_nope[0], g_mla_k_rope[0], g_dsa_q[0], g_dsa_k[0],
                       g_out[0], w_out[0], w_ffn_in[0], w_ffn_out[0])

    mod = _ada(jnp.concatenate([c_prompt, c_sample], axis=0), w_ada[0], b_ada[0])
    mods_p = [m.reshape(batch, 1, D_MODEL) for m in jnp.split(mod[:batch], 6, axis=-1)]
    mods_s = jnp.split(mod[batch:], 6, axis=-1)

    tm = _pick_tile(seq, (512, 256, 128))
    xp2 = x_prompt.reshape(batch * seq, D_MODEL)
    cos_p, sin_p = _rope_tables(jnp.arange(seq))
    pin = _in_stage(xp2, mods_p[1], mods_p[0], cos_p, sin_p, wp, tm, per_token=False)
    o_mla_p = _mla_prompt(pin["qcat"], pin["kcat"], pin["vmla"], batch, seq,
                          _pick_tile(seq, (256, 128)), _pick_tile(seq, (512, 256, 128)))
    o_dsa_p = _dsa_prompt(rel_bias, pin["qidx"], pin["tail"], pin["kidxdup"], pin["qdsa"], pin["kdup"], pin["vone"],
                          batch, seq, _pick_tile(seq, (256, 128)))
    y_p = _out_stage(xp2, o_mla_p, o_dsa_p, mods_p[2], mods_p[3], mods_p[4], mods_p[5], wp, tm,
                     per_token=False, tiles_per_seq=seq // tm)

    xs2 = x_sample.reshape(dec, D_MODEL)
    cos_s, sin_s = _rope_tables(jnp.full((1,), past))
    sin_ = _in_stage(xs2, mods_s[1], mods_s[0], cos_s, sin_s, wp, dec, per_token=True)
    pps_wide = _pick_tile(n_pages, (64, 32, 16, 8, 4, 2))
    qidx3 = sin_["qidx"].reshape(dec, IDX_HEADS, IDX_DIM)
    widx3 = sin_["tail"][:, T_WIDX:T_WIDX + IDX_HEADS].reshape(dec, IDX_HEADS, 1)
    scores = _sample_scores(page_table, qidx3, widx3, _pages_feature_major(cache_idx), pps_wide)
    selp, seln = _sample_select(scores, sin_["qidx"], sin_["tail"])
    qc = sin_["qcat"].reshape(dec, MLA_HEADS, LANES)
    qn_row = qc[:, :, 0:MLA_NOPE].astype(F32).reshape(dec, 1, MLA_HEADS * MLA_NOPE)
    qr = qc[:, :, T_ROPE:T_ROPE + MLA_ROPE]
    gk_row = jnp.tile(wp["g_k_nope"], MLA_HEADS).reshape(1, MLA_HEADS * MLA_NOPE)
    qd = sin_["qdsa"].astype(F32).reshape(dec, DSA_HEADS, DSA_HEAD_DIM)
    on_c = (jnp.arange(DSA_KV_HEADS)[None, :] == (jnp.arange(DSA_HEADS) // DSA_GROUP)[:, None]).astype(F32)
    qd2 = (qd[:, :, None, :] * on_c[None, :, :, None]).reshape(dec, DSA_HEADS, LANES)
    bias_past = _bias_table(rel_bias, 1, past, 0, -1, past).reshape(DSA_HEADS, past)
    bias_new = _bias_table(rel_bias, 1, LANES, 0, 0, 0)[:, 0, 0:1]
    o_mla_s, o8 = _decode_attention(
        page_table, qn_row, qr, sin_["row"].reshape(dec, MLA_ROW, 1), wp["w_uk_t"], wp["w_uv"], gk_row,
        _pages_feature_major(cache_mla),
        qd2, sin_["kv"].reshape(dec, 1, 256), selp.reshape(dec, 1, past), seln.reshape(dec, 1, LANES),
        bias_past, bias_new, _pages_feature_major(cache_kv), pps_wide)
    o_mla_s = o_mla_s.reshape(dec, MLA_HEADS * MLA_V)
    o_dsa_s = o8.reshape(dec, DSA_HEADS, DSA_KV_HEADS, DSA_HEAD_DIM).sum(axis=2).reshape(dec, DSA_HEADS * DSA_HEAD_DIM)
    y_s = _out_stage(xs2, o_mla_s, o_dsa_s, mods_s[2], mods_s[3], mods_s[4], mods_s[5], wp, dec,
                     per_token=True, tiles_per_seq=1)

    return (y_p.reshape(batch, seq, D_MODEL), y_s.reshape(dec, 1, D_MODEL),
            jnp.swapaxes(pin["row"], 1, 2)[None],
            jnp.moveaxis(pin["kv"].reshape(batch, 2, DSA_KV_HEADS, DSA_HEAD_DIM, seq), -1, 1)[None],
            jnp.swapaxes(pin["kidx"], 1, 2)[None],
            sin_["row"].reshape(1, dec, 1, MLA_ROW),
            sin_["kv"].reshape(1, dec, 1, 2, DSA_KV_HEADS, DSA_HEAD_DIM),
            sin_["kidx"].reshape(1, dec, 1, IDX_DIM))
```

```python
import functools
import math

import numpy as np
import jax
import jax.numpy as jnp
from jax import lax
from jax.experimental import pallas as pl
from jax.experimental.pallas import tpu as pltpu

D_MODEL = 1024
PAGE_SIZE = 128
MLA_HEADS = 8
MLA_NOPE = 64
MLA_ROPE = 32
MLA_V = 64
MLA_Q_RANK = 384
MLA_KV_RANK = 256
MLA_ROW = MLA_KV_RANK + MLA_ROPE
DSA_HEADS = 8
DSA_KV_HEADS = 2
DSA_HEAD_DIM = 64
DSA_GROUP = DSA_HEADS // DSA_KV_HEADS
IDX_HEADS = 8
IDX_DIM = 64
TOPK_MAX = 256
N_BUCKETS = 32
MAX_DISTANCE = 128
ROPE_THETA = 10000.0
D_FF = ((8 * D_MODEL + 3 * 256 - 1) // (3 * 256)) * 256
EPS = 1e-6
MLA_SCALE = (MLA_NOPE + MLA_ROPE) ** -0.5
DSA_SCALE = DSA_HEAD_DIM ** -0.5
IDX_SCALE = (IDX_DIM * IDX_HEADS) ** -0.5

LANES = 128
VMEM_LIMIT = 56 * 1024 * 1024

F32 = jnp.float32
BF16 = jnp.bfloat16
I32 = jnp.int32
NEG_BIG = -1e30
INT_MIN = -(2 ** 31)
HALF16 = 2 ** 15

C_QLAT = 0
C_KVLAT = C_QLAT + MLA_Q_RANK
C_Q = C_KVLAT + MLA_KV_RANK
C_K = C_Q + DSA_HEADS * DSA_HEAD_DIM
C_V = C_K + DSA_KV_HEADS * DSA_HEAD_DIM
C_QIDX = C_V + DSA_KV_HEADS * DSA_HEAD_DIM
C_TAIL = C_QIDX + IDX_HEADS * IDX_DIM
C_END = C_TAIL + LANES
T_ROPE = IDX_DIM
T_WIDX = IDX_DIM + MLA_ROPE
HALF_ROPE = MLA_ROPE // 2
KB = 128


def _dot(a, b):
    return jnp.dot(a, b, preferred_element_type=F32)


def _dot_nt(a, b):
    return lax.dot_general(a, b, (((1,), (1,)), ((), ())), preferred_element_type=F32)


def _split(a):
    hi = a.astype(BF16)
    lo = (a - hi.astype(F32)).astype(BF16)
    return hi, lo


def _dot3(a, b):
    ah, al = _split(a)
    bh, bl = _split(b)
    return _dot(ah, bh) + (_dot(al, bh) + _dot(ah, bl))


def _cparams(sem):
    return pltpu.CompilerParams(dimension_semantics=sem, vmem_limit_bytes=VMEM_LIMIT)


def _full(shape):
    n = len(shape)
    return pl.BlockSpec(shape, lambda *a, _n=n: (0,) * _n, pipeline_mode=pl.Buffered(1))


def _ada_kernel(c_ref, w_ref, b_ref, o_ref):
    c = c_ref[...]
    s = c / (1.0 + jnp.exp(-c))
    o_ref[...] = _dot3(s, w_ref[...]) + b_ref[...]


def _ada(c, w, b):
    n = c.shape[0]
    return pl.pallas_call(
        _ada_kernel,
        grid=(6,),
        in_specs=[pl.BlockSpec((n, D_MODEL), lambda j: (0, 0)),
                  pl.BlockSpec((D_MODEL, D_MODEL), lambda j: (0, j)),
                  pl.BlockSpec((1, D_MODEL), lambda j: (0, j))],
        out_specs=pl.BlockSpec((n, D_MODEL), lambda j: (0, j)),
        out_shape=jax.ShapeDtypeStruct((n, 6 * D_MODEL), F32),
        compiler_params=_cparams(("arbitrary",)),
    )(c, w, b.reshape(1, -1))


def _group_matrix(width, groups):
    g = np.zeros((width, width), np.float32)
    for lo, hi in groups:
        g[lo:hi, lo:hi] = 1.0 / (hi - lo)
    return jnp.asarray(g, BF16)


def _in_kernel(x_ref, sc_ref, sh_ref, cos_ref, sin_ref, gmix_ref, win_ref, gqlat_ref, wuq_ref, gqcat_ref,
               gkvlat_ref, wukv_ref, gkcat_ref, gtail_ref, gq_ref, gk_ref, mcat_ref, m64_ref,
               qcat_o, kcat_o, vmla_o, row_o, kv_o, kidx_o, tail_o, kidxdup_o, qdsa_o, kdup_o, vone_o, qidx_o,
               *, rows_feature_major):
    tm = x_ref.shape[0]
    x = x_ref[...]
    h = x * lax.rsqrt(jnp.mean(x * x, axis=-1, keepdims=True) + EPS) * gmix_ref[...]
    hb = (h * (1.0 + sc_ref[...]) + sh_ref[...]).astype(BF16)

    def proj(lo, hi):
        return _dot(hb, win_ref[:, lo:hi])

    cos = cos_ref[...]
    sin = sin_ref[...]
    lane = lax.broadcasted_iota(I32, (tm, LANES), 1)
    first_half = lane < T_ROPE + HALF_ROPE
    rope_lanes = (lane >= T_ROPE) & (lane < T_ROPE + MLA_ROPE)
    low64 = lane < 64

    def rope(xh):
        rot = jnp.where(first_half, pltpu.roll(xh, LANES - HALF_ROPE, 1), pltpu.roll(xh, HALF_ROPE, 1))
        return xh * cos + rot * sin

    def gnorm(blk, m_ref):
        ms = _dot((blk * blk).astype(BF16), m_ref[...])
        return blk * lax.rsqrt(ms + EPS)

    ql = proj(C_QLAT, C_KVLAT)
    qln = ql * lax.rsqrt(jnp.mean(ql * ql, axis=-1, keepdims=True) + EPS) * gqlat_ref[...]
    qm = _dot(qln.astype(BF16), wuq_ref[...])
    for j in range(4):
        y = gnorm(qm[:, 256 * j:256 * j + 256], mcat_ref) * gqcat_ref[:, 256 * j:256 * j + 256]
        for t in range(2):
            qcat_o[:, 256 * j + 128 * t:256 * j + 128 * t + 128] = rope(y[:, 128 * t:128 * t + 128]).astype(BF16)

    tail = proj(C_TAIL, C_END)
    ssq = jnp.sum(jnp.where(rope_lanes, tail * tail, 0.0), axis=-1, keepdims=True) * (1.0 / MLA_ROPE)
    tn = jnp.where(rope_lanes, tail * lax.rsqrt(ssq + EPS) * gtail_ref[...], tail)
    tr = rope(tn)
    if rows_feature_major:
        kidx_o[...] = tail.T[0:IDX_DIM, :]
    else:
        kidx_o[...] = tail[:, 0:IDX_DIM]
    tail_o[...] = tail
    kidxdup_o[...] = jnp.where(low64, tail, pltpu.roll(tail, 64, 1)).astype(BF16)
    krope = jnp.where(rope_lanes, tr, 0.0)

    kvl = proj(C_KVLAT, C_Q)
    lat = kvl * lax.rsqrt(jnp.mean(kvl * kvl, axis=-1, keepdims=True) + EPS) * gkvlat_ref[...]
    if rows_feature_major:
        row_o[0:MLA_KV_RANK, :] = lat.T
        row_o[MLA_KV_RANK:MLA_ROW, :] = tr.T[T_ROPE:T_ROPE + MLA_ROPE, :]
    else:
        row_o[:, 0:MLA_KV_RANK] = lat
        row_o[:, MLA_KV_RANK:MLA_ROW] = tr[:, T_ROPE:T_ROPE + MLA_ROPE]
    latb = lat.astype(BF16)
    kx = _dot(latb, wukv_ref[:, 0:1024])
    for j in range(4):
        y = gnorm(kx[:, 256 * j:256 * j + 256], mcat_ref) * gkcat_ref[:, 256 * j:256 * j + 256]
        for t in range(2):
            kcat_o[:, 256 * j + 128 * t:256 * j + 128 * t + 128] = (y[:, 128 * t:128 * t + 128] + krope).astype(BF16)
    ones_half = (lax.broadcasted_iota(I32, (tm, 1024), 1) % LANES) >= MLA_V
    vmla_o[...] = jnp.where(ones_half, 1.0, _dot(latb, wukv_ref[:, 1024:2048])).astype(BF16)

    q = proj(C_Q, C_K)
    for j in range(2):
        qdsa_o[:, 256 * j:256 * j + 256] = (gnorm(q[:, 256 * j:256 * j + 256], m64_ref)
                                            * gq_ref[:, 256 * j:256 * j + 256] * DSA_SCALE).astype(BF16)
    k = proj(C_K, C_V)
    ms = _dot((k * k).astype(BF16), m64_ref[0:128, 0:128])
    kn = k * lax.rsqrt(ms + EPS) * gk_ref[...]
    v = proj(C_V, C_QIDX)
    if rows_feature_major:
        kv_o[0:128, :] = kn.T
        kv_o[128:256, :] = v.T
    else:
        kv_o[:, 0:128] = kn
        kv_o[:, 128:256] = v
    kr = pltpu.roll(kn, 64, 1)
    kdup_o[:, 0:128] = jnp.where(low64, kn, kr).astype(BF16)
    kdup_o[:, 128:256] = jnp.where(low64, kr, kn).astype(BF16)
    vone_o[:, 0:128] = jnp.where(low64, v, 1.0).astype(BF16)
    vone_o[:, 128:256] = jnp.where(low64, pltpu.roll(v, 64, 1), 1.0).astype(BF16)
    qidx_o[...] = proj(C_QIDX, C_TAIL).astype(BF16)


_IN_OUT_WIDTHS = (("qcat", 1024, BF16), ("kcat", 1024, BF16), ("vmla", 1024, BF16), ("row", MLA_ROW, F32),
                  ("kv", 256, F32), ("kidx", IDX_DIM, F32), ("tail", LANES, F32), ("kidxdup", LANES, BF16),
                  ("qdsa", 512, BF16), ("kdup", 256, BF16), ("vone", 256, BF16), ("qidx", 512, BF16))


def _in_stage(x2d, scale, shift, cos, sin, wp, tm, per_token):
    t_total = x2d.shape[0]
    n_tiles = t_total // tm
    if per_token:
        mod_spec = pl.BlockSpec((tm, D_MODEL), lambda i: (i, 0))
        tab_spec = pl.BlockSpec((1, LANES), lambda i: (0, 0))
    else:
        tiles_per_seq = cos.shape[0] // tm
        mod_spec = pl.BlockSpec((None, 1, D_MODEL), lambda i: (i // tiles_per_seq, 0, 0))
        tab_spec = pl.BlockSpec((tm, LANES), lambda i: (i % tiles_per_seq, 0))
    consts = [wp["g_mix"], wp["w_in"], wp["g_qlat"], wp["w_uq"], wp["g_qcat"], wp["g_kvlat"], wp["w_ukv"],
              wp["g_kcat"], wp["g_tail"], wp["g_q"], wp["g_k"], wp["m_cat"], wp["m_64"]]
    in_specs = ([pl.BlockSpec((tm, D_MODEL), lambda i: (i, 0)), mod_spec, mod_spec, tab_spec, tab_spec]
                + [_full(c.shape) for c in consts])
    out_specs = [pl.BlockSpec((tm, w), lambda i: (i, 0)) for _, w, _ in _IN_OUT_WIDTHS]
    out_shape = [jax.ShapeDtypeStruct((t_total, w), dt) for _, w, dt in _IN_OUT_WIDTHS]
    if not per_token:
        for j, (name, w, dt) in enumerate(_IN_OUT_WIDTHS):
            if name in ("row", "kv", "kidx"):
                out_specs[j] = pl.BlockSpec((None, w, tm), lambda i: (i // tiles_per_seq, 0, i % tiles_per_seq))
                out_shape[j] = jax.ShapeDtypeStruct((n_tiles // tiles_per_seq, w, tiles_per_seq * tm), dt)
    outs = pl.pallas_call(
        functools.partial(_in_kernel, rows_feature_major=not per_token),
        grid=(n_tiles,), in_specs=in_specs, out_specs=out_specs, out_shape=out_shape,
        compiler_params=_cparams(("parallel",)),
    )(x2d, scale, shift, cos, sin, *consts)
    return {name: o for (name, _, _), o in zip(_IN_OUT_WIDTHS, outs)}


def _mla_prompt_kernel(q_ref, k_ref, v_ref, o_ref, m_scr, acc_scr, s_scr, *, tk):
    tq = q_ref.shape[0]
    qi = pl.program_id(1)
    n_chunks = (qi * tq + tq + tk - 1) // tk
    m_scr[...] = jnp.full(m_scr.shape, -jnp.inf, F32)
    acc_scr[...] = jnp.zeros(acc_scr.shape, F32)
    low64 = lax.broadcasted_iota(I32, (tq, LANES), 1) < 64
    qpos = qi * tq + lax.broadcasted_iota(I32, (tq, tk), 0)
    kcol = lax.broadcasted_iota(I32, (tq, tk), 1)
    exp_scale = MLA_SCALE * math.log2(math.e)

    def raw_scores(c, h, masked):
        off = pl.multiple_of(c * tk, tk)
        s = _dot_nt(q_ref[:, h * 128:(h + 1) * 128], k_ref[pl.ds(off, tk), h * 128:(h + 1) * 128])
        if masked:
            s = jnp.where(c * tk + kcol <= qpos, s, -jnp.inf)
        return s

    def pass_max(c, masked):
        for h in range(MLA_HEADS):
            s = raw_scores(c, h, masked)
            s_scr[c, h] = s
            m = m_scr[h]
            for j in range(tk // LANES):
                m = jnp.maximum(m, s[:, j * LANES:(j + 1) * LANES])
            m_scr[h] = m

    def pass_sum(c, masked):
        off = pl.multiple_of(c * tk, tk)
        for h in range(MLA_HEADS):
            s = s_scr[c, h]
            m = m_scr[h]
            p = jnp.concatenate([jnp.exp2((s[:, j * LANES:(j + 1) * LANES] - m) * exp_scale)
                                 for j in range(tk // LANES)], axis=1).astype(BF16)
            acc_scr[h] += _dot(p, v_ref[pl.ds(off, tk), h * 128:(h + 1) * 128])

    def run(step):
        def body(c, carry):
            step(c, False)
            return carry

        lax.fori_loop(0, n_chunks - 1, body, 0)
        step(n_chunks - 1, True)

    run(pass_max)
    for h in range(MLA_HEADS):
        m_scr[h] = jnp.broadcast_to(jnp.max(m_scr[h], axis=-1, keepdims=True), (tq, LANES))
    run(pass_sum)
    for p in range(MLA_HEADS // 2):
        even, odd = acc_scr[2 * p], acc_scr[2 * p + 1]
        o_ref[:, 128 * p:128 * p + 128] = jnp.where(low64, even / pltpu.roll(even, 64, 1),
                                                    pltpu.roll(odd, 64, 1) / odd)


def _mla_prompt(qcat, kcat, vmla, batch, seq, tq, tk):
    nq = seq // tq
    return pl.pallas_call(
        functools.partial(_mla_prompt_kernel, tk=tk),
        grid=(batch, nq),
        in_specs=[pl.BlockSpec((tq, 1024), lambda b, i: (b * nq + i, 0)),
                  pl.BlockSpec((seq, 1024), lambda b, i: (b, 0)),
                  pl.BlockSpec((seq, 1024), lambda b, i: (b, 0))],
        out_specs=pl.BlockSpec((tq, 512), lambda b, i: (b * nq + i, 0)),
        out_shape=jax.ShapeDtypeStruct((batch * seq, 512), F32),
        scratch_shapes=[pltpu.VMEM((MLA_HEADS, tq, LANES), F32), pltpu.VMEM((MLA_HEADS, tq, LANES), F32),
                        pltpu.VMEM((seq // tk, MLA_HEADS, tq, tk), F32)],
        compiler_params=_cparams(("parallel", "arbitrary")),
    )(qcat, kcat, vmla)


def _sort_key(score):
    bits = lax.bitcast_convert_type(score + 0.0, I32)
    return bits ^ ((bits >> 31) & 0x7FFFFFFF)


def _kth_key(count_ge, n_sel, shape, bits=32):
    def body(i, t):
        cand = t + (jnp.int32(1) << (bits - 1 - i))
        return jnp.where(count_ge(cand) >= n_sel, cand, t)

    return lax.fori_loop(0, bits, body, jnp.full(shape, -(2 ** (bits - 1)), I32))


def _bucket(dist):
    max_exact = N_BUCKETS // 2
    d = jnp.maximum(dist, 0)
    log_ratio = jnp.log(jnp.maximum(d, max_exact).astype(F32) / max_exact) / math.log(MAX_DISTANCE / max_exact)
    large = jnp.minimum(max_exact + (log_ratio * (N_BUCKETS - max_exact)).astype(I32), N_BUCKETS - 1)
    return jnp.where(d < max_exact, d, large)


def _bias_kernel(rb_ref, o_ref, *, a, b, c):
    rows, cols = o_ref.shape[1], o_ref.shape[2]
    dist = (a * lax.broadcasted_iota(I32, (rows, cols), 0) + b * lax.broadcasted_iota(I32, (rows, cols), 1) + c)
    bucket = _bucket(dist)
    for h in range(DSA_HEADS):
        acc = jnp.zeros((rows, cols), F32)
        for n in range(N_BUCKETS):
            acc = jnp.where(bucket == n, rb_ref[n, h], acc)
        o_ref[h] = acc


def _bias_table(rel_bias, rows, cols, a, b, c):
    return pl.pallas_call(
        functools.partial(_bias_kernel, a=a, b=b, c=c),
        in_specs=[pl.BlockSpec(memory_space=pltpu.SMEM)],
        out_specs=pl.BlockSpec(memory_space=pltpu.VMEM),
        out_shape=jax.ShapeDtypeStruct((DSA_HEADS, rows, cols), F32),
    )(rel_bias)


def _loop2(n, body):
    n = jnp.maximum(n, 0)

    def pair(j, carry):
        body(2 * j)
        body(2 * j + 1)
        return carry

    lax.fori_loop(0, n // 2, pair, 0)

    @pl.when(n % 2 == 1)
    def _():
        body(n - 1)


def _colreduce(x, op):
    return op(x.reshape(x.shape[0] // 8, 8, x.shape[1]), axis=0)


def _dsa_prompt_kernel(rb_ref, qidx_ref, tail_ref, kidx_ref, q_ref, k_ref, v_ref, bias_ref, o_ref,
                       qs_scr, qd_scr, wb_scr, key_scr, hi_scr, lo_scr, tie_scr, m_scr, p_scr, acc_scr, s_scr,
                       *, n_sel, idx_bits):
    tq = q_ref.shape[0]
    kq = tq // KB
    qi = pl.program_id(1)
    krow = lax.broadcasted_iota(I32, (KB, tq), 0)
    qcol = lax.broadcasted_iota(I32, (KB, tq), 1)
    low64 = lax.broadcasted_iota(I32, (tq, LANES), 1) < 64

    tail = tail_ref[...]
    for h in range(IDX_HEADS):
        mine = low64 if h % 2 == 0 else jnp.logical_not(low64)
        zero = jnp.zeros((tq, LANES), BF16)
        qs_scr[h * tq:(h + 1) * tq, :] = jnp.where(mine, qidx_ref[:, (h // 2) * 128:(h // 2 + 1) * 128], zero)
        qd_scr[h * tq:(h + 1) * tq, :] = jnp.where(mine, q_ref[:, (h // 2) * 128:(h // 2 + 1) * 128], zero)
        wb_scr[h] = jnp.broadcast_to(tail[:, T_WIDX + h:T_WIDX + h + 1], (tq, LANES))

    def rows_of(c):
        return pl.ds(pl.multiple_of(c * tq, tq), tq)

    def score_chunk(c, in_tile):
        d = _dot_nt(qs_scr[...], kidx_ref[rows_of(c), :])
        rb = 64
        for t in range(kq):
            parts = []
            for r in range(0, tq, rb):
                sc = jnp.zeros((rb, KB), F32)
                for h in range(IDX_HEADS):
                    sc = sc + (jnp.maximum(d[h * tq + r:h * tq + r + rb, t * KB:(t + 1) * KB], 0.0)
                               * wb_scr[h, r:r + rb, :])
                parts.append(sc)
            sc_t = (jnp.concatenate(parts, axis=0) * IDX_SCALE).T
            if in_tile:
                sc_t = jnp.where(t * KB + krow <= qcol, sc_t, -jnp.inf)
            key = _sort_key(sc_t)
            key_scr[c * kq + t] = key
            hi_scr[c * kq + t] = (key >> 16).astype(jnp.int16)
            lo_scr[c * kq + t] = ((key & 0xFFFF) - HALF16).astype(jnp.int16)

    _loop2(qi, lambda c: score_chunk(c, False))
    score_chunk(qi, True)
    n_chunks = qi + 1

    def count(pred):
        part = 32
        def body(c, acc):
            for t in range(kq):
                kb = c * kq + t
                hit = jnp.where(pred(key_scr[kb], kb), 1.0, 0.0)
                acc = acc + jnp.sum(hit.reshape(KB // part, part, tq), axis=0)
            return acc
        acc = lax.fori_loop(0, n_chunks, body, jnp.zeros((part, tq), F32))
        return jnp.sum(_colreduce(acc, jnp.sum), axis=0, keepdims=True)

    def count16(src, pred):
        part = 32
        one, zero = jnp.ones((), BF16), jnp.zeros((), BF16)
        def body(c, acc):
            for t in range(kq):
                hit = jnp.where(pred(src[c * kq + t]), one, zero).reshape(KB // part, part, tq)
                for j in range(KB // part):
                    acc = acc + hit[j]
            return acc
        acc = lax.fori_loop(0, n_chunks, body, jnp.zeros((part, tq), BF16))
        return jnp.sum(_colreduce(acc.astype(F32), jnp.sum), axis=0, keepdims=True)

    t_hi = _kth_key(lambda cand: count16(hi_scr, lambda b: b >= cand.astype(jnp.int16)),
                    float(n_sel), (1, tq), bits=16)
    t_hi16 = t_hi.astype(jnp.int16)
    above = count16(hi_scr, lambda b: b > t_hi16)

    def keep_bucket(c, carry):
        for t in range(kq):
            kb = c * kq + t
            lo_scr[kb] = jnp.where(hi_scr[kb] == t_hi16, lo_scr[kb], jnp.int16(-HALF16))
        return carry

    lax.fori_loop(0, n_chunks, keep_bucket, 0)
    t_lo = _kth_key(lambda cand: above + count16(lo_scr, lambda b: b >= cand.astype(jnp.int16)),
                    float(n_sel), (1, tq), bits=16)
    thr = t_hi * (2 * HALF16) + (t_lo + HALF16)
    cnt_gt = count(lambda k, kb: k > thr)
    cnt_ge = count(lambda k, kb: k >= thr)
    tie_scr[...] = jnp.full((1, tq), 2 ** 30, I32)
    excess = jnp.where((cnt_ge > float(n_sel)) & (thr > INT_MIN), 1.0, 0.0)

    @pl.when(jnp.max(excess) > 0.0)
    def _():
        need = float(n_sel) - cnt_gt

        def body(i, p):
            cand = p + (jnp.int32(1) << (idx_bits - 1 - i))
            c = count(lambda k, kb: (k == thr) & (kb * KB + krow < cand))
            return jnp.where(c < need, cand, p)

        tie_scr[...] = lax.fori_loop(0, idx_bits, body, jnp.zeros((1, tq), I32))

    last_tie = tie_scr[...]

    def mask_chunk(c, in_tile):
        halves = []
        for t in range(kq):
            kb = c * kq + t
            key = key_scr[kb]
            sel = (key > thr) | ((key == thr) & (kb * KB + krow <= last_tie))
            if in_tile:
                sel = sel & (t * KB + krow <= qcol)
            halves.append(jnp.where(sel, 0.0, NEG_BIG).T)
        return halves

    far_bias = [rb_ref[N_BUCKETS - 1, h] for h in range(DSA_HEADS)]

    def scores(c, g, kind, mask):
        sg = _dot_nt(qd_scr[g * DSA_GROUP * tq:(g + 1) * DSA_GROUP * tq, :], k_ref[rows_of(c), g * 128:(g + 1) * 128])
        out = []
        for hh in range(DSA_GROUP):
            h = g * DSA_GROUP + hh
            cols = []
            for t in range(kq):
                s = sg[hh * tq:(hh + 1) * tq, t * KB:(t + 1) * KB] + mask[t]
                if kind == "tile":
                    s = s + (bias_ref[1 + t, h] - far_bias[h])
                elif kind == "before" and t == kq - 1:
                    s = s + (bias_ref[0, h] - far_bias[h])
                cols.append(s)
            out.append(cols)
        return out

    m_scr[...] = jnp.full(m_scr.shape, NEG_BIG, F32)
    acc_scr[...] = jnp.zeros(acc_scr.shape, F32)

    def pass_max(c, kind):
        mask = mask_chunk(c, kind == "tile")
        for g in range(DSA_KV_HEADS):
            for hh, cols in enumerate(scores(c, g, kind, mask)):
                h = g * DSA_GROUP + hh
                m = m_scr[h]
                for t, s in enumerate(cols):
                    s_scr[c, h, :, t * KB:(t + 1) * KB] = s
                    m = jnp.maximum(m, s)
                m_scr[h] = m

    def pass_sum(c, kind):
        for g in range(DSA_KV_HEADS):
            for hh in range(DSA_GROUP):
                h = g * DSA_GROUP + hh
                m = m_scr[h]
                for t in range(kq):
                    s = s_scr[c, h, :, t * KB:(t + 1) * KB]
                    p_scr[g, hh * tq:(hh + 1) * tq, t * KB:(t + 1) * KB] = jnp.exp(s - m).astype(BF16)
            acc_scr[g] += _dot(p_scr[g], v_ref[rows_of(c), g * 128:(g + 1) * 128])

    def run(step):
        _loop2(qi - 1, lambda c: step(c, "far"))

        @pl.when(qi >= 1)
        def _():
            step(qi - 1, "before")

        step(qi, "tile")

    run(pass_max)
    for h in range(DSA_HEADS):
        m_scr[h] = jnp.broadcast_to(jnp.max(m_scr[h], axis=-1, keepdims=True), (tq, LANES))
    run(pass_sum)

    for p in range(DSA_HEADS // 2):
        g, hh = (2 * p) // DSA_GROUP, (2 * p) % DSA_GROUP
        even = acc_scr[g, hh * tq:(hh + 1) * tq, :]
        odd = acc_scr[g, (hh + 1) * tq:(hh + 2) * tq, :]
        o_ref[:, 128 * p:128 * p + 128] = jnp.where(low64, even / pltpu.roll(even, 64, 1),
                                                    pltpu.roll(odd, 64, 1) / odd)


def _dsa_prompt(rel_bias, qidx, tail, kidxdup, qdsa, kdup, vone, batch, seq, tq):
    assert MAX_DISTANCE <= KB, "keys two or more blocks back must all fall in the last bucket"
    nq = seq // tq
    nkb = seq // KB
    kq = tq // KB
    n_sel = min(TOPK_MAX, seq // 4)
    idx_bits = max(1, (seq - 1).bit_length())
    bias = jnp.stack([_bias_table(rel_bias, tq, KB, 1, -1, KB - t * KB) for t in range(kq + 1)])
    kern = functools.partial(_dsa_prompt_kernel, n_sel=n_sel, idx_bits=idx_bits)
    qblk = lambda w: pl.BlockSpec((tq, w), lambda b, i: (b * nq + i, 0))
    seqblk = lambda w: pl.BlockSpec((seq, w), lambda b, i: (b, 0))
    return pl.pallas_call(
        kern,
        grid=(batch, nq),
        in_specs=[pl.BlockSpec(memory_space=pltpu.SMEM),
                  qblk(512), qblk(LANES), seqblk(LANES), qblk(512), seqblk(256), seqblk(256), _full(bias.shape)],
        out_specs=qblk(512),
        out_shape=jax.ShapeDtypeStruct((batch * seq, 512), F32),
        scratch_shapes=[pltpu.VMEM((IDX_HEADS * tq, LANES), BF16), pltpu.VMEM((DSA_HEADS * tq, LANES), BF16),
                        pltpu.VMEM((IDX_HEADS, tq, LANES), F32),
                        pltpu.VMEM((nkb, KB, tq), I32),
                        pltpu.VMEM((nkb, KB, tq), jnp.int16), pltpu.VMEM((nkb, KB, tq), jnp.int16),
                        pltpu.VMEM((1, tq), I32), pltpu.VMEM((DSA_HEADS, tq, LANES), F32),
                        pltpu.VMEM((DSA_KV_HEADS, DSA_GROUP * tq, tq), BF16),
                        pltpu.VMEM((DSA_KV_HEADS, DSA_GROUP * tq, LANES), F32),
                        pltpu.VMEM((nq, DSA_HEADS, tq, tq), F32)],
        compiler_params=_cparams(("parallel", "arbitrary")),
    )(rel_bias, qidx, tail, kidxdup, qdsa, kdup, vone, bias)


def _fetch_pages(cache_ref, pt_ref, buf, sem, step, n_steps, locate, pps):
    def copy(slot, i, page):
        return pltpu.make_async_copy(cache_ref.at[0, page], buf.at[slot, i], sem.at[slot])

    def start(st, slot):
        b, first = locate(st)
        for i in range(pps):
            copy(slot, i, pt_ref[b, first + i]).start()

    @pl.when(step == 0)
    def _():
        start(step, 0)

    @pl.when(step + 1 < n_steps)
    def _():
        start(step + 1, (step + 1) % 2)

    slot = step % 2
    for i in range(pps):
        copy(slot, i, 0).wait()
    return slot


def _page_scratch(pps, width):
    return [pltpu.VMEM((2, pps, width, PAGE_SIZE), F32), pltpu.SemaphoreType.DMA((2,))]


def _pages_feature_major(cache):
    c = jnp.moveaxis(cache, 2, -1)
    return c.reshape(cache.shape[0], cache.shape[1], -1, PAGE_SIZE)


def _sample_score_kernel(pt_ref, q_ref, w_ref, cache_ref, o_ref, buf, sem, kb_scr, *, pps, grp):
    bo, c, bi = pl.program_id(0), pl.program_id(1), pl.program_id(2)
    n_chunks = pl.num_programs(1)
    step = (bo * n_chunks + c) * grp + bi

    def locate(st):
        return (st // (n_chunks * grp)) * grp + st % grp, ((st // grp) % n_chunks) * pps

    slot = _fetch_pages(cache_ref, pt_ref, buf, sem, step, pl.num_programs(0) * n_chunks * grp, locate, pps)
    for i in range(pps):
        kb_scr[:, i * PAGE_SIZE:(i + 1) * PAGE_SIZE] = buf[slot, i].astype(BF16)
    d = _dot(q_ref[...], kb_scr[...])
    sc = jnp.sum(jnp.maximum(d, 0.0) * w_ref[...], axis=0, keepdims=True) * IDX_SCALE
    o_ref[pl.ds(bi, 1), :] = sc


def _sample_scores(page_table, qidx3, widx3, cache_idx_t, pps):
    dec, n_pages = page_table.shape
    n_chunks = n_pages // pps
    grp = 8
    grid_spec = pltpu.PrefetchScalarGridSpec(
        num_scalar_prefetch=1,
        grid=(dec // grp, n_chunks, grp),
        in_specs=[pl.BlockSpec((None, IDX_HEADS, IDX_DIM), lambda bo, c, bi, pt: (bo * grp + bi, 0, 0)),
                  pl.BlockSpec((None, IDX_HEADS, 1), lambda bo, c, bi, pt: (bo * grp + bi, 0, 0)),
                  pl.BlockSpec(memory_space=pl.ANY)],
        out_specs=pl.BlockSpec((grp, pps * PAGE_SIZE), lambda bo, c, bi, pt: (bo, c)),
        scratch_shapes=_page_scratch(pps, IDX_DIM) + [pltpu.VMEM((IDX_DIM, pps * PAGE_SIZE), BF16)],
    )
    return pl.pallas_call(
        functools.partial(_sample_score_kernel, pps=pps, grp=grp), grid_spec=grid_spec,
        out_shape=jax.ShapeDtypeStruct((dec, n_pages * PAGE_SIZE), F32),
        compiler_params=_cparams(("arbitrary", "arbitrary", "arbitrary")),
    )(page_table, qidx3, widx3, cache_idx_t)


def _sample_select_kernel(sc_ref, qidx_ref, tail_ref, selp_ref, seln_ref, key_scr, *, n_sel, idx_bits):
    rows, past = sc_ref.shape
    nblk = past // LANES
    lane = lax.broadcasted_iota(I32, (rows, LANES), 1)
    tail = tail_ref[...]
    kidx2 = jnp.where(lane < 64, tail, pltpu.roll(tail, 64, 1))
    sc_new = jnp.zeros((rows, 1), F32)
    for h in range(IDX_HEADS):
        prod = qidx_ref[:, (h // 2) * 128:(h // 2 + 1) * 128].astype(F32) * kidx2.astype(BF16).astype(F32)
        mine = (lane < 64) if h % 2 == 0 else (lane >= 64)
        dot = jnp.sum(jnp.where(mine, prod, 0.0), axis=-1, keepdims=True)
        sc_new = sc_new + jnp.maximum(dot, 0.0) * tail[:, T_WIDX + h:T_WIDX + h + 1]
    key_new = _sort_key(sc_new * IDX_SCALE)

    for j in range(nblk):
        key_scr[j] = _sort_key(sc_ref[:, j * LANES:(j + 1) * LANES])

    def count(pred, pred_new):
        def body(j, c):
            return c + jnp.where(pred(key_scr[j], j), 1.0, 0.0)
        c = lax.fori_loop(0, nblk, body, jnp.zeros((rows, LANES), F32), unroll=4)
        return jnp.sum(c, axis=-1, keepdims=True) + jnp.where(pred_new, 1.0, 0.0)

    def wide(col):
        return jnp.broadcast_to(col, (rows, LANES))

    def count_ge(cand):
        cand_w = wide(cand)
        return count(lambda k, j: k >= cand_w, key_new >= cand)

    thr = _kth_key(count_ge, float(n_sel), (rows, 1))
    thr_w = wide(thr)
    cnt_gt = count(lambda k, j: k > thr_w, key_new > thr)
    need = float(n_sel) - cnt_gt

    def body(i, p):
        cand = p + (jnp.int32(1) << (idx_bits - 1 - i))
        cand_w = wide(cand)
        c = count(lambda k, j: (k == thr_w) & (j * LANES + lane < cand_w), (key_new == thr) & (past < cand))
        return jnp.where(c < need, cand, p)

    last_tie = lax.fori_loop(0, idx_bits, body, jnp.zeros((rows, 1), I32))
    last_w = wide(last_tie)

    for j in range(nblk):
        k = key_scr[j]
        sel = (k > thr_w) | ((k == thr_w) & (j * LANES + lane <= last_w))
        selp_ref[:, j * LANES:(j + 1) * LANES] = jnp.where(sel, 1.0, 0.0)
    sel_new = (key_new > thr) | ((key_new == thr) & (past <= last_tie))
    seln_ref[...] = jnp.broadcast_to(jnp.where(sel_new, 1.0, 0.0), (rows, LANES))


def _sample_select(scores, qidx, tail):
    dec, past = scores.shape
    n_sel = min(TOPK_MAX, (past + 1) // 4)
    idx_bits = max(1, past.bit_length())
    rows = 64 if dec % 64 == 0 else dec
    kern = functools.partial(_sample_select_kernel, n_sel=n_sel, idx_bits=idx_bits)
    return pl.pallas_call(
        kern,
        grid=(dec // rows,),
        in_specs=[pl.BlockSpec((rows, past), lambda i: (i, 0)), pl.BlockSpec((rows, 512), lambda i: (i, 0)),
                  pl.BlockSpec((rows, LANES), lambda i: (i, 0))],
        out_specs=[pl.BlockSpec((rows, past), lambda i: (i, 0)), pl.BlockSpec((rows, LANES), lambda i: (i, 0))],
        out_shape=[jax.ShapeDtypeStruct((dec, past), F32), jax.ShapeDtypeStruct((dec, LANES), F32)],
        scratch_shapes=[pltpu.VMEM((past // LANES, rows, LANES), I32)],
        compiler_params=_cparams(("parallel",)),
    )(scores, qidx, tail)


def _mla_sample_kernel(pt_ref, qn_ref, qr_ref, rown_ref, wuk_ref, wuv_ref, gk_ref, cache_ref, o_ref,
                       buf, sem, lat_scr, kr_scr, a_scr, m_scr, l_scr, acc_scr, *, pps):
    c = pl.program_id(1)
    n_chunks = pl.num_programs(1)
    heads = MLA_HEADS
    slot = _fetch_pages(cache_ref, pt_ref, buf, sem, pl.program_id(0) * n_chunks + c,
                        pl.num_programs(0) * n_chunks, lambda st: (st // n_chunks, (st % n_chunks) * pps), pps)

    @pl.when(c == 0)
    def _():
        m_scr[...] = jnp.full(m_scr.shape, -jnp.inf, F32)
        l_scr[...] = jnp.zeros(l_scr.shape, F32)
        acc_scr[...] = jnp.zeros(acc_scr.shape, F32)
        qg = qn_ref[...] * gk_ref[...]
        own = (lax.broadcasted_iota(I32, (heads, heads * MLA_NOPE), 1) // MLA_NOPE
               == lax.broadcasted_iota(I32, (heads, heads * MLA_NOPE), 0))
        n_up = heads * MLA_NOPE
        a_scr[0:n_up, :] = wuk_ref[...]
        a = _dot(jnp.where(own, qg, 0.0).astype(BF16), wuk_ref[...]).astype(BF16)
        a_scr[n_up:n_up + 16, :] = jnp.concatenate([a, jnp.zeros_like(a)], axis=0)

    qr = qr_ref[...]

    def scores(latb, krb):
        n_up = heads * MLA_NOPE
        up = _dot(a_scr[...], latb)
        knt = up[0:n_up, :]
        ms = jnp.mean((knt * knt).reshape(heads, MLA_NOPE, latb.shape[1]), axis=1)
        return (up[n_up:n_up + heads, :] * lax.rsqrt(ms + EPS) + _dot(qr, krb)) * MLA_SCALE

    def update(latb, krb, n_valid):
        n = latb.shape[1]
        sub = min(n, 512)
        s = jnp.concatenate([scores(latb[:, j:j + sub], krb[:, j:j + sub]) for j in range(0, n, sub)], axis=1)
        if n_valid < n:
            s = jnp.where(lax.broadcasted_iota(I32, s.shape, 1) < n_valid, s, -jnp.inf)
        m_old = m_scr[...]
        m_new = jnp.maximum(m_old, jnp.max(s, axis=-1, keepdims=True))
        alpha = jnp.exp(m_old - m_new)
        p = jnp.exp(s - m_new)
        l_scr[...] = alpha * l_scr[...] + jnp.sum(p, axis=-1, keepdims=True)
        m_scr[...] = m_new
        acc_scr[...] = alpha * acc_scr[...] + _dot_nt(p.astype(BF16), latb)

    for i in range(pps):
        lat_scr[:, i * PAGE_SIZE:(i + 1) * PAGE_SIZE] = buf[slot, i, 0:MLA_KV_RANK, :].astype(BF16)
        kr_scr[:, i * PAGE_SIZE:(i + 1) * PAGE_SIZE] = buf[slot, i, MLA_KV_RANK:MLA_ROW, :].astype(BF16)
    update(lat_scr[...], kr_scr[...], lat_scr.shape[1])

    @pl.when(c == n_chunks - 1)
    def _():
        rn = jnp.broadcast_to(rown_ref[...], (MLA_ROW, PAGE_SIZE))
        update(rn[0:MLA_KV_RANK, :].astype(BF16), rn[MLA_KV_RANK:MLA_ROW, :].astype(BF16), 1)
        o_lat = (acc_scr[...] / l_scr[...]).astype(BF16)
        full = _dot(o_lat, wuv_ref[...])
        hd = lax.broadcasted_iota(I32, full.shape, 1) // MLA_V
        own = hd == lax.broadcasted_iota(I32, full.shape, 0)
        o_ref[...] = jnp.sum(jnp.where(own, full, 0.0), axis=0, keepdims=True)


def _dsa_sample_kernel(pt_ref, q_ref, kvn_ref, selp_ref, seln_ref, biasp_ref, biasn_ref, cache_ref, o_ref,
                       buf, sem, k_scr, v_scr, m_scr, l_scr, acc_scr, *, pps):
    c = pl.program_id(1)
    n_chunks = pl.num_programs(1)
    slot = _fetch_pages(cache_ref, pt_ref, buf, sem, pl.program_id(0) * n_chunks + c,
                        pl.num_programs(0) * n_chunks, lambda st: (st // n_chunks, (st % n_chunks) * pps), pps)

    @pl.when(c == 0)
    def _():
        m_scr[...] = jnp.full(m_scr.shape, NEG_BIG, F32)
        l_scr[...] = jnp.zeros(l_scr.shape, F32)
        acc_scr[...] = jnp.zeros(acc_scr.shape, F32)

    q = q_ref[...]

    def update(s, sel, pv_of):
        sh = jnp.where(sel, s, NEG_BIG)
        m_old = m_scr[...]
        m_new = jnp.maximum(m_old, jnp.max(sh, axis=-1, keepdims=True))
        alpha = jnp.exp(m_old - m_new)
        p = jnp.where(sel, jnp.exp(sh - m_new), 0.0)
        l_scr[...] = alpha * l_scr[...] + jnp.sum(p, axis=-1, keepdims=True)
        m_scr[...] = m_new
        acc_scr[...] = alpha * acc_scr[...] + pv_of(p)

    for i in range(pps):
        k_scr[:, i * PAGE_SIZE:(i + 1) * PAGE_SIZE] = buf[slot, i, 0:128, :].astype(BF16)
        v_scr[:, i * PAGE_SIZE:(i + 1) * PAGE_SIZE] = buf[slot, i, 128:256, :].astype(BF16)
    s = _dot(q.astype(BF16), k_scr[...]) + biasp_ref[...]
    update(s, selp_ref[...] > 0.5, lambda p: _dot_nt(p.astype(BF16), v_scr[...]))

    @pl.when(c == n_chunks - 1)
    def _():
        kvn = kvn_ref[...]
        kn = kvn[:, 0:128].astype(BF16).astype(F32)
        vn = kvn[:, 128:256].astype(BF16).astype(F32)
        s_new = jnp.sum(q.astype(F32) * kn, axis=-1, keepdims=True) + biasn_ref[...]
        update(s_new, seln_ref[:, 0:1] > 0.5, lambda p: p.astype(BF16).astype(F32) * vn)
        o = acc_scr[...] / l_scr[...]
        lane = lax.broadcasted_iota(I32, o.shape, 1)
        row = lax.broadcasted_iota(I32, o.shape, 0)
        own = (lane // DSA_HEAD_DIM) == (row // DSA_GROUP)
        o_ref[...] = jnp.where(own, o, 0.0)


_N_MLA_IN, _N_MLA_SCRATCH = 7, 8
_N_DSA_IN = 7


def _decode_kernel(pt_ref, *refs, pps):
    mla_in = refs[:_N_MLA_IN]
    dsa_in = refs[_N_MLA_IN:_N_MLA_IN + _N_DSA_IN]
    o_mla, o_dsa = refs[_N_MLA_IN + _N_DSA_IN:_N_MLA_IN + _N_DSA_IN + 2]
    scratch = refs[_N_MLA_IN + _N_DSA_IN + 2:]
    _mla_sample_kernel(pt_ref, *mla_in, o_mla, *scratch[:_N_MLA_SCRATCH], pps=pps)
    _dsa_sample_kernel(pt_ref, *dsa_in, o_dsa, *scratch[_N_MLA_SCRATCH:], pps=pps)


def _decode_attention(page_table, qn_row, qr, row_new, wuk_t, wuv, gk_row, cache_mla,
                      qs, kv_new, selp, seln, bias_past, bias_new, cache_kv4, pps):
    dec, n_pages = page_table.shape
    n_chunks = n_pages // pps
    chunk = pps * PAGE_SIZE
    per_b = lambda *shape: pl.BlockSpec((None,) + shape, lambda b, c, pt: (b,) + (0,) * len(shape))
    const = lambda a: pl.BlockSpec(a.shape, lambda b, c, pt: (0,) * a.ndim)
    grid_spec = pltpu.PrefetchScalarGridSpec(
        num_scalar_prefetch=1,
        grid=(dec, n_chunks),
        in_specs=[per_b(1, MLA_HEADS * MLA_NOPE), per_b(MLA_HEADS, MLA_ROPE), per_b(MLA_ROW, 1),
                  const(wuk_t), const(wuv), const(gk_row), pl.BlockSpec(memory_space=pl.ANY),
                  per_b(DSA_HEADS, LANES), per_b(1, 256),
                  pl.BlockSpec((None, 1, chunk), lambda b, c, pt: (b, 0, c)), per_b(1, LANES),
                  pl.BlockSpec((DSA_HEADS, chunk), lambda b, c, pt: (0, c)), const(bias_new),
                  pl.BlockSpec(memory_space=pl.ANY)],
        out_specs=[per_b(1, MLA_HEADS * MLA_V), per_b(DSA_HEADS, LANES)],
        scratch_shapes=_page_scratch(pps, MLA_ROW) + [
            pltpu.VMEM((MLA_KV_RANK, chunk), BF16), pltpu.VMEM((MLA_ROPE, chunk), BF16),
            pltpu.VMEM((MLA_HEADS * MLA_NOPE + 16, MLA_KV_RANK), BF16),
            pltpu.VMEM((MLA_HEADS, 1), F32), pltpu.VMEM((MLA_HEADS, 1), F32),
            pltpu.VMEM((MLA_HEADS, MLA_KV_RANK), F32)]
        + _page_scratch(pps, 256) + [
            pltpu.VMEM((128, chunk), BF16), pltpu.VMEM((128, chunk), BF16),
            pltpu.VMEM((DSA_HEADS, 1), F32), pltpu.VMEM((DSA_HEADS, 1), F32),
            pltpu.VMEM((DSA_HEADS, LANES), F32)],
    )
    return pl.pallas_call(
        functools.partial(_decode_kernel, pps=pps), grid_spec=grid_spec,
        out_shape=[jax.ShapeDtypeStruct((dec, 1, MLA_HEADS * MLA_V), F32),
                   jax.ShapeDtypeStruct((dec, DSA_HEADS, LANES), F32)],
        compiler_params=_cparams(("arbitrary", "arbitrary")),
    )(page_table, qn_row, qr, row_new, wuk_t, wuv, gk_row, cache_mla,
      qs, kv_new, selp, seln, bias_past, bias_new, cache_kv4)


def _out_kernel(x_ref, oa_ref, ob_ref, gm_ref, sf_ref, scf_ref, gf_ref, gout_ref, gffn_ref,
                wout_ref, wg_ref, wu_ref, wdown_ref, y_ref, *, ff_chunk):
    def rms(v, g):
        return v * lax.rsqrt(jnp.mean(v * v, axis=-1, keepdims=True) + EPS) * g

    half = oa_ref.shape[1]
    na = rms(oa_ref[...], gout_ref[:, 0:half]).astype(BF16)
    nb = rms(ob_ref[...], gout_ref[:, half:2 * half]).astype(BF16)
    mix = _dot(na, wout_ref[0:half, :]) + _dot(nb, wout_ref[half:2 * half, :])
    x1 = x_ref[...] + gm_ref[...] * mix
    hb = (rms(x1, gffn_ref[...]) * (1.0 + scf_ref[...]) + sf_ref[...]).astype(BF16)
    acc = jnp.zeros(x1.shape, F32)
    for j in range(D_FF // ff_chunk):
        g = _dot(hb, wg_ref[:, j * ff_chunk:(j + 1) * ff_chunk])
        u = _dot(hb, wu_ref[:, j * ff_chunk:(j + 1) * ff_chunk])
        a = (g / (1.0 + jnp.exp(-g))) * u
        acc = acc + _dot(a.astype(BF16), wdown_ref[j * ff_chunk:(j + 1) * ff_chunk, :])
    y_ref[...] = x1 + gf_ref[...] * acc


def _out_stage(x2d, oa, ob, gate_m, shift_f, scale_f, gate_f, wp, tm, per_token, tiles_per_seq):
    t_total = x2d.shape[0]
    if per_token:
        mod_spec = pl.BlockSpec((tm, D_MODEL), lambda i: (i, 0))
    else:
        mod_spec = pl.BlockSpec((None, 1, D_MODEL), lambda i: (i // tiles_per_seq, 0, 0))
    consts = [wp["g_out"], wp["g_ffn"], wp["w_out"], wp["w_gate"], wp["w_up"], wp["w_down"]]
    tok = lambda w: pl.BlockSpec((tm, w), lambda i: (i, 0))
    return pl.pallas_call(
        functools.partial(_out_kernel, ff_chunk=256),
        grid=(t_total // tm,),
        in_specs=[tok(D_MODEL), tok(512), tok(512), mod_spec, mod_spec, mod_spec, mod_spec]
                 + [_full(c.shape) for c in consts],
        out_specs=tok(D_MODEL),
        out_shape=jax.ShapeDtypeStruct((t_total, D_MODEL), F32),
        compiler_params=_cparams(("parallel",)),
    )(x2d, oa, ob, gate_m, shift_f, scale_f, gate_f, *consts)


def _prep_weights(w_in, g_norm_mix, g_norm_ffn, g_q_lat, w_uq, g_kv_lat, w_ukv, g_mla_q_nope, g_mla_q_rope,
                  g_mla_k_nope, g_mla_k_rope, g_dsa_q, g_dsa_k, g_out, w_out, w_ffn_in, w_ffn_out):
    splits = np.cumsum([MLA_Q_RANK, MLA_KV_RANK, MLA_ROPE, 512, 128, 128, 512, IDX_DIM, IDX_HEADS])
    s = [0] + splits.tolist()
    col = lambda i: w_in[:, s[i]:s[i + 1]]
    pad = jnp.zeros((D_MODEL, C_END - C_TAIL - IDX_DIM - MLA_ROPE - IDX_HEADS), w_in.dtype)
    w_in_r = jnp.concatenate([col(0), col(1), col(3), col(4), col(5), col(6), col(7), col(2), col(8), pad], axis=1)
    zq = jnp.zeros((MLA_Q_RANK, MLA_HEADS, LANES - MLA_NOPE - MLA_ROPE), w_uq.dtype)
    w_uq_cat = jnp.concatenate([w_uq, zq], axis=2).reshape(MLA_Q_RANK, MLA_HEADS * LANES)
    zk = jnp.zeros((MLA_KV_RANK, MLA_HEADS, LANES - MLA_NOPE), w_ukv.dtype)
    w_uk_cat = jnp.concatenate([w_ukv[:, :, :MLA_NOPE], zk], axis=2).reshape(MLA_KV_RANK, MLA_HEADS * LANES)
    w_uv = w_ukv[:, :, MLA_NOPE:].reshape(MLA_KV_RANK, MLA_HEADS * MLA_V)
    zv = jnp.zeros((MLA_KV_RANK, MLA_HEADS, LANES - MLA_V), w_ukv.dtype)
    w_uv_pad = jnp.concatenate([w_ukv[:, :, MLA_NOPE:], zv], axis=2).reshape(MLA_KV_RANK, MLA_HEADS * LANES)
    w_uk_t = w_ukv[:, :, :MLA_NOPE].reshape(MLA_KV_RANK, MLA_HEADS * MLA_NOPE).T
    z32 = jnp.zeros((LANES - MLA_NOPE - MLA_ROPE,), F32)
    z64 = jnp.zeros((LANES - MLA_NOPE,), F32)
    g_qcat = jnp.tile(jnp.concatenate([g_mla_q_nope, g_mla_q_rope, z32]), MLA_HEADS)
    g_kcat = jnp.tile(jnp.concatenate([g_mla_k_nope, z64]), MLA_HEADS)
    g_tail = jnp.concatenate([jnp.zeros((T_ROPE,), F32), g_mla_k_rope, z32])
    row = lambda v: v.reshape(1, -1).astype(F32)
    return {
        "g_mix": row(g_norm_mix), "g_ffn": row(g_norm_ffn), "w_in": w_in_r.astype(BF16),
        "g_qlat": row(g_q_lat), "w_uq": w_uq_cat.astype(BF16), "g_qcat": row(g_qcat),
        "g_kvlat": row(g_kv_lat), "w_ukv": jnp.concatenate([w_uk_cat, w_uv_pad], axis=1).astype(BF16),
        "g_kcat": row(g_kcat), "g_tail": row(g_tail),
        "g_q": row(jnp.tile(g_dsa_q, DSA_HEADS)), "g_k": row(jnp.tile(g_dsa_k, DSA_KV_HEADS)),
        "m_cat": _group_matrix(256, [(0, 64), (64, 96), (128, 192), (192, 224)]),
        "m_64": _group_matrix(256, [(0, 64), (64, 128), (128, 192), (192, 256)]),
        "w_uk_t": w_uk_t.astype(BF16), "w_uv": w_uv.astype(BF16), "g_k_nope": g_mla_k_nope,
        "g_out": row(g_out), "w_out": w_out.astype(BF16),
        "w_gate": w_ffn_in[:, :D_FF].astype(BF16), "w_up": w_ffn_in[:, D_FF:].astype(BF16),
        "w_down": w_ffn_out.astype(BF16),
    }


def _rope_tables(pos):
    freq = ROPE_THETA ** (-jnp.arange(HALF_ROPE, dtype=F32) / HALF_ROPE)
    ang = pos.astype(F32)[:, None] * freq[None, :]
    cos, sin = jnp.cos(ang), jnp.sin(ang)
    n = pos.shape[0]
    ones = jnp.ones((n, T_ROPE), F32)
    zeros = jnp.zeros((n, T_ROPE), F32)
    tail1 = jnp.ones((n, LANES - T_ROPE - MLA_ROPE), F32)
    tail0 = jnp.zeros((n, LANES - T_ROPE - MLA_ROPE), F32)
    return (jnp.concatenate([ones, cos, cos, tail1], axis=1),
            jnp.concatenate([zeros, -sin, sin, tail0], axis=1))


def _pick_tile(n, prefs):
    for t in prefs:
        if n % t == 0:
            return t
    return n


def kernel(x_prompt, x_sample, c_prompt, c_sample, cache_mla, cache_kv, cache_idx, page_table, rel_bias, w_ada, b_ada, g_norm_mix, g_norm_ffn, w_in, g_q_lat, w_uq, g_kv_lat, w_ukv, g_mla_q_nope, g_mla_q_rope, g_mla_k_nope, g_mla_k_rope, g_dsa_q, g_dsa_k, g_out, w_out, w_ffn_in, w_ffn_out):
    assert w_ada.shape[0] == 1 and x_sample.shape[1] == 1, "one layer, one new token per sample"
    batch, seq, _ = x_prompt.shape
    dec = x_sample.shape[0]
    n_pages = page_table.shape[1]
    past = n_pages * PAGE_SIZE
    wp = _prep_weights(w_in[0], g_norm_mix[0], g_norm_ffn[0], g_q_lat[0], w_uq[0], g_kv_lat[0], w_ukv[0],
                       g_mla_q_nope[0], g_mla_q_rope[0], g_mla_k_nope[0], g_mla_k_rope[0], g_dsa_q[0], g_dsa_k[0],
                       g_out[0], w_out[0], w_ffn_in[0], w_ffn_out[0])

    mod = _ada(jnp.concatenate([c_prompt, c_sample], axis=0), w_ada[0], b_ada[0])
    mods_p = [m.reshape(batch, 1, D_MODEL) for m in jnp.split(mod[:batch], 6, axis=-1)]
    mods_s = jnp.split(mod[batch:], 6, axis=-1)

    tm = _pick_tile(seq, (512, 256, 128))
    xp2 = x_prompt.reshape(batch * seq, D_MODEL)
    cos_p, sin_p = _rope_tables(jnp.arange(seq))
    pin = _in_stage(xp2, mods_p[1], mods_p[0], cos_p, sin_p, wp, tm, per_token=False)
    o_mla_p = _mla_prompt(pin["qcat"], pin["kcat"], pin["vmla"], batch, seq,
                          _pick_tile(seq, (256, 128)), _pick_tile(seq, (512, 256, 128)))
    o_dsa_p = _dsa_prompt(rel_bias, pin["qidx"], pin["tail"], pin["kidxdup"], pin["qdsa"], pin["kdup"], pin["vone"],
                          batch, seq, _pick_tile(seq, (256, 128)))
    y_p = _out_stage(xp2, o_mla_p, o_dsa_p, mods_p[2], mods_p[3], mods_p[4], mods_p[5], wp, tm,
                     per_token=False, tiles_per_seq=seq // tm)

    xs2 = x_sample.reshape(dec, D_MODEL)
    cos_s, sin_s = _rope_tables(jnp.full((1,), past))
    sin_ = _in_stage(xs2, mods_s[1], mods_s[0], cos_s, sin_s, wp, dec, per_token=True)
    pps_wide = _pick_tile(n_pages, (64, 32, 16, 8, 4, 2))
    qidx3 = sin_["qidx"].reshape(dec, IDX_HEADS, IDX_DIM)
    widx3 = sin_["tail"][:, T_WIDX:T_WIDX + IDX_HEADS].reshape(dec, IDX_HEADS, 1)
    scores = _sample_scores(page_table, qidx3, widx3, _pages_feature_major(cache_idx), pps_wide)
    selp, seln = _sample_select(scores, sin_["qidx"], sin_["tail"])
    qc = sin_["qcat"].reshape(dec, MLA_HEADS, LANES)
    qn_row = qc[:, :, 0:MLA_NOPE].astype(F32).reshape(dec, 1, MLA_HEADS * MLA_NOPE)
    qr = qc[:, :, T_ROPE:T_ROPE + MLA_ROPE]
    gk_row = jnp.tile(wp["g_k_nope"], MLA_HEADS).reshape(1, MLA_HEADS * MLA_NOPE)
    qd = sin_["qdsa"].astype(F32).reshape(dec, DSA_HEADS, DSA_HEAD_DIM)
    on_c = (jnp.arange(DSA_KV_HEADS)[None, :] == (jnp.arange(DSA_HEADS) // DSA_GROUP)[:, None]).astype(F32)
    qd2 = (qd[:, :, None, :] * on_c[None, :, :, None]).reshape(dec, DSA_HEADS, LANES)
    bias_past = _bias_table(rel_bias, 1, past, 0, -1, past).reshape(DSA_HEADS, past)
    bias_new = _bias_table(rel_bias, 1, LANES, 0, 0, 0)[:, 0, 0:1]
    o_mla_s, o8 = _decode_attention(
        page_table, qn_row, qr, sin_["row"].reshape(dec, MLA_ROW, 1), wp["w_uk_t"], wp["w_uv"], gk_row,
        _pages_feature_major(cache_mla),
        qd2, sin_["kv"].reshape(dec, 1, 256), selp.reshape(dec, 1, past), seln.reshape(dec, 1, LANES),
        bias_past, bias_new, _pages_feature_major(cache_kv), pps_wide)
    o_mla_s = o_mla_s.reshape(dec, MLA_HEADS * MLA_V)
    o_dsa_s = o8.reshape(dec, DSA_HEADS, DSA_KV_HEADS, DSA_HEAD_DIM).sum(axis=2).reshape(dec, DSA_HEADS * DSA_HEAD_DIM)
    y_s = _out_stage(xs2, o_mla_s, o_dsa_s, mods_s[2], mods_s[3], mods_s[4], mods_s[5], wp, dec,
                     per_token=True, tiles_per_seq=1)

    return (y_p.reshape(batch, seq, D_MODEL), y_s.reshape(dec, 1, D_MODEL),
            jnp.swapaxes(pin["row"], 1, 2)[None],
            jnp.moveaxis(pin["kv"].reshape(batch, 2, DSA_KV_HEADS, DSA_HEAD_DIM, seq), -1, 1)[None],
            jnp.swapaxes(pin["kidx"], 1, 2)[None],
            sin_["row"].reshape(1, dec, 1, MLA_ROW),
            sin_["kv"].reshape(1, dec, 1, 2, DSA_KV_HEADS, DSA_HEAD_DIM),
            sin_["kidx"].reshape(1, dec, 1, IDX_DIM))
```

```python
import functools
import math

import numpy as np
import jax
import jax.numpy as jnp
from jax import lax
from jax.experimental import pallas as pl
from jax.experimental.pallas import tpu as pltpu

D_MODEL = 1024
PAGE_SIZE = 128
MLA_HEADS = 8
MLA_NOPE = 64
MLA_ROPE = 32
MLA_V = 64
MLA_Q_RANK = 384
MLA_KV_RANK = 256
MLA_ROW = MLA_KV_RANK + MLA_ROPE
DSA_HEADS = 8
DSA_KV_HEADS = 2
DSA_HEAD_DIM = 64
DSA_GROUP = DSA_HEADS // DSA_KV_HEADS
IDX_HEADS = 8
IDX_DIM = 64
TOPK_MAX = 256
N_BUCKETS = 32
MAX_DISTANCE = 128
ROPE_THETA = 10000.0
D_FF = ((8 * D_MODEL + 3 * 256 - 1) // (3 * 256)) * 256
EPS = 1e-6
MLA_SCALE = (MLA_NOPE + MLA_ROPE) ** -0.5
DSA_SCALE = DSA_HEAD_DIM ** -0.5
IDX_SCALE = (IDX_DIM * IDX_HEADS) ** -0.5

LANES = 128
VMEM_LIMIT = 56 * 1024 * 1024

F32 = jnp.float32
BF16 = jnp.bfloat16
I32 = jnp.int32
NEG_BIG = -1e30
INT_MIN = -(2 ** 31)
HALF16 = 2 ** 15

C_QLAT = 0
C_KVLAT = C_QLAT + MLA_Q_RANK
C_Q = C_KVLAT + MLA_KV_RANK
C_K = C_Q + DSA_HEADS * DSA_HEAD_DIM
C_V = C_K + DSA_KV_HEADS * DSA_HEAD_DIM
C_QIDX = C_V + DSA_KV_HEADS * DSA_HEAD_DIM
C_TAIL = C_QIDX + IDX_HEADS * IDX_DIM
C_END = C_TAIL + LANES
T_ROPE = IDX_DIM
T_WIDX = IDX_DIM + MLA_ROPE
HALF_ROPE = MLA_ROPE // 2
KB = 128


def _dot(a, b):
    return jnp.dot(a, b, preferred_element_type=F32)


def _dot_nt(a, b):
    return lax.dot_general(a, b, (((1,), (1,)), ((), ())), preferred_element_type=F32)


def _split(a):
    hi = a.astype(BF16)
    lo = (a - hi.astype(F32)).astype(BF16)
    return hi, lo


def _dot3(a, b):
    ah, al = _split(a)
    bh, bl = _split(b)
    return _dot(ah, bh) + (_dot(al, bh) + _dot(ah, bl))


def _cparams(sem):
    return pltpu.CompilerParams(dimension_semantics=sem, vmem_limit_bytes=VMEM_LIMIT)


def _full(shape):
    n = len(shape)
    return pl.BlockSpec(shape, lambda *a, _n=n: (0,) * _n, pipeline_mode=pl.Buffered(1))


def _ada_kernel(c_ref, w_ref, b_ref, o_ref):
    c = c_ref[...]
    s = c / (1.0 + jnp.exp(-c))
    o_ref[...] = _dot3(s, w_ref[...]) + b_ref[...]


def _ada(c, w, b):
    n = c.shape[0]
    return pl.pallas_call(
        _ada_kernel,
        grid=(6,),
        in_specs=[pl.BlockSpec((n, D_MODEL), lambda j: (0, 0)),
                  pl.BlockSpec((D_MODEL, D_MODEL), lambda j: (0, j)),
                  pl.BlockSpec((1, D_MODEL), lambda j: (0, j))],
        out_specs=pl.BlockSpec((n, D_MODEL), lambda j: (0, j)),
        out_shape=jax.ShapeDtypeStruct((n, 6 * D_MODEL), F32),
        compiler_params=_cparams(("arbitrary",)),
    )(c, w, b.reshape(1, -1))


def _group_matrix(width, groups):
    g = np.zeros((width, width), np.float32)
    for lo, hi in groups:
        g[lo:hi, lo:hi] = 1.0 / (hi - lo)
    return jnp.asarray(g, BF16)


def _in_kernel(x_ref, sc_ref, sh_ref, cos_ref, sin_ref, gmix_ref, win_ref, gqlat_ref, wuq_ref, gqcat_ref,
               gkvlat_ref, wukv_ref, gkcat_ref, gtail_ref, gq_ref, gk_ref, mcat_ref, m64_ref,
               qcat_o, kcat_o, vmla_o, row_o, kv_o, kidx_o, tail_o, kidxdup_o, qdsa_o, kdup_o, vone_o, qidx_o,
               *, rows_feature_major):
    tm = x_ref.shape[0]
    x = x_ref[...]
    h = x * lax.rsqrt(jnp.mean(x * x, axis=-1, keepdims=True) + EPS) * gmix_ref[...]
    hb = (h * (1.0 + sc_ref[...]) + sh_ref[...]).astype(BF16)

    def proj(lo, hi):
        return _dot(hb, win_ref[:, lo:hi])

    cos = cos_ref[...]
    sin = sin_ref[...]
    lane = lax.broadcasted_iota(I32, (tm, LANES), 1)
    first_half = lane < T_ROPE + HALF_ROPE
    rope_lanes = (lane >= T_ROPE) & (lane < T_ROPE + MLA_ROPE)
    low64 = lane < 64

    def rope(xh):
        rot = jnp.where(first_half, pltpu.roll(xh, LANES - HALF_ROPE, 1), pltpu.roll(xh, HALF_ROPE, 1))
        return xh * cos + rot * sin

    def gnorm(blk, m_ref):
        ms = _dot((blk * blk).astype(BF16), m_ref[...])
        return blk * lax.rsqrt(ms + EPS)

    ql = proj(C_QLAT, C_KVLAT)
    qln = ql * lax.rsqrt(jnp.mean(ql * ql, axis=-1, keepdims=True) + EPS) * gqlat_ref[...]
    qm = _dot(qln.astype(BF16), wuq_ref[...])
    for j in range(4):
        y = gnorm(qm[:, 256 * j:256 * j + 256], mcat_ref) * gqcat_ref[:, 256 * j:256 * j + 256]
        for t in range(2):
            qcat_o[:, 256 * j + 128 * t:256 * j + 128 * t + 128] = rope(y[:, 128 * t:128 * t + 128]).astype(BF16)

    tail = proj(C_TAIL, C_END)
    ssq = jnp.sum(jnp.where(rope_lanes, tail * tail, 0.0), axis=-1, keepdims=True) * (1.0 / MLA_ROPE)
    tn = jnp.where(rope_lanes, tail * lax.rsqrt(ssq + EPS) * gtail_ref[...], tail)
    tr = rope(tn)
    if rows_feature_major:
        kidx_o[...] = tail.T[0:IDX_DIM, :]
    else:
        kidx_o[...] = tail[:, 0:IDX_DIM]
    tail_o[...] = tail
    kidxdup_o[...] = jnp.where(low64, tail, pltpu.roll(tail, 64, 1)).astype(BF16)
    krope = jnp.where(rope_lanes, tr, 0.0)

    kvl = proj(C_KVLAT, C_Q)
    lat = kvl * lax.rsqrt(jnp.mean(kvl * kvl, axis=-1, keepdims=True) + EPS) * gkvlat_ref[...]
    if rows_feature_major:
        row_o[0:MLA_KV_RANK, :] = lat.T
        row_o[MLA_KV_RANK:MLA_ROW, :] = tr.T[T_ROPE:T_ROPE + MLA_ROPE, :]
    else:
        row_o[:, 0:MLA_KV_RANK] = lat
        row_o[:, MLA_KV_RANK:MLA_ROW] = tr[:, T_ROPE:T_ROPE + MLA_ROPE]
    latb = lat.astype(BF16)
    kx = _dot(latb, wukv_ref[:, 0:1024])
    for j in range(4):
        y = gnorm(kx[:, 256 * j:256 * j + 256], mcat_ref) * gkcat_ref[:, 256 * j:256 * j + 256]
        for t in range(2):
            kcat_o[:, 256 * j + 128 * t:256 * j + 128 * t + 128] = (y[:, 128 * t:128 * t + 128] + krope).astype(BF16)
    ones_half = (lax.broadcasted_iota(I32, (tm, 1024), 1) % LANES) >= MLA_V
    vmla_o[...] = jnp.where(ones_half, 1.0, _dot(latb, wukv_ref[:, 1024:2048])).astype(BF16)

    q = proj(C_Q, C_K)
    for j in range(2):
        qdsa_o[:, 256 * j:256 * j + 256] = (gnorm(q[:, 256 * j:256 * j + 256], m64_ref)
                                            * gq_ref[:, 256 * j:256 * j + 256] * DSA_SCALE).astype(BF16)
    k = proj(C_K, C_V)
    ms = _dot((k * k).astype(BF16), m64_ref[0:128, 0:128])
    kn = k * lax.rsqrt(ms + EPS) * gk_ref[...]
    v = proj(C_V, C_QIDX)
    if rows_feature_major:
        kv_o[0:128, :] = kn.T
        kv_o[128:256, :] = v.T
    else:
        kv_o[:, 0:128] = kn
        kv_o[:, 128:256] = v
    kr = pltpu.roll(kn, 64, 1)
    kdup_o[:, 0:128] = jnp.where(low64, kn, kr).astype(BF16)
    kdup_o[:, 128:256] = jnp.where(low64, kr, kn).astype(BF16)
    vone_o[:, 0:128] = jnp.where(low64, v, 1.0).astype(BF16)
    vone_o[:, 128:256] = jnp.where(low64, pltpu.roll(v, 64, 1), 1.0).astype(BF16)
    qidx_o[...] = proj(C_QIDX, C_TAIL).astype(BF16)


_IN_OUT_WIDTHS = (("qcat", 1024, BF16), ("kcat", 1024, BF16), ("vmla", 1024, BF16), ("row", MLA_ROW, F32),
                  ("kv", 256, F32), ("kidx", IDX_DIM, F32), ("tail", LANES, F32), ("kidxdup", LANES, BF16),
                  ("qdsa", 512, BF16), ("kdup", 256, BF16), ("vone", 256, BF16), ("qidx", 512, BF16))


def _in_stage(x2d, scale, shift, cos, sin, wp, tm, per_token):
    t_total = x2d.shape[0]
    n_tiles = t_total // tm
    if per_token:
        mod_spec = pl.BlockSpec((tm, D_MODEL), lambda i: (i, 0))
        tab_spec = pl.BlockSpec((1, LANES), lambda i: (0, 0))
    else:
        tiles_per_seq = cos.shape[0] // tm
        mod_spec = pl.BlockSpec((None, 1, D_MODEL), lambda i: (i // tiles_per_seq, 0, 0))
        tab_spec = pl.BlockSpec((tm, LANES), lambda i: (i % tiles_per_seq, 0))
    consts = [wp["g_mix"], wp["w_in"], wp["g_qlat"], wp["w_uq"], wp["g_qcat"], wp["g_kvlat"], wp["w_ukv"],
              wp["g_kcat"], wp["g_tail"], wp["g_q"], wp["g_k"], wp["m_cat"], wp["m_64"]]
    in_specs = ([pl.BlockSpec((tm, D_MODEL), lambda i: (i, 0)), mod_spec, mod_spec, tab_spec, tab_spec]
                + [_full(c.shape) for c in consts])
    out_specs = [pl.BlockSpec((tm, w), lambda i: (i, 0)) for _, w, _ in _IN_OUT_WIDTHS]
    out_shape = [jax.ShapeDtypeStruct((t_total, w), dt) for _, w, dt in _IN_OUT_WIDTHS]
    if not per_token:
        for j, (name, w, dt) in enumerate(_IN_OUT_WIDTHS):
            if name in ("row", "kv", "kidx"):
                out_specs[j] = pl.BlockSpec((None, w, tm), lambda i: (i // tiles_per_seq, 0, i % tiles_per_seq))
                out_shape[j] = jax.ShapeDtypeStruct((n_tiles // tiles_per_seq, w, tiles_per_seq * tm), dt)
    outs = pl.pallas_call(
        functools.partial(_in_kernel, rows_feature_major=not per_token),
        grid=(n_tiles,), in_specs=in_specs, out_specs=out_specs, out_shape=out_shape,
        compiler_params=_cparams(("parallel",)),
    )(x2d, scale, shift, cos, sin, *consts)
    return {name: o for (name, _, _), o in zip(_IN_OUT_WIDTHS, outs)}


def _mla_prompt_kernel(q_ref, k_ref, v_ref, o_ref, m_scr, acc_scr, s_scr, *, tk):
    tq = q_ref.shape[0]
    qi = pl.program_id(1)
    n_chunks = (qi * tq + tq + tk - 1) // tk
    m_scr[...] = jnp.full(m_scr.shape, -jnp.inf, F32)
    acc_scr[...] = jnp.zeros(acc_scr.shape, F32)
    low64 = lax.broadcasted_iota(I32, (tq, LANES), 1) < 64
    qpos = qi * tq + lax.broadcasted_iota(I32, (tq, tk), 0)
    kcol = lax.broadcasted_iota(I32, (tq, tk), 1)
    exp_scale = MLA_SCALE * math.log2(math.e)

    def raw_scores(c, h, masked):
        off = pl.multiple_of(c * tk, tk)
        s = _dot_nt(q_ref[:, h * 128:(h + 1) * 128], k_ref[pl.ds(off, tk), h * 128:(h + 1) * 128])
        if masked:
            s = jnp.where(c * tk + kcol <= qpos, s, -jnp.inf)
        return s

    def pass_max(c, masked):
        for h in range(MLA_HEADS):
            s = raw_scores(c, h, masked)
            s_scr[c, h] = s
            m = m_scr[h]
            for j in range(tk // LANES):
                m = jnp.maximum(m, s[:, j * LANES:(j + 1) * LANES])
            m_scr[h] = m

    def pass_sum(c, masked):
        off = pl.multiple_of(c * tk, tk)
        for h in range(MLA_HEADS):
            s = s_scr[c, h]
            m = m_scr[h]
            p = jnp.concatenate([jnp.exp2((s[:, j * LANES:(j + 1) * LANES] - m) * exp_scale)
                                 for j in range(tk // LANES)], axis=1).astype(BF16)
            acc_scr[h] += _dot(p, v_ref[pl.ds(off, tk), h * 128:(h + 1) * 128])

    def run(step):
        def body(c, carry):
            step(c, False)
            return carry

        lax.fori_loop(0, n_chunks - 1, body, 0)
        step(n_chunks - 1, True)

    run(pass_max)
    for h in range(MLA_HEADS):
        m_scr[h] = jnp.broadcast_to(jnp.max(m_scr[h], axis=-1, keepdims=True), (tq, LANES))
    run(pass_sum)
    for p in range(MLA_HEADS // 2):
        even, odd = acc_scr[2 * p], acc_scr[2 * p + 1]
        o_ref[:, 128 * p:128 * p + 128] = jnp.where(low64, even / pltpu.roll(even, 64, 1),
                                                    pltpu.roll(odd, 64, 1) / odd)


def _mla_prompt(qcat, kcat, vmla, batch, seq, tq, tk):
    nq = seq // tq
    return pl.pallas_call(
        functools.partial(_mla_prompt_kernel, tk=tk),
        grid=(batch, nq),
        in_specs=[pl.BlockSpec((tq, 1024), lambda b, i: (b * nq + i, 0)),
                  pl.BlockSpec((seq, 1024), lambda b, i: (b, 0)),
                  pl.BlockSpec((seq, 1024), lambda b, i: (b, 0))],
        out_specs=pl.BlockSpec((tq, 512), lambda b, i: (b * nq + i, 0)),
        out_shape=jax.ShapeDtypeStruct((batch * seq, 512), F32),
        scratch_shapes=[pltpu.VMEM((MLA_HEADS, tq, LANES), F32), pltpu.VMEM((MLA_HEADS, tq, LANES), F32),
                        pltpu.VMEM((seq // tk, MLA_HEADS, tq, tk), F32)],
        compiler_params=_cparams(("parallel", "arbitrary")),
    )(qcat, kcat, vmla)


def _sort_key(score):
    bits = lax.bitcast_convert_type(score + 0.0, I32)
    return bits ^ ((bits >> 31) & 0x7FFFFFFF)


def _kth_key(count_ge, n_sel, shape, bits=32):
    def body(i, t):
        cand = t + (jnp.int32(1) << (bits - 1 - i))
        return jnp.where(count_ge(cand) >= n_sel, cand, t)

    return lax.fori_loop(0, bits, body, jnp.full(shape, -(2 ** (bits - 1)), I32))


def _bucket(dist):
    max_exact = N_BUCKETS // 2
    d = jnp.maximum(dist, 0)
    log_ratio = jnp.log(jnp.maximum(d, max_exact).astype(F32) / max_exact) / math.log(MAX_DISTANCE / max_exact)
    large = jnp.minimum(max_exact + (log_ratio * (N_BUCKETS - max_exact)).astype(I32), N_BUCKETS - 1)
    return jnp.where(d < max_exact, d, large)


def _bias_kernel(rb_ref, o_ref, *, a, b, c):
    rows, cols = o_ref.shape[1], o_ref.shape[2]
    dist = (a * lax.broadcasted_iota(I32, (rows, cols), 0) + b * lax.broadcasted_iota(I32, (rows, cols), 1) + c)
    bucket = _bucket(dist)
    for h in range(DSA_HEADS):
        acc = jnp.zeros((rows, cols), F32)
        for n in range(N_BUCKETS):
            acc = jnp.where(bucket == n, rb_ref[n, h], acc)
        o_ref[h] = acc


def _bias_table(rel_bias, rows, cols, a, b, c):
    return pl.pallas_call(
        functools.partial(_bias_kernel, a=a, b=b, c=c),
        in_specs=[pl.BlockSpec(memory_space=pltpu.SMEM)],
        out_specs=pl.BlockSpec(memory_space=pltpu.VMEM),
        out_shape=jax.ShapeDtypeStruct((DSA_HEADS, rows, cols), F32),
    )(rel_bias)


def _loop2(n, body):
    n = jnp.maximum(n, 0)

    def pair(j, carry):
        body(2 * j)
        body(2 * j + 1)
        return carry

    lax.fori_loop(0, n // 2, pair, 0)

    @pl.when(n % 2 == 1)
    def _():
        body(n - 1)


def _colreduce(x, op):
    return op(x.reshape(x.shape[0] // 8, 8, x.shape[1]), axis=0)


def _dsa_prompt_kernel(rb_ref, qidx_ref, tail_ref, kidx_ref, q_ref, k_ref, v_ref, bias_ref, o_ref,
                       qs_scr, qd_scr, wb_scr, key_scr, hi_scr, lo_scr, tie_scr, m_scr, p_scr, acc_scr, s_scr,
                       *, n_sel, idx_bits):
    tq = q_ref.shape[0]
    kq = tq // KB
    qi = pl.program_id(1)
    krow = lax.broadcasted_iota(I32, (KB, tq), 0)
    qcol = lax.broadcasted_iota(I32, (KB, tq), 1)
    low64 = lax.broadcasted_iota(I32, (tq, LANES), 1) < 64

    tail = tail_ref[...]
    for h in range(IDX_HEADS):
        mine = low64 if h % 2 == 0 else jnp.logical_not(low64)
        zero = jnp.zeros((tq, LANES), BF16)
        qs_scr[h * tq:(h + 1) * tq, :] = jnp.where(mine, qidx_ref[:, (h // 2) * 128:(h // 2 + 1) * 128], zero)
        qd_scr[h * tq:(h + 1) * tq, :] = jnp.where(mine, q_ref[:, (h // 2) * 128:(h // 2 + 1) * 128], zero)
        wb_scr[h] = jnp.broadcast_to(tail[:, T_WIDX + h:T_WIDX + h + 1], (tq, LANES))

    def rows_of(c):
        return pl.ds(pl.multiple_of(c * tq, tq), tq)

    def score_chunk(c, in_tile):
        d = _dot_nt(qs_scr[...], kidx_ref[rows_of(c), :])
        rb = 64
        for t in range(kq):
            parts = []
            for r in range(0, tq, rb):
                sc = jnp.zeros((rb, KB), F32)
                for h in range(IDX_HEADS):
                    sc = sc + (jnp.maximum(d[h * tq + r:h * tq + r + rb, t * KB:(t + 1) * KB], 0.0)
                               * wb_scr[h, r:r + rb, :])
                parts.append(sc)
            sc_t = (jnp.concatenate(parts, axis=0) * IDX_SCALE).T
            if in_tile:
                sc_t = jnp.where(t * KB + krow <= qcol, sc_t, -jnp.inf)
            key = _sort_key(sc_t)
            key_scr[c * kq + t] = key
            hi_scr[c * kq + t] = (key >> 16).astype(jnp.int16)
            lo_scr[c * kq + t] = ((key & 0xFFFF) - HALF16).astype(jnp.int16)

    _loop2(qi, lambda c: score_chunk(c, False))
    score_chunk(qi, True)
    n_chunks = qi + 1

    def count(pred):
        part = 32
        def body(c, acc):
            for t in range(kq):
                kb = c * kq + t
                hit = jnp.where(pred(key_scr[kb], kb), 1.0, 0.0)
                acc = acc + jnp.sum(hit.reshape(KB // part, part, tq), axis=0)
            return acc
        acc = lax.fori_loop(0, n_chunks, body, jnp.zeros((part, tq), F32))
        return jnp.sum(_colreduce(acc, jnp.sum), axis=0, keepdims=True)

    def count16(src, pred):
        part = 32
        one, zero = jnp.ones((), BF16), jnp.zeros((), BF16)
        def body(c, acc):
            for t in range(kq):
                hit = jnp.where(pred(src[c * kq + t]), one, zero).reshape(KB // part, part, tq)
                for j in range(KB // part):
                    acc = acc + hit[j]
            return acc
        acc = lax.fori_loop(0, n_chunks, body, jnp.zeros((part, tq), BF16))
        return jnp.sum(_colreduce(acc.astype(F32), jnp.sum), axis=0, keepdims=True)

    t_hi = _kth_key(lambda cand: count16(hi_scr, lambda b: b >= cand.astype(jnp.int16)),
                    float(n_sel), (1, tq), bits=16)
    t_hi16 = t_hi.astype(jnp.int16)
    above = count16(hi_scr, lambda b: b > t_hi16)

    def keep_bucket(c, carry):
        for t in range(kq):
            kb = c * kq + t
            lo_scr[kb] = jnp.where(hi_scr[kb] == t_hi16, lo_scr[kb], jnp.int16(-HALF16))
        return carry

    lax.fori_loop(0, n_chunks, keep_bucket, 0)
    t_lo = _kth_key(lambda cand: above + count16(lo_scr, lambda b: b >= cand.astype(jnp.int16)),
                    float(n_sel), (1, tq), bits=16)
    thr = t_hi * (2 * HALF16) + (t_lo + HALF16)
    cnt_gt = count(lambda k, kb: k > thr)
    cnt_ge = count(lambda k, kb: k >= thr)
    tie_scr[...] = jnp.full((1, tq), 2 ** 30, I32)
    excess = jnp.where((cnt_ge > float(n_sel)) & (thr > INT_MIN), 1.0, 0.0)

    @pl.when(jnp.max(excess) > 0.0)
    def _():
        need = float(n_sel) - cnt_gt

        def body(i, p):
            cand = p + (jnp.int32(1) << (idx_bits - 1 - i))
            c = count(lambda k, kb: (k == thr) & (kb * KB + krow < cand))
            return jnp.where(c < need, cand, p)

        tie_scr[...] = lax.fori_loop(0, idx_bits, body, jnp.zeros((1, tq), I32))

    last_tie = tie_scr[...]

    def mask_chunk(c, in_tile):
        halves = []
        for t in range(kq):
            kb = c * kq + t
            key = key_scr[kb]
            sel = (key > thr) | ((key == thr) & (kb * KB + krow <= last_tie))
            if in_tile:
                sel = sel & (t * KB + krow <= qcol)
            halves.append(jnp.where(sel, 0.0, NEG_BIG).T)
        return halves

    far_bias = [rb_ref[N_BUCKETS - 1, h] for h in range(DSA_HEADS)]

    def scores(c, g, kind, mask):
        sg = _dot_nt(qd_scr[g * DSA_GROUP * tq:(g + 1) * DSA_GROUP * tq, :], k_ref[rows_of(c), g * 128:(g + 1) * 128])
        out = []
        for hh in range(DSA_GROUP):
            h = g * DSA_GROUP + hh
            cols = []
            for t in range(kq):
                s = sg[hh * tq:(hh + 1) * tq, t * KB:(t + 1) * KB] + mask[t]
                if kind == "tile":
                    s = s + (bias_ref[1 + t, h] - far_bias[h])
                elif kind == "before" and t == kq - 1:
                    s = s + (bias_ref[0, h] - far_bias[h])
                cols.append(s)
            out.append(cols)
        return out

    m_scr[...] = jnp.full(m_scr.shape, NEG_BIG, F32)
    acc_scr[...] = jnp.zeros(acc_scr.shape, F32)

    def pass_max(c, kind):
        mask = mask_chunk(c, kind == "tile")
        for g in range(DSA_KV_HEADS):
            for hh, cols in enumerate(scores(c, g, kind, mask)):
                h = g * DSA_GROUP + hh
                m = m_scr[h]
                for t, s in enumerate(cols):
                    s_scr[c, h, :, t * KB:(t + 1) * KB] = s
                    m = jnp.maximum(m, s)
                m_scr[h] = m

    def pass_sum(c, kind):
        for g in range(DSA_KV_HEADS):
            for hh in range(DSA_GROUP):
                h = g * DSA_GROUP + hh
                m = m_scr[h]
                for t in range(kq):
                    s = s_scr[c, h, :, t * KB:(t + 1) * KB]
                    p_scr[g, hh * tq:(hh + 1) * tq, t * KB:(t + 1) * KB] = jnp.exp(s - m).astype(BF16)
            acc_scr[g] += _dot(p_scr[g], v_ref[rows_of(c), g * 128:(g + 1) * 128])

    def run(step):
        _loop2(qi - 1, lambda c: step(c, "far"))

        @pl.when(qi >= 1)
        def _():
            step(qi - 1, "before")
            step(qi, "tile")

        @pl.when(qi == 0)
        def _():
            step(qi, "tile")

    run(pass_max)
    for h in range(DSA_HEADS):
        m_scr[h] = jnp.broadcast_to(jnp.max(m_scr[h], axis=-1, keepdims=True), (tq, LANES))
    run(pass_sum)

    for p in range(DSA_HEADS // 2):
        g, hh = (2 * p) // DSA_GROUP, (2 * p) % DSA_GROUP
        even = acc_scr[g, hh * tq:(hh + 1) * tq, :]
        odd = acc_scr[g, (hh + 1) * tq:(hh + 2) * tq, :]
        o_ref[:, 128 * p:128 * p + 128] = jnp.where(low64, even / pltpu.roll(even, 64, 1),
                                                    pltpu.roll(odd, 64, 1) / odd)


def _dsa_prompt(rel_bias, qidx, tail, kidxdup, qdsa, kdup, vone, batch, seq, tq):
    assert MAX_DISTANCE <= KB, "keys two or more blocks back must all fall in the last bucket"
    nq = seq // tq
    nkb = seq // KB
    kq = tq // KB
    n_sel = min(TOPK_MAX, seq // 4)
    idx_bits = max(1, (seq - 1).bit_length())
    bias = jnp.stack([_bias_table(rel_bias, tq, KB, 1, -1, KB - t * KB) for t in range(kq + 1)])
    kern = functools.partial(_dsa_prompt_kernel, n_sel=n_sel, idx_bits=idx_bits)
    qblk = lambda w: pl.BlockSpec((tq, w), lambda b, i: (b * nq + i, 0))
    seqblk = lambda w: pl.BlockSpec((seq, w), lambda b, i: (b, 0))
    return pl.pallas_call(
        kern,
        grid=(batch, nq),
        in_specs=[pl.BlockSpec(memory_space=pltpu.SMEM),
                  qblk(512), qblk(LANES), seqblk(LANES), qblk(512), seqblk(256), seqblk(256), _full(bias.shape)],
        out_specs=qblk(512),
        out_shape=jax.ShapeDtypeStruct((batch * seq, 512), F32),
        scratch_shapes=[pltpu.VMEM((IDX_HEADS * tq, LANES), BF16), pltpu.VMEM((DSA_HEADS * tq, LANES), BF16),
                        pltpu.VMEM((IDX_HEADS, tq, LANES), F32),
                        pltpu.VMEM((nkb, KB, tq), I32),
                        pltpu.VMEM((nkb, KB, tq), jnp.int16), pltpu.VMEM((nkb, KB, tq), jnp.int16),
                        pltpu.VMEM((1, tq), I32), pltpu.VMEM((DSA_HEADS, tq, LANES), F32),
                        pltpu.VMEM((DSA_KV_HEADS, DSA_GROUP * tq, tq), BF16),
                        pltpu.VMEM((DSA_KV_HEADS, DSA_GROUP * tq, LANES), F32),
                        pltpu.VMEM((nq, DSA_HEADS, tq, tq), F32)],
        compiler_params=_cparams(("parallel", "arbitrary")),
    )(rel_bias, qidx, tail, kidxdup, qdsa, kdup, vone, bias)


def _fetch_pages(cache_ref, pt_ref, buf, sem, step, n_steps, locate, pps):
    def copy(slot, i, page):
        return pltpu.make_async_copy(cache_ref.at[0, page], buf.at[slot, i], sem.at[slot])

    def start(st, slot):
        b, first = locate(st)
        for i in range(pps):
            copy(slot, i, pt_ref[b, first + i]).start()

    @pl.when(step == 0)
    def _():
        start(step, 0)

    @pl.when(step + 1 < n_steps)
    def _():
        start(step + 1, (step + 1) % 2)

    slot = step % 2
    for i in range(pps):
        copy(slot, i, 0).wait()
    return slot


def _page_scratch(pps, width):
    return [pltpu.VMEM((2, pps, width, PAGE_SIZE), F32), pltpu.SemaphoreType.DMA((2,))]


def _pages_feature_major(cache):
    c = jnp.moveaxis(cache, 2, -1)
    return c.reshape(cache.shape[0], cache.shape[1], -1, PAGE_SIZE)


def _sample_score_kernel(pt_ref, q_ref, w_ref, cache_ref, o_ref, buf, sem, kb_scr, *, pps, grp):
    bo, c, bi = pl.program_id(0), pl.program_id(1), pl.program_id(2)
    n_chunks = pl.num_programs(1)
    step = (bo * n_chunks + c) * grp + bi

    def locate(st):
        return (st // (n_chunks * grp)) * grp + st % grp, ((st // grp) % n_chunks) * pps

    slot = _fetch_pages(cache_ref, pt_ref, buf, sem, step, pl.num_programs(0) * n_chunks * grp, locate, pps)
    for i in range(pps):
        kb_scr[:, i * PAGE_SIZE:(i + 1) * PAGE_SIZE] = buf[slot, i].astype(BF16)
    d = _dot(q_ref[...], kb_scr[...])
    sc = jnp.sum(jnp.maximum(d, 0.0) * w_ref[...], axis=0, keepdims=True) * IDX_SCALE
    o_ref[pl.ds(bi, 1), :] = sc


def _sample_scores(page_table, qidx3, widx3, cache_idx_t, pps):
    dec, n_pages = page_table.shape
    n_chunks = n_pages // pps
    grp = 8
    grid_spec = pltpu.PrefetchScalarGridSpec(
        num_scalar_prefetch=1,
        grid=(dec // grp, n_chunks, grp),
        in_specs=[pl.BlockSpec((None, IDX_HEADS, IDX_DIM), lambda bo, c, bi, pt: (bo * grp + bi, 0, 0)),
                  pl.BlockSpec((None, IDX_HEADS, 1), lambda bo, c, bi, pt: (bo * grp + bi, 0, 0)),
                  pl.BlockSpec(memory_space=pl.ANY)],
        out_specs=pl.BlockSpec((grp, pps * PAGE_SIZE), lambda bo, c, bi, pt: (bo, c)),
        scratch_shapes=_page_scratch(pps, IDX_DIM) + [pltpu.VMEM((IDX_DIM, pps * PAGE_SIZE), BF16)],
    )
    return pl.pallas_call(
        functools.partial(_sample_score_kernel, pps=pps, grp=grp), grid_spec=grid_spec,
        out_shape=jax.ShapeDtypeStruct((dec, n_pages * PAGE_SIZE), F32),
        compiler_params=_cparams(("arbitrary", "arbitrary", "arbitrary")),
    )(page_table, qidx3, widx3, cache_idx_t)


def _sample_select_kernel(sc_ref, qidx_ref, tail_ref, selp_ref, seln_ref, key_scr, *, n_sel, idx_bits):
    rows, past = sc_ref.shape
    nblk = past // LANES
    lane = lax.broadcasted_iota(I32, (rows, LANES), 1)
    tail = tail_ref[...]
    kidx2 = jnp.where(lane < 64, tail, pltpu.roll(tail, 64, 1))
    sc_new = jnp.zeros((rows, 1), F32)
    for h in range(IDX_HEADS):
        prod = qidx_ref[:, (h // 2) * 128:(h // 2 + 1) * 128].astype(F32) * kidx2.astype(BF16).astype(F32)
        mine = (lane < 64) if h % 2 == 0 else (lane >= 64)
        dot = jnp.sum(jnp.where(mine, prod, 0.0), axis=-1, keepdims=True)
        sc_new = sc_new + jnp.maximum(dot, 0.0) * tail[:, T_WIDX + h:T_WIDX + h + 1]
    key_new = _sort_key(sc_new * IDX_SCALE)

    for j in range(nblk):
        key_scr[j] = _sort_key(sc_ref[:, j * LANES:(j + 1) * LANES])

    def count(pred, pred_new):
        def body(j, c):
            return c + jnp.where(pred(key_scr[j], j), 1.0, 0.0)
        c = lax.fori_loop(0, nblk, body, jnp.zeros((rows, LANES), F32), unroll=4)
        return jnp.sum(c, axis=-1, keepdims=True) + jnp.where(pred_new, 1.0, 0.0)

    def wide(col):
        return jnp.broadcast_to(col, (rows, LANES))

    def count_ge(cand):
        cand_w = wide(cand)
        return count(lambda k, j: k >= cand_w, key_new >= cand)

    thr = _kth_key(count_ge, float(n_sel), (rows, 1))
    thr_w = wide(thr)
    cnt_gt = count(lambda k, j: k > thr_w, key_new > thr)
    need = float(n_sel) - cnt_gt

    def body(i, p):
        cand = p + (jnp.int32(1) << (idx_bits - 1 - i))
        cand_w = wide(cand)
        c = count(lambda k, j: (k == thr_w) & (j * LANES + lane < cand_w), (key_new == thr) & (past < cand))
        return jnp.where(c < need, cand, p)

    last_tie = lax.fori_loop(0, idx_bits, body, jnp.zeros((rows, 1), I32))
    last_w = wide(last_tie)

    for j in range(nblk):
        k = key_scr[j]
        sel = (k > thr_w) | ((k == thr_w) & (j * LANES + lane <= last_w))
        selp_ref[:, j * LANES:(j + 1) * LANES] = jnp.where(sel, 1.0, 0.0)
    sel_new = (key_new > thr) | ((key_new == thr) & (past <= last_tie))
    seln_ref[...] = jnp.broadcast_to(jnp.where(sel_new, 1.0, 0.0), (rows, LANES))


def _sample_select(scores, qidx, tail):
    dec, past = scores.shape
    n_sel = min(TOPK_MAX, (past + 1) // 4)
    idx_bits = max(1, past.bit_length())
    rows = 64 if dec % 64 == 0 else dec
    kern = functools.partial(_sample_select_kernel, n_sel=n_sel, idx_bits=idx_bits)
    return pl.pallas_call(
        kern,
        grid=(dec // rows,),
        in_specs=[pl.BlockSpec((rows, past), lambda i: (i, 0)), pl.BlockSpec((rows, 512), lambda i: (i, 0)),
                  pl.BlockSpec((rows, LANES), lambda i: (i, 0))],
        out_specs=[pl.BlockSpec((rows, past), lambda i: (i, 0)), pl.BlockSpec((rows, LANES), lambda i: (i, 0))],
        out_shape=[jax.ShapeDtypeStruct((dec, past), F32), jax.ShapeDtypeStruct((dec, LANES), F32)],
        scratch_shapes=[pltpu.VMEM((past // LANES, rows, LANES), I32)],
        compiler_params=_cparams(("parallel",)),
    )(scores, qidx, tail)


def _mla_sample_kernel(pt_ref, qn_ref, qr_ref, rown_ref, wuk_ref, wuv_ref, gk_ref, cache_ref, o_ref,
                       buf, sem, lat_scr, kr_scr, a_scr, m_scr, l_scr, acc_scr, *, pps):
    c = pl.program_id(1)
    n_chunks = pl.num_programs(1)
    heads = MLA_HEADS
    slot = _fetch_pages(cache_ref, pt_ref, buf, sem, pl.program_id(0) * n_chunks + c,
                        pl.num_programs(0) * n_chunks, lambda st: (st // n_chunks, (st % n_chunks) * pps), pps)

    @pl.when(c == 0)
    def _():
        m_scr[...] = jnp.full(m_scr.shape, -jnp.inf, F32)
        l_scr[...] = jnp.zeros(l_scr.shape, F32)
        acc_scr[...] = jnp.zeros(acc_scr.shape, F32)
        qg = qn_ref[...] * gk_ref[...]
        own = (lax.broadcasted_iota(I32, (heads, heads * MLA_NOPE), 1) // MLA_NOPE
               == lax.broadcasted_iota(I32, (heads, heads * MLA_NOPE), 0))
        n_up = heads * MLA_NOPE
        a_scr[0:n_up, :] = wuk_ref[...]
        a = _dot(jnp.where(own, qg, 0.0).astype(BF16), wuk_ref[...]).astype(BF16)
        a_scr[n_up:n_up + 16, :] = jnp.concatenate([a, jnp.zeros_like(a)], axis=0)

    qr = qr_ref[...]

    def scores(latb, krb):
        n_up = heads * MLA_NOPE
        up = _dot(a_scr[...], latb)
        knt = up[0:n_up, :]
        ms = jnp.mean((knt * knt).reshape(heads, MLA_NOPE, latb.shape[1]), axis=1)
        return (up[n_up:n_up + heads, :] * lax.rsqrt(ms + EPS) + _dot(qr, krb)) * MLA_SCALE

    def update(latb, krb, n_valid):
        n = latb.shape[1]
        sub = min(n, 512)
        s = jnp.concatenate([scores(latb[:, j:j + sub], krb[:, j:j + sub]) for j in range(0, n, sub)], axis=1)
        if n_valid < n:
            s = jnp.where(lax.broadcasted_iota(I32, s.shape, 1) < n_valid, s, -jnp.inf)
        m_old = m_scr[...]
        m_new = jnp.maximum(m_old, jnp.max(s, axis=-1, keepdims=True))
        alpha = jnp.exp(m_old - m_new)
        p = jnp.exp(s - m_new)
        l_scr[...] = alpha * l_scr[...] + jnp.sum(p, axis=-1, keepdims=True)
        m_scr[...] = m_new
        acc_scr[...] = alpha * acc_scr[...] + _dot_nt(p.astype(BF16), latb)

    for i in range(pps):
        lat_scr[:, i * PAGE_SIZE:(i + 1) * PAGE_SIZE] = buf[slot, i, 0:MLA_KV_RANK, :].astype(BF16)
        kr_scr[:, i * PAGE_SIZE:(i + 1) * PAGE_SIZE] = buf[slot, i, MLA_KV_RANK:MLA_ROW, :].astype(BF16)
    update(lat_scr[...], kr_scr[...], lat_scr.shape[1])

    @pl.when(c == n_chunks - 1)
    def _():
        rn = jnp.broadcast_to(rown_ref[...], (MLA_ROW, PAGE_SIZE))
        update(rn[0:MLA_KV_RANK, :].astype(BF16), rn[MLA_KV_RANK:MLA_ROW, :].astype(BF16), 1)
        o_lat = (acc_scr[...] / l_scr[...]).astype(BF16)
        full = _dot(o_lat, wuv_ref[...])
        hd = lax.broadcasted_iota(I32, full.shape, 1) // MLA_V
        own = hd == lax.broadcasted_iota(I32, full.shape, 0)
        o_ref[...] = jnp.sum(jnp.where(own, full, 0.0), axis=0, keepdims=True)


def _dsa_sample_kernel(pt_ref, q_ref, kvn_ref, selp_ref, seln_ref, biasp_ref, biasn_ref, cache_ref, o_ref,
                       buf, sem, k_scr, v_scr, m_scr, l_scr, acc_scr, *, pps):
    c = pl.program_id(1)
    n_chunks = pl.num_programs(1)
    slot = _fetch_pages(cache_ref, pt_ref, buf, sem, pl.program_id(0) * n_chunks + c,
                        pl.num_programs(0) * n_chunks, lambda st: (st // n_chunks, (st % n_chunks) * pps), pps)

    @pl.when(c == 0)
    def _():
        m_scr[...] = jnp.full(m_scr.shape, NEG_BIG, F32)
        l_scr[...] = jnp.zeros(l_scr.shape, F32)
        acc_scr[...] = jnp.zeros(acc_scr.shape, F32)

    q = q_ref[...]

    def update(s, sel, pv_of):
        sh = jnp.where(sel, s, NEG_BIG)
        m_old = m_scr[...]
        m_new = jnp.maximum(m_old, jnp.max(sh, axis=-1, keepdims=True))
        alpha = jnp.exp(m_old - m_new)
        p = jnp.where(sel, jnp.exp(sh - m_new), 0.0)
        l_scr[...] = alpha * l_scr[...] + jnp.sum(p, axis=-1, keepdims=True)
        m_scr[...] = m_new
        acc_scr[...] = alpha * acc_scr[...] + pv_of(p)

    for i in range(pps):
        k_scr[:, i * PAGE_SIZE:(i + 1) * PAGE_SIZE] = buf[slot, i, 0:128, :].astype(BF16)
        v_scr[:, i * PAGE_SIZE:(i + 1) * PAGE_SIZE] = buf[slot, i, 128:256, :].astype(BF16)
    s = _dot(q.astype(BF16), k_scr[...]) + biasp_ref[...]
    update(s, selp_ref[...] > 0.5, lambda p: _dot_nt(p.astype(BF16), v_scr[...]))

    @pl.when(c == n_chunks - 1)
    def _():
        kvn = kvn_ref[...]
        kn = kvn[:, 0:128].astype(BF16).astype(F32)
        vn = kvn[:, 128:256].astype(BF16).astype(F32)
        s_new = jnp.sum(q.astype(F32) * kn, axis=-1, keepdims=True) + biasn_ref[...]
        update(s_new, seln_ref[:, 0:1] > 0.5, lambda p: p.astype(BF16).astype(F32) * vn)
        o = acc_scr[...] / l_scr[...]
        lane = lax.broadcasted_iota(I32, o.shape, 1)
        row = lax.broadcasted_iota(I32, o.shape, 0)
        own = (lane // DSA_HEAD_DIM) == (row // DSA_GROUP)
        o_ref[...] = jnp.where(own, o, 0.0)


_N_MLA_IN, _N_MLA_SCRATCH = 7, 8
_N_DSA_IN = 7


def _decode_kernel(pt_ref, *refs, pps):
    mla_in = refs[:_N_MLA_IN]
    dsa_in = refs[_N_MLA_IN:_N_MLA_IN + _N_DSA_IN]
    o_mla, o_dsa = refs[_N_MLA_IN + _N_DSA_IN:_N_MLA_IN + _N_DSA_IN + 2]
    scratch = refs[_N_MLA_IN + _N_DSA_IN + 2:]
    _mla_sample_kernel(pt_ref, *mla_in, o_mla, *scratch[:_N_MLA_SCRATCH], pps=pps)
    _dsa_sample_kernel(pt_ref, *dsa_in, o_dsa, *scratch[_N_MLA_SCRATCH:], pps=pps)


def _decode_attention(page_table, qn_row, qr, row_new, wuk_t, wuv, gk_row, cache_mla,
                      qs, kv_new, selp, seln, bias_past, bias_new, cache_kv4, pps):
    dec, n_pages = page_table.shape
    n_chunks = n_pages // pps
    chunk = pps * PAGE_SIZE
    per_b = lambda *shape: pl.BlockSpec((None,) + shape, lambda b, c, pt: (b,) + (0,) * len(shape))
    const = lambda a: pl.BlockSpec(a.shape, lambda b, c, pt: (0,) * a.ndim)
    grid_spec = pltpu.PrefetchScalarGridSpec(
        num_scalar_prefetch=1,
        grid=(dec, n_chunks),
        in_specs=[per_b(1, MLA_HEADS * MLA_NOPE), per_b(MLA_HEADS, MLA_ROPE), per_b(MLA_ROW, 1),
                  const(wuk_t), const(wuv), const(gk_row), pl.BlockSpec(memory_space=pl.ANY),
                  per_b(DSA_HEADS, LANES), per_b(1, 256),
                  pl.BlockSpec((None, 1, chunk), lambda b, c, pt: (b, 0, c)), per_b(1, LANES),
                  pl.BlockSpec((DSA_HEADS, chunk), lambda b, c, pt: (0, c)), const(bias_new),
                  pl.BlockSpec(memory_space=pl.ANY)],
        out_specs=[per_b(1, MLA_HEADS * MLA_V), per_b(DSA_HEADS, LANES)],
        scratch_shapes=_page_scratch(pps, MLA_ROW) + [
            pltpu.VMEM((MLA_KV_RANK, chunk), BF16), pltpu.VMEM((MLA_ROPE, chunk), BF16),
            pltpu.VMEM((MLA_HEADS * MLA_NOPE + 16, MLA_KV_RANK), BF16),
            pltpu.VMEM((MLA_HEADS, 1), F32), pltpu.VMEM((MLA_HEADS, 1), F32),
            pltpu.VMEM((MLA_HEADS, MLA_KV_RANK), F32)]
        + _page_scratch(pps, 256) + [
            pltpu.VMEM((128, chunk), BF16), pltpu.VMEM((128, chunk), BF16),
            pltpu.VMEM((DSA_HEADS, 1), F32), pltpu.VMEM((DSA_HEADS, 1), F32),
            pltpu.VMEM((DSA_HEADS, LANES), F32)],
    )
    return pl.pallas_call(
        functools.partial(_decode_kernel, pps=pps), grid_spec=grid_spec,
        out_shape=[jax.ShapeDtypeStruct((dec, 1, MLA_HEADS * MLA_V), F32),
                   jax.ShapeDtypeStruct((dec, DSA_HEADS, LANES), F32)],
        compiler_params=_cparams(("arbitrary", "arbitrary")),
    )(page_table, qn_row, qr, row_new, wuk_t, wuv, gk_row, cache_mla,
      qs, kv_new, selp, seln, bias_past, bias_new, cache_kv4)


def _out_kernel(x_ref, oa_ref, ob_ref, gm_ref, sf_ref, scf_ref, gf_ref, gout_ref, gffn_ref,
                wout_ref, wg_ref, wu_ref, wdown_ref, y_ref, *, ff_chunk):
    def rms(v, g):
        return v * lax.rsqrt(jnp.mean(v * v, axis=-1, keepdims=True) + EPS) * g

    half = oa_ref.shape[1]
    na = rms(oa_ref[...], gout_ref[:, 0:half]).astype(BF16)
    nb = rms(ob_ref[...], gout_ref[:, half:2 * half]).astype(BF16)
    mix = _dot(na, wout_ref[0:half, :]) + _dot(nb, wout_ref[half:2 * half, :])
    x1 = x_ref[...] + gm_ref[...] * mix
    hb = (rms(x1, gffn_ref[...]) * (1.0 + scf_ref[...]) + sf_ref[...]).astype(BF16)
    acc = jnp.zeros(x1.shape, F32)
    for j in range(D_FF // ff_chunk):
        g = _dot(hb, wg_ref[:, j * ff_chunk:(j + 1) * ff_chunk])
        u = _dot(hb, wu_ref[:, j * ff_chunk:(j + 1) * ff_chunk])
        a = (g / (1.0 + jnp.exp(-g))) * u
        acc = acc + _dot(a.astype(BF16), wdown_ref[j * ff_chunk:(j + 1) * ff_chunk, :])
    y_ref[...] = x1 + gf_ref[...] * acc


def _out_stage(x2d, oa, ob, gate_m, shift_f, scale_f, gate_f, wp, tm, per_token, tiles_per_seq):
    t_total = x2d.shape[0]
    if per_token:
        mod_spec = pl.BlockSpec((tm, D_MODEL), lambda i: (i, 0))
    else:
        mod_spec = pl.BlockSpec((None, 1, D_MODEL), lambda i: (i // tiles_per_seq, 0, 0))
    consts = [wp["g_out"], wp["g_ffn"], wp["w_out"], wp["w_gate"], wp["w_up"], wp["w_down"]]
    tok = lambda w: pl.BlockSpec((tm, w), lambda i: (i, 0))
    return pl.pallas_call(
        functools.partial(_out_kernel, ff_chunk=256),
        grid=(t_total // tm,),
        in_specs=[tok(D_MODEL), tok(512), tok(512), mod_spec, mod_spec, mod_spec, mod_spec]
                 + [_full(c.shape) for c in consts],
        out_specs=tok(D_MODEL),
        out_shape=jax.ShapeDtypeStruct((t_total, D_MODEL), F32),
        compiler_params=_cparams(("parallel",)),
    )(x2d, oa, ob, gate_m, shift_f, scale_f, gate_f, *consts)


def _prep_weights(w_in, g_norm_mix, g_norm_ffn, g_q_lat, w_uq, g_kv_lat, w_ukv, g_mla_q_nope, g_mla_q_rope,
                  g_mla_k_nope, g_mla_k_rope, g_dsa_q, g_dsa_k, g_out, w_out, w_ffn_in, w_ffn_out):
    splits = np.cumsum([MLA_Q_RANK, MLA_KV_RANK, MLA_ROPE, 512, 128, 128, 512, IDX_DIM, IDX_HEADS])
    s = [0] + splits.tolist()
    col = lambda i: w_in[:, s[i]:s[i + 1]]
    pad = jnp.zeros((D_MODEL, C_END - C_TAIL - IDX_DIM - MLA_ROPE - IDX_HEADS), w_in.dtype)
    w_in_r = jnp.concatenate([col(0), col(1), col(3), col(4), col(5), col(6), col(7), col(2), col(8), pad], axis=1)
    zq = jnp.zeros((MLA_Q_RANK, MLA_HEADS, LANES - MLA_NOPE - MLA_ROPE), w_uq.dtype)
    w_uq_cat = jnp.concatenate([w_uq, zq], axis=2).reshape(MLA_Q_RANK, MLA_HEADS * LANES)
    zk = jnp.zeros((MLA_KV_RANK, MLA_HEADS, LANES - MLA_NOPE), w_ukv.dtype)
    w_uk_cat = jnp.concatenate([w_ukv[:, :, :MLA_NOPE], zk], axis=2).reshape(MLA_KV_RANK, MLA_HEADS * LANES)
    w_uv = w_ukv[:, :, MLA_NOPE:].reshape(MLA_KV_RANK, MLA_HEADS * MLA_V)
    zv = jnp.zeros((MLA_KV_RANK, MLA_HEADS, LANES - MLA_V), w_ukv.dtype)
    w_uv_pad = jnp.concatenate([w_ukv[:, :, MLA_NOPE:], zv], axis=2).reshape(MLA_KV_RANK, MLA_HEADS * LANES)
    w_uk_t = w_ukv[:, :, :MLA_NOPE].reshape(MLA_KV_RANK, MLA_HEADS * MLA_NOPE).T
    z32 = jnp.zeros((LANES - MLA_NOPE - MLA_ROPE,), F32)
    z64 = jnp.zeros((LANES - MLA_NOPE,), F32)
    g_qcat = jnp.tile(jnp.concatenate([g_mla_q_nope, g_mla_q_rope, z32]), MLA_HEADS)
    g_kcat = jnp.tile(jnp.concatenate([g_mla_k_nope, z64]), MLA_HEADS)
    g_tail = jnp.concatenate([jnp.zeros((T_ROPE,), F32), g_mla_k_rope, z32])
    row = lambda v: v.reshape(1, -1).astype(F32)
    return {
        "g_mix": row(g_norm_mix), "g_ffn": row(g_norm_ffn), "w_in": w_in_r.astype(BF16),
        "g_qlat": row(g_q_lat), "w_uq": w_uq_cat.astype(BF16), "g_qcat": row(g_qcat),
        "g_kvlat": row(g_kv_lat), "w_ukv": jnp.concatenate([w_uk_cat, w_uv_pad], axis=1).astype(BF16),
        "g_kcat": row(g_kcat), "g_tail": row(g_tail),
        "g_q": row(jnp.tile(g_dsa_q, DSA_HEADS)), "g_k": row(jnp.tile(g_dsa_k, DSA_KV_HEADS)),
        "m_cat": _group_matrix(256, [(0, 64), (64, 96), (128, 192), (192, 224)]),
        "m_64": _group_matrix(256, [(0, 64), (64, 128), (128, 192), (192, 256)]),
        "w_uk_t": w_uk_t.astype(BF16), "w_uv": w_uv.astype(BF16), "g_k_nope": g_mla_k_nope,
        "g_out": row(g_out), "w_out": w_out.astype(BF16),
        "w_gate": w_ffn_in[:, :D_FF].astype(BF16), "w_up": w_ffn_in[:, D_FF:].astype(BF16),
        "w_down": w_ffn_out.astype(BF16),
    }


def _rope_tables(pos):
    freq = ROPE_THETA ** (-jnp.arange(HALF_ROPE, dtype=F32) / HALF_ROPE)
    ang = pos.astype(F32)[:, None] * freq[None, :]
    cos, sin = jnp.cos(ang), jnp.sin(ang)
    n = pos.shape[0]
    ones = jnp.ones((n, T_ROPE), F32)
    zeros = jnp.zeros((n, T_ROPE), F32)
    tail1 = jnp.ones((n, LANES - T_ROPE - MLA_ROPE), F32)
    tail0 = jnp.zeros((n, LANES - T_ROPE - MLA_ROPE), F32)
    return (jnp.concatenate([ones, cos, cos, tail1], axis=1),
            jnp.concatenate([zeros, -sin, sin, tail0], axis=1))


def _pick_tile(n, prefs):
    for t in prefs:
        if n % t == 0:
            return t
    return n


def kernel(x_prompt, x_sample, c_prompt, c_sample, cache_mla, cache_kv, cache_idx, page_table, rel_bias, w_ada, b_ada, g_norm_mix, g_norm_ffn, w_in, g_q_lat, w_uq, g_kv_lat, w_ukv, g_mla_q_nope, g_mla_q_rope, g_mla_k_nope, g_mla_k_rope, g_dsa_q, g_dsa_k, g_out, w_out, w_ffn_in, w_ffn_out):
    assert w_ada.shape[0] == 1 and x_sample.shape[1] == 1, "one layer, one new token per sample"
    batch, seq, _ = x_prompt.shape
    dec = x_sample.shape[0]
    n_pages = page_table.shape[1]
    past = n_pages * PAGE_SIZE
    wp = _prep_weights(w_in[0], g_norm_mix[0], g_norm_ffn[0], g_q_lat[0], w_uq[0], g_kv_lat[0], w_ukv[0],
                       g_mla_q_nope[0], g_mla_q_rope[0], g_mla_k_nope[0], g_mla_k_rope[0], g_dsa_q[0], g_dsa_k[0],
                       g_out[0], w_out[0], w_ffn_in[0], w_ffn_out[0])

    mod = _ada(jnp.concatenate([c_prompt, c_sample], axis=0), w_ada[0], b_ada[0])
    mods_p = [m.reshape(batch, 1, D_MODEL) for m in jnp.split(mod[:batch], 6, axis=-1)]
    mods_s = jnp.split(mod[batch:], 6, axis=-1)

    tm = _pick_tile(seq, (512, 256, 128))
    xp2 = x_prompt.reshape(batch * seq, D_MODEL)
    cos_p, sin_p = _rope_tables(jnp.arange(seq))
    pin = _in_stage(xp2, mods_p[1], mods_p[0], cos_p, sin_p, wp, tm, per_token=False)
    o_mla_p = _mla_prompt(pin["qcat"], pin["kcat"], pin["vmla"], batch, seq,
                          _pick_tile(seq, (256, 128)), _pick_tile(seq, (512, 256, 128)))
    o_dsa_p = _dsa_prompt(rel_bias, pin["qidx"], pin["tail"], pin["kidxdup"], pin["qdsa"], pin["kdup"], pin["vone"],
                          batch, seq, _pick_tile(seq, (256, 128)))
    y_p = _out_stage(xp2, o_mla_p, o_dsa_p, mods_p[2], mods_p[3], mods_p[4], mods_p[5], wp, tm,
                     per_token=False, tiles_per_seq=seq // tm)

    xs2 = x_sample.reshape(dec, D_MODEL)
    cos_s, sin_s = _rope_tables(jnp.full((1,), past))
    sin_ = _in_stage(xs2, mods_s[1], mods_s[0], cos_s, sin_s, wp, dec, per_token=True)
    pps_wide = _pick_tile(n_pages, (64, 32, 16, 8, 4, 2))
    qidx3 = sin_["qidx"].reshape(dec, IDX_HEADS, IDX_DIM)
    widx3 = sin_["tail"][:, T_WIDX:T_WIDX + IDX_HEADS].reshape(dec, IDX_HEADS, 1)
    scores = _sample_scores(page_table, qidx3, widx3, _pages_feature_major(cache_idx), pps_wide)
    selp, seln = _sample_select(scores, sin_["qidx"], sin_["tail"])
    qc = sin_["qcat"].reshape(dec, MLA_HEADS, LANES)
    qn_row = qc[:, :, 0:MLA_NOPE].astype(F32).reshape(dec, 1, MLA_HEADS * MLA_NOPE)
    qr = qc[:, :, T_ROPE:T_ROPE + MLA_ROPE]
    gk_row = jnp.tile(wp["g_k_nope"], MLA_HEADS).reshape(1, MLA_HEADS * MLA_NOPE)
    qd = sin_["qdsa"].astype(F32).reshape(dec, DSA_HEADS, DSA_HEAD_DIM)
    on_c = (jnp.arange(DSA_KV_HEADS)[None, :] == (jnp.arange(DSA_HEADS) // DSA_GROUP)[:, None]).astype(F32)
    qd2 = (qd[:, :, None, :] * on_c[None, :, :, None]).reshape(dec, DSA_HEADS, LANES)
    bias_past = _bias_table(rel_bias, 1, past, 0, -1, past).reshape(DSA_HEADS, past)
    bias_new = _bias_table(rel_bias, 1, LANES, 0, 0, 0)[:, 0, 0:1]
    o_mla_s, o8 = _decode_attention(
        page_table, qn_row, qr, sin_["row"].reshape(dec, MLA_ROW, 1), wp["w_uk_t"], wp["w_uv"], gk_row,
        _pages_feature_major(cache_mla),
        qd2, sin_["kv"].reshape(dec, 1, 256), selp.reshape(dec, 1, past), seln.reshape(dec, 1, LANES),
        bias_past, bias_new, _pages_feature_major(cache_kv), pps_wide)
    o_mla_s = o_mla_s.reshape(dec, MLA_HEADS * MLA_V)
    o_dsa_s = o8.reshape(dec, DSA_HEADS, DSA_KV_HEADS, DSA_HEAD_DIM).sum(axis=2).reshape(dec, DSA_HEADS * DSA_HEAD_DIM)
    y_s = _out_stage(xs2, o_mla_s, o_dsa_s, mods_s[2], mods_s[3], mods_s[4], mods_s[5], wp, dec,
                     per_token=True, tiles_per_seq=1)

    return (y_p.reshape(batch, seq, D_MODEL), y_s.reshape(dec, 1, D_MODEL),
            jnp.swapaxes(pin["row"], 1, 2)[None],
            jnp.moveaxis(pin["kv"].reshape(batch, 2, DSA_KV_HEADS, DSA_HEAD_DIM, seq), -1, 1)[None],
            jnp.swapaxes(pin["kidx"], 1, 2)[None],
            sin_["row"].reshape(1, dec, 1, MLA_ROW),
            sin_["kv"].reshape(1, dec, 1, 2, DSA_KV_HEADS, DSA_HEAD_DIM),
            sin_["kidx"].reshape(1, dec, 1, IDX_DIM))
```

```python
import functools
import math

import numpy as np
import jax
import jax.numpy as jnp
from jax import lax
from jax.experimental import pallas as pl
from jax.experimental.pallas import tpu as pltpu

D_MODEL = 1024
PAGE_SIZE = 128
MLA_HEADS = 8
MLA_NOPE = 64
MLA_ROPE = 32
MLA_V = 64
MLA_Q_RANK = 384
MLA_KV_RANK = 256
MLA_ROW = MLA_KV_RANK + MLA_ROPE
DSA_HEADS = 8
DSA_KV_HEADS = 2
DSA_HEAD_DIM = 64
DSA_GROUP = DSA_HEADS // DSA_KV_HEADS
IDX_HEADS = 8
IDX_DIM = 64
TOPK_MAX = 256
N_BUCKETS = 32
MAX_DISTANCE = 128
ROPE_THETA = 10000.0
D_FF = ((8 * D_MODEL + 3 * 256 - 1) // (3 * 256)) * 256
EPS = 1e-6
MLA_SCALE = (MLA_NOPE + MLA_ROPE) ** -0.5
DSA_SCALE = DSA_HEAD_DIM ** -0.5
IDX_SCALE = (IDX_DIM * IDX_HEADS) ** -0.5

LANES = 128
VMEM_LIMIT = 56 * 1024 * 1024

F32 = jnp.float32
BF16 = jnp.bfloat16
I32 = jnp.int32
NEG_BIG = -1e30
INT_MIN = -(2 ** 31)
HALF16 = 2 ** 15

C_QLAT = 0
C_KVLAT = C_QLAT + MLA_Q_RANK
C_Q = C_KVLAT + MLA_KV_RANK
C_K = C_Q + DSA_HEADS * DSA_HEAD_DIM
C_V = C_K + DSA_KV_HEADS * DSA_HEAD_DIM
C_QIDX = C_V + DSA_KV_HEADS * DSA_HEAD_DIM
C_TAIL = C_QIDX + IDX_HEADS * IDX_DIM
C_END = C_TAIL + LANES
T_ROPE = IDX_DIM
T_WIDX = IDX_DIM + MLA_ROPE
HALF_ROPE = MLA_ROPE // 2
KB = 128


def _dot(a, b):
    return jnp.dot(a, b, preferred_element_type=F32)


def _dot_nt(a, b):
    return lax.dot_general(a, b, (((1,), (1,)), ((), ())), preferred_element_type=F32)


def _split(a):
    hi = a.astype(BF16)
    lo = (a - hi.astype(F32)).astype(BF16)
    return hi, lo


def _dot3(a, b):
    ah, al = _split(a)
    bh, bl = _split(b)
    return _dot(ah, bh) + (_dot(al, bh) + _dot(ah, bl))


def _cparams(sem):
    return pltpu.CompilerParams(dimension_semantics=sem, vmem_limit_bytes=VMEM_LIMIT)


def _full(shape):
    n = len(shape)
    return pl.BlockSpec(shape, lambda *a, _n=n: (0,) * _n, pipeline_mode=pl.Buffered(1))


def _ada_kernel(c_ref, w_ref, b_ref, o_ref):
    c = c_ref[...]
    s = c / (1.0 + jnp.exp(-c))
    o_ref[...] = _dot3(s, w_ref[...]) + b_ref[...]


def _ada(c, w, b):
    n = c.shape[0]
    return pl.pallas_call(
        _ada_kernel,
        grid=(6,),
        in_specs=[pl.BlockSpec((n, D_MODEL), lambda j: (0, 0)),
                  pl.BlockSpec((D_MODEL, D_MODEL), lambda j: (0, j)),
                  pl.BlockSpec((1, D_MODEL), lambda j: (0, j))],
        out_specs=pl.BlockSpec((n, D_MODEL), lambda j: (0, j)),
        out_shape=jax.ShapeDtypeStruct((n, 6 * D_MODEL), F32),
        compiler_params=_cparams(("arbitrary",)),
    )(c, w, b.reshape(1, -1))


def _group_matrix(width, groups):
    g = np.zeros((width, width), np.float32)
    for lo, hi in groups:
        g[lo:hi, lo:hi] = 1.0 / (hi - lo)
    return jnp.asarray(g, BF16)


def _in_kernel(x_ref, sc_ref, sh_ref, cos_ref, sin_ref, gmix_ref, win_ref, gqlat_ref, wuq_ref, gqcat_ref,
               gkvlat_ref, wukv_ref, gkcat_ref, gtail_ref, gq_ref, gk_ref, mcat_ref, m64_ref,
               qcat_o, kcat_o, vmla_o, row_o, kv_o, kidx_o, tail_o, kidxdup_o, qdsa_o, kdup_o, vone_o, qidx_o,
               *, rows_feature_major):
    tm = x_ref.shape[0]
    x = x_ref[...]
    h = x * lax.rsqrt(jnp.mean(x * x, axis=-1, keepdims=True) + EPS) * gmix_ref[...]
    hb = (h * (1.0 + sc_ref[...]) + sh_ref[...]).astype(BF16)

    def proj(lo, hi):
        return _dot(hb, win_ref[:, lo:hi])

    cos = cos_ref[...]
    sin = sin_ref[...]
    lane = lax.broadcasted_iota(I32, (tm, LANES), 1)
    first_half = lane < T_ROPE + HALF_ROPE
    rope_lanes = (lane >= T_ROPE) & (lane < T_ROPE + MLA_ROPE)
    low64 = lane < 64

    def rope(xh):
        rot = jnp.where(first_half, pltpu.roll(xh, LANES - HALF_ROPE, 1), pltpu.roll(xh, HALF_ROPE, 1))
        return xh * cos + rot * sin

    def gnorm(blk, m_ref):
        ms = _dot((blk * blk).astype(BF16), m_ref[...])
        return blk * lax.rsqrt(ms + EPS)

    ql = proj(C_QLAT, C_KVLAT)
    qln = ql * lax.rsqrt(jnp.mean(ql * ql, axis=-1, keepdims=True) + EPS) * gqlat_ref[...]
    qm = _dot(qln.astype(BF16), wuq_ref[...])
    for j in range(4):
        y = gnorm(qm[:, 256 * j:256 * j + 256], mcat_ref) * gqcat_ref[:, 256 * j:256 * j + 256]
        for t in range(2):
            qcat_o[:, 256 * j + 128 * t:256 * j + 128 * t + 128] = rope(y[:, 128 * t:128 * t + 128]).astype(BF16)

    tail = proj(C_TAIL, C_END)
    ssq = jnp.sum(jnp.where(rope_lanes, tail * tail, 0.0), axis=-1, keepdims=True) * (1.0 / MLA_ROPE)
    tn = jnp.where(rope_lanes, tail * lax.rsqrt(ssq + EPS) * gtail_ref[...], tail)
    tr = rope(tn)
    if rows_feature_major:
        kidx_o[...] = tail.T[0:IDX_DIM, :]
    else:
        kidx_o[...] = tail[:, 0:IDX_DIM]
    tail_o[...] = tail
    kidxdup_o[...] = jnp.where(low64, tail, pltpu.roll(tail, 64, 1)).astype(BF16)
    krope = jnp.where(rope_lanes, tr, 0.0)

    kvl = proj(C_KVLAT, C_Q)
    lat = kvl * lax.rsqrt(jnp.mean(kvl * kvl, axis=-1, keepdims=True) + EPS) * gkvlat_ref[...]
    if rows_feature_major:
        row_o[0:MLA_KV_RANK, :] = lat.T
        row_o[MLA_KV_RANK:MLA_ROW, :] = tr.T[T_ROPE:T_ROPE + MLA_ROPE, :]
    else:
        row_o[:, 0:MLA_KV_RANK] = lat
        row_o[:, MLA_KV_RANK:MLA_ROW] = tr[:, T_ROPE:T_ROPE + MLA_ROPE]
    latb = lat.astype(BF16)
    kx = _dot(latb, wukv_ref[:, 0:1024])
    for j in range(4):
        y = gnorm(kx[:, 256 * j:256 * j + 256], mcat_ref) * gkcat_ref[:, 256 * j:256 * j + 256]
        for t in range(2):
            kcat_o[:, 256 * j + 128 * t:256 * j + 128 * t + 128] = (y[:, 128 * t:128 * t + 128] + krope).astype(BF16)
    ones_half = (lax.broadcasted_iota(I32, (tm, 1024), 1) % LANES) >= MLA_V
    vmla_o[...] = jnp.where(ones_half, 1.0, _dot(latb, wukv_ref[:, 1024:2048])).astype(BF16)

    q = proj(C_Q, C_K)
    for j in range(2):
        qdsa_o[:, 256 * j:256 * j + 256] = (gnorm(q[:, 256 * j:256 * j + 256], m64_ref)
                                            * gq_ref[:, 256 * j:256 * j + 256] * DSA_SCALE).astype(BF16)
    k = proj(C_K, C_V)
    ms = _dot((k * k).astype(BF16), m64_ref[0:128, 0:128])
    kn = k * lax.rsqrt(ms + EPS) * gk_ref[...]
    v = proj(C_V, C_QIDX)
    if rows_feature_major:
        kv_o[0:128, :] = kn.T
        kv_o[128:256, :] = v.T
    else:
        kv_o[:, 0:128] = kn
        kv_o[:, 128:256] = v
    kr = pltpu.roll(kn, 64, 1)
    kdup_o[:, 0:128] = jnp.where(low64, kn, kr).astype(BF16)
    kdup_o[:, 128:256] = jnp.where(low64, kr, kn).astype(BF16)
    vone_o[:, 0:128] = jnp.where(low64, v, 1.0).astype(BF16)
    vone_o[:, 128:256] = jnp.where(low64, pltpu.roll(v, 64, 1), 1.0).astype(BF16)
    qidx_o[...] = proj(C_QIDX, C_TAIL).astype(BF16)


_IN_OUT_WIDTHS = (("qcat", 1024, BF16), ("kcat", 1024, BF16), ("vmla", 1024, BF16), ("row", MLA_ROW, F32),
                  ("kv", 256, F32), ("kidx", IDX_DIM, F32), ("tail", LANES, F32), ("kidxdup", LANES, BF16),
                  ("qdsa", 512, BF16), ("kdup", 256, BF16), ("vone", 256, BF16), ("qidx", 512, BF16))


def _in_stage(x2d, scale, shift, cos, sin, wp, tm, per_token):
    t_total = x2d.shape[0]
    n_tiles = t_total // tm
    if per_token:
        mod_spec = pl.BlockSpec((tm, D_MODEL), lambda i: (i, 0))
        tab_spec = pl.BlockSpec((1, LANES), lambda i: (0, 0))
    else:
        tiles_per_seq = cos.shape[0] // tm
        mod_spec = pl.BlockSpec((None, 1, D_MODEL), lambda i: (i // tiles_per_seq, 0, 0))
        tab_spec = pl.BlockSpec((tm, LANES), lambda i: (i % tiles_per_seq, 0))
    consts = [wp["g_mix"], wp["w_in"], wp["g_qlat"], wp["w_uq"], wp["g_qcat"], wp["g_kvlat"], wp["w_ukv"],
              wp["g_kcat"], wp["g_tail"], wp["g_q"], wp["g_k"], wp["m_cat"], wp["m_64"]]
    in_specs = ([pl.BlockSpec((tm, D_MODEL), lambda i: (i, 0)), mod_spec, mod_spec, tab_spec, tab_spec]
                + [_full(c.shape) for c in consts])
    out_specs = [pl.BlockSpec((tm, w), lambda i: (i, 0)) for _, w, _ in _IN_OUT_WIDTHS]
    out_shape = [jax.ShapeDtypeStruct((t_total, w), dt) for _, w, dt in _IN_OUT_WIDTHS]
    if not per_token:
        for j, (name, w, dt) in enumerate(_IN_OUT_WIDTHS):
            if name in ("row", "kv", "kidx"):
                out_specs[j] = pl.BlockSpec((None, w, tm), lambda i: (i // tiles_per_seq, 0, i % tiles_per_seq))
                out_shape[j] = jax.ShapeDtypeStruct((n_tiles // tiles_per_seq, w, tiles_per_seq * tm), dt)
    outs = pl.pallas_call(
        functools.partial(_in_kernel, rows_feature_major=not per_token),
        grid=(n_tiles,), in_specs=in_specs, out_specs=out_specs, out_shape=out_shape,
        compiler_params=_cparams(("parallel",)),
    )(x2d, scale, shift, cos, sin, *consts)
    return {name: o for (name, _, _), o in zip(_IN_OUT_WIDTHS, outs)}


def _mla_prompt_kernel(q_ref, k_ref, v_ref, o_ref, m_scr, acc_scr, s_scr, *, tk):
    tq = q_ref.shape[0]
    qi = pl.program_id(1)
    n_chunks = (qi * tq + tq + tk - 1) // tk
    m_scr[...] = jnp.full(m_scr.shape, -jnp.inf, F32)
    acc_scr[...] = jnp.zeros(acc_scr.shape, F32)
    low64 = lax.broadcasted_iota(I32, (tq, LANES), 1) < 64
    qpos = qi * tq + lax.broadcasted_iota(I32, (tq, tk), 0)
    kcol = lax.broadcasted_iota(I32, (tq, tk), 1)
    exp_scale = MLA_SCALE * math.log2(math.e)

    def raw_scores(c, h, masked):
        off = pl.multiple_of(c * tk, tk)
        s = _dot_nt(q_ref[:, h * 128:(h + 1) * 128], k_ref[pl.ds(off, tk), h * 128:(h + 1) * 128])
        if masked:
            s = jnp.where(c * tk + kcol <= qpos, s, -jnp.inf)
        return s

    def pass_max(c, masked):
        for h in range(MLA_HEADS):
            s = raw_scores(c, h, masked)
            s_scr[c, h] = s
            m = m_scr[h]
            for j in range(tk // LANES):
                m = jnp.maximum(m, s[:, j * LANES:(j + 1) * LANES])
            m_scr[h] = m

    def pass_sum(c, masked):
        off = pl.multiple_of(c * tk, tk)
        for h in range(MLA_HEADS):
            s = s_scr[c, h]
            m = m_scr[h]
            p = jnp.concatenate([jnp.exp2((s[:, j * LANES:(j + 1) * LANES] - m) * exp_scale)
                                 for j in range(tk // LANES)], axis=1).astype(BF16)
            acc_scr[h] += _dot(p, v_ref[pl.ds(off, tk), h * 128:(h + 1) * 128])

    def run(step):
        def body(c, carry):
            step(c, False)
            return carry

        lax.fori_loop(0, n_chunks - 1, body, 0)
        step(n_chunks - 1, True)

    run(pass_max)
    for h in range(MLA_HEADS):
        m_scr[h] = jnp.broadcast_to(jnp.max(m_scr[h], axis=-1, keepdims=True), (tq, LANES))
    run(pass_sum)
    for p in range(MLA_HEADS // 2):
        even, odd = acc_scr[2 * p], acc_scr[2 * p + 1]
        o_ref[:, 128 * p:128 * p + 128] = jnp.where(low64, even / pltpu.roll(even, 64, 1),
                                                    pltpu.roll(odd, 64, 1) / odd)


def _mla_prompt(qcat, kcat, vmla, batch, seq, tq, tk):
    nq = seq // tq
    return pl.pallas_call(
        functools.partial(_mla_prompt_kernel, tk=tk),
        grid=(batch, nq),
        in_specs=[pl.BlockSpec((tq, 1024), lambda b, i: (b * nq + i, 0)),
                  pl.BlockSpec((seq, 1024), lambda b, i: (b, 0)),
                  pl.BlockSpec((seq, 1024), lambda b, i: (b, 0))],
        out_specs=pl.BlockSpec((tq, 512), lambda b, i: (b * nq + i, 0)),
        out_shape=jax.ShapeDtypeStruct((batch * seq, 512), F32),
        scratch_shapes=[pltpu.VMEM((MLA_HEADS, tq, LANES), F32), pltpu.VMEM((MLA_HEADS, tq, LANES), F32),
                        pltpu.VMEM((seq // tk, MLA_HEADS, tq, tk), F32)],
        compiler_params=_cparams(("parallel", "arbitrary")),
    )(qcat, kcat, vmla)


def _sort_key(score):
    bits = lax.bitcast_convert_type(score + 0.0, I32)
    return bits ^ ((bits >> 31) & 0x7FFFFFFF)


def _kth_key(count_ge, n_sel, shape, bits=32):
    def body(i, t):
        cand = t + (jnp.int32(1) << (bits - 1 - i))
        return jnp.where(count_ge(cand) >= n_sel, cand, t)

    return lax.fori_loop(0, bits, body, jnp.full(shape, -(2 ** (bits - 1)), I32))


def _bucket(dist):
    max_exact = N_BUCKETS // 2
    d = jnp.maximum(dist, 0)
    log_ratio = jnp.log(jnp.maximum(d, max_exact).astype(F32) / max_exact) / math.log(MAX_DISTANCE / max_exact)
    large = jnp.minimum(max_exact + (log_ratio * (N_BUCKETS - max_exact)).astype(I32), N_BUCKETS - 1)
    return jnp.where(d < max_exact, d, large)


def _bias_kernel(rb_ref, o_ref, *, a, b, c):
    rows, cols = o_ref.shape[1], o_ref.shape[2]
    dist = (a * lax.broadcasted_iota(I32, (rows, cols), 0) + b * lax.broadcasted_iota(I32, (rows, cols), 1) + c)
    bucket = _bucket(dist)
    for h in range(DSA_HEADS):
        acc = jnp.zeros((rows, cols), F32)
        for n in range(N_BUCKETS):
            acc = jnp.where(bucket == n, rb_ref[n, h], acc)
        o_ref[h] = acc


def _bias_table(rel_bias, rows, cols, a, b, c):
    return pl.pallas_call(
        functools.partial(_bias_kernel, a=a, b=b, c=c),
        in_specs=[pl.BlockSpec(memory_space=pltpu.SMEM)],
        out_specs=pl.BlockSpec(memory_space=pltpu.VMEM),
        out_shape=jax.ShapeDtypeStruct((DSA_HEADS, rows, cols), F32),
    )(rel_bias)


def _loop2(n, body):
    n = jnp.maximum(n, 0)

    def pair(j, carry):
        body(2 * j)
        body(2 * j + 1)
        return carry

    lax.fori_loop(0, n // 2, pair, 0)

    @pl.when(n % 2 == 1)
    def _():
        body(n - 1)


def _colreduce(x, op):
    return op(x.reshape(x.shape[0] // 8, 8, x.shape[1]), axis=0)


def _dsa_prompt_kernel(rb_ref, qidx_ref, tail_ref, kidx_ref, q_ref, k_ref, v_ref, bias_ref, o_ref,
                       qs_scr, qd_scr, wb_scr, key_scr, hi_scr, lo_scr, tie_scr, m_scr, p_scr, acc_scr, s_scr,
                       *, n_sel, idx_bits):
    tq = q_ref.shape[0]
    kq = tq // KB
    qi = pl.program_id(1)
    krow = lax.broadcasted_iota(I32, (KB, tq), 0)
    qcol = lax.broadcasted_iota(I32, (KB, tq), 1)
    low64 = lax.broadcasted_iota(I32, (tq, LANES), 1) < 64

    tail = tail_ref[...]
    for h in range(IDX_HEADS):
        mine = low64 if h % 2 == 0 else jnp.logical_not(low64)
        zero = jnp.zeros((tq, LANES), BF16)
        qs_scr[h * tq:(h + 1) * tq, :] = jnp.where(mine, qidx_ref[:, (h // 2) * 128:(h // 2 + 1) * 128], zero)
        qd_scr[h * tq:(h + 1) * tq, :] = jnp.where(mine, q_ref[:, (h // 2) * 128:(h // 2 + 1) * 128], zero)
        wb_scr[h] = jnp.broadcast_to(tail[:, T_WIDX + h:T_WIDX + h + 1], (tq, LANES))

    def rows_of(c):
        return pl.ds(pl.multiple_of(c * tq, tq), tq)

    def score_chunk(c, in_tile):
        d = _dot_nt(qs_scr[...], kidx_ref[rows_of(c), :])
        rb = 64
        for t in range(kq):
            parts = []
            for r in range(0, tq, rb):
                sc = jnp.zeros((rb, KB), F32)
                for h in range(IDX_HEADS):
                    sc = sc + (jnp.maximum(d[h * tq + r:h * tq + r + rb, t * KB:(t + 1) * KB], 0.0)
                               * wb_scr[h, r:r + rb, :])
                parts.append(sc)
            sc_t = (jnp.concatenate(parts, axis=0) * IDX_SCALE).T
            if in_tile:
                sc_t = jnp.where(t * KB + krow <= qcol, sc_t, -jnp.inf)
            key = _sort_key(sc_t)
            key_scr[c * kq + t] = key
            hi_scr[c * kq + t] = (key >> 16).astype(jnp.int16)
            lo_scr[c * kq + t] = ((key & 0xFFFF) - HALF16).astype(jnp.int16)

    _loop2(qi, lambda c: score_chunk(c, False))
    score_chunk(qi, True)
    n_chunks = qi + 1

    def count(pred):
        part = 32
        def body(c, acc):
            for t in range(kq):
                kb = c * kq + t
                hit = jnp.where(pred(key_scr[kb], kb), 1.0, 0.0)
                acc = acc + jnp.sum(hit.reshape(KB // part, part, tq), axis=0)
            return acc
        acc = lax.fori_loop(0, n_chunks, body, jnp.zeros((part, tq), F32))
        return jnp.sum(_colreduce(acc, jnp.sum), axis=0, keepdims=True)

    def count16(src, pred):
        part = 32
        one, zero = jnp.ones((), BF16), jnp.zeros((), BF16)
        def body(c, acc):
            for t in range(kq):
                hit = jnp.where(pred(src[c * kq + t]), one, zero).reshape(KB // part, part, tq)
                for j in range(KB // part):
                    acc = acc + hit[j]
            return acc
        acc = lax.fori_loop(0, n_chunks, body, jnp.zeros((part, tq), BF16))
        return jnp.sum(_colreduce(acc.astype(F32), jnp.sum), axis=0, keepdims=True)

    t_hi = _kth_key(lambda cand: count16(hi_scr, lambda b: b >= cand.astype(jnp.int16)),
                    float(n_sel), (1, tq), bits=16)
    t_hi16 = t_hi.astype(jnp.int16)
    above = count16(hi_scr, lambda b: b > t_hi16)

    def keep_bucket(c, carry):
        for t in range(kq):
            kb = c * kq + t
            lo_scr[kb] = jnp.where(hi_scr[kb] == t_hi16, lo_scr[kb], jnp.int16(-HALF16))
        return carry

    lax.fori_loop(0, n_chunks, keep_bucket, 0)
    t_lo = _kth_key(lambda cand: above + count16(lo_scr, lambda b: b >= cand.astype(jnp.int16)),
                    float(n_sel), (1, tq), bits=16)
    thr = t_hi * (2 * HALF16) + (t_lo + HALF16)
    cnt_gt = count(lambda k, kb: k > thr)
    cnt_ge = count(lambda k, kb: k >= thr)
    tie_scr[...] = jnp.full((1, tq), 2 ** 30, I32)
    excess = jnp.where((cnt_ge > float(n_sel)) & (thr > INT_MIN), 1.0, 0.0)

    @pl.when(jnp.max(excess) > 0.0)
    def _():
        need = float(n_sel) - cnt_gt

        def body(i, p):
            cand = p + (jnp.int32(1) << (idx_bits - 1 - i))
            c = count(lambda k, kb: (k == thr) & (kb * KB + krow < cand))
            return jnp.where(c < need, cand, p)

        tie_scr[...] = lax.fori_loop(0, idx_bits, body, jnp.zeros((1, tq), I32))

    last_tie = tie_scr[...]

    def mask_chunk(c, in_tile):
        halves = []
        for t in range(kq):
            kb = c * kq + t
            key = key_scr[kb]
            sel = (key > thr) | ((key == thr) & (kb * KB + krow <= last_tie))
            if in_tile:
                sel = sel & (t * KB + krow <= qcol)
            halves.append(jnp.where(sel, 0.0, NEG_BIG).T)
        return halves

    far_bias = [rb_ref[N_BUCKETS - 1, h] for h in range(DSA_HEADS)]

    def scores(c, g, kind, mask):
        sg = _dot_nt(qd_scr[g * DSA_GROUP * tq:(g + 1) * DSA_GROUP * tq, :], k_ref[rows_of(c), g * 128:(g + 1) * 128])
        out = []
        for hh in range(DSA_GROUP):
            h = g * DSA_GROUP + hh
            cols = []
            for t in range(kq):
                s = sg[hh * tq:(hh + 1) * tq, t * KB:(t + 1) * KB] + mask[t]
                if kind == "tile":
                    s = s + (bias_ref[1 + t, h] - far_bias[h])
                elif kind == "before" and t == kq - 1:
                    s = s + (bias_ref[0, h] - far_bias[h])
                cols.append(s)
            out.append(cols)
        return out

    m_scr[...] = jnp.full(m_scr.shape, NEG_BIG, F32)
    acc_scr[...] = jnp.zeros(acc_scr.shape, F32)

    def pass_max(c, kind):
        mask = mask_chunk(c, kind == "tile")
        for g in range(DSA_KV_HEADS):
            for hh, cols in enumerate(scores(c, g, kind, mask)):
                h = g * DSA_GROUP + hh
                m = m_scr[h]
                for t, s in enumerate(cols):
                    s_scr[c, h, :, t * KB:(t + 1) * KB] = s
                    m = jnp.maximum(m, s)
                m_scr[h] = m

    def pass_sum(c, kind):
        for g in range(DSA_KV_HEADS):
            for hh in range(DSA_GROUP):
                h = g * DSA_GROUP + hh
                m = m_scr[h]
                for t in range(kq):
                    s = s_scr[c, h, :, t * KB:(t + 1) * KB]
                    p_scr[g, hh * tq:(hh + 1) * tq, t * KB:(t + 1) * KB] = jnp.exp(s - m).astype(BF16)
            acc_scr[g] += _dot(p_scr[g], v_ref[rows_of(c), g * 128:(g + 1) * 128])

    def run(step):
        _loop2(qi - 1, lambda c: step(c, "far"))

        @pl.when(qi >= 1)
        def _():
            step(qi - 1, "before")

        step(qi, "tile")

    run(pass_max)
    for h in range(DSA_HEADS):
        m_scr[h] = jnp.broadcast_to(jnp.max(m_scr[h], axis=-1, keepdims=True), (tq, LANES))
    run(pass_sum)

    for p in range(DSA_HEADS // 2):
        g, hh = (2 * p) // DSA_GROUP, (2 * p) % DSA_GROUP
        even = acc_scr[g, hh * tq:(hh + 1) * tq, :]
        odd = acc_scr[g, (hh + 1) * tq:(hh + 2) * tq, :]
        o_ref[:, 128 * p:128 * p + 128] = jnp.where(low64, even / pltpu.roll(even, 64, 1),
                                                    pltpu.roll(odd, 64, 1) / odd)


def _dsa_prompt(rel_bias, qidx, tail, kidxdup, qdsa, kdup, vone, batch, seq, tq):
    assert MAX_DISTANCE <= KB, "keys two or more blocks back must all fall in the last bucket"
    nq = seq // tq
    nkb = seq // KB
    kq = tq // KB
    n_sel = min(TOPK_MAX, seq // 4)
    idx_bits = max(1, (seq - 1).bit_length())
    bias = jnp.stack([_bias_table(rel_bias, tq, KB, 1, -1, KB - t * KB) for t in range(kq + 1)])
    kern = functools.partial(_dsa_prompt_kernel, n_sel=n_sel, idx_bits=idx_bits)
    qblk = lambda w: pl.BlockSpec((tq, w), lambda b, i: (b * nq + i, 0))
    seqblk = lambda w: pl.BlockSpec((seq, w), lambda b, i: (b, 0))
    return pl.pallas_call(
        kern,
        grid=(batch, nq),
        in_specs=[pl.BlockSpec(memory_space=pltpu.SMEM),
                  qblk(512), qblk(LANES), seqblk(LANES), qblk(512), seqblk(256), seqblk(256), _full(bias.shape)],
        out_specs=qblk(512),
        out_shape=jax.ShapeDtypeStruct((batch * seq, 512), F32),
        scratch_shapes=[pltpu.VMEM((IDX_HEADS * tq, LANES), BF16), pltpu.VMEM((DSA_HEADS * tq, LANES), BF16),
                        pltpu.VMEM((IDX_HEADS, tq, LANES), F32),
                        pltpu.VMEM((nkb, KB, tq), I32),
                        pltpu.VMEM((nkb, KB, tq), jnp.int16), pltpu.VMEM((nkb, KB, tq), jnp.int16),
                        pltpu.VMEM((1, tq), I32), pltpu.VMEM((DSA_HEADS, tq, LANES), F32),
                        pltpu.VMEM((DSA_KV_HEADS, DSA_GROUP * tq, tq), BF16),
                        pltpu.VMEM((DSA_KV_HEADS, DSA_GROUP * tq, LANES), F32),
                        pltpu.VMEM((nq, DSA_HEADS, tq, tq), F32)],
        compiler_params=_cparams(("parallel", "arbitrary")),
    )(rel_bias, qidx, tail, kidxdup, qdsa, kdup, vone, bias)


def _fetch_pages(cache_ref, pt_ref, buf, sem, step, n_steps, locate, pps):
    def copy(slot, i, page):
        return pltpu.make_async_copy(cache_ref.at[0, page], buf.at[slot, i], sem.at[slot])

    def start(st, slot):
        b, first = locate(st)
        for i in range(pps):
            copy(slot, i, pt_ref[b, first + i]).start(priority=i % 2)

    @pl.when(step == 0)
    def _():
        start(step, 0)

    @pl.when(step + 1 < n_steps)
    def _():
        start(step + 1, (step + 1) % 2)

    slot = step % 2
    for i in range(pps):
        copy(slot, i, 0).wait()
    return slot


def _page_scratch(pps, width):
    return [pltpu.VMEM((2, pps, width, PAGE_SIZE), F32), pltpu.SemaphoreType.DMA((2,))]


def _pages_feature_major(cache):
    c = jnp.moveaxis(cache, 2, -1)
    return c.reshape(cache.shape[0], cache.shape[1], -1, PAGE_SIZE)


def _sample_score_kernel(pt_ref, q_ref, w_ref, cache_ref, o_ref, buf, sem, kb_scr, *, pps, grp):
    bo, c, bi = pl.program_id(0), pl.program_id(1), pl.program_id(2)
    n_chunks = pl.num_programs(1)
    step = (bo * n_chunks + c) * grp + bi

    def locate(st):
        return (st // (n_chunks * grp)) * grp + st % grp, ((st // grp) % n_chunks) * pps

    slot = _fetch_pages(cache_ref, pt_ref, buf, sem, step, pl.num_programs(0) * n_chunks * grp, locate, pps)
    for i in range(pps):
        kb_scr[:, i * PAGE_SIZE:(i + 1) * PAGE_SIZE] = buf[slot, i].astype(BF16)
    d = _dot(q_ref[...], kb_scr[...])
    sc = jnp.sum(jnp.maximum(d, 0.0) * w_ref[...], axis=0, keepdims=True) * IDX_SCALE
    o_ref[pl.ds(bi, 1), :] = sc


def _sample_scores(page_table, qidx3, widx3, cache_idx_t, pps):
    dec, n_pages = page_table.shape
    n_chunks = n_pages // pps
    grp = 8
    grid_spec = pltpu.PrefetchScalarGridSpec(
        num_scalar_prefetch=1,
        grid=(dec // grp, n_chunks, grp),
        in_specs=[pl.BlockSpec((None, IDX_HEADS, IDX_DIM), lambda bo, c, bi, pt: (bo * grp + bi, 0, 0)),
                  pl.BlockSpec((None, IDX_HEADS, 1), lambda bo, c, bi, pt: (bo * grp + bi, 0, 0)),
                  pl.BlockSpec(memory_space=pl.ANY)],
        out_specs=pl.BlockSpec((grp, pps * PAGE_SIZE), lambda bo, c, bi, pt: (bo, c)),
        scratch_shapes=_page_scratch(pps, IDX_DIM) + [pltpu.VMEM((IDX_DIM, pps * PAGE_SIZE), BF16)],
    )
    return pl.pallas_call(
        functools.partial(_sample_score_kernel, pps=pps, grp=grp), grid_spec=grid_spec,
        out_shape=jax.ShapeDtypeStruct((dec, n_pages * PAGE_SIZE), F32),
        compiler_params=_cparams(("arbitrary", "arbitrary", "arbitrary")),
    )(page_table, qidx3, widx3, cache_idx_t)


def _sample_select_kernel(sc_ref, qidx_ref, tail_ref, selp_ref, seln_ref, key_scr, *, n_sel, idx_bits):
    rows, past = sc_ref.shape
    nblk = past // LANES
    lane = lax.broadcasted_iota(I32, (rows, LANES), 1)
    tail = tail_ref[...]
    kidx2 = jnp.where(lane < 64, tail, pltpu.roll(tail, 64, 1))
    sc_new = jnp.zeros((rows, 1), F32)
    for h in range(IDX_HEADS):
        prod = qidx_ref[:, (h // 2) * 128:(h // 2 + 1) * 128].astype(F32) * kidx2.astype(BF16).astype(F32)
        mine = (lane < 64) if h % 2 == 0 else (lane >= 64)
        dot = jnp.sum(jnp.where(mine, prod, 0.0), axis=-1, keepdims=True)
        sc_new = sc_new + jnp.maximum(dot, 0.0) * tail[:, T_WIDX + h:T_WIDX + h + 1]
    key_new = _sort_key(sc_new * IDX_SCALE)

    for j in range(nblk):
        key_scr[j] = _sort_key(sc_ref[:, j * LANES:(j + 1) * LANES])

    def count(pred, pred_new):
        def body(j, c):
            return c + jnp.where(pred(key_scr[j], j), 1.0, 0.0)
        c = lax.fori_loop(0, nblk, body, jnp.zeros((rows, LANES), F32), unroll=4)
        return jnp.sum(c, axis=-1, keepdims=True) + jnp.where(pred_new, 1.0, 0.0)

    def wide(col):
        return jnp.broadcast_to(col, (rows, LANES))

    def count_ge(cand):
        cand_w = wide(cand)
        return count(lambda k, j: k >= cand_w, key_new >= cand)

    thr = _kth_key(count_ge, float(n_sel), (rows, 1))
    thr_w = wide(thr)
    cnt_gt = count(lambda k, j: k > thr_w, key_new > thr)
    need = float(n_sel) - cnt_gt

    def body(i, p):
        cand = p + (jnp.int32(1) << (idx_bits - 1 - i))
        cand_w = wide(cand)
        c = count(lambda k, j: (k == thr_w) & (j * LANES + lane < cand_w), (key_new == thr) & (past < cand))
        return jnp.where(c < need, cand, p)

    last_tie = lax.fori_loop(0, idx_bits, body, jnp.zeros((rows, 1), I32))
    last_w = wide(last_tie)

    for j in range(nblk):
        k = key_scr[j]
        sel = (k > thr_w) | ((k == thr_w) & (j * LANES + lane <= last_w))
        selp_ref[:, j * LANES:(j + 1) * LANES] = jnp.where(sel, 1.0, 0.0)
    sel_new = (key_new > thr) | ((key_new == thr) & (past <= last_tie))
    seln_ref[...] = jnp.broadcast_to(jnp.where(sel_new, 1.0, 0.0), (rows, LANES))


def _sample_select(scores, qidx, tail):
    dec, past = scores.shape
    n_sel = min(TOPK_MAX, (past + 1) // 4)
    idx_bits = max(1, past.bit_length())
    rows = 64 if dec % 64 == 0 else dec
    kern = functools.partial(_sample_select_kernel, n_sel=n_sel, idx_bits=idx_bits)
    return pl.pallas_call(
        kern,
        grid=(dec // rows,),
        in_specs=[pl.BlockSpec((rows, past), lambda i: (i, 0)), pl.BlockSpec((rows, 512), lambda i: (i, 0)),
                  pl.BlockSpec((rows, LANES), lambda i: (i, 0))],
        out_specs=[pl.BlockSpec((rows, past), lambda i: (i, 0)), pl.BlockSpec((rows, LANES), lambda i: (i, 0))],
        out_shape=[jax.ShapeDtypeStruct((dec, past), F32), jax.ShapeDtypeStruct((dec, LANES), F32)],
        scratch_shapes=[pltpu.VMEM((past // LANES, rows, LANES), I32)],
        compiler_params=_cparams(("parallel",)),
    )(scores, qidx, tail)


def _mla_sample_kernel(pt_ref, qn_ref, qr_ref, rown_ref, wuk_ref, wuv_ref, gk_ref, cache_ref, o_ref,
                       buf, sem, lat_scr, kr_scr, a_scr, m_scr, l_scr, acc_scr, *, pps):
    c = pl.program_id(1)
    n_chunks = pl.num_programs(1)
    heads = MLA_HEADS
    slot = _fetch_pages(cache_ref, pt_ref, buf, sem, pl.program_id(0) * n_chunks + c,
                        pl.num_programs(0) * n_chunks, lambda st: (st // n_chunks, (st % n_chunks) * pps), pps)

    @pl.when(c == 0)
    def _():
        m_scr[...] = jnp.full(m_scr.shape, -jnp.inf, F32)
        l_scr[...] = jnp.zeros(l_scr.shape, F32)
        acc_scr[...] = jnp.zeros(acc_scr.shape, F32)
        qg = qn_ref[...] * gk_ref[...]
        own = (lax.broadcasted_iota(I32, (heads, heads * MLA_NOPE), 1) // MLA_NOPE
               == lax.broadcasted_iota(I32, (heads, heads * MLA_NOPE), 0))
        n_up = heads * MLA_NOPE
        a_scr[0:n_up, :] = wuk_ref[...]
        a = _dot(jnp.where(own, qg, 0.0).astype(BF16), wuk_ref[...]).astype(BF16)
        a_scr[n_up:n_up + 16, :] = jnp.concatenate([a, jnp.zeros_like(a)], axis=0)

    qr = qr_ref[...]

    def scores(latb, krb):
        n_up = heads * MLA_NOPE
        up = _dot(a_scr[...], latb)
        knt = up[0:n_up, :]
        ms = jnp.mean((knt * knt).reshape(heads, MLA_NOPE, latb.shape[1]), axis=1)
        return (up[n_up:n_up + heads, :] * lax.rsqrt(ms + EPS) + _dot(qr, krb)) * MLA_SCALE

    def update(latb, krb, n_valid):
        n = latb.shape[1]
        sub = min(n, 512)
        s = jnp.concatenate([scores(latb[:, j:j + sub], krb[:, j:j + sub]) for j in range(0, n, sub)], axis=1)
        if n_valid < n:
            s = jnp.where(lax.broadcasted_iota(I32, s.shape, 1) < n_valid, s, -jnp.inf)
        m_old = m_scr[...]
        m_new = jnp.maximum(m_old, jnp.max(s, axis=-1, keepdims=True))
        alpha = jnp.exp(m_old - m_new)
        p = jnp.exp(s - m_new)
        l_scr[...] = alpha * l_scr[...] + jnp.sum(p, axis=-1, keepdims=True)
        m_scr[...] = m_new
        acc_scr[...] = alpha * acc_scr[...] + _dot_nt(p.astype(BF16), latb)

    for i in range(pps):
        lat_scr[:, i * PAGE_SIZE:(i + 1) * PAGE_SIZE] = buf[slot, i, 0:MLA_KV_RANK, :].astype(BF16)
        kr_scr[:, i * PAGE_SIZE:(i + 1) * PAGE_SIZE] = buf[slot, i, MLA_KV_RANK:MLA_ROW, :].astype(BF16)
    update(lat_scr[...], kr_scr[...], lat_scr.shape[1])

    @pl.when(c == n_chunks - 1)
    def _():
        rn = jnp.broadcast_to(rown_ref[...], (MLA_ROW, PAGE_SIZE))
        update(rn[0:MLA_KV_RANK, :].astype(BF16), rn[MLA_KV_RANK:MLA_ROW, :].astype(BF16), 1)
        o_lat = (acc_scr[...] / l_scr[...]).astype(BF16)
        full = _dot(o_lat, wuv_ref[...])
        hd = lax.broadcasted_iota(I32, full.shape, 1) // MLA_V
        own = hd == lax.broadcasted_iota(I32, full.shape, 0)
        o_ref[...] = jnp.sum(jnp.where(own, full, 0.0), axis=0, keepdims=True)


def _dsa_sample_kernel(pt_ref, q_ref, kvn_ref, selp_ref, seln_ref, biasp_ref, biasn_ref, cache_ref, o_ref,
                       buf, sem, k_scr, v_scr, m_scr, l_scr, acc_scr, *, pps):
    c = pl.program_id(1)
    n_chunks = pl.num_programs(1)
    slot = _fetch_pages(cache_ref, pt_ref, buf, sem, pl.program_id(0) * n_chunks + c,
                        pl.num_programs(0) * n_chunks, lambda st: (st // n_chunks, (st % n_chunks) * pps), pps)

    @pl.when(c == 0)
    def _():
        m_scr[...] = jnp.full(m_scr.shape, NEG_BIG, F32)
        l_scr[...] = jnp.zeros(l_scr.shape, F32)
        acc_scr[...] = jnp.zeros(acc_scr.shape, F32)

    q = q_ref[...]

    def update(s, sel, pv_of):
        sh = jnp.where(sel, s, NEG_BIG)
        m_old = m_scr[...]
        m_new = jnp.maximum(m_old, jnp.max(sh, axis=-1, keepdims=True))
        alpha = jnp.exp(m_old - m_new)
        p = jnp.where(sel, jnp.exp(sh - m_new), 0.0)
        l_scr[...] = alpha * l_scr[...] + jnp.sum(p, axis=-1, keepdims=True)
        m_scr[...] = m_new
        acc_scr[...] = alpha * acc_scr[...] + pv_of(p)

    for i in range(pps):
        k_scr[:, i * PAGE_SIZE:(i + 1) * PAGE_SIZE] = buf[slot, i, 0:128, :].astype(BF16)
        v_scr[:, i * PAGE_SIZE:(i + 1) * PAGE_SIZE] = buf[slot, i, 128:256, :].astype(BF16)
    s = _dot(q.astype(BF16), k_scr[...]) + biasp_ref[...]
    update(s, selp_ref[...] > 0.5, lambda p: _dot_nt(p.astype(BF16), v_scr[...]))

    @pl.when(c == n_chunks - 1)
    def _():
        kvn = kvn_ref[...]
        kn = kvn[:, 0:128].astype(BF16).astype(F32)
        vn = kvn[:, 128:256].astype(BF16).astype(F32)
        s_new = jnp.sum(q.astype(F32) * kn, axis=-1, keepdims=True) + biasn_ref[...]
        update(s_new, seln_ref[:, 0:1] > 0.5, lambda p: p.astype(BF16).astype(F32) * vn)
        o = acc_scr[...] / l_scr[...]
        lane = lax.broadcasted_iota(I32, o.shape, 1)
        row = lax.broadcasted_iota(I32, o.shape, 0)
        own = (lane // DSA_HEAD_DIM) == (row // DSA_GROUP)
        o_ref[...] = jnp.where(own, o, 0.0)


_N_MLA_IN, _N_MLA_SCRATCH = 7, 8
_N_DSA_IN = 7


def _decode_kernel(pt_ref, *refs, pps):
    mla_in = refs[:_N_MLA_IN]
    dsa_in = refs[_N_MLA_IN:_N_MLA_IN + _N_DSA_IN]
    o_mla, o_dsa = refs[_N_MLA_IN + _N_DSA_IN:_N_MLA_IN + _N_DSA_IN + 2]
    scratch = refs[_N_MLA_IN + _N_DSA_IN + 2:]
    _mla_sample_kernel(pt_ref, *mla_in, o_mla, *scratch[:_N_MLA_SCRATCH], pps=pps)
    _dsa_sample_kernel(pt_ref, *dsa_in, o_dsa, *scratch[_N_MLA_SCRATCH:], pps=pps)


def _decode_attention(page_table, qn_row, qr, row_new, wuk_t, wuv, gk_row, cache_mla,
                      qs, kv_new, selp, seln, bias_past, bias_new, cache_kv4, pps):
    dec, n_pages = page_table.shape
    n_chunks = n_pages // pps
    chunk = pps * PAGE_SIZE
    per_b = lambda *shape: pl.BlockSpec((None,) + shape, lambda b, c, pt: (b,) + (0,) * len(shape))
    const = lambda a: pl.BlockSpec(a.shape, lambda b, c, pt: (0,) * a.ndim)
    grid_spec = pltpu.PrefetchScalarGridSpec(
        num_scalar_prefetch=1,
        grid=(dec, n_chunks),
        in_specs=[per_b(1, MLA_HEADS * MLA_NOPE), per_b(MLA_HEADS, MLA_ROPE), per_b(MLA_ROW, 1),
                  const(wuk_t), const(wuv), const(gk_row), pl.BlockSpec(memory_space=pl.ANY),
                  per_b(DSA_HEADS, LANES), per_b(1, 256),
                  pl.BlockSpec((None, 1, chunk), lambda b, c, pt: (b, 0, c)), per_b(1, LANES),
                  pl.BlockSpec((DSA_HEADS, chunk), lambda b, c, pt: (0, c)), const(bias_new),
                  pl.BlockSpec(memory_space=pl.ANY)],
        out_specs=[per_b(1, MLA_HEADS * MLA_V), per_b(DSA_HEADS, LANES)],
        scratch_shapes=_page_scratch(pps, MLA_ROW) + [
            pltpu.VMEM((MLA_KV_RANK, chunk), BF16), pltpu.VMEM((MLA_ROPE, chunk), BF16),
            pltpu.VMEM((MLA_HEADS * MLA_NOPE + 16, MLA_KV_RANK), BF16),
            pltpu.VMEM((MLA_HEADS, 1), F32), pltpu.VMEM((MLA_HEADS, 1), F32),
            pltpu.VMEM((MLA_HEADS, MLA_KV_RANK), F32)]
        + _page_scratch(pps, 256) + [
            pltpu.VMEM((128, chunk), BF16), pltpu.VMEM((128, chunk), BF16),
            pltpu.VMEM((DSA_HEADS, 1), F32), pltpu.VMEM((DSA_HEADS, 1), F32),
            pltpu.VMEM((DSA_HEADS, LANES), F32)],
    )
    return pl.pallas_call(
        functools.partial(_decode_kernel, pps=pps), grid_spec=grid_spec,
        out_shape=[jax.ShapeDtypeStruct((dec, 1, MLA_HEADS * MLA_V), F32),
                   jax.ShapeDtypeStruct((dec, DSA_HEADS, LANES), F32)],
        compiler_params=_cparams(("arbitrary", "arbitrary")),
    )(page_table, qn_row, qr, row_new, wuk_t, wuv, gk_row, cache_mla,
      qs, kv_new, selp, seln, bias_past, bias_new, cache_kv4)


def _out_kernel(x_ref, oa_ref, ob_ref, gm_ref, sf_ref, scf_ref, gf_ref, gout_ref, gffn_ref,
                wout_ref, wg_ref, wu_ref, wdown_ref, y_ref, *, ff_chunk):
    def rms(v, g):
        return v * lax.rsqrt(jnp.mean(v * v, axis=-1, keepdims=True) + EPS) * g

    half = oa_ref.shape[1]
    na = rms(oa_ref[...], gout_ref[:, 0:half]).astype(BF16)
    nb = rms(ob_ref[...], gout_ref[:, half:2 * half]).astype(BF16)
    mix = _dot(na, wout_ref[0:half, :]) + _dot(nb, wout_ref[half:2 * half, :])
    x1 = x_ref[...] + gm_ref[...] * mix
    hb = (rms(x1, gffn_ref[...]) * (1.0 + scf_ref[...]) + sf_ref[...]).astype(BF16)
    acc = jnp.zeros(x1.shape, F32)
    for j in range(D_FF // ff_chunk):
        g = _dot(hb, wg_ref[:, j * ff_chunk:(j + 1) * ff_chunk])
        u = _dot(hb, wu_ref[:, j * ff_chunk:(j + 1) * ff_chunk])
        a = (g / (1.0 + jnp.exp(-g))) * u
        acc = acc + _dot(a.astype(BF16), wdown_ref[j * ff_chunk:(j + 1) * ff_chunk, :])
    y_ref[...] = x1 + gf_ref[...] * acc


def _out_stage(x2d, oa, ob, gate_m, shift_f, scale_f, gate_f, wp, tm, per_token, tiles_per_seq):
    t_total = x2d.shape[0]
    if per_token:
        mod_spec = pl.BlockSpec((tm, D_MODEL), lambda i: (i, 0))
    else:
        mod_spec = pl.BlockSpec((None, 1, D_MODEL), lambda i: (i // tiles_per_seq, 0, 0))
    consts = [wp["g_out"], wp["g_ffn"], wp["w_out"], wp["w_gate"], wp["w_up"], wp["w_down"]]
    tok = lambda w: pl.BlockSpec((tm, w), lambda i: (i, 0))
    return pl.pallas_call(
        functools.partial(_out_kernel, ff_chunk=256),
        grid=(t_total // tm,),
        in_specs=[tok(D_MODEL), tok(512), tok(512), mod_spec, mod_spec, mod_spec, mod_spec]
                 + [_full(c.shape) for c in consts],
        out_specs=tok(D_MODEL),
        out_shape=jax.ShapeDtypeStruct((t_total, D_MODEL), F32),
        compiler_params=_cparams(("parallel",)),
    )(x2d, oa, ob, gate_m, shift_f, scale_f, gate_f, *consts)


def _prep_weights(w_in, g_norm_mix, g_norm_ffn, g_q_lat, w_uq, g_kv_lat, w_ukv, g_mla_q_nope, g_mla_q_rope,
                  g_mla_k_nope, g_mla_k_rope, g_dsa_q, g_dsa_k, g_out, w_out, w_ffn_in, w_ffn_out):
    splits = np.cumsum([MLA_Q_RANK, MLA_KV_RANK, MLA_ROPE, 512, 128, 128, 512, IDX_DIM, IDX_HEADS])
    s = [0] + splits.tolist()
    col = lambda i: w_in[:, s[i]:s[i + 1]]
    pad = jnp.zeros((D_MODEL, C_END - C_TAIL - IDX_DIM - MLA_ROPE - IDX_HEADS), w_in.dtype)
    w_in_r = jnp.concatenate([col(0), col(1), col(3), col(4), col(5), col(6), col(7), col(2), col(8), pad], axis=1)
    zq = jnp.zeros((MLA_Q_RANK, MLA_HEADS, LANES - MLA_NOPE - MLA_ROPE), w_uq.dtype)
    w_uq_cat = jnp.concatenate([w_uq, zq], axis=2).reshape(MLA_Q_RANK, MLA_HEADS * LANES)
    zk = jnp.zeros((MLA_KV_RANK, MLA_HEADS, LANES - MLA_NOPE), w_ukv.dtype)
    w_uk_cat = jnp.concatenate([w_ukv[:, :, :MLA_NOPE], zk], axis=2).reshape(MLA_KV_RANK, MLA_HEADS * LANES)
    w_uv = w_ukv[:, :, MLA_NOPE:].reshape(MLA_KV_RANK, MLA_HEADS * MLA_V)
    zv = jnp.zeros((MLA_KV_RANK, MLA_HEADS, LANES - MLA_V), w_ukv.dtype)
    w_uv_pad = jnp.concatenate([w_ukv[:, :, MLA_NOPE:], zv], axis=2).reshape(MLA_KV_RANK, MLA_HEADS * LANES)
    w_uk_t = w_ukv[:, :, :MLA_NOPE].reshape(MLA_KV_RANK, MLA_HEADS * MLA_NOPE).T
    z32 = jnp.zeros((LANES - MLA_NOPE - MLA_ROPE,), F32)
    z64 = jnp.zeros((LANES - MLA_NOPE,), F32)
    g_qcat = jnp.tile(jnp.concatenate([g_mla_q_nope, g_mla_q_rope, z32]), MLA_HEADS)
    g_kcat = jnp.tile(jnp.concatenate([g_mla_k_nope, z64]), MLA_HEADS)
    g_tail = jnp.concatenate([jnp.zeros((T_ROPE,), F32), g_mla_k_rope, z32])
    row = lambda v: v.reshape(1, -1).astype(F32)
    return {
        "g_mix": row(g_norm_mix), "g_ffn": row(g_norm_ffn), "w_in": w_in_r.astype(BF16),
        "g_qlat": row(g_q_lat), "w_uq": w_uq_cat.astype(BF16), "g_qcat": row(g_qcat),
        "g_kvlat": row(g_kv_lat), "w_ukv": jnp.concatenate([w_uk_cat, w_uv_pad], axis=1).astype(BF16),
        "g_kcat": row(g_kcat), "g_tail": row(g_tail),
        "g_q": row(jnp.tile(g_dsa_q, DSA_HEADS)), "g_k": row(jnp.tile(g_dsa_k, DSA_KV_HEADS)),
        "m_cat": _group_matrix(256, [(0, 64), (64, 96), (128, 192), (192, 224)]),
        "m_64": _group_matrix(256, [(0, 64), (64, 128), (128, 192), (192, 256)]),
        "w_uk_t": w_uk_t.astype(BF16), "w_uv": w_uv.astype(BF16), "g_k_nope": g_mla_k_nope,
        "g_out": row(g_out), "w_out": w_out.astype(BF16),
        "w_gate": w_ffn_in[:, :D_FF].astype(BF16), "w_up": w_ffn_in[:, D_FF:].astype(BF16),
        "w_down": w_ffn_out.astype(BF16),
    }


def _rope_tables(pos):
    freq = ROPE_THETA ** (-jnp.arange(HALF_ROPE, dtype=F32) / HALF_ROPE)
    ang = pos.astype(F32)[:, None] * freq[None, :]
    cos, sin = jnp.cos(ang), jnp.sin(ang)
    n = pos.shape[0]
    ones = jnp.ones((n, T_ROPE), F32)
    zeros = jnp.zeros((n, T_ROPE), F32)
    tail1 = jnp.ones((n, LANES - T_ROPE - MLA_ROPE), F32)
    tail0 = jnp.zeros((n, LANES - T_ROPE - MLA_ROPE), F32)
    return (jnp.concatenate([ones, cos, cos, tail1], axis=1),
            jnp.concatenate([zeros, -sin, sin, tail0], axis=1))


def _pick_tile(n, prefs):
    for t in prefs:
        if n % t == 0:
            return t
    return n


def kernel(x_prompt, x_sample, c_prompt, c_sample, cache_mla, cache_kv, cache_idx, page_table, rel_bias, w_ada, b_ada, g_norm_mix, g_norm_ffn, w_in, g_q_lat, w_uq, g_kv_lat, w_ukv, g_mla_q_nope, g_mla_q_rope, g_mla_k_nope, g_mla_k_rope, g_dsa_q, g_dsa_k, g_out, w_out, w_ffn_in, w_ffn_out):
    assert w_ada.shape[0] == 1 and x_sample.shape[1] == 1, "one layer, one new token per sample"
    batch, seq, _ = x_prompt.shape
    dec = x_sample.shape[0]
    n_pages = page_table.shape[1]
    past = n_pages * PAGE_SIZE
    wp = _prep_weights(w_in[0], g_norm_mix[0], g_norm_ffn[0], g_q_lat[0], w_uq[0], g_kv_lat[0], w_ukv[0],
                       g_mla_q_nope[0], g_mla_q_rope[0], g_mla_k_nope[0], g_mla_k_rope[0], g_dsa_q[0], g_dsa_k[0],
                       g_out[0], w_out[0], w_ffn_in[0], w_ffn_out[0])

    mod = _ada(jnp.concatenate([c_prompt, c_sample], axis=0), w_ada[0], b_ada[0])
    mods_p = [m.reshape(batch, 1, D_MODEL) for m in jnp.split(mod[:batch], 6, axis=-1)]
    mods_s = jnp.split(mod[batch:], 6, axis=-1)

    tm = _pick_tile(seq, (512, 256, 128))
    xp2 = x_prompt.reshape(batch * seq, D_MODEL)
    cos_p, sin_p = _rope_tables(jnp.arange(seq))
    pin = _in_stage(xp2, mods_p[1], mods_p[0], cos_p, sin_p, wp, tm, per_token=False)
    o_mla_p = _mla_prompt(pin["qcat"], pin["kcat"], pin["vmla"], batch, seq,
                          _pick_tile(seq, (256, 128)), _pick_tile(seq, (512, 256, 128)))
    o_dsa_p = _dsa_prompt(rel_bias, pin["qidx"], pin["tail"], pin["kidxdup"], pin["qdsa"], pin["kdup"], pin["vone"],
                          batch, seq, _pick_tile(seq, (256, 128)))
    y_p = _out_stage(xp2, o_mla_p, o_dsa_p, mods_p[2], mods_p[3], mods_p[4], mods_p[5], wp, tm,
                     per_token=False, tiles_per_seq=seq // tm)

    xs2 = x_sample.reshape(dec, D_MODEL)
    cos_s, sin_s = _rope_tables(jnp.full((1,), past))
    sin_ = _in_stage(xs2, mods_s[1], mods_s[0], cos_s, sin_s, wp, dec, per_token=True)
    pps_wide = _pick_tile(n_pages, (64, 32, 16, 8, 4, 2))
    qidx3 = sin_["qidx"].reshape(dec, IDX_HEADS, IDX_DIM)
    widx3 = sin_["tail"][:, T_WIDX:T_WIDX + IDX_HEADS].reshape(dec, IDX_HEADS, 1)
    scores = _sample_scores(page_table, qidx3, widx3, _pages_feature_major(cache_idx), pps_wide)
    selp, seln = _sample_select(scores, sin_["qidx"], sin_["tail"])
    qc = sin_["qcat"].reshape(dec, MLA_HEADS, LANES)
    qn_row = qc[:, :, 0:MLA_NOPE].astype(F32).reshape(dec, 1, MLA_HEADS * MLA_NOPE)
    qr = qc[:, :, T_ROPE:T_ROPE + MLA_ROPE]
    gk_row = jnp.tile(wp["g_k_nope"], MLA_HEADS).reshape(1, MLA_HEADS * MLA_NOPE)
    qd = sin_["qdsa"].astype(F32).reshape(dec, DSA_HEADS, DSA_HEAD_DIM)
    on_c = (jnp.arange(DSA_KV_HEADS)[None, :] == (jnp.arange(DSA_HEADS) // DSA_GROUP)[:, None]).astype(F32)
    qd2 = (qd[:, :, None, :] * on_c[None, :, :, None]).reshape(dec, DSA_HEADS, LANES)
    bias_past = _bias_table(rel_bias, 1, past, 0, -1, past).reshape(DSA_HEADS, past)
    bias_new = _bias_table(rel_bias, 1, LANES, 0, 0, 0)[:, 0, 0:1]
    o_mla_s, o8 = _decode_attention(
        page_table, qn_row, qr, sin_["row"].reshape(dec, MLA_ROW, 1), wp["w_uk_t"], wp["w_uv"], gk_row,
        _pages_feature_major(cache_mla),
        qd2, sin_["kv"].reshape(dec, 1, 256), selp.reshape(dec, 1, past), seln.reshape(dec, 1, LANES),
        bias_past, bias_new, _pages_feature_major(cache_kv), pps_wide)
    o_mla_s = o_mla_s.reshape(dec, MLA_HEADS * MLA_V)
    o_dsa_s = o8.reshape(dec, DSA_HEADS, DSA_KV_HEADS, DSA_HEAD_DIM).sum(axis=2).reshape(dec, DSA_HEADS * DSA_HEAD_DIM)
    y_s = _out_stage(xs2, o_mla_s, o_dsa_s, mods_s[2], mods_s[3], mods_s[4], mods_s[5], wp, dec,
                     per_token=True, tiles_per_seq=1)

    return (y_p.reshape(batch, seq, D_MODEL), y_s.reshape(dec, 1, D_MODEL),
            jnp.swapaxes(pin["row"], 1, 2)[None],
            jnp.moveaxis(pin["kv"].reshape(batch, 2, DSA_KV_HEADS, DSA_HEAD_DIM, seq), -1, 1)[None],
            jnp.swapaxes(pin["kidx"], 1, 2)[None],
            sin_["row"].reshape(1, dec, 1, MLA_ROW),
            sin_["kv"].reshape(1, dec, 1, 2, DSA_KV_HEADS, DSA_HEAD_DIM),
            sin_["kidx"].reshape(1, dec, 1, IDX_DIM))
```
